```python
import math
import jax
import jax.numpy as jnp
from jax import lax
import numpy as np

D_MODEL = 2048
BATCH = 2
SEQ = 4096
DEPTH = 2
DEC_BATCH = 16
DEC_SEQ = 64
PAST_LEN = 1024

CHUNK = 64
Q_BLOCK = 128
EPS = 1e-6
NEG_INF = -1e30

A_WIDTH = 512
A_KW = 31
B_HEADS = 4
B_DK = 128
B_DV = 256
ROPE_BASE = 10000.0
C_HEADS = 4
C_DQK = 64
C_DV = 128
REL_BUCKETS = 32
REL_MAX_DIST = 128

MIX_WIDTH = A_WIDTH + B_HEADS * B_DV + C_HEADS * C_DV
A_COLS = 2 * A_WIDTH
B_COLS = B_HEADS * (2 * B_DK + 2 * B_DV)
C_COLS = C_HEADS * (4 * C_DQK + C_DV)
IN_COLS = A_COLS + B_COLS + C_COLS

MEM_LEN = 256
M_HEADS = 4
M_DH = 128
D_FF = 5632
F_KW = 3

kernel_name = "hybrid_stream_encoder_step"

F32 = jnp.float32


def rms_norm(x, g):
    xf = x.astype(F32)
    y = xf * lax.rsqrt(jnp.mean(xf * xf, axis=-1, keepdims=True) + EPS)
    return (y * g.astype(F32)).astype(x.dtype)


def layer_norm(x, g, b):
    xf = x.astype(F32)
    mu = jnp.mean(xf, axis=-1, keepdims=True)
    xc = xf - mu
    var = jnp.mean(xc * xc, axis=-1, keepdims=True)
    return (xc * lax.rsqrt(var + EPS) * g.astype(F32) + b.astype(F32)).astype(x.dtype)


def causal_dwconv(x_ext, w, b):
    c = x_ext.shape[-1]
    y = lax.conv_general_dilated(x_ext, w.astype(x_ext.dtype)[:, None, :], window_strides=(1,),
                                 padding="VALID", dimension_numbers=("NWC", "WIO", "NWC"),
                                 feature_group_count=c)
    return y + b.astype(y.dtype)


def rope(x, pos):
    half = x.shape[-1] // 2
    inv = 1.0 / (ROPE_BASE ** (jnp.arange(half, dtype=F32) / half))
    ang = pos.astype(F32)[:, None] * inv[None, :]
    cos = jnp.cos(ang)[None, :, None, :]
    sin = jnp.sin(ang)[None, :, None, :]
    x1 = x[..., :half].astype(F32)
    x2 = x[..., half:].astype(F32)
    return jnp.concatenate([x1 * cos - x2 * sin, x1 * sin + x2 * cos], axis=-1).astype(x.dtype)


def rel_bucket(rel):
    half = REL_BUCKETS // 2
    exact = half // 2
    n = jnp.abs(rel)
    large = exact + (jnp.log(jnp.maximum(n, 1).astype(F32) / exact) / math.log(REL_MAX_DIST / exact)
                     * (half - exact)).astype(jnp.int32)
    large = jnp.minimum(large, half - 1)
    return jnp.where(rel > 0, half, 0) + jnp.where(n < exact, n, large)


def retention_block(q, k, v, state, log_gamma):
    L = q.shape[1]
    idx = jnp.arange(L, dtype=F32)
    diff = idx[:, None] - idx[None, :]
    decay = jnp.where(diff >= 0, jnp.exp(log_gamma[:, None, None] * jnp.maximum(diff, 0.0)), 0.0)
    qf, kf, vf = q.astype(F32), k.astype(F32), v.astype(F32)
    scores = jnp.einsum("bihd,bjhd->bhij", qf, kf) * decay
    inner = jnp.einsum("bhij,bjhe->bihe", scores, vf)
    q_decay = jnp.exp(log_gamma[None, :] * (idx[:, None] + 1.0))
    cross = jnp.einsum("bihd,bhde->bihe", qf, state) * q_decay[None, :, :, None]
    k_decay = jnp.exp(log_gamma[None, :] * (L - 1.0 - idx[:, None]))
    new_state = (state * jnp.exp(log_gamma * L)[None, :, None, None]
                 + jnp.einsum("bjhd,jh,bjhe->bhde", kf, k_decay, vf))
    return inner + cross, new_state


def diff_attention(q, k, v, q_pos, k_pos, rel_bias, lam, lam_init, subln_g):
    logits = jnp.einsum("bqhmd,bkhmd->bmhqk", q.astype(F32), k.astype(F32)) * (C_DQK ** -0.5)
    rel = k_pos[None, :] - q_pos[:, None]
    bias = jnp.transpose(rel_bias.astype(F32)[rel_bucket(rel)], (2, 0, 1))
    visible = (k_pos[None, :] // CHUNK) <= (q_pos[:, None] // CHUNK)
    logits = jnp.where(visible, logits + bias, NEG_INF)
    probs = jax.nn.softmax(logits, axis=-1)
    weights = probs[:, 0] - lam * probs[:, 1]
    o = jnp.einsum("bhqk,bkhe->bqhe", weights, v.astype(F32))
    return rms_norm(o, subln_g) * (1.0 - lam_init)


def mixing_block(x, pos0, conv_hist, ret_state, past_k, past_v, lam, lam_init, w, rel_bias):
    (norm_g, w_in, conv_w, conv_b, ln_g, ln_b, ret_gn_g, qn_g, kn_g, subln_g, w_out) = w
    bsz, t, _ = x.shape
    h = rms_norm(x, norm_g)
    proj = h @ w_in
    a_in, b_in, c_in = jnp.split(proj, [A_COLS, A_COLS + B_COLS], axis=-1)

    a_val, a_gate = jnp.split(a_in, 2, axis=-1)
    a = a_val * jax.nn.sigmoid(a_gate)
    a_ext = jnp.concatenate([conv_hist.astype(a.dtype), a], axis=1)
    a = jax.nn.silu(layer_norm(causal_dwconv(a_ext, conv_w, conv_b), ln_g, ln_b))
    new_conv_hist = a_ext[:, -(A_KW - 1):]

    qk = B_HEADS * B_DK
    bq, bk, bv, bg = jnp.split(b_in, [qk, 2 * qk, 2 * qk + B_HEADS * B_DV], axis=-1)
    pos = pos0 + jnp.arange(t, dtype=jnp.int32)
    bq = rope(bq.reshape(bsz, t, B_HEADS, B_DK), pos)
    bk = rope(bk.reshape(bsz, t, B_HEADS, B_DK), pos) * (B_DK ** -0.5)
    bv = bv.reshape(bsz, t, B_HEADS, B_DV)
    log_gamma = jnp.log(1.0 - 2.0 ** (-5.0 - jnp.arange(B_HEADS, dtype=F32)))
    s0 = ret_state.astype(F32)
    if past_k is None:
        nc = t // CHUNK

        def to_chunks(z):
            return z.reshape(bsz, nc, CHUNK, z.shape[2], z.shape[3]).swapaxes(0, 1)

        def step(s, inp):
            o, s_new = retention_block(inp[0], inp[1], inp[2], s, log_gamma)
            return s_new, o

        new_ret, bo = lax.scan(step, s0, (to_chunks(bq), to_chunks(bk), to_chunks(bv)))
        bo = bo.swapaxes(0, 1).reshape(bsz, t, B_HEADS, B_DV)
    else:
        bo, new_ret = retention_block(bq, bk, bv, s0, log_gamma)
    bo = rms_norm(bo, ret_gn_g.reshape(B_HEADS, B_DV)).reshape(bsz, t, B_HEADS * B_DV)
    bo = bo.astype(x.dtype) * jax.nn.silu(bg)

    cq, ck, cv = jnp.split(c_in, [C_HEADS * 2 * C_DQK, C_HEADS * 4 * C_DQK], axis=-1)
    cq = rms_norm(cq.reshape(bsz, t, C_HEADS, 2, C_DQK), qn_g)
    ck = rms_norm(ck.reshape(bsz, t, C_HEADS, 2, C_DQK), kn_g)
    cv = cv.reshape(bsz, t, C_HEADS, C_DV)
    if past_k is None:
        k_pos = jnp.arange(t, dtype=jnp.int32)
        nb = t // Q_BLOCK
        q_blocks = cq.reshape(bsz, nb, Q_BLOCK, C_HEADS, 2, C_DQK).swapaxes(0, 1)

        def attend_block(args):
            q_blk, i = args
            q_pos = i * Q_BLOCK + jnp.arange(Q_BLOCK, dtype=jnp.int32)
            return diff_attention(q_blk, ck, cv, q_pos, k_pos, rel_bias, lam, lam_init, subln_g)

        co = lax.map(attend_block, (q_blocks, jnp.arange(nb, dtype=jnp.int32)))
        co = co.swapaxes(0, 1).reshape(bsz, t, C_HEADS * C_DV)
    else:
        p = past_k.shape[1]
        k_all = jnp.concatenate([past_k.reshape(bsz, p, C_HEADS, 2, C_DQK).astype(ck.dtype), ck], axis=1)
        v_all = jnp.concatenate([past_v.astype(cv.dtype), cv], axis=1)
        co = diff_attention(cq, k_all, v_all, pos, jnp.arange(p + t, dtype=jnp.int32),
                            rel_bias, lam, lam_init, subln_g)
        co = co.reshape(bsz, t, C_HEADS * C_DV)

    mix = jnp.concatenate([a, bo, co.astype(x.dtype)], axis=-1)
    y = x + mix @ w_out
    return y, new_conv_hist, new_ret.astype(x.dtype), ck.reshape(bsz, t, C_HEADS, 2 * C_DQK), cv


def memory_kv(mem, mem_g, w_k, w_v, kn_g):
    bsz, m, _ = mem.shape
    hm = rms_norm(mem, mem_g)
    k = rms_norm((hm @ w_k).reshape(bsz, m, M_HEADS, M_DH), kn_g)
    v = (hm @ w_v).reshape(bsz, m, M_HEADS, M_DH)
    return k, v


def memory_block(x, mem_k, mem_v, norm_g, w_q, qn_g, w_o):
    bsz, t, _ = x.shape
    q = rms_norm((rms_norm(x, norm_g) @ w_q).reshape(bsz, t, M_HEADS, M_DH), qn_g)
    logits = jnp.einsum("bqhd,bkhd->bhqk", q.astype(F32), mem_k.astype(F32)) * (M_DH ** -0.5)
    probs = jax.nn.softmax(logits, axis=-1)
    o = jnp.einsum("bhqk,bkhd->bqhd", probs, mem_v.astype(F32)).astype(x.dtype)
    return x + o.reshape(bsz, t, M_HEADS * M_DH) @ w_o


def conv_ffn_block(x, hist, norm_g, w_up, conv_w, conv_b, w_down):
    u = rms_norm(x, norm_g) @ w_up
    u_ext = jnp.concatenate([hist.astype(u.dtype), u], axis=1)
    c = causal_dwconv(u_ext, conv_w, conv_b)
    val, gate = jnp.split(c, 2, axis=-1)
    return x + (jax.nn.silu(gate) * val) @ w_down, u_ext[:, -(F_KW - 1):]


def setup_inputs(seed: int = 0) -> dict:
    key = jax.random.key(seed)
    ks = iter(jax.random.split(key, 48))

    def nrm(shape, scale=1.0):
        return jax.random.normal(next(ks), shape, F32) * scale

    def gain(shape):
        return 1.0 + nrm(shape, 0.02)

    L = DEPTH
    return {
        "x_prompt": nrm((BATCH, SEQ, D_MODEL)),
        "x_sample": nrm((DEC_BATCH, DEC_SEQ, D_MODEL)),
        "mem_prompt": nrm((BATCH, MEM_LEN, D_MODEL)),
        "state_conv_a": nrm((L, DEC_BATCH, A_KW - 1, A_WIDTH), 0.5),
        "state_ret": nrm((L, DEC_BATCH, B_HEADS, B_DK, B_DV), 0.5),
        "cache_diff_k": nrm((L, DEC_BATCH, PAST_LEN, C_HEADS, 2 * C_DQK)),
        "cache_diff_v": nrm((L, DEC_BATCH, PAST_LEN, C_HEADS, C_DV)),
        "cache_mem_k": nrm((L, DEC_BATCH, MEM_LEN, M_HEADS, M_DH)),
        "cache_mem_v": nrm((L, DEC_BATCH, MEM_LEN, M_HEADS, M_DH)),
        "state_conv_f": nrm((L, DEC_BATCH, F_KW - 1, 2 * D_FF)),
        "norm1_g": gain((L, D_MODEL)),
        "w_in": nrm((L, D_MODEL, IN_COLS), D_MODEL ** -0.5),
        "conv_a_w": nrm((L, A_KW, A_WIDTH), A_KW ** -0.5),
        "conv_a_b": nrm((L, A_WIDTH), 0.02),
        "ln_a_g": gain((L, A_WIDTH)),
        "ln_a_b": nrm((L, A_WIDTH), 0.02),
        "ret_gn_g": gain((L, B_HEADS * B_DV)),
        "diff_qn_g": gain((L, C_DQK)),
        "diff_kn_g": gain((L, C_DQK)),
        "diff_lq1": nrm((L, C_DQK), 0.1),
        "diff_lk1": nrm((L, C_DQK), 0.1),
        "diff_lq2": nrm((L, C_DQK), 0.1),
        "diff_lk2": nrm((L, C_DQK), 0.1),
        "diff_subln_g": gain((L, C_DV)),
        "w_out": nrm((L, MIX_WIDTH, D_MODEL), MIX_WIDTH ** -0.5),
        "rel_bias": nrm((REL_BUCKETS, C_HEADS), 0.5),
        "norm2_g": gain((L, D_MODEL)),
        "mem_norm_g": gain((L, D_MODEL)),
        "w_xq": nrm((L, D_MODEL, M_HEADS * M_DH), D_MODEL ** -0.5),
        "w_xk": nrm((L, D_MODEL, M_HEADS * M_DH), D_MODEL ** -0.5),
        "w_xv": nrm((L, D_MODEL, M_HEADS * M_DH), D_MODEL ** -0.5),
        "xqn_g": gain((L, M_DH)),
        "xkn_g": gain((L, M_DH)),
        "w_xo": nrm((L, M_HEADS * M_DH, D_MODEL), (M_HEADS * M_DH) ** -0.5),
        "norm3_g": gain((L, D_MODEL)),
        "w_up": nrm((L, D_MODEL, 2 * D_FF), D_MODEL ** -0.5),
        "conv_f_w": nrm((L, F_KW, 2 * D_FF), F_KW ** -0.5),
        "conv_f_b": nrm((L, 2 * D_FF), 0.02),
        "w_down": nrm((L, D_FF, D_MODEL), D_FF ** -0.5),
    }


def reference(x_prompt, x_sample, mem_prompt, state_conv_a, state_ret, cache_diff_k, cache_diff_v,
              cache_mem_k, cache_mem_v, state_conv_f, norm1_g, w_in, conv_a_w, conv_a_b, ln_a_g, ln_a_b,
              ret_gn_g, diff_qn_g, diff_kn_g, diff_lq1, diff_lk1, diff_lq2, diff_lk2, diff_subln_g, w_out,
              rel_bias, norm2_g, mem_norm_g, w_xq, w_xk, w_xv, xqn_g, xkn_g, w_xo, norm3_g, w_up,
              conv_f_w, conv_f_b, w_down):
    xp, xs = x_prompt, x_sample
    bp, bs = xp.shape[0], xs.shape[0]
    past = cache_diff_k.shape[2]
    p_ca, p_rs, p_k, p_v, p_mk, p_mv, p_cf = [], [], [], [], [], [], []
    s_ca, s_rs, s_k, s_v, s_cf = [], [], [], [], []
    for l in range(DEPTH):
        lam_init = 0.8 - 0.6 * math.exp(-0.3 * l)
        lam = (jnp.exp(jnp.sum(diff_lq1[l].astype(F32) * diff_lk1[l].astype(F32)))
               - jnp.exp(jnp.sum(diff_lq2[l].astype(F32) * diff_lk2[l].astype(F32))) + lam_init)
        mix_w = (norm1_g[l], w_in[l], conv_a_w[l], conv_a_b[l], ln_a_g[l], ln_a_b[l], ret_gn_g[l],
                 diff_qn_g[l], diff_kn_g[l], diff_subln_g[l], w_out[l])

        zero_a = jnp.zeros((bp, A_KW - 1, A_WIDTH), xp.dtype)
        zero_r = jnp.zeros((bp, B_HEADS, B_DK, B_DV), F32)
        xp, ca, rs, kn, vn = mixing_block(xp, 0, zero_a, zero_r, None, None, lam, lam_init, mix_w, rel_bias)
        mk, mv = memory_kv(mem_prompt, mem_norm_g[l], w_xk[l], w_xv[l], xkn_g[l])
        xp = memory_block(xp, mk, mv, norm2_g[l], w_xq[l], xqn_g[l], w_xo[l])
        zero_f = jnp.zeros((bp, F_KW - 1, 2 * D_FF), xp.dtype)
        xp, cf = conv_ffn_block(xp, zero_f, norm3_g[l], w_up[l], conv_f_w[l], conv_f_b[l], w_down[l])
        p_ca.append(ca); p_rs.append(rs); p_k.append(kn); p_v.append(vn)
        p_mk.append(mk); p_mv.append(mv); p_cf.append(cf)

        xs, sca, srs, skn, svn = mixing_block(xs, past, state_conv_a[l], state_ret[l], cache_diff_k[l],
                                              cache_diff_v[l], lam, lam_init, mix_w, rel_bias)
        xs = memory_block(xs, cache_mem_k[l], cache_mem_v[l], norm2_g[l], w_xq[l], xqn_g[l], w_xo[l])
        xs, scf = conv_ffn_block(xs, state_conv_f[l], norm3_g[l], w_up[l], conv_f_w[l], conv_f_b[l], w_down[l])
        s_ca.append(sca); s_rs.append(srs); s_k.append(skn); s_v.append(svn); s_cf.append(scf)

    return (xp, xs,
            jnp.stack(p_ca), jnp.stack(p_rs), jnp.stack(p_k), jnp.stack(p_v),
            jnp.stack(p_mk), jnp.stack(p_mv), jnp.stack(p_cf),
            jnp.stack(s_ca), jnp.stack(s_rs), jnp.stack(s_k), jnp.stack(s_v), jnp.stack(s_cf))
```

```python
import functools
import math

import numpy as np
import jax
import jax.numpy as jnp
from jax import lax
from jax.experimental import pallas as pl
from jax.experimental.pallas import tpu as pltpu

F32 = jnp.float32
BF16 = jnp.bfloat16
EPS = 1e-6
NEG_INF = -1e30

D_MODEL = 2048
CHUNK = 64
A_WIDTH = 512
A_KW = 31
A_HIST = 32
B_HEADS = 4
B_DK = 128
B_DV = 256
ROPE_BASE = 10000.0
C_HEADS = 4
C_DQK = 64
C_DV = 128
REL_BUCKETS = 32
REL_MAX_DIST = 128
M_HEADS = 4
M_DH = 128
MEM_LEN = 256
D_FF = 5632
F_KW = 3
F_HIST = 8

A_COLS = 2 * A_WIDTH
B_COLS = B_HEADS * (2 * B_DK + 2 * B_DV)
C_COLS = C_HEADS * (4 * C_DQK + C_DV)
IN_COLS = A_COLS + B_COLS + C_COLS
B_OFF = A_COLS
C_OFF = A_COLS + B_COLS

VMEM_LIMIT_BYTES = 56 * 1024 * 1024
ATT_TILE = 256


def _params(*sem):
    return pltpu.CompilerParams(dimension_semantics=sem, vmem_limit_bytes=VMEM_LIMIT_BYTES)


def _rms(x, g):
    return x * lax.rsqrt(jnp.mean(x * x, axis=-1, keepdims=True) + EPS) * g


def _dot(a, b):
    return jnp.dot(a, b, preferred_element_type=F32)


def _dot_nt(a, b):
    return lax.dot_general(a, b, (((1,), (1,)), ((), ())), preferred_element_type=F32)


def _dot_tn(a, b):
    return lax.dot_general(a, b, (((0,), (0,)), ((), ())), preferred_element_type=F32)


def _in_proj_body(x_ref, g_ref, w_ref, o_ref, h_ref):
    @pl.when(pl.program_id(1) == 0)
    def _():
        h_ref[...] = _rms(x_ref[...], g_ref[...]).astype(BF16)

    o_ref[...] = _dot(h_ref[...], w_ref[...])


def _in_proj(x, g, w, *, tm=512, tn=512):
    n = x.shape[0]
    cols = w.shape[1]
    return pl.pallas_call(
        _in_proj_body,
        grid=(n // tm, cols // tn),
        in_specs=[
            pl.BlockSpec((tm, D_MODEL), lambda i, j: (i, 0)),
            pl.BlockSpec((1, D_MODEL), lambda i, j: (0, 0)),
            pl.BlockSpec((D_MODEL, tn), lambda i, j: (0, j)),
        ],
        out_specs=pl.BlockSpec((tm, tn), lambda i, j: (i, j)),
        out_shape=jax.ShapeDtypeStruct((n, cols), F32),
        scratch_shapes=[pltpu.VMEM((tm, D_MODEL), BF16)],
        compiler_params=_params("parallel", "arbitrary"),
        name="in_proj",
    )(x, g, w)


def _mixer_body(proj_ref, hist_ref, state_ref, cos_ref, sin_ref, cw_ref, cb_ref, lng_ref, lnb_ref,
                gng_ref, qg_ref, kg_ref,
                a_ref, bo_ref, qn_ref, kn_ref, vb_ref, ck_ref, cv_ref, nh_ref, nr_ref,
                aext, sret, *, tt, log_gammas):
    t = pl.program_id(1)
    nt = pl.num_programs(1)

    @pl.when(t == 0)
    def _():
        aext[0:A_HIST, :] = hist_ref[0]
        sret[...] = state_ref[0]

    glu = proj_ref[:, 0:A_WIDTH] * jax.nn.sigmoid(proj_ref[:, A_WIDTH:A_COLS])
    aext[A_HIST:A_HIST + tt, :] = glu
    first = A_HIST - (A_KW - 1)
    acc = jnp.zeros((tt, A_WIDTH), F32) + cb_ref[...]
    for k in range(A_KW):
        acc = acc + aext[first + k:first + k + tt, :] * cw_ref[k:k + 1, :]
    mu = jnp.mean(acc, axis=-1, keepdims=True)
    xc = acc - mu
    var = jnp.mean(xc * xc, axis=-1, keepdims=True)
    ln = xc * lax.rsqrt(var + EPS) * lng_ref[...] + lnb_ref[...]
    a_ref[...] = (ln * jax.nn.sigmoid(ln)).astype(BF16)

    @pl.when(t == nt - 1)
    def _():
        nh_ref[0] = aext[tt:tt + A_HIST, :]

    aext[0:A_HIST, :] = aext[tt:tt + A_HIST, :]

    cos = cos_ref[...]
    sin = sin_ref[...]
    ri = lax.broadcasted_iota(jnp.int32, (tt, tt), 0)
    ci = lax.broadcasted_iota(jnp.int32, (tt, tt), 1)
    dij = (ri - ci).astype(F32)
    causal = ri >= ci
    rowf = lax.broadcasted_iota(jnp.int32, (tt, 1), 0).astype(F32)
    for h in range(B_HEADS):
        lg = log_gammas[h]
        q = proj_ref[:, B_OFF + h * B_DK:B_OFF + (h + 1) * B_DK]
        k = proj_ref[:, B_OFF + B_HEADS * B_DK + h * B_DK:B_OFF + B_HEADS * B_DK + (h + 1) * B_DK]
        voff = B_OFF + 2 * B_HEADS * B_DK
        v = proj_ref[:, voff + h * B_DV:voff + (h + 1) * B_DV]
        goff = voff + B_HEADS * B_DV
        g = proj_ref[:, goff + h * B_DV:goff + (h + 1) * B_DV]
        qr = q * cos + pltpu.roll(q, B_DK // 2, 1) * sin
        kr = (k * cos + pltpu.roll(k, B_DK // 2, 1) * sin) * (B_DK ** -0.5)
        qb = qr.astype(BF16)
        vb = v.astype(BF16)
        decay = jnp.where(causal, jnp.exp(lg * jnp.maximum(dij, 0.0)), 0.0)
        scores = _dot_nt(qb, kr.astype(BF16)) * decay
        inner = _dot(scores.astype(BF16), vb)
        s_old = sret[h]
        cross = _dot(qb, s_old.astype(BF16)) * jnp.exp(lg * (rowf + 1.0))
        o = inner + cross
        kd = kr * jnp.exp(lg * (tt - 1.0 - rowf))
        sret[h] = s_old * math.exp(lg * tt) + _dot_tn(kd.astype(BF16), vb)
        y = _rms(o, gng_ref[:, h * B_DV:(h + 1) * B_DV])
        bo_ref[:, h * B_DV:(h + 1) * B_DV] = (y * (g * jax.nn.sigmoid(g))).astype(BF16)

    @pl.when(t == nt - 1)
    def _():
        nr_ref[0] = sret[...]

    lane = lax.broadcasted_iota(jnp.int32, (tt, 2 * C_DQK), 1)
    lo = lane < C_DQK

    def qk_norm(x, g2):
        sq = x * x
        s_lo = jnp.sum(jnp.where(lo, sq, 0.0), axis=-1, keepdims=True)
        s_hi = jnp.sum(jnp.where(lo, 0.0, sq), axis=-1, keepdims=True)
        ms = jnp.where(lo, s_lo, s_hi) * (1.0 / C_DQK)
        return x * lax.rsqrt(ms + EPS) * g2

    w = 2 * C_DQK
    for h in range(C_HEADS):
        cq = proj_ref[:, C_OFF + h * w:C_OFF + (h + 1) * w]
        ck = proj_ref[:, C_OFF + C_HEADS * w + h * w:C_OFF + C_HEADS * w + (h + 1) * w]
        qn = qk_norm(cq, qg_ref[...])
        kn = qk_norm(ck, kg_ref[...])
        qn_ref[:, h * w:(h + 1) * w] = (qn * (C_DQK ** -0.5)).astype(BF16)
        kn_ref[:, h * w:(h + 1) * w] = kn.astype(BF16)
        ck_ref[:, h * w:(h + 1) * w] = kn
    cv = proj_ref[:, C_OFF + 2 * C_HEADS * w:IN_COLS]
    cv_ref[...] = cv
    vb_ref[...] = cv.astype(BF16)


def _mixer(proj, hist, state, cos2, sin2, cw, cb, lng, lnb, gng, qg2, kg2, *, bsz, t, tt):
    n = bsz * t
    nt = t // tt
    log_gammas = tuple(math.log(1.0 - 2.0 ** (-5.0 - h)) for h in range(B_HEADS))
    tok = lambda cols: pl.BlockSpec((tt, cols), lambda b, i: (b * nt + i, 0))
    const = lambda r, c: pl.BlockSpec((r, c), lambda b, i: (0, 0))
    cw_dim = C_HEADS * 2 * C_DQK
    return pl.pallas_call(
        functools.partial(_mixer_body, tt=tt, log_gammas=log_gammas),
        grid=(bsz, nt),
        in_specs=[
            tok(IN_COLS),
            pl.BlockSpec((1, A_HIST, A_WIDTH), lambda b, i: (b, 0, 0)),
            pl.BlockSpec((1, B_HEADS, B_DK, B_DV), lambda b, i: (b, 0, 0, 0)),
            pl.BlockSpec((tt, B_DK), lambda b, i: (i, 0)),
            pl.BlockSpec((tt, B_DK), lambda b, i: (i, 0)),
            const(A_KW, A_WIDTH), const(1, A_WIDTH), const(1, A_WIDTH), const(1, A_WIDTH),
            const(1, B_HEADS * B_DV), const(1, 2 * C_DQK), const(1, 2 * C_DQK),
        ],
        out_specs=[
            tok(A_WIDTH), tok(B_HEADS * B_DV), tok(cw_dim), tok(cw_dim), tok(C_HEADS * C_DV),
            tok(cw_dim), tok(C_HEADS * C_DV),
            pl.BlockSpec((1, A_HIST, A_WIDTH), lambda b, i: (b, 0, 0)),
            pl.BlockSpec((1, B_HEADS, B_DK, B_DV), lambda b, i: (b, 0, 0, 0)),
        ],
        out_shape=[
            jax.ShapeDtypeStruct((n, A_WIDTH), BF16),
            jax.ShapeDtypeStruct((n, B_HEADS * B_DV), BF16),
            jax.ShapeDtypeStruct((n, cw_dim), BF16),
            jax.ShapeDtypeStruct((n, cw_dim), BF16),
            jax.ShapeDtypeStruct((n, C_HEADS * C_DV), BF16),
            jax.ShapeDtypeStruct((n, cw_dim), F32),
            jax.ShapeDtypeStruct((n, C_HEADS * C_DV), F32),
            jax.ShapeDtypeStruct((bsz, A_HIST, A_WIDTH), F32),
            jax.ShapeDtypeStruct((bsz, B_HEADS, B_DK, B_DV), F32),
        ],
        scratch_shapes=[pltpu.VMEM((A_HIST + tt, A_WIDTH), F32), pltpu.VMEM((B_HEADS, B_DK, B_DV), F32)],
        compiler_params=_params("parallel", "arbitrary"),
        name="mixer",
    )(proj, hist, state, cos2, sin2, cw, cb, lng, lnb, gng, qg2, kg2)


def _t5_bucket(rel):
    half = REL_BUCKETS // 2
    exact = half // 2
    n = np.abs(rel)
    large = exact + (np.log(np.maximum(n, 1).astype(np.float32) / exact) / math.log(REL_MAX_DIST / exact)
                     * (half - exact)).astype(np.int32)
    large = np.minimum(large, half - 1)
    return (np.where(rel > 0, half, 0) + np.where(n < exact, n, large)).astype(np.int32)


def _bias_body(rb_ref, idx_ref, vis_ref, o_ref):
    h = pl.program_id(0)
    idx = idx_ref[...]
    far = rb_ref[REL_BUCKETS // 2 - 1, h]
    acc = jnp.zeros(idx.shape, F32)
    for b in range(REL_BUCKETS):
        acc = jnp.where(idx == b, rb_ref[b, h], acc)
    o_ref[0] = jnp.where(vis_ref[...] != 0, acc - far, NEG_INF)


def _bias_tiles(rel_bias, idx, vis):
    r, c = idx.shape
    return pl.pallas_call(
        _bias_body,
        grid=(C_HEADS,),
        in_specs=[
            pl.BlockSpec(memory_space=pltpu.SMEM),
            pl.BlockSpec((r, c), lambda h: (0, 0)),
            pl.BlockSpec((r, c), lambda h: (0, 0)),
        ],
        out_specs=pl.BlockSpec((1, r, c), lambda h: (h, 0, 0)),
        out_shape=jax.ShapeDtypeStruct((C_HEADS, r, c), F32),
        compiler_params=_params("arbitrary"),
        name="bias_tiles",
    )(rel_bias, idx, vis)


def _lambda(lp_ref, lam_init):
    lp = lp_ref[...]
    e1 = jnp.exp(jnp.sum(lp[0:1] * lp[1:2], axis=-1, keepdims=True))
    e2 = jnp.exp(jnp.sum(lp[2:3] * lp[3:4], axis=-1, keepdims=True))
    return e1 - e2 + lam_init


def _stack_maps(q):
    lane = lax.broadcasted_iota(jnp.int32, q.shape, 1)
    zero = jnp.zeros_like(q)
    return jnp.concatenate([jnp.where(lane < C_DQK, q, zero), jnp.where(lane < C_DQK, zero, q)], axis=0)


def _attn_finish(acc, l, lam, g, lam_init, tq):
    o = acc / l
    o = o[0:tq] - lam * o[tq:2 * tq]
    return _rms(o, g) * (1.0 - lam_init)


def _attn_prompt_body(q_ref, k_ref, v_ref, bias_ref, lp_ref, g_ref, o_ref, m_scr, l_scr, acc_scr, *, tq, lam_init):
    qi = pl.program_id(2)
    qq = _stack_maps(q_ref[...])
    m_scr[...] = jnp.full(m_scr.shape, NEG_INF, F32)
    l_scr[...] = jnp.zeros(l_scr.shape, F32)
    acc_scr[...] = jnp.zeros(acc_scr.shape, F32)

    def step(ki, bias):
        start = pl.multiple_of(ki * tq, tq)
        kb = k_ref[pl.ds(start, tq), :]
        vb = v_ref[pl.ds(start, tq), :]
        s = _dot_nt(qq, kb)
        if bias is not None:
            s = s + bias
        m_old = m_scr[...]
        m_new = jnp.maximum(m_old, jnp.max(s, axis=-1, keepdims=True))
        alpha = jnp.exp(m_old - m_new)
        p = jnp.exp(s - m_new)
        l_scr[...] = alpha * l_scr[...] + jnp.sum(p, axis=-1, keepdims=True)
        acc_scr[...] = alpha * acc_scr[...] + _dot(p.astype(BF16), vb)
        m_scr[...] = m_new

    n_far = jnp.maximum(qi - 1, 0)

    def far_body(ki, c):
        step(ki, None)
        return c

    lax.fori_loop(0, n_far, far_body, 0)

    def near_body(ki, c):
        b = bias_ref[0, qi - ki]
        step(ki, jnp.concatenate([b, b], axis=0))
        return c

    lax.fori_loop(n_far, qi + 1, near_body, 0)

    lam = _lambda(lp_ref, lam_init)
    o_ref[...] = _attn_finish(acc_scr[...], l_scr[...], lam, g_ref[...], lam_init, tq).astype(BF16)


def _attn_prompt(qn, kn, vb, bias, lp, g, *, bsz, t, lam_init):
    tq = ATT_TILE
    nq = t // tq
    n = bsz * t
    w = 2 * C_DQK
    return pl.pallas_call(
        functools.partial(_attn_prompt_body, tq=tq, lam_init=lam_init),
        grid=(bsz, C_HEADS, nq),
        in_specs=[
            pl.BlockSpec((tq, w), lambda b, h, i: (b * nq + i, h)),
            pl.BlockSpec((t, w), lambda b, h, i: (b, h)),
            pl.BlockSpec((t, C_DV), lambda b, h, i: (b, h)),
            pl.BlockSpec((1, 2, tq, tq), lambda b, h, i: (h, 0, 0, 0)),
            pl.BlockSpec((4, C_DQK), lambda b, h, i: (0, 0)),
            pl.BlockSpec((1, C_DV), lambda b, h, i: (0, 0)),
        ],
        out_specs=pl.BlockSpec((tq, C_DV), lambda b, h, i: (b * nq + i, h)),
        out_shape=jax.ShapeDtypeStruct((n, C_HEADS * C_DV), BF16),
        scratch_shapes=[pltpu.VMEM((2 * tq, 1), F32), pltpu.VMEM((2 * tq, 1), F32), pltpu.VMEM((2 * tq, C_DV), F32)],
        compiler_params=_params("parallel", "parallel", "arbitrary"),
        name="attn_prompt",
    )(qn, kn, vb, bias, lp, g)


def _attn_sample_body(q_ref, k_ref, v_ref, pk_ref, pv_ref, bp_ref, bn_ref, lp_ref, g_ref, o_ref, *, tq, lam_init):
    qq = _stack_maps(q_ref[...])
    bp = bp_ref[0]
    bn = bn_ref[0]
    s_p = _dot_nt(qq, pk_ref[0].astype(BF16)) + jnp.concatenate([bp, bp], axis=0)
    s_n = _dot_nt(qq, k_ref[...]) + jnp.concatenate([bn, bn], axis=0)
    m = jnp.maximum(jnp.max(s_p, axis=-1, keepdims=True), jnp.max(s_n, axis=-1, keepdims=True))
    p_p = jnp.exp(s_p - m)
    p_n = jnp.exp(s_n - m)
    l = jnp.sum(p_p, axis=-1, keepdims=True) + jnp.sum(p_n, axis=-1, keepdims=True)
    acc = _dot(p_p.astype(BF16), pv_ref[0].astype(BF16)) + _dot(p_n.astype(BF16), v_ref[...])
    lam = _lambda(lp_ref, lam_init)
    o_ref[...] = _attn_finish(acc, l, lam, g_ref[...], lam_init, tq).astype(BF16)


def _attn_sample(qn, kn, vb, past_k, past_v, bias_p, bias_n, lp, g, *, bsz, t, lam_init):
    n = bsz * t
    w = 2 * C_DQK
    past = past_k.shape[1]
    return pl.pallas_call(
        functools.partial(_attn_sample_body, tq=t, lam_init=lam_init),
        grid=(bsz, C_HEADS),
        in_specs=[
            pl.BlockSpec((t, w), lambda b, h: (b, h)),
            pl.BlockSpec((t, w), lambda b, h: (b, h)),
            pl.BlockSpec((t, C_DV), lambda b, h: (b, h)),
            pl.BlockSpec((1, past, w), lambda b, h: (b, 0, h)),
            pl.BlockSpec((1, past, C_DV), lambda b, h: (b, 0, h)),
            pl.BlockSpec((1, t, past), lambda b, h: (h, 0, 0)),
            pl.BlockSpec((1, t, t), lambda b, h: (h, 0, 0)),
            pl.BlockSpec((4, C_DQK), lambda b, h: (0, 0)),
            pl.BlockSpec((1, C_DV), lambda b, h: (0, 0)),
        ],
        out_specs=pl.BlockSpec((t, C_DV), lambda b, h: (b, h)),
        out_shape=jax.ShapeDtypeStruct((n, C_HEADS * C_DV), BF16),
        compiler_params=_params("parallel", "parallel"),
        name="attn_sample",
    )(qn, kn, vb, past_k, past_v, bias_p, bias_n, lp, g)


def _memkv_body(mem_ref, g_ref, wk_ref, wv_ref, kg_ref, k_ref, v_ref, kb_ref, vb_ref):
    hm = _rms(mem_ref[0], g_ref[...]).astype(BF16)
    k = _dot(hm, wk_ref[...])
    v = _dot(hm, wv_ref[...])
    for h in range(M_HEADS):
        kn = _rms(k[:, h * M_DH:(h + 1) * M_DH], kg_ref[...])
        k_ref[0, :, h * M_DH:(h + 1) * M_DH] = kn
        kb_ref[0, :, h * M_DH:(h + 1) * M_DH] = kn.astype(BF16)
    v_ref[0] = v
    vb_ref[0] = v.astype(BF16)


def _memkv(mem, g, wk, wv, kg):
    bsz, m, _ = mem.shape
    w = M_HEADS * M_DH
    blk = pl.BlockSpec((1, m, w), lambda b: (b, 0, 0))
    return pl.pallas_call(
        _memkv_body,
        grid=(bsz,),
        in_specs=[
            pl.BlockSpec((1, m, D_MODEL), lambda b: (b, 0, 0)),
            pl.BlockSpec((1, D_MODEL), lambda b: (0, 0)),
            pl.BlockSpec((D_MODEL, w), lambda b: (0, 0)),
            pl.BlockSpec((D_MODEL, w), lambda b: (0, 0)),
            pl.BlockSpec((1, M_DH), lambda b: (0, 0)),
        ],
        out_specs=[blk, blk, blk, blk],
        out_shape=[jax.ShapeDtypeStruct((bsz, m, w), F32), jax.ShapeDtypeStruct((bsz, m, w), F32),
                   jax.ShapeDtypeStruct((bsz, m, w), BF16), jax.ShapeDtypeStruct((bsz, m, w), BF16)],
        compiler_params=_params("parallel"),
        name="memkv",
    )(mem, g, wk, wv, kg)


def _post_body(x_ref, a_ref, bo_ref, co_ref, wo_ref, g_ref, wq_ref, qg_ref, mk_ref, mv_ref, wxo_ref,
               o_ref, att_scr, *, nseq, rps):
    y = (_dot(a_ref[...], wo_ref[0:A_WIDTH, :])
         + _dot(bo_ref[...], wo_ref[A_WIDTH:A_WIDTH + B_HEADS * B_DV, :])
         + _dot(co_ref[...], wo_ref[A_WIDTH + B_HEADS * B_DV:, :]))
    x1 = x_ref[...] + y
    q = _dot(_rms(x1, g_ref[...]).astype(BF16), wq_ref[...])
    for h in range(M_HEADS):
        sl = slice(h * M_DH, (h + 1) * M_DH)
        qn = _rms(q[:, sl], qg_ref[...]).astype(BF16)
        for s in range(nseq):
            rows = slice(s * rps, (s + 1) * rps)
            logits = _dot_nt(qn[rows], mk_ref[s, :, sl]) * (M_DH ** -0.5)
            m = jnp.max(logits, axis=-1, keepdims=True)
            p = jnp.exp(logits - m)
            l = jnp.sum(p, axis=-1, keepdims=True)
            o = _dot(p.astype(BF16), mv_ref[s, :, sl]) / l
            att_scr[rows, sl] = o.astype(BF16)
    o_ref[...] = x1 + _dot(att_scr[...], wxo_ref[...])


def _post(x, a, bo, co, wo, g, wq, qg, mk, mv, wxo, *, t, tm):
    n = x.shape[0]
    w = M_HEADS * M_DH
    if t >= tm:
        nseq, rps = 1, tm
        per = t // tm
        mem_map = lambda i: (i // per, 0, 0)
    else:
        nseq, rps = tm // t, t
        mem_map = lambda i: (i, 0, 0)
    tok = lambda cols: pl.BlockSpec((tm, cols), lambda i: (i, 0))
    const = lambda r, c: pl.BlockSpec((r, c), lambda i: (0, 0), pipeline_mode=pl.Buffered(1))
    return pl.pallas_call(
        functools.partial(_post_body, nseq=nseq, rps=rps),
        grid=(n // tm,),
        in_specs=[
            tok(D_MODEL), tok(A_WIDTH), tok(B_HEADS * B_DV), tok(C_HEADS * C_DV),
            const(D_MODEL, D_MODEL), const(1, D_MODEL), const(D_MODEL, w), const(1, M_DH),
            pl.BlockSpec((nseq, MEM_LEN, w), mem_map),
            pl.BlockSpec((nseq, MEM_LEN, w), mem_map),
            const(w, D_MODEL),
        ],
        out_specs=tok(D_MODEL),
        out_shape=jax.ShapeDtypeStruct((n, D_MODEL), F32),
        scratch_shapes=[pltpu.VMEM((tm, w), BF16)],
        compiler_params=_params("parallel"),
        name="post",
    )(x, a, bo, co, wo, g, wq, qg, mk, mv, wxo)


def _ffn_body(x_ref, g_ref, wv_ref, wg_ref, cwv_ref, cwg_ref, cbv_ref, cbg_ref, hv_ref, hg_ref, wd_ref,
              o_ref, nv_ref, ng_ref, h_scr, acc_scr, ubuf, tail, *, nseq, rps, per):
    i = pl.program_id(0)
    j = pl.program_id(1)
    nj = pl.num_programs(1)

    @pl.when(j == 0)
    def _():
        h_scr[...] = _rms(x_ref[...], g_ref[...]).astype(BF16)
        acc_scr[...] = jnp.zeros(acc_scr.shape, F32)

    seq_start = (i % per) == 0
    stride = rps + F_HIST

    def conv(half, w_ref, cw_ref, cb_ref, hist_ref, new_ref):
        u = _dot(h_scr[...], w_ref[...])
        outs = []
        for s in range(nseq):
            base = s * stride
            if per == 1:
                ubuf[half, base:base + F_HIST, :] = hist_ref[s]
            else:
                @pl.when(seq_start)
                def _():
                    ubuf[half, base:base + F_HIST, :] = hist_ref[s]

                @pl.when(jnp.logical_not(seq_start))
                def _():
                    ubuf[half, base:base + F_HIST, :] = tail[half, j]
            ubuf[half, base + F_HIST:base + stride, :] = u[s * rps:(s + 1) * rps]
            c = cb_ref[...] + u[s * rps:(s + 1) * rps] * cw_ref[F_KW - 1:F_KW, :]
            for k in range(F_KW - 1):
                off = base + F_HIST - (F_KW - 1) + k
                c = c + ubuf[half, off:off + rps, :] * cw_ref[k:k + 1, :]
            outs.append(c)
            last = ubuf[half, base + rps:base + stride, :]
            new_ref[s] = last
            if per > 1:
                tail[half, j] = last
        return outs[0] if nseq == 1 else jnp.concatenate(outs, axis=0)

    val = conv(0, wv_ref, cwv_ref, cbv_ref, hv_ref, nv_ref)
    gate = conv(1, wg_ref, cwg_ref, cbg_ref, hg_ref, ng_ref)
    act = (gate * jax.nn.sigmoid(gate) * val).astype(BF16)
    acc_scr[...] += _dot(act, wd_ref[...])

    @pl.when(j == nj - 1)
    def _():
        o_ref[...] = x_ref[...] + acc_scr[...]


def _ffn(x, g, w_up, cw, cb, hist, w_down, *, t, tm=512, tn=512):
    n = x.shape[0]
    nj = D_FF // tn
    if t >= tm:
        nseq, rps, per = 1, tm, t // tm
        seq_map = lambda i: i // per
    else:
        nseq, rps, per = tm // t, t, 1
        seq_map = lambda i: i
    bsz = hist.shape[0]
    hist_spec = lambda off: pl.BlockSpec((nseq, F_HIST, tn), lambda i, j: (seq_map(i), 0, j + off))
    new_spec = pl.BlockSpec((nseq, F_HIST, tn), lambda i, j: (i, 0, j))
    x_out, tail_v, tail_g = pl.pallas_call(
        functools.partial(_ffn_body, nseq=nseq, rps=rps, per=per),
        grid=(n // tm, nj),
        in_specs=[
            pl.BlockSpec((tm, D_MODEL), lambda i, j: (i, 0)),
            pl.BlockSpec((1, D_MODEL), lambda i, j: (0, 0)),
            pl.BlockSpec((D_MODEL, tn), lambda i, j: (0, j)),
            pl.BlockSpec((D_MODEL, tn), lambda i, j: (0, j + nj)),
            pl.BlockSpec((F_KW, tn), lambda i, j: (0, j)),
            pl.BlockSpec((F_KW, tn), lambda i, j: (0, j + nj)),
            pl.BlockSpec((1, tn), lambda i, j: (0, j)),
            pl.BlockSpec((1, tn), lambda i, j: (0, j + nj)),
            hist_spec(0), hist_spec(nj),
            pl.BlockSpec((tn, D_MODEL), lambda i, j: (j, 0)),
        ],
        out_specs=[pl.BlockSpec((tm, D_MODEL), lambda i, j: (i, 0)), new_spec, new_spec],
        out_shape=[jax.ShapeDtypeStruct((n, D_MODEL), F32),
                   jax.ShapeDtypeStruct((bsz * per, F_HIST, D_FF), F32),
                   jax.ShapeDtypeStruct((bsz * per, F_HIST, D_FF), F32)],
        scratch_shapes=[
            pltpu.VMEM((tm, D_MODEL), BF16),
            pltpu.VMEM((tm, D_MODEL), F32),
            pltpu.VMEM((2, nseq * (rps + F_HIST), tn), F32),
            pltpu.VMEM((2, nj, F_HIST, tn), F32),
        ],
        compiler_params=_params("arbitrary", "arbitrary"),
        name="ffn",
    )(x, g, w_up, w_up, cw, cw, cb, cb, hist, hist, w_down)
    last = lambda a: a.reshape(bsz, per, F_HIST, D_FF)[:, per - 1]
    return x_out, last(tail_v), last(tail_g)


def _rope_tables(pos0, t):
    half = B_DK // 2
    inv = 1.0 / (ROPE_BASE ** (jnp.arange(half, dtype=F32) / half))
    ang = (pos0 + jnp.arange(t, dtype=jnp.int32)).astype(F32)[:, None] * inv[None, :]
    cos = jnp.cos(ang)
    sin = jnp.sin(ang)
    return jnp.concatenate([cos, cos], axis=-1), jnp.concatenate([-sin, sin], axis=-1)


def _pad_rows(h, rows):
    return jnp.pad(h, ((0, 0), (rows - h.shape[1], 0), (0, 0)))


def _row(v):
    return v.reshape(1, -1).astype(F32)


def kernel(x_prompt, x_sample, mem_prompt, state_conv_a, state_ret, cache_diff_k, cache_diff_v, cache_mem_k,
           cache_mem_v, state_conv_f, norm1_g, w_in, conv_a_w, conv_a_b, ln_a_g, ln_a_b, ret_gn_g, diff_qn_g,
           diff_kn_g, diff_lq1, diff_lk1, diff_lq2, diff_lk2, diff_subln_g, w_out, rel_bias, norm2_g, mem_norm_g,
           w_xq, w_xk, w_xv, xqn_g, xkn_g, w_xo, norm3_g, w_up, conv_f_w, conv_f_b, w_down):
    bp, tp, _ = x_prompt.shape
    bs, ts, _ = x_sample.shape
    depth = w_in.shape[0]
    past = cache_diff_k.shape[2]
    cw_dim = C_HEADS * 2 * C_DQK

    tq = ATT_TILE
    r = np.arange(tq)
    rel_diag = r[None, :] - r[:, None]
    idx_p = np.stack([_t5_bucket(rel_diag), _t5_bucket(rel_diag - tq)]).reshape(2 * tq, tq)
    vis_p = np.stack([(r[None, :] // CHUNK) <= (r[:, None] // CHUNK), np.ones((tq, tq), bool)])
    vis_p = vis_p.reshape(2 * tq, tq).astype(np.int32)
    bias_p = _bias_tiles(rel_bias, jnp.asarray(idx_p), jnp.asarray(vis_p)).reshape(C_HEADS, 2, tq, tq)
    rel_s = np.arange(past + ts)[None, :] - (past + np.arange(ts))[:, None]
    bias_s = _bias_tiles(rel_bias, jnp.asarray(_t5_bucket(rel_s)), jnp.ones(rel_s.shape, jnp.int32))
    bias_s_past, bias_s_new = bias_s[:, :, :past], bias_s[:, :, past:]

    cos_p, sin_p = _rope_tables(0, tp)
    cos_s, sin_s = _rope_tables(past, ts)

    xp = x_prompt.reshape(bp * tp, D_MODEL)
    xs = x_sample.reshape(bs * ts, D_MODEL)
    zero_a = jnp.zeros((bp, A_HIST, A_WIDTH), F32)
    zero_r = jnp.zeros((bp, B_HEADS, B_DK, B_DV), F32)
    zero_f = jnp.zeros((bp, F_HIST, 2 * D_FF), F32)

    outs = {k: [] for k in ("p_ca", "p_rs", "p_k", "p_v", "p_mk", "p_mv", "p_cf", "s_ca", "s_rs", "s_k", "s_v", "s_cf")}
    for l in range(depth):
        lam_init = 0.8 - 0.6 * math.exp(-0.3 * l)
        w_in_b = w_in[l].astype(BF16)
        w_out_b = w_out[l].astype(BF16)
        w_xq_b = w_xq[l].astype(BF16)
        w_xo_b = w_xo[l].astype(BF16)
        w_up_b = w_up[l].astype(BF16)
        w_down_b = w_down[l].astype(BF16)
        lp = jnp.stack([diff_lq1[l], diff_lk1[l], diff_lq2[l], diff_lk2[l]]).astype(F32)
        qg2 = _row(jnp.concatenate([diff_qn_g[l], diff_qn_g[l]]))
        kg2 = _row(jnp.concatenate([diff_kn_g[l], diff_kn_g[l]]))
        mix_args = (conv_a_w[l], _row(conv_a_b[l]), _row(ln_a_g[l]), _row(ln_a_b[l]), _row(ret_gn_g[l]), qg2, kg2)
        subln = _row(diff_subln_g[l])

        def block(x, bsz, t, tt, hist_a, state_r, cos2, sin2, attend, mk_b, mv_b, hist_f, tm_post):
            proj = _in_proj(x, _row(norm1_g[l]), w_in_b)
            a, bo, qn, kn, vb, ck, cv, nh, nr = _mixer(proj, hist_a, state_r, cos2, sin2, *mix_args,
                                                        bsz=bsz, t=t, tt=tt)
            co = attend(qn, kn, vb)
            x = _post(x, a, bo, co, w_out_b, _row(norm2_g[l]), w_xq_b, _row(xqn_g[l]), mk_b, mv_b, w_xo_b,
                      t=t, tm=tm_post)
            x, nfv, nfg = _ffn(x, _row(norm3_g[l]), w_up_b, conv_f_w[l], _row(conv_f_b[l]), hist_f, w_down_b, t=t)
            new_f = jnp.concatenate([nfv[:, F_HIST - (F_KW - 1):], nfg[:, F_HIST - (F_KW - 1):]], axis=-1)
            return (x, nh[:, A_HIST - (A_KW - 1):], nr, ck.reshape(bsz, t, C_HEADS, 2 * C_DQK),
                    cv.reshape(bsz, t, C_HEADS, C_DV), new_f)

        mk, mv, mk_b, mv_b = _memkv(mem_prompt, _row(mem_norm_g[l]), w_xk[l].astype(BF16), w_xv[l].astype(BF16),
                                    _row(xkn_g[l]))
        attend_p = lambda qn, kn, vb: _attn_prompt(qn, kn, vb, bias_p, lp, subln, bsz=bp, t=tp, lam_init=lam_init)
        xp, ca, rs, kn_, vn_, cf = block(xp, bp, tp, 256, zero_a, zero_r, cos_p, sin_p, attend_p, mk_b, mv_b,
                                         zero_f, 256)
        outs["p_ca"].append(ca); outs["p_rs"].append(rs); outs["p_k"].append(kn_); outs["p_v"].append(vn_)
        outs["p_mk"].append(mk.reshape(bp, MEM_LEN, M_HEADS, M_DH))
        outs["p_mv"].append(mv.reshape(bp, MEM_LEN, M_HEADS, M_DH))
        outs["p_cf"].append(cf)

        pk = cache_diff_k[l].reshape(bs, past, cw_dim)
        pv = cache_diff_v[l].reshape(bs, past, C_HEADS * C_DV)
        attend_s = lambda qn, kn, vb: _attn_sample(qn, kn, vb, pk, pv, bias_s_past, bias_s_new, lp, subln,
                                                   bsz=bs, t=ts, lam_init=lam_init)
        smk = cache_mem_k[l].reshape(bs, MEM_LEN, M_HEADS * M_DH).astype(BF16)
        smv = cache_mem_v[l].reshape(bs, MEM_LEN, M_HEADS * M_DH).astype(BF16)
        xs, sca, srs, skn, svn, scf = block(xs, bs, ts, ts, _pad_rows(state_conv_a[l], A_HIST), state_ret[l],
                                            cos_s, sin_s, attend_s, smk, smv, _pad_rows(state_conv_f[l], F_HIST), 256)
        outs["s_ca"].append(sca); outs["s_rs"].append(srs); outs["s_k"].append(skn); outs["s_v"].append(svn)
        outs["s_cf"].append(scf)

    st = lambda k: jnp.stack(outs[k])
    return (xp.reshape(bp, tp, D_MODEL), xs.reshape(bs, ts, D_MODEL),
            st("p_ca"), st("p_rs"), st("p_k"), st("p_v"), st("p_mk"), st("p_mv"), st("p_cf"),
            st("s_ca"), st("s_rs"), st("s_k"), st("s_v"), st("s_cf"))
```

```python
import functools
import math

import numpy as np
import jax
import jax.numpy as jnp
from jax import lax
from jax.experimental import pallas as pl
from jax.experimental.pallas import tpu as pltpu

F32 = jnp.float32
BF16 = jnp.bfloat16
EPS = 1e-6
NEG_INF = -1e30

D_MODEL = 2048
CHUNK = 64
A_WIDTH = 512
A_KW = 31
A_HIST = 32
B_HEADS = 4
B_DK = 128
B_DV = 256
ROPE_BASE = 10000.0
C_HEADS = 4
C_DQK = 64
C_DV = 128
REL_BUCKETS = 32
REL_MAX_DIST = 128
M_HEADS = 4
M_DH = 128
MEM_LEN = 256
D_FF = 5632
F_KW = 3
F_HIST = 8

A_COLS = 2 * A_WIDTH
B_COLS = B_HEADS * (2 * B_DK + 2 * B_DV)
C_COLS = C_HEADS * (4 * C_DQK + C_DV)
IN_COLS = A_COLS + B_COLS + C_COLS
B_OFF = A_COLS
C_OFF = A_COLS + B_COLS

VMEM_LIMIT_BYTES = 56 * 1024 * 1024
ATT_TILE = 512
ATT_ROWS = 256
ATT_LANES = 128
FFN_CHAINS = 2


def _params(*sem):
    return pltpu.CompilerParams(dimension_semantics=sem, vmem_limit_bytes=VMEM_LIMIT_BYTES)


def _rms(x, g):
    return x * lax.rsqrt(jnp.mean(x * x, axis=-1, keepdims=True) + EPS) * g


def _dot(a, b):
    return jnp.dot(a, b, preferred_element_type=F32)


def _dot_nt(a, b):
    return lax.dot_general(a, b, (((1,), (1,)), ((), ())), preferred_element_type=F32)


def _dot_tn(a, b):
    return lax.dot_general(a, b, (((0,), (0,)), ((), ())), preferred_element_type=F32)


def _in_proj_body(x_ref, g_ref, w_ref, o_ref, h_ref):
    @pl.when(pl.program_id(1) == 0)
    def _():
        h_ref[...] = _rms(x_ref[...], g_ref[...]).astype(BF16)

    o_ref[...] = _dot(h_ref[...], w_ref[...])


def _in_proj(x, g, w, *, tm=512, tn=512):
    n = x.shape[0]
    cols = w.shape[1]
    return pl.pallas_call(
        _in_proj_body,
        grid=(n // tm, cols // tn),
        in_specs=[
            pl.BlockSpec((tm, D_MODEL), lambda i, j: (i, 0)),
            pl.BlockSpec((1, D_MODEL), lambda i, j: (0, 0)),
            pl.BlockSpec((D_MODEL, tn), lambda i, j: (0, j)),
        ],
        out_specs=pl.BlockSpec((tm, tn), lambda i, j: (i, j)),
        out_shape=jax.ShapeDtypeStruct((n, cols), F32),
        scratch_shapes=[pltpu.VMEM((tm, D_MODEL), BF16)],
        compiler_params=_params("parallel", "arbitrary"),
        name="in_proj",
    )(x, g, w)


def _mixer_body(proj_ref, hist_ref, state_ref, cos_ref, sin_ref, cw_ref, cb_ref, lng_ref, lnb_ref,
                gng_ref, qg_ref, kg_ref,
                a_ref, bo_ref, qn_ref, kn_ref, vb_ref, ck_ref, cv_ref, nh_ref, nr_ref,
                aext, sret, *, tt, log_gammas):
    t = pl.program_id(1)
    nt = pl.num_programs(1)

    @pl.when(t == 0)
    def _():
        aext[0:A_HIST, :] = hist_ref[0]
        sret[...] = state_ref[0]

    glu = proj_ref[:, 0:A_WIDTH] * jax.nn.sigmoid(proj_ref[:, A_WIDTH:A_COLS])
    aext[A_HIST:A_HIST + tt, :] = glu
    first = A_HIST - (A_KW - 1)
    acc = jnp.zeros((tt, A_WIDTH), F32) + cb_ref[...]
    for k in range(A_KW):
        acc = acc + aext[first + k:first + k + tt, :] * cw_ref[k:k + 1, :]
    mu = jnp.mean(acc, axis=-1, keepdims=True)
    xc = acc - mu
    var = jnp.mean(xc * xc, axis=-1, keepdims=True)
    ln = xc * lax.rsqrt(var + EPS) * lng_ref[...] + lnb_ref[...]
    a_ref[...] = (ln * jax.nn.sigmoid(ln)).astype(BF16)

    @pl.when(t == nt - 1)
    def _():
        nh_ref[0] = aext[tt:tt + A_HIST, :]

    aext[0:A_HIST, :] = aext[tt:tt + A_HIST, :]

    cos = cos_ref[...]
    sin = sin_ref[...]
    ri = lax.broadcasted_iota(jnp.int32, (tt, tt), 0)
    ci = lax.broadcasted_iota(jnp.int32, (tt, tt), 1)
    dij = (ri - ci).astype(F32)
    causal = ri >= ci
    rowf = lax.broadcasted_iota(jnp.int32, (tt, 1), 0).astype(F32)
    for h in range(B_HEADS):
        lg = log_gammas[h]
        q = proj_ref[:, B_OFF + h * B_DK:B_OFF + (h + 1) * B_DK]
        k = proj_ref[:, B_OFF + B_HEADS * B_DK + h * B_DK:B_OFF + B_HEADS * B_DK + (h + 1) * B_DK]
        voff = B_OFF + 2 * B_HEADS * B_DK
        v = proj_ref[:, voff + h * B_DV:voff + (h + 1) * B_DV]
        goff = voff + B_HEADS * B_DV
        g = proj_ref[:, goff + h * B_DV:goff + (h + 1) * B_DV]
        qr = q * cos + pltpu.roll(q, B_DK // 2, 1) * sin
        kr = (k * cos + pltpu.roll(k, B_DK // 2, 1) * sin) * (B_DK ** -0.5)
        qb = qr.astype(BF16)
        vb = v.astype(BF16)
        decay = jnp.where(causal, jnp.exp(lg * jnp.maximum(dij, 0.0)), 0.0)
        scores = _dot_nt(qb, kr.astype(BF16)) * decay
        inner = _dot(scores.astype(BF16), vb)
        s_old = sret[h]
        cross = _dot(qb, s_old.astype(BF16)) * jnp.exp(lg * (rowf + 1.0))
        o = inner + cross
        kd = kr * jnp.exp(lg * (tt - 1.0 - rowf))
        sret[h] = s_old * math.exp(lg * tt) + _dot_tn(kd.astype(BF16), vb)
        y = _rms(o, gng_ref[:, h * B_DV:(h + 1) * B_DV])
        bo_ref[:, h * B_DV:(h + 1) * B_DV] = (y * (g * jax.nn.sigmoid(g))).astype(BF16)

    @pl.when(t == nt - 1)
    def _():
        nr_ref[0] = sret[...]

    lane = lax.broadcasted_iota(jnp.int32, (tt, 2 * C_DQK), 1)
    lo = lane < C_DQK

    def qk_norm(x, g2):
        sq = x * x
        s_lo = jnp.sum(jnp.where(lo, sq, 0.0), axis=-1, keepdims=True)
        s_hi = jnp.sum(jnp.where(lo, 0.0, sq), axis=-1, keepdims=True)
        ms = jnp.where(lo, s_lo, s_hi) * (1.0 / C_DQK)
        return x * lax.rsqrt(ms + EPS) * g2

    w = 2 * C_DQK
    for h in range(C_HEADS):
        cq = proj_ref[:, C_OFF + h * w:C_OFF + (h + 1) * w]
        ck = proj_ref[:, C_OFF + C_HEADS * w + h * w:C_OFF + C_HEADS * w + (h + 1) * w]
        qn = qk_norm(cq, qg_ref[...])
        kn = qk_norm(ck, kg_ref[...])
        qn_ref[:, h * w:(h + 1) * w] = (qn * (C_DQK ** -0.5)).astype(BF16)
        kn_ref[:, h * w:(h + 1) * w] = kn.astype(BF16)
        ck_ref[:, h * w:(h + 1) * w] = kn
    cv = proj_ref[:, C_OFF + 2 * C_HEADS * w:IN_COLS]
    cv_ref[...] = cv
    ones = jnp.ones((tt, C_DV), BF16)
    for h in range(C_HEADS):
        vb_ref[:, 2 * h * C_DV:(2 * h + 1) * C_DV] = cv[:, h * C_DV:(h + 1) * C_DV].astype(BF16)
        vb_ref[:, (2 * h + 1) * C_DV:(2 * h + 2) * C_DV] = ones


def _mixer(proj, hist, state, cos2, sin2, cw, cb, lng, lnb, gng, qg2, kg2, *, bsz, t, tt):
    n = bsz * t
    nt = t // tt
    log_gammas = tuple(math.log(1.0 - 2.0 ** (-5.0 - h)) for h in range(B_HEADS))
    tok = lambda cols: pl.BlockSpec((tt, cols), lambda b, i: (b * nt + i, 0))
    const = lambda r, c: pl.BlockSpec((r, c), lambda b, i: (0, 0))
    cw_dim = C_HEADS * 2 * C_DQK
    return pl.pallas_call(
        functools.partial(_mixer_body, tt=tt, log_gammas=log_gammas),
        grid=(bsz, nt),
        in_specs=[
            tok(IN_COLS),
            pl.BlockSpec((1, A_HIST, A_WIDTH), lambda b, i: (b, 0, 0)),
            pl.BlockSpec((1, B_HEADS, B_DK, B_DV), lambda b, i: (b, 0, 0, 0)),
            pl.BlockSpec((tt, B_DK), lambda b, i: (i, 0)),
            pl.BlockSpec((tt, B_DK), lambda b, i: (i, 0)),
            const(A_KW, A_WIDTH), const(1, A_WIDTH), const(1, A_WIDTH), const(1, A_WIDTH),
            const(1, B_HEADS * B_DV), const(1, 2 * C_DQK), const(1, 2 * C_DQK),
        ],
        out_specs=[
            tok(A_WIDTH), tok(B_HEADS * B_DV), tok(cw_dim), tok(cw_dim), tok(2 * C_HEADS * C_DV),
            tok(cw_dim), tok(C_HEADS * C_DV),
            pl.BlockSpec((1, A_HIST, A_WIDTH), lambda b, i: (b, 0, 0)),
            pl.BlockSpec((1, B_HEADS, B_DK, B_DV), lambda b, i: (b, 0, 0, 0)),
        ],
        out_shape=[
            jax.ShapeDtypeStruct((n, A_WIDTH), BF16),
            jax.ShapeDtypeStruct((n, B_HEADS * B_DV), BF16),
            jax.ShapeDtypeStruct((n, cw_dim), BF16),
            jax.ShapeDtypeStruct((n, cw_dim), BF16),
            jax.ShapeDtypeStruct((n, 2 * C_HEADS * C_DV), BF16),
            jax.ShapeDtypeStruct((n, cw_dim), F32),
            jax.ShapeDtypeStruct((n, C_HEADS * C_DV), F32),
            jax.ShapeDtypeStruct((bsz, A_HIST, A_WIDTH), F32),
            jax.ShapeDtypeStruct((bsz, B_HEADS, B_DK, B_DV), F32),
        ],
        scratch_shapes=[pltpu.VMEM((A_HIST + tt, A_WIDTH), F32), pltpu.VMEM((B_HEADS, B_DK, B_DV), F32)],
        compiler_params=_params("parallel", "arbitrary"),
        name="mixer",
    )(proj, hist, state, cos2, sin2, cw, cb, lng, lnb, gng, qg2, kg2)


def _t5_bucket(rel):
    half = REL_BUCKETS // 2
    exact = half // 2
    n = np.abs(rel)
    large = exact + (np.log(np.maximum(n, 1).astype(np.float32) / exact) / math.log(REL_MAX_DIST / exact)
                     * (half - exact)).astype(np.int32)
    large = np.minimum(large, half - 1)
    return (np.where(rel > 0, half, 0) + np.where(n < exact, n, large)).astype(np.int32)


def _bias_body(rb_ref, idx_ref, vis_ref, o_ref):
    h = pl.program_id(0)
    idx = idx_ref[...]
    far = rb_ref[REL_BUCKETS // 2 - 1, h]
    acc = jnp.zeros(idx.shape, F32)
    for b in range(REL_BUCKETS):
        acc = jnp.where(idx == b, rb_ref[b, h], acc)
    o_ref[0] = jnp.where(vis_ref[...] != 0, acc - far, NEG_INF)


def _bias_tiles(rel_bias, idx, vis):
    r, c = idx.shape
    return pl.pallas_call(
        _bias_body,
        grid=(C_HEADS,),
        in_specs=[
            pl.BlockSpec(memory_space=pltpu.SMEM),
            pl.BlockSpec((r, c), lambda h: (0, 0)),
            pl.BlockSpec((r, c), lambda h: (0, 0)),
        ],
        out_specs=pl.BlockSpec((1, r, c), lambda h: (h, 0, 0)),
        out_shape=jax.ShapeDtypeStruct((C_HEADS, r, c), F32),
        compiler_params=_params("arbitrary"),
        name="bias_tiles",
    )(rel_bias, idx, vis)


def _lambda(lp_ref, lam_init):
    lp = lp_ref[...]
    e1 = jnp.exp(jnp.sum(lp[0:1] * lp[1:2], axis=-1, keepdims=True))
    e2 = jnp.exp(jnp.sum(lp[2:3] * lp[3:4], axis=-1, keepdims=True))
    return e1 - e2 + lam_init


def _stack_maps(q):
    lane = lax.broadcasted_iota(jnp.int32, q.shape, 1)
    zero = jnp.zeros_like(q)
    return jnp.concatenate([jnp.where(lane < C_DQK, q, zero), jnp.where(lane < C_DQK, zero, q)], axis=0)


def _attn_finish(acc, l, lam, g, lam_init, tq):
    o = acc / l
    o = o[0:tq] - lam * o[tq:2 * tq]
    return _rms(o, g) * (1.0 - lam_init)


def _attn_prompt_body(q_ref, k_ref, v_ref, bias_ref, lp_ref, g_ref, o_ref, qq_scr, m_scr, acc_scr, *, tq, lam_init):
    qi = pl.program_id(2)
    qq_scr[...] = _stack_maps(q_ref[...])
    m_scr[...] = jnp.full(m_scr.shape, NEG_INF, F32)
    acc_scr[...] = jnp.zeros(acc_scr.shape, F32)
    nl = tq // ATT_LANES

    def step(ki, bias_idx):
        start = pl.multiple_of(ki * tq, tq)
        kb = k_ref[pl.ds(start, tq), :]
        vb = v_ref[pl.ds(start, tq), :]
        for c in range(2 * tq // ATT_ROWS):
            rows = slice(c * ATT_ROWS, (c + 1) * ATT_ROWS)
            s = _dot_nt(qq_scr[rows, :], kb)
            if bias_idx is not None:
                s = s + bias_ref[0, bias_idx, pl.ds((c * ATT_ROWS) % tq, ATT_ROWS), :]
            slabs = [s[:, j * ATT_LANES:(j + 1) * ATT_LANES] for j in range(nl)]
            m_old = m_scr[rows, :]
            m_new = jnp.maximum(m_old, jnp.max(functools.reduce(jnp.maximum, slabs), axis=-1, keepdims=True))
            alpha = jnp.exp(m_old - m_new)
            p = jnp.concatenate([jnp.exp(sl - m_new) for sl in slabs], axis=-1).astype(BF16)
            acc_scr[rows, :] = jnp.concatenate([alpha, alpha], axis=-1) * acc_scr[rows, :] + _dot(p, vb)
            m_scr[rows, :] = m_new

    n_far = jnp.maximum(qi - 1, 0)

    def far_body(ki, c):
        step(ki, None)
        return c

    lax.fori_loop(0, n_far, far_body, 0)

    def near_body(ki, c):
        step(ki, qi - ki)
        return c

    lax.fori_loop(n_far, qi + 1, near_body, 0)

    lam = _lambda(lp_ref, lam_init)
    acc = acc_scr[...]
    o_ref[...] = _attn_finish(acc[:, 0:C_DV], acc[:, C_DV:2 * C_DV], lam, g_ref[...], lam_init, tq).astype(BF16)


def _attn_prompt(qn, kn, vb, bias, lp, g, *, bsz, t, lam_init):
    tq = ATT_TILE
    nq = t // tq
    n = bsz * t
    w = 2 * C_DQK
    return pl.pallas_call(
        functools.partial(_attn_prompt_body, tq=tq, lam_init=lam_init),
        grid=(bsz, C_HEADS, nq),
        in_specs=[
            pl.BlockSpec((tq, w), lambda b, h, i: (b * nq + i, h)),
            pl.BlockSpec((t, w), lambda b, h, i: (b, h)),
            pl.BlockSpec((t, 2 * C_DV), lambda b, h, i: (b, h)),
            pl.BlockSpec((1, 2, tq, tq), lambda b, h, i: (h, 0, 0, 0)),
            pl.BlockSpec((4, C_DQK), lambda b, h, i: (0, 0)),
            pl.BlockSpec((1, C_DV), lambda b, h, i: (0, 0)),
        ],
        out_specs=pl.BlockSpec((tq, C_DV), lambda b, h, i: (b * nq + i, h)),
        out_shape=jax.ShapeDtypeStruct((n, C_HEADS * C_DV), BF16),
        scratch_shapes=[pltpu.VMEM((2 * tq, w), BF16), pltpu.VMEM((2 * tq, ATT_LANES), F32),
                        pltpu.VMEM((2 * tq, 2 * C_DV), F32)],
        compiler_params=_params("parallel", "parallel", "arbitrary"),
        name="attn_prompt",
    )(qn, kn, vb, bias, lp, g)


def _attn_sample_body(q_ref, k_ref, v_ref, pk_ref, pv_ref, bp_ref, bn_ref, lp_ref, g_ref, o_ref, *, tq, lam_init):
    qq = _stack_maps(q_ref[...])
    bp = bp_ref[0]
    bn = bn_ref[0]
    s_p = _dot_nt(qq, pk_ref[0].astype(BF16)) + jnp.concatenate([bp, bp], axis=0)
    s_n = _dot_nt(qq, k_ref[...]) + jnp.concatenate([bn, bn], axis=0)
    m = jnp.maximum(jnp.max(s_p, axis=-1, keepdims=True), jnp.max(s_n, axis=-1, keepdims=True))
    p_p = jnp.exp(s_p - m)
    p_n = jnp.exp(s_n - m)
    l = jnp.sum(p_p, axis=-1, keepdims=True) + jnp.sum(p_n, axis=-1, keepdims=True)
    acc = _dot(p_p.astype(BF16), pv_ref[0].astype(BF16)) + _dot(p_n.astype(BF16), v_ref[:, 0:C_DV])
    lam = _lambda(lp_ref, lam_init)
    o_ref[...] = _attn_finish(acc, l, lam, g_ref[...], lam_init, tq).astype(BF16)


def _attn_sample(qn, kn, vb, past_k, past_v, bias_p, bias_n, lp, g, *, bsz, t, lam_init):
    n = bsz * t
    w = 2 * C_DQK
    past = past_k.shape[1]
    return pl.pallas_call(
        functools.partial(_attn_sample_body, tq=t, lam_init=lam_init),
        grid=(bsz, C_HEADS),
        in_specs=[
            pl.BlockSpec((t, w), lambda b, h: (b, h)),
            pl.BlockSpec((t, w), lambda b, h: (b, h)),
            pl.BlockSpec((t, 2 * C_DV), lambda b, h: (b, h)),
            pl.BlockSpec((1, past, w), lambda b, h: (b, 0, h)),
            pl.BlockSpec((1, past, C_DV), lambda b, h: (b, 0, h)),
            pl.BlockSpec((1, t, past), lambda b, h: (h, 0, 0)),
            pl.BlockSpec((1, t, t), lambda b, h: (h, 0, 0)),
            pl.BlockSpec((4, C_DQK), lambda b, h: (0, 0)),
            pl.BlockSpec((1, C_DV), lambda b, h: (0, 0)),
        ],
        out_specs=pl.BlockSpec((t, C_DV), lambda b, h: (b, h)),
        out_shape=jax.ShapeDtypeStruct((n, C_HEADS * C_DV), BF16),
        compiler_params=_params("parallel", "parallel"),
        name="attn_sample",
    )(qn, kn, vb, past_k, past_v, bias_p, bias_n, lp, g)


def _memkv_body(mem_ref, g_ref, wk_ref, wv_ref, kg_ref, k_ref, v_ref, kb_ref, vb_ref):
    hm = _rms(mem_ref[0], g_ref[...]).astype(BF16)
    k = _dot(hm, wk_ref[...])
    v = _dot(hm, wv_ref[...])
    for h in range(M_HEADS):
        kn = _rms(k[:, h * M_DH:(h + 1) * M_DH], kg_ref[...])
        k_ref[0, :, h * M_DH:(h + 1) * M_DH] = kn
        kb_ref[0, :, h * M_DH:(h + 1) * M_DH] = kn.astype(BF16)
    v_ref[0] = v
    vb_ref[0] = v.astype(BF16)


def _memkv(mem, g, wk, wv, kg):
    bsz, m, _ = mem.shape
    w = M_HEADS * M_DH
    blk = pl.BlockSpec((1, m, w), lambda b: (b, 0, 0))
    return pl.pallas_call(
        _memkv_body,
        grid=(bsz,),
        in_specs=[
            pl.BlockSpec((1, m, D_MODEL), lambda b: (b, 0, 0)),
            pl.BlockSpec((1, D_MODEL), lambda b: (0, 0)),
            pl.BlockSpec((D_MODEL, w), lambda b: (0, 0)),
            pl.BlockSpec((D_MODEL, w), lambda b: (0, 0)),
            pl.BlockSpec((1, M_DH), lambda b: (0, 0)),
        ],
        out_specs=[blk, blk, blk, blk],
        out_shape=[jax.ShapeDtypeStruct((bsz, m, w), F32), jax.ShapeDtypeStruct((bsz, m, w), F32),
                   jax.ShapeDtypeStruct((bsz, m, w), BF16), jax.ShapeDtypeStruct((bsz, m, w), BF16)],
        compiler_params=_params("parallel"),
        name="memkv",
    )(mem, g, wk, wv, kg)


def _post_body(x_ref, a_ref, bo_ref, co_ref, wo_ref, g_ref, wq_ref, qg_ref, mk_ref, mv_ref, wxo_ref,
               o_ref, att_scr, *, nseq, rps):
    y = (_dot(a_ref[...], wo_ref[0:A_WIDTH, :])
         + _dot(bo_ref[...], wo_ref[A_WIDTH:A_WIDTH + B_HEADS * B_DV, :])
         + _dot(co_ref[...], wo_ref[A_WIDTH + B_HEADS * B_DV:, :]))
    x1 = x_ref[...] + y
    q = _dot(_rms(x1, g_ref[...]).astype(BF16), wq_ref[...])
    for h in range(M_HEADS):
        sl = slice(h * M_DH, (h + 1) * M_DH)
        qn = _rms(q[:, sl], qg_ref[...]).astype(BF16)
        for s in range(nseq):
            rows = slice(s * rps, (s + 1) * rps)
            logits = _dot_nt(qn[rows], mk_ref[s, :, sl]) * (M_DH ** -0.5)
            m = jnp.max(logits, axis=-1, keepdims=True)
            p = jnp.exp(logits - m)
            l = jnp.sum(p, axis=-1, keepdims=True)
            o = _dot(p.astype(BF16), mv_ref[s, :, sl]) / l
            att_scr[rows, sl] = o.astype(BF16)
    o_ref[...] = x1 + _dot(att_scr[...], wxo_ref[...])


def _post(x, a, bo, co, wo, g, wq, qg, mk, mv, wxo, *, t, tm):
    n = x.shape[0]
    w = M_HEADS * M_DH
    if t >= tm:
        nseq, rps = 1, tm
        per = t // tm
        mem_map = lambda i: (i // per, 0, 0)
    else:
        nseq, rps = tm // t, t
        mem_map = lambda i: (i, 0, 0)
    tok = lambda cols: pl.BlockSpec((tm, cols), lambda i: (i, 0))
    const = lambda r, c: pl.BlockSpec((r, c), lambda i: (0, 0), pipeline_mode=pl.Buffered(1))
    return pl.pallas_call(
        functools.partial(_post_body, nseq=nseq, rps=rps),
        grid=(n // tm,),
        in_specs=[
            tok(D_MODEL), tok(A_WIDTH), tok(B_HEADS * B_DV), tok(C_HEADS * C_DV),
            const(D_MODEL, D_MODEL), const(1, D_MODEL), const(D_MODEL, w), const(1, M_DH),
            pl.BlockSpec((nseq, MEM_LEN, w), mem_map),
            pl.BlockSpec((nseq, MEM_LEN, w), mem_map),
            const(w, D_MODEL),
        ],
        out_specs=tok(D_MODEL),
        out_shape=jax.ShapeDtypeStruct((n, D_MODEL), F32),
        scratch_shapes=[pltpu.VMEM((tm, w), BF16)],
        compiler_params=_params("parallel"),
        name="post",
    )(x, a, bo, co, wo, g, wq, qg, mk, mv, wxo)


def _ffn_body(x_ref, g_ref, wv_ref, wg_ref, cwv_ref, cwg_ref, cbv_ref, cbg_ref, hv_ref, hg_ref, wd_ref,
              o_ref, nv_ref, ng_ref, h_scr, acc_scr, ubuf, tail, *, nseq, rps, per):
    i = pl.program_id(0)
    j = pl.program_id(1)
    nj = pl.num_programs(1)

    @pl.when(j == 0)
    def _():
        h_scr[...] = _rms(x_ref[...], g_ref[...]).astype(BF16)
        acc_scr[...] = jnp.zeros(acc_scr.shape, F32)

    if per > 1:
        @pl.when((i == 0) & (j == 0))
        def _():
            tail[...] = jnp.zeros(tail.shape, F32)

    seq_start = (i % per) == 0
    stride = rps + F_HIST
    tn = wv_ref.shape[1]
    tc = tn // FFN_CHAINS

    def conv(half, u, cols, cw_ref, cb_ref, hist_ref, new_ref):
        outs = []
        for s in range(nseq):
            base = s * stride
            if per == 1:
                prev = hist_ref[s, :, cols]
            else:
                prev = jnp.where(seq_start, hist_ref[s, :, cols], tail[half, j, :, cols])
            ubuf[half, base:base + F_HIST, cols] = prev
            ubuf[half, base + F_HIST:base + stride, cols] = u[s * rps:(s + 1) * rps]
            c = cb_ref[:, cols] + u[s * rps:(s + 1) * rps] * cw_ref[F_KW - 1:F_KW, cols]
            for k in range(F_KW - 1):
                off = base + F_HIST - (F_KW - 1) + k
                c = c + ubuf[half, off:off + rps, cols] * cw_ref[k:k + 1, cols]
            outs.append(c)
            last = ubuf[half, base + rps:base + stride, cols]
            new_ref[s, :, cols] = last
            if per > 1:
                tail[half, j, :, cols] = last
        return outs[0] if nseq == 1 else jnp.concatenate(outs, axis=0)

    h = h_scr[...]
    chains = [slice(c * tc, (c + 1) * tc) for c in range(FFN_CHAINS)]
    us = [(_dot(h, wv_ref[:, cols]), _dot(h, wg_ref[:, cols])) for cols in chains]
    for cols, (uv, ug) in zip(chains, us):
        val = conv(0, uv, cols, cwv_ref, cbv_ref, hv_ref, nv_ref)
        gate = conv(1, ug, cols, cwg_ref, cbg_ref, hg_ref, ng_ref)
        act = (gate * jax.nn.sigmoid(gate) * val).astype(BF16)
        acc_scr[...] += _dot(act, wd_ref[cols, :])

    @pl.when(j == nj - 1)
    def _():
        o_ref[...] = x_ref[...] + acc_scr[...]


def _ffn(x, g, w_up, cw, cb, hist, w_down, *, t, tm=512, tn=512):
    n = x.shape[0]
    nj = D_FF // tn
    if t >= tm:
        nseq, rps, per = 1, tm, t // tm
        seq_map = lambda i: i // per
    else:
        nseq, rps, per = tm // t, t, 1
        seq_map = lambda i: i
    bsz = hist.shape[0]
    hist_spec = lambda off: pl.BlockSpec((nseq, F_HIST, tn), lambda i, j: (seq_map(i), 0, j + off))
    new_spec = pl.BlockSpec((nseq, F_HIST, tn), lambda i, j: (i, 0, j))
    x_out, tail_v, tail_g = pl.pallas_call(
        functools.partial(_ffn_body, nseq=nseq, rps=rps, per=per),
        grid=(n // tm, nj),
        in_specs=[
            pl.BlockSpec((tm, D_MODEL), lambda i, j: (i, 0)),
            pl.BlockSpec((1, D_MODEL), lambda i, j: (0, 0)),
            pl.BlockSpec((D_MODEL, tn), lambda i, j: (0, j)),
            pl.BlockSpec((D_MODEL, tn), lambda i, j: (0, j + nj)),
            pl.BlockSpec((F_KW, tn), lambda i, j: (0, j)),
            pl.BlockSpec((F_KW, tn), lambda i, j: (0, j + nj)),
            pl.BlockSpec((1, tn), lambda i, j: (0, j)),
            pl.BlockSpec((1, tn), lambda i, j: (0, j + nj)),
            hist_spec(0), hist_spec(nj),
            pl.BlockSpec((tn, D_MODEL), lambda i, j: (j, 0)),
        ],
        out_specs=[pl.BlockSpec((tm, D_MODEL), lambda i, j: (i, 0)), new_spec, new_spec],
        out_shape=[jax.ShapeDtypeStruct((n, D_MODEL), F32),
                   jax.ShapeDtypeStruct((bsz * per, F_HIST, D_FF), F32),
                   jax.ShapeDtypeStruct((bsz * per, F_HIST, D_FF), F32)],
        scratch_shapes=[
            pltpu.VMEM((tm, D_MODEL), BF16),
            pltpu.VMEM((tm, D_MODEL), F32),
            pltpu.VMEM((2, nseq * (rps + F_HIST), tn), F32),
            pltpu.VMEM((2, nj, F_HIST, tn), F32),
        ],
        compiler_params=_params("arbitrary", "arbitrary"),
        name="ffn",
    )(x, g, w_up, w_up, cw, cw, cb, cb, hist, hist, w_down)
    last = lambda a: a.reshape(bsz, per, F_HIST, D_FF)[:, per - 1]
    return x_out, last(tail_v), last(tail_g)


def _rope_tables(pos0, t):
    half = B_DK // 2
    inv = 1.0 / (ROPE_BASE ** (jnp.arange(half, dtype=F32) / half))
    ang = (pos0 + jnp.arange(t, dtype=jnp.int32)).astype(F32)[:, None] * inv[None, :]
    cos = jnp.cos(ang)
    sin = jnp.sin(ang)
    return jnp.concatenate([cos, cos], axis=-1), jnp.concatenate([-sin, sin], axis=-1)


def _pad_rows(h, rows):
    return jnp.pad(h, ((0, 0), (rows - h.shape[1], 0), (0, 0)))


def _row(v):
    return v.reshape(1, -1).astype(F32)


def kernel(x_prompt, x_sample, mem_prompt, state_conv_a, state_ret, cache_diff_k, cache_diff_v, cache_mem_k,
           cache_mem_v, state_conv_f, norm1_g, w_in, conv_a_w, conv_a_b, ln_a_g, ln_a_b, ret_gn_g, diff_qn_g,
           diff_kn_g, diff_lq1, diff_lk1, diff_lq2, diff_lk2, diff_subln_g, w_out, rel_bias, norm2_g, mem_norm_g,
           w_xq, w_xk, w_xv, xqn_g, xkn_g, w_xo, norm3_g, w_up, conv_f_w, conv_f_b, w_down):
    bp, tp, _ = x_prompt.shape
    bs, ts, _ = x_sample.shape
    depth = w_in.shape[0]
    past = cache_diff_k.shape[2]
    cw_dim = C_HEADS * 2 * C_DQK

    tq = ATT_TILE
    r = np.arange(tq)
    rel_diag = r[None, :] - r[:, None]
    idx_p = np.stack([_t5_bucket(rel_diag), _t5_bucket(rel_diag - tq)]).reshape(2 * tq, tq)
    vis_p = np.stack([(r[None, :] // CHUNK) <= (r[:, None] // CHUNK), np.ones((tq, tq), bool)])
    vis_p = vis_p.reshape(2 * tq, tq).astype(np.int32)
    bias_p = _bias_tiles(rel_bias, jnp.asarray(idx_p), jnp.asarray(vis_p)).reshape(C_HEADS, 2, tq, tq)
    rel_s = np.arange(past + ts)[None, :] - (past + np.arange(ts))[:, None]
    bias_s = _bias_tiles(rel_bias, jnp.asarray(_t5_bucket(rel_s)), jnp.ones(rel_s.shape, jnp.int32))
    bias_s_past, bias_s_new = bias_s[:, :, :past], bias_s[:, :, past:]

    cos_p, sin_p = _rope_tables(0, tp)
    cos_s, sin_s = _rope_tables(past, ts)

    xp = x_prompt.reshape(bp * tp, D_MODEL)
    xs = x_sample.reshape(bs * ts, D_MODEL)
    zero_a = jnp.zeros((bp, A_HIST, A_WIDTH), F32)
    zero_r = jnp.zeros((bp, B_HEADS, B_DK, B_DV), F32)
    zero_f = jnp.zeros((bp, F_HIST, 2 * D_FF), F32)

    outs = {k: [] for k in ("p_ca", "p_rs", "p_k", "p_v", "p_mk", "p_mv", "p_cf", "s_ca", "s_rs", "s_k", "s_v", "s_cf")}
    for l in range(depth):
        lam_init = 0.8 - 0.6 * math.exp(-0.3 * l)
        w_in_b = w_in[l].astype(BF16)
        w_out_b = w_out[l].astype(BF16)
        w_xq_b = w_xq[l].astype(BF16)
        w_xo_b = w_xo[l].astype(BF16)
        w_up_b = w_up[l].astype(BF16)
        w_down_b = w_down[l].astype(BF16)
        lp = jnp.stack([diff_lq1[l], diff_lk1[l], diff_lq2[l], diff_lk2[l]]).astype(F32)
        qg2 = _row(jnp.concatenate([diff_qn_g[l], diff_qn_g[l]]))
        kg2 = _row(jnp.concatenate([diff_kn_g[l], diff_kn_g[l]]))
        mix_args = (conv_a_w[l], _row(conv_a_b[l]), _row(ln_a_g[l]), _row(ln_a_b[l]), _row(ret_gn_g[l]), qg2, kg2)
        subln = _row(diff_subln_g[l])

        def block(x, bsz, t, tt, hist_a, state_r, cos2, sin2, attend, mk_b, mv_b, hist_f, tm_post):
            proj = _in_proj(x, _row(norm1_g[l]), w_in_b)
            a, bo, qn, kn, vb, ck, cv, nh, nr = _mixer(proj, hist_a, state_r, cos2, sin2, *mix_args,
                                                        bsz=bsz, t=t, tt=tt)
            co = attend(qn, kn, vb)
            x = _post(x, a, bo, co, w_out_b, _row(norm2_g[l]), w_xq_b, _row(xqn_g[l]), mk_b, mv_b, w_xo_b,
                      t=t, tm=tm_post)
            x, nfv, nfg = _ffn(x, _row(norm3_g[l]), w_up_b, conv_f_w[l], _row(conv_f_b[l]), hist_f, w_down_b, t=t)
            new_f = jnp.concatenate([nfv[:, F_HIST - (F_KW - 1):], nfg[:, F_HIST - (F_KW - 1):]], axis=-1)
            return (x, nh[:, A_HIST - (A_KW - 1):], nr, ck.reshape(bsz, t, C_HEADS, 2 * C_DQK),
                    cv.reshape(bsz, t, C_HEADS, C_DV), new_f)

        mk, mv, mk_b, mv_b = _memkv(mem_prompt, _row(mem_norm_g[l]), w_xk[l].astype(BF16), w_xv[l].astype(BF16),
                                    _row(xkn_g[l]))
        attend_p = lambda qn, kn, vb: _attn_prompt(qn, kn, vb, bias_p, lp, subln, bsz=bp, t=tp, lam_init=lam_init)
        xp, ca, rs, kn_, vn_, cf = block(xp, bp, tp, 256, zero_a, zero_r, cos_p, sin_p, attend_p, mk_b, mv_b,
                                         zero_f, 256)
        outs["p_ca"].append(ca); outs["p_rs"].append(rs); outs["p_k"].append(kn_); outs["p_v"].append(vn_)
        outs["p_mk"].append(mk.reshape(bp, MEM_LEN, M_HEADS, M_DH))
        outs["p_mv"].append(mv.reshape(bp, MEM_LEN, M_HEADS, M_DH))
        outs["p_cf"].append(cf)

        pk = cache_diff_k[l].reshape(bs, past, cw_dim)
        pv = cache_diff_v[l].reshape(bs, past, C_HEADS * C_DV)
        attend_s = lambda qn, kn, vb: _attn_sample(qn, kn, vb, pk, pv, bias_s_past, bias_s_new, lp, subln,
                                                   bsz=bs, t=ts, lam_init=lam_init)
        smk = cache_mem_k[l].reshape(bs, MEM_LEN, M_HEADS * M_DH).astype(BF16)
        smv = cache_mem_v[l].reshape(bs, MEM_LEN, M_HEADS * M_DH).astype(BF16)
        xs, sca, srs, skn, svn, scf = block(xs, bs, ts, ts, _pad_rows(state_conv_a[l], A_HIST), state_ret[l],
                                            cos_s, sin_s, attend_s, smk, smv, _pad_rows(state_conv_f[l], F_HIST), 256)
        outs["s_ca"].append(sca); outs["s_rs"].append(srs); outs["s_k"].append(skn); outs["s_v"].append(svn)
        outs["s_cf"].append(scf)

    st = lambda k: jnp.stack(outs[k])
    return (xp.reshape(bp, tp, D_MODEL), xs.reshape(bs, ts, D_MODEL),
            st("p_ca"), st("p_rs"), st("p_k"), st("p_v"), st("p_mk"), st("p_mv"), st("p_cf"),
            st("s_ca"), st("s_rs"), st("s_k"), st("s_v"), st("s_cf"))
```

```python
import functools
import math

import numpy as np
import jax
import jax.numpy as jnp
from jax import lax
from jax.experimental import pallas as pl
from jax.experimental.pallas import tpu as pltpu

F32 = jnp.float32
BF16 = jnp.bfloat16
EPS = 1e-6
NEG_INF = -1e30

D_MODEL = 2048
CHUNK = 64
A_WIDTH = 512
A_KW = 31
A_HIST = 32
B_HEADS = 4
B_DK = 128
B_DV = 256
ROPE_BASE = 10000.0
C_HEADS = 4
C_DQK = 64
C_DV = 128
REL_BUCKETS = 32
REL_MAX_DIST = 128
M_HEADS = 4
M_DH = 128
MEM_LEN = 256
D_FF = 5632
F_KW = 3
F_HIST = 8

A_COLS = 2 * A_WIDTH
B_COLS = B_HEADS * (2 * B_DK + 2 * B_DV)
C_COLS = C_HEADS * (4 * C_DQK + C_DV)
IN_COLS = A_COLS + B_COLS + C_COLS
B_OFF = A_COLS
C_OFF = A_COLS + B_COLS

VMEM_LIMIT_BYTES = 56 * 1024 * 1024
ATT_TILE = 512
ATT_ROWS = 256
ATT_LANES = 128
FFN_CHAINS = 2


def _params(*sem):
    return pltpu.CompilerParams(dimension_semantics=sem, vmem_limit_bytes=VMEM_LIMIT_BYTES)


def _rms(x, g):
    return x * lax.rsqrt(jnp.mean(x * x, axis=-1, keepdims=True) + EPS) * g


def _dot(a, b):
    return jnp.dot(a, b, preferred_element_type=F32)


def _dot_nt(a, b):
    return lax.dot_general(a, b, (((1,), (1,)), ((), ())), preferred_element_type=F32)


def _dot_tn(a, b):
    return lax.dot_general(a, b, (((0,), (0,)), ((), ())), preferred_element_type=F32)


def _in_proj_body(x_ref, g_ref, w_ref, o_ref, h_ref):
    @pl.when(pl.program_id(1) == 0)
    def _():
        h_ref[...] = _rms(x_ref[...], g_ref[...]).astype(BF16)

    o_ref[...] = _dot(h_ref[...], w_ref[...])


def _in_proj(x, g, w, layer, *, tm=1024, tn=512):
    n = x.shape[0]
    cols = w.shape[2]
    return pl.pallas_call(
        _in_proj_body,
        grid=(n // tm, cols // tn),
        in_specs=[
            pl.BlockSpec((tm, D_MODEL), lambda i, j: (i, 0)),
            pl.BlockSpec((1, D_MODEL), lambda i, j: (0, 0)),
            pl.BlockSpec((None, D_MODEL, tn), lambda i, j: (layer, 0, j)),
        ],
        out_specs=pl.BlockSpec((tm, tn), lambda i, j: (i, j)),
        out_shape=jax.ShapeDtypeStruct((n, cols), F32),
        scratch_shapes=[pltpu.VMEM((tm, D_MODEL), BF16)],
        compiler_params=_params("parallel", "arbitrary"),
        name="in_proj",
    )(x, g, w)


def _mixer_body(proj_ref, hist_ref, state_ref, cos_ref, sin_ref, cw_ref, cb_ref, lng_ref, lnb_ref,
                gng_ref, qg_ref, kg_ref,
                a_ref, bo_ref, qn_ref, kn_ref, vb_ref, ck_ref, cv_ref, nh_ref, nr_ref,
                aext, sret, *, tt, log_gammas):
    t = pl.program_id(1)
    nt = pl.num_programs(1)

    @pl.when(t == 0)
    def _():
        aext[0:A_HIST, :] = hist_ref[0]
        sret[...] = state_ref[0]

    glu = proj_ref[:, 0:A_WIDTH] * jax.nn.sigmoid(proj_ref[:, A_WIDTH:A_COLS])
    aext[A_HIST:A_HIST + tt, :] = glu
    first = A_HIST - (A_KW - 1)
    acc = jnp.zeros((tt, A_WIDTH), F32) + cb_ref[...]
    for k in range(A_KW):
        acc = acc + aext[first + k:first + k + tt, :] * cw_ref[k:k + 1, :]
    mu = jnp.mean(acc, axis=-1, keepdims=True)
    xc = acc - mu
    var = jnp.mean(xc * xc, axis=-1, keepdims=True)
    ln = xc * lax.rsqrt(var + EPS) * lng_ref[...] + lnb_ref[...]
    a_ref[...] = (ln * jax.nn.sigmoid(ln)).astype(BF16)

    @pl.when(t == nt - 1)
    def _():
        nh_ref[0] = aext[tt:tt + A_HIST, :]

    aext[0:A_HIST, :] = aext[tt:tt + A_HIST, :]

    cos = cos_ref[...]
    sin = sin_ref[...]
    ri = lax.broadcasted_iota(jnp.int32, (tt, tt), 0)
    ci = lax.broadcasted_iota(jnp.int32, (tt, tt), 1)
    dij = (ri - ci).astype(F32)
    causal = ri >= ci
    rowf = lax.broadcasted_iota(jnp.int32, (tt, 1), 0).astype(F32)
    for h in range(B_HEADS):
        lg = log_gammas[h]
        q = proj_ref[:, B_OFF + h * B_DK:B_OFF + (h + 1) * B_DK]
        k = proj_ref[:, B_OFF + B_HEADS * B_DK + h * B_DK:B_OFF + B_HEADS * B_DK + (h + 1) * B_DK]
        voff = B_OFF + 2 * B_HEADS * B_DK
        v = proj_ref[:, voff + h * B_DV:voff + (h + 1) * B_DV]
        goff = voff + B_HEADS * B_DV
        g = proj_ref[:, goff + h * B_DV:goff + (h + 1) * B_DV]
        qr = q * cos + pltpu.roll(q, B_DK // 2, 1) * sin
        kr = (k * cos + pltpu.roll(k, B_DK // 2, 1) * sin) * (B_DK ** -0.5)
        qb = qr.astype(BF16)
        vb = v.astype(BF16)
        decay = jnp.where(causal, jnp.exp(lg * jnp.maximum(dij, 0.0)), 0.0)
        scores = _dot_nt(qb, kr.astype(BF16)) * decay
        inner = _dot(scores.astype(BF16), vb)
        s_old = sret[h]
        cross = _dot(qb, s_old.astype(BF16)) * jnp.exp(lg * (rowf + 1.0))
        o = inner + cross
        kd = kr * jnp.exp(lg * (tt - 1.0 - rowf))
        sret[h] = s_old * math.exp(lg * tt) + _dot_tn(kd.astype(BF16), vb)
        y = _rms(o, gng_ref[:, h * B_DV:(h + 1) * B_DV])
        bo_ref[:, h * B_DV:(h + 1) * B_DV] = (y * (g * jax.nn.sigmoid(g))).astype(BF16)

    @pl.when(t == nt - 1)
    def _():
        nr_ref[0] = sret[...]

    lane = lax.broadcasted_iota(jnp.int32, (tt, 2 * C_DQK), 1)
    lo = lane < C_DQK

    def qk_norm(x, g2):
        sq = x * x
        s_lo = jnp.sum(jnp.where(lo, sq, 0.0), axis=-1, keepdims=True)
        s_hi = jnp.sum(jnp.where(lo, 0.0, sq), axis=-1, keepdims=True)
        ms = jnp.where(lo, s_lo, s_hi) * (1.0 / C_DQK)
        return x * lax.rsqrt(ms + EPS) * g2

    w = 2 * C_DQK
    for h in range(C_HEADS):
        cq = proj_ref[:, C_OFF + h * w:C_OFF + (h + 1) * w]
        ck = proj_ref[:, C_OFF + C_HEADS * w + h * w:C_OFF + C_HEADS * w + (h + 1) * w]
        qn = qk_norm(cq, qg_ref[...])
        kn = qk_norm(ck, kg_ref[...])
        qn_ref[:, h * w:(h + 1) * w] = (qn * (C_DQK ** -0.5)).astype(BF16)
        kn_ref[:, h * w:(h + 1) * w] = kn.astype(BF16)
        ck_ref[:, h, :] = kn
    cv = proj_ref[:, C_OFF + 2 * C_HEADS * w:IN_COLS]
    ones = jnp.ones((tt, C_DV), BF16)
    for h in range(C_HEADS):
        cv_ref[:, h, :] = cv[:, h * C_DV:(h + 1) * C_DV]
        vb_ref[:, 2 * h * C_DV:(2 * h + 1) * C_DV] = cv[:, h * C_DV:(h + 1) * C_DV].astype(BF16)
        vb_ref[:, (2 * h + 1) * C_DV:(2 * h + 2) * C_DV] = ones


def _mixer(proj, hist, state, cos2, sin2, cw, cb, lng, lnb, gng, qg2, kg2, *, bsz, t, tt):
    n = bsz * t
    nt = t // tt
    log_gammas = tuple(math.log(1.0 - 2.0 ** (-5.0 - h)) for h in range(B_HEADS))
    tok = lambda cols: pl.BlockSpec((tt, cols), lambda b, i: (b * nt + i, 0))
    const = lambda r, c: pl.BlockSpec((r, c), lambda b, i: (0, 0))
    cw_dim = C_HEADS * 2 * C_DQK
    return pl.pallas_call(
        functools.partial(_mixer_body, tt=tt, log_gammas=log_gammas),
        grid=(bsz, nt),
        in_specs=[
            tok(IN_COLS),
            pl.BlockSpec((1, A_HIST, A_WIDTH), lambda b, i: (b, 0, 0)),
            pl.BlockSpec((1, B_HEADS, B_DK, B_DV), lambda b, i: (b, 0, 0, 0)),
            pl.BlockSpec((tt, B_DK), lambda b, i: (i, 0)),
            pl.BlockSpec((tt, B_DK), lambda b, i: (i, 0)),
            const(A_KW, A_WIDTH), const(1, A_WIDTH), const(1, A_WIDTH), const(1, A_WIDTH),
            const(1, B_HEADS * B_DV), const(1, 2 * C_DQK), const(1, 2 * C_DQK),
        ],
        out_specs=[
            tok(A_WIDTH), tok(B_HEADS * B_DV), tok(cw_dim), tok(cw_dim), tok(2 * C_HEADS * C_DV),
            pl.BlockSpec((None, tt, C_HEADS, 2 * C_DQK), lambda b, i: (b, i, 0, 0)),
            pl.BlockSpec((None, tt, C_HEADS, C_DV), lambda b, i: (b, i, 0, 0)),
            pl.BlockSpec((1, A_HIST, A_WIDTH), lambda b, i: (b, 0, 0)),
            pl.BlockSpec((1, B_HEADS, B_DK, B_DV), lambda b, i: (b, 0, 0, 0)),
        ],
        out_shape=[
            jax.ShapeDtypeStruct((n, A_WIDTH), BF16),
            jax.ShapeDtypeStruct((n, B_HEADS * B_DV), BF16),
            jax.ShapeDtypeStruct((n, cw_dim), BF16),
            jax.ShapeDtypeStruct((n, cw_dim), BF16),
            jax.ShapeDtypeStruct((n, 2 * C_HEADS * C_DV), BF16),
            jax.ShapeDtypeStruct((bsz, t, C_HEADS, 2 * C_DQK), F32),
            jax.ShapeDtypeStruct((bsz, t, C_HEADS, C_DV), F32),
            jax.ShapeDtypeStruct((bsz, A_HIST, A_WIDTH), F32),
            jax.ShapeDtypeStruct((bsz, B_HEADS, B_DK, B_DV), F32),
        ],
        scratch_shapes=[pltpu.VMEM((A_HIST + tt, A_WIDTH), F32), pltpu.VMEM((B_HEADS, B_DK, B_DV), F32)],
        compiler_params=_params("parallel", "arbitrary"),
        name="mixer",
    )(proj, hist, state, cos2, sin2, cw, cb, lng, lnb, gng, qg2, kg2)


def _t5_bucket(rel):
    half = REL_BUCKETS // 2
    exact = half // 2
    n = np.abs(rel)
    large = exact + (np.log(np.maximum(n, 1).astype(np.float32) / exact) / math.log(REL_MAX_DIST / exact)
                     * (half - exact)).astype(np.int32)
    large = np.minimum(large, half - 1)
    return (np.where(rel > 0, half, 0) + np.where(n < exact, n, large)).astype(np.int32)


def _bias_body(rb_ref, idx_ref, vis_ref, o_ref):
    h = pl.program_id(0)
    idx = idx_ref[...]
    far = rb_ref[REL_BUCKETS // 2 - 1, h]
    acc = jnp.zeros(idx.shape, F32)
    for b in range(REL_BUCKETS):
        acc = jnp.where(idx == b, rb_ref[b, h], acc)
    o_ref[0] = jnp.where(vis_ref[...] != 0, acc - far, NEG_INF)


def _bias_tiles(rel_bias, idx, vis):
    r, c = idx.shape
    return pl.pallas_call(
        _bias_body,
        grid=(C_HEADS,),
        in_specs=[
            pl.BlockSpec(memory_space=pltpu.SMEM),
            pl.BlockSpec((r, c), lambda h: (0, 0)),
            pl.BlockSpec((r, c), lambda h: (0, 0)),
        ],
        out_specs=pl.BlockSpec((1, r, c), lambda h: (h, 0, 0)),
        out_shape=jax.ShapeDtypeStruct((C_HEADS, r, c), F32),
        compiler_params=_params("arbitrary"),
        name="bias_tiles",
    )(rel_bias, idx, vis)


def _lambda(lp_ref, lam_init):
    lp = lp_ref[...]
    e1 = jnp.exp(jnp.sum(lp[0:1] * lp[1:2], axis=-1, keepdims=True))
    e2 = jnp.exp(jnp.sum(lp[2:3] * lp[3:4], axis=-1, keepdims=True))
    return e1 - e2 + lam_init


def _stack_maps(q):
    lane = lax.broadcasted_iota(jnp.int32, q.shape, 1)
    zero = jnp.zeros_like(q)
    return jnp.concatenate([jnp.where(lane < C_DQK, q, zero), jnp.where(lane < C_DQK, zero, q)], axis=0)


def _attn_finish(acc, l, lam, g, lam_init, tq):
    o = acc / l
    o = o[0:tq] - lam * o[tq:2 * tq]
    return _rms(o, g) * (1.0 - lam_init)


def _attn_prompt_body(q_ref, k_ref, v_ref, bias_ref, lp_ref, g_ref, o_ref, qq_scr, m_scr, acc_scr, *, tq, lam_init):
    qi = pl.program_id(2)
    qq_scr[...] = _stack_maps(q_ref[...])
    m_scr[...] = jnp.full(m_scr.shape, NEG_INF, F32)
    acc_scr[...] = jnp.zeros(acc_scr.shape, F32)
    nl = tq // ATT_LANES

    def step(ki, bias_idx):
        start = pl.multiple_of(ki * tq, tq)
        kb = k_ref[pl.ds(start, tq), :]
        vb = v_ref[pl.ds(start, tq), :]
        for c in range(2 * tq // ATT_ROWS):
            rows = slice(c * ATT_ROWS, (c + 1) * ATT_ROWS)
            s = _dot_nt(qq_scr[rows, :], kb)
            if bias_idx is not None:
                s = s + bias_ref[0, bias_idx, pl.ds((c * ATT_ROWS) % tq, ATT_ROWS), :]
            slabs = [s[:, j * ATT_LANES:(j + 1) * ATT_LANES] for j in range(nl)]
            m_old = m_scr[rows, :]
            m_new = jnp.maximum(m_old, jnp.max(functools.reduce(jnp.maximum, slabs), axis=-1, keepdims=True))
            alpha = jnp.exp(m_old - m_new)
            p = jnp.concatenate([jnp.exp(sl - m_new) for sl in slabs], axis=-1).astype(BF16)
            acc_scr[rows, :] = jnp.concatenate([alpha, alpha], axis=-1) * acc_scr[rows, :] + _dot(p, vb)
            m_scr[rows, :] = m_new

    n_far = jnp.maximum(qi - 1, 0)

    def far_body(ki, c):
        step(ki, None)
        return c

    lax.fori_loop(0, n_far, far_body, 0)

    def near_body(ki, c):
        step(ki, qi - ki)
        return c

    lax.fori_loop(n_far, qi + 1, near_body, 0)

    lam = _lambda(lp_ref, lam_init)
    acc = acc_scr[...]
    o_ref[...] = _attn_finish(acc[:, 0:C_DV], acc[:, C_DV:2 * C_DV], lam, g_ref[...], lam_init, tq).astype(BF16)


def _attn_prompt(qn, kn, vb, bias, lp, g, *, bsz, t, lam_init):
    tq = ATT_TILE
    nq = t // tq
    n = bsz * t
    w = 2 * C_DQK
    return pl.pallas_call(
        functools.partial(_attn_prompt_body, tq=tq, lam_init=lam_init),
        grid=(bsz, C_HEADS, nq),
        in_specs=[
            pl.BlockSpec((tq, w), lambda b, h, i: (b * nq + i, h)),
            pl.BlockSpec((t, w), lambda b, h, i: (b, h)),
            pl.BlockSpec((t, 2 * C_DV), lambda b, h, i: (b, h)),
            pl.BlockSpec((1, 2, tq, tq), lambda b, h, i: (h, 0, 0, 0)),
            pl.BlockSpec((4, C_DQK), lambda b, h, i: (0, 0)),
            pl.BlockSpec((1, C_DV), lambda b, h, i: (0, 0)),
        ],
        out_specs=pl.BlockSpec((tq, C_DV), lambda b, h, i: (b * nq + i, h)),
        out_shape=jax.ShapeDtypeStruct((n, C_HEADS * C_DV), BF16),
        scratch_shapes=[pltpu.VMEM((2 * tq, w), BF16), pltpu.VMEM((2 * tq, ATT_LANES), F32),
                        pltpu.VMEM((2 * tq, 2 * C_DV), F32)],
        compiler_params=_params("parallel", "parallel", "arbitrary"),
        name="attn_prompt",
    )(qn, kn, vb, bias, lp, g)


def _attn_sample_body(q_ref, k_ref, v_ref, pk_ref, pv_ref, bp_ref, bn_ref, lp_ref, g_ref, o_ref, *, tq, lam_init):
    lam = _lambda(lp_ref, lam_init)
    w = 2 * C_DQK
    for h in range(C_HEADS):
        qq = _stack_maps(q_ref[:, h * w:(h + 1) * w])
        bp = bp_ref[h]
        bn = bn_ref[h]
        s_p = _dot_nt(qq, pk_ref[:, h, :].astype(BF16)) + jnp.concatenate([bp, bp], axis=0)
        s_n = _dot_nt(qq, k_ref[:, h * w:(h + 1) * w]) + jnp.concatenate([bn, bn], axis=0)
        m = jnp.maximum(jnp.max(s_p, axis=-1, keepdims=True), jnp.max(s_n, axis=-1, keepdims=True))
        p_p = jnp.exp(s_p - m)
        p_n = jnp.exp(s_n - m)
        l = jnp.sum(p_p, axis=-1, keepdims=True) + jnp.sum(p_n, axis=-1, keepdims=True)
        acc = (_dot(p_p.astype(BF16), pv_ref[:, h, :].astype(BF16))
               + _dot(p_n.astype(BF16), v_ref[:, 2 * h * C_DV:(2 * h + 1) * C_DV]))
        o_ref[:, h * C_DV:(h + 1) * C_DV] = _attn_finish(acc, l, lam, g_ref[...], lam_init, tq).astype(BF16)


def _attn_sample(qn, kn, vb, past_k, past_v, bias_p, bias_n, lp, g, layer, *, bsz, t, lam_init):
    n = bsz * t
    w = 2 * C_DQK
    past = past_k.shape[2]
    full = lambda a: pl.BlockSpec(a.shape, lambda b: (0,) * a.ndim)
    return pl.pallas_call(
        functools.partial(_attn_sample_body, tq=t, lam_init=lam_init),
        grid=(bsz,),
        in_specs=[
            pl.BlockSpec((t, C_HEADS * w), lambda b: (b, 0)),
            pl.BlockSpec((t, C_HEADS * w), lambda b: (b, 0)),
            pl.BlockSpec((t, 2 * C_HEADS * C_DV), lambda b: (b, 0)),
            pl.BlockSpec((None, None, past, C_HEADS, w), lambda b: (layer, b, 0, 0, 0)),
            pl.BlockSpec((None, None, past, C_HEADS, C_DV), lambda b: (layer, b, 0, 0, 0)),
            full(bias_p), full(bias_n),
            pl.BlockSpec((4, C_DQK), lambda b: (0, 0)),
            pl.BlockSpec((1, C_DV), lambda b: (0, 0)),
        ],
        out_specs=pl.BlockSpec((t, C_HEADS * C_DV), lambda b: (b, 0)),
        out_shape=jax.ShapeDtypeStruct((n, C_HEADS * C_DV), BF16),
        compiler_params=_params("parallel"),
        name="attn_sample",
    )(qn, kn, vb, past_k, past_v, bias_p, bias_n, lp, g)


def _memkv_body(mem_ref, g_ref, wk_ref, wv_ref, kg_ref, k_ref, v_ref, kb_ref, vb_ref):
    hm = _rms(mem_ref[0], g_ref[...]).astype(BF16)
    k = _dot(hm, wk_ref[...])
    v = _dot(hm, wv_ref[...])
    for h in range(M_HEADS):
        kn = _rms(k[:, h * M_DH:(h + 1) * M_DH], kg_ref[...])
        k_ref[0, :, h * M_DH:(h + 1) * M_DH] = kn
        kb_ref[0, :, h * M_DH:(h + 1) * M_DH] = kn.astype(BF16)
    v_ref[0] = v
    vb_ref[0] = v.astype(BF16)


def _memkv(mem, g, wk, wv, kg):
    bsz, m, _ = mem.shape
    w = M_HEADS * M_DH
    blk = pl.BlockSpec((1, m, w), lambda b: (b, 0, 0))
    return pl.pallas_call(
        _memkv_body,
        grid=(bsz,),
        in_specs=[
            pl.BlockSpec((1, m, D_MODEL), lambda b: (b, 0, 0)),
            pl.BlockSpec((1, D_MODEL), lambda b: (0, 0)),
            pl.BlockSpec((D_MODEL, w), lambda b: (0, 0)),
            pl.BlockSpec((D_MODEL, w), lambda b: (0, 0)),
            pl.BlockSpec((1, M_DH), lambda b: (0, 0)),
        ],
        out_specs=[blk, blk, blk, blk],
        out_shape=[jax.ShapeDtypeStruct((bsz, m, w), F32), jax.ShapeDtypeStruct((bsz, m, w), F32),
                   jax.ShapeDtypeStruct((bsz, m, w), BF16), jax.ShapeDtypeStruct((bsz, m, w), BF16)],
        compiler_params=_params("parallel"),
        name="memkv",
    )(mem, g, wk, wv, kg)


def _post_body(x_ref, a_ref, bo_ref, co_ref, wo_ref, g_ref, wq_ref, qg_ref, mk_ref, mv_ref, wxo_ref,
               o_ref, att_scr, *, nseq, rps):
    y = (_dot(a_ref[...], wo_ref[0:A_WIDTH, :])
         + _dot(bo_ref[...], wo_ref[A_WIDTH:A_WIDTH + B_HEADS * B_DV, :])
         + _dot(co_ref[...], wo_ref[A_WIDTH + B_HEADS * B_DV:, :]))
    x1 = x_ref[...] + y
    q = _dot(_rms(x1, g_ref[...]).astype(BF16), wq_ref[...])
    for h in range(M_HEADS):
        sl = slice(h * M_DH, (h + 1) * M_DH)
        qn = _rms(q[:, sl], qg_ref[...]).astype(BF16)
        for s in range(nseq):
            rows = slice(s * rps, (s + 1) * rps)
            logits = _dot_nt(qn[rows], mk_ref[s, :, sl]) * (M_DH ** -0.5)
            m = jnp.max(logits, axis=-1, keepdims=True)
            p = jnp.exp(logits - m)
            l = jnp.sum(p, axis=-1, keepdims=True)
            o = _dot(p.astype(BF16), mv_ref[s, :, sl]) / l
            att_scr[rows, sl] = o.astype(BF16)
    o_ref[...] = x1 + _dot(att_scr[...], wxo_ref[...])


def _post(x, a, bo, co, wo, g, wq, qg, mk, mv, wxo, layer, *, t, tm):
    n = x.shape[0]
    w = M_HEADS * M_DH
    if t >= tm:
        nseq, rps = 1, tm
        per = t // tm
        mem_map = lambda i: (i // per, 0, 0)
    else:
        nseq, rps = tm // t, t
        mem_map = lambda i: (i, 0, 0)
    tok = lambda cols: pl.BlockSpec((tm, cols), lambda i: (i, 0))
    const = lambda r, c: pl.BlockSpec((r, c), lambda i: (0, 0), pipeline_mode=pl.Buffered(1))
    stacked = lambda r, c: pl.BlockSpec((None, r, c), lambda i: (layer, 0, 0), pipeline_mode=pl.Buffered(1))
    return pl.pallas_call(
        functools.partial(_post_body, nseq=nseq, rps=rps),
        grid=(n // tm,),
        in_specs=[
            tok(D_MODEL), tok(A_WIDTH), tok(B_HEADS * B_DV), tok(C_HEADS * C_DV),
            stacked(D_MODEL, D_MODEL), const(1, D_MODEL), stacked(D_MODEL, w), const(1, M_DH),
            pl.BlockSpec((nseq, MEM_LEN, w), mem_map),
            pl.BlockSpec((nseq, MEM_LEN, w), mem_map),
            stacked(w, D_MODEL),
        ],
        out_specs=tok(D_MODEL),
        out_shape=jax.ShapeDtypeStruct((n, D_MODEL), F32),
        scratch_shapes=[pltpu.VMEM((tm, w), BF16)],
        compiler_params=_params("parallel"),
        name="post",
    )(x, a, bo, co, wo, g, wq, qg, mk, mv, wxo)


def _ffn_body(x_ref, g_ref, wv_ref, wg_ref, cwv_ref, cwg_ref, cbv_ref, cbg_ref, hv_ref, hg_ref, wd_ref,
              o_ref, nv_ref, ng_ref, h_scr, acc_scr, ubuf, tail, *, nseq, rps, per):
    i = pl.program_id(0)
    j = pl.program_id(1)
    nj = pl.num_programs(1)

    @pl.when(j == 0)
    def _():
        h_scr[...] = _rms(x_ref[...], g_ref[...]).astype(BF16)
        acc_scr[...] = jnp.zeros(acc_scr.shape, F32)

    if per > 1:
        @pl.when((i == 0) & (j == 0))
        def _():
            tail[...] = jnp.zeros(tail.shape, F32)

    seq_start = (i % per) == 0
    stride = rps + F_HIST
    tn = wv_ref.shape[1]
    tc = tn // FFN_CHAINS

    def conv(half, u, cols, cw_ref, cb_ref, hist_ref, new_ref):
        outs = []
        for s in range(nseq):
            base = s * stride
            if per == 1:
                prev = hist_ref[s, :, cols]
            else:
                prev = jnp.where(seq_start, hist_ref[s, :, cols], tail[half, j, :, cols])
            ubuf[half, base:base + F_HIST, cols] = prev
            ubuf[half, base + F_HIST:base + stride, cols] = u[s * rps:(s + 1) * rps]
            c = cb_ref[:, cols] + u[s * rps:(s + 1) * rps] * cw_ref[F_KW - 1:F_KW, cols]
            for k in range(F_KW - 1):
                off = base + F_HIST - (F_KW - 1) + k
                c = c + ubuf[half, off:off + rps, cols] * cw_ref[k:k + 1, cols]
            outs.append(c)
            last = ubuf[half, base + rps:base + stride, cols]
            new_ref[s, :, cols] = last
            if per > 1:
                tail[half, j, :, cols] = last
        return outs[0] if nseq == 1 else jnp.concatenate(outs, axis=0)

    h = h_scr[...]
    chains = [slice(c * tc, (c + 1) * tc) for c in range(FFN_CHAINS)]
    us = [(_dot(h, wv_ref[:, cols]), _dot(h, wg_ref[:, cols])) for cols in chains]
    for cols, (uv, ug) in zip(chains, us):
        val = conv(0, uv, cols, cwv_ref, cbv_ref, hv_ref, nv_ref)
        gate = conv(1, ug, cols, cwg_ref, cbg_ref, hg_ref, ng_ref)
        act = (gate * jax.nn.sigmoid(gate) * val).astype(BF16)
        acc_scr[...] += _dot(act, wd_ref[cols, :])

    @pl.when(j == nj - 1)
    def _():
        o_ref[...] = x_ref[...] + acc_scr[...]


def _ffn(x, g, w_up, cw, cb, hist, w_down, layer, *, t, tm=512, tn=512):
    n = x.shape[0]
    nj = D_FF // tn
    if t >= tm:
        nseq, rps, per = 1, tm, t // tm
        seq_map = lambda i: i // per
    else:
        nseq, rps, per = tm // t, t, 1
        seq_map = lambda i: i
    bsz = hist.shape[0]
    hist_spec = lambda off: pl.BlockSpec((nseq, F_HIST, tn), lambda i, j: (seq_map(i), 0, j + off))
    new_spec = pl.BlockSpec((nseq, F_HIST, tn), lambda i, j: (i, 0, j))
    x_out, tail_v, tail_g = pl.pallas_call(
        functools.partial(_ffn_body, nseq=nseq, rps=rps, per=per),
        grid=(n // tm, nj),
        in_specs=[
            pl.BlockSpec((tm, D_MODEL), lambda i, j: (i, 0)),
            pl.BlockSpec((1, D_MODEL), lambda i, j: (0, 0)),
            pl.BlockSpec((None, D_MODEL, tn), lambda i, j: (layer, 0, j)),
            pl.BlockSpec((None, D_MODEL, tn), lambda i, j: (layer, 0, j + nj)),
            pl.BlockSpec((F_KW, tn), lambda i, j: (0, j)),
            pl.BlockSpec((F_KW, tn), lambda i, j: (0, j + nj)),
            pl.BlockSpec((1, tn), lambda i, j: (0, j)),
            pl.BlockSpec((1, tn), lambda i, j: (0, j + nj)),
            hist_spec(0), hist_spec(nj),
            pl.BlockSpec((None, tn, D_MODEL), lambda i, j: (layer, j, 0)),
        ],
        out_specs=[pl.BlockSpec((tm, D_MODEL), lambda i, j: (i, 0)), new_spec, new_spec],
        out_shape=[jax.ShapeDtypeStruct((n, D_MODEL), F32),
                   jax.ShapeDtypeStruct((bsz * per, F_HIST, D_FF), F32),
                   jax.ShapeDtypeStruct((bsz * per, F_HIST, D_FF), F32)],
        scratch_shapes=[
            pltpu.VMEM((tm, D_MODEL), BF16),
            pltpu.VMEM((tm, D_MODEL), F32),
            pltpu.VMEM((2, nseq * (rps + F_HIST), tn), F32),
            pltpu.VMEM((2, nj, F_HIST, tn), F32),
        ],
        compiler_params=_params("arbitrary", "arbitrary"),
        name="ffn",
    )(x, g, w_up, w_up, cw, cw, cb, cb, hist, hist, w_down)
    last = lambda a: a.reshape(bsz, per, F_HIST, D_FF)[:, per - 1]
    return x_out, last(tail_v), last(tail_g)


def _rope_tables(pos0, t):
    half = B_DK // 2
    inv = 1.0 / (ROPE_BASE ** (jnp.arange(half, dtype=F32) / half))
    ang = (pos0 + jnp.arange(t, dtype=jnp.int32)).astype(F32)[:, None] * inv[None, :]
    cos = jnp.cos(ang)
    sin = jnp.sin(ang)
    return jnp.concatenate([cos, cos], axis=-1), jnp.concatenate([-sin, sin], axis=-1)


def _pad_rows(h, rows):
    return jnp.pad(h, ((0, 0), (rows - h.shape[1], 0), (0, 0)))


def _row(v):
    return v.reshape(1, -1).astype(F32)


def kernel(x_prompt, x_sample, mem_prompt, state_conv_a, state_ret, cache_diff_k, cache_diff_v, cache_mem_k,
           cache_mem_v, state_conv_f, norm1_g, w_in, conv_a_w, conv_a_b, ln_a_g, ln_a_b, ret_gn_g, diff_qn_g,
           diff_kn_g, diff_lq1, diff_lk1, diff_lq2, diff_lk2, diff_subln_g, w_out, rel_bias, norm2_g, mem_norm_g,
           w_xq, w_xk, w_xv, xqn_g, xkn_g, w_xo, norm3_g, w_up, conv_f_w, conv_f_b, w_down):
    bp, tp, _ = x_prompt.shape
    bs, ts, _ = x_sample.shape
    depth = w_in.shape[0]
    past = cache_diff_k.shape[2]
    cw_dim = C_HEADS * 2 * C_DQK

    tq = ATT_TILE
    r = np.arange(tq)
    rel_diag = r[None, :] - r[:, None]
    idx_p = np.stack([_t5_bucket(rel_diag), _t5_bucket(rel_diag - tq)]).reshape(2 * tq, tq)
    vis_p = np.stack([(r[None, :] // CHUNK) <= (r[:, None] // CHUNK), np.ones((tq, tq), bool)])
    vis_p = vis_p.reshape(2 * tq, tq).astype(np.int32)
    bias_p = _bias_tiles(rel_bias, jnp.asarray(idx_p), jnp.asarray(vis_p)).reshape(C_HEADS, 2, tq, tq)
    rel_s = np.arange(past + ts)[None, :] - (past + np.arange(ts))[:, None]
    bias_s = _bias_tiles(rel_bias, jnp.asarray(_t5_bucket(rel_s)), jnp.ones(rel_s.shape, jnp.int32))
    bias_s_past, bias_s_new = bias_s[:, :, :past], bias_s[:, :, past:]

    cos_p, sin_p = _rope_tables(0, tp)
    cos_s, sin_s = _rope_tables(past, ts)

    xp = x_prompt.reshape(bp * tp, D_MODEL)
    xs = x_sample.reshape(bs * ts, D_MODEL)
    zero_a = jnp.zeros((bp, A_HIST, A_WIDTH), F32)
    zero_r = jnp.zeros((bp, B_HEADS, B_DK, B_DV), F32)
    zero_f = jnp.zeros((bp, F_HIST, 2 * D_FF), F32)

    outs = {k: [] for k in ("p_ca", "p_rs", "p_k", "p_v", "p_mk", "p_mv", "p_cf", "s_ca", "s_rs", "s_k", "s_v", "s_cf")}
    w_in_b = w_in.astype(BF16)
    w_out_b = w_out.astype(BF16)
    w_xq_b = w_xq.astype(BF16)
    w_xo_b = w_xo.astype(BF16)
    w_up_b = w_up.astype(BF16)
    w_down_b = w_down.astype(BF16)
    for l in range(depth):
        lam_init = 0.8 - 0.6 * math.exp(-0.3 * l)
        lp = jnp.stack([diff_lq1[l], diff_lk1[l], diff_lq2[l], diff_lk2[l]]).astype(F32)
        qg2 = _row(jnp.concatenate([diff_qn_g[l], diff_qn_g[l]]))
        kg2 = _row(jnp.concatenate([diff_kn_g[l], diff_kn_g[l]]))
        mix_args = (conv_a_w[l], _row(conv_a_b[l]), _row(ln_a_g[l]), _row(ln_a_b[l]), _row(ret_gn_g[l]), qg2, kg2)
        subln = _row(diff_subln_g[l])

        def block(x, bsz, t, tt, hist_a, state_r, cos2, sin2, attend, mk_b, mv_b, hist_f, tm_post):
            proj = _in_proj(x, _row(norm1_g[l]), w_in_b, l)
            a, bo, qn, kn, vb, ck, cv, nh, nr = _mixer(proj, hist_a, state_r, cos2, sin2, *mix_args,
                                                        bsz=bsz, t=t, tt=tt)
            co = attend(qn, kn, vb)
            x = _post(x, a, bo, co, w_out_b, _row(norm2_g[l]), w_xq_b, _row(xqn_g[l]), mk_b, mv_b, w_xo_b,
                      l, t=t, tm=tm_post)
            x, nfv, nfg = _ffn(x, _row(norm3_g[l]), w_up_b, conv_f_w[l], _row(conv_f_b[l]), hist_f, w_down_b, l,
                               t=t)
            new_f = jnp.concatenate([nfv[:, F_HIST - (F_KW - 1):], nfg[:, F_HIST - (F_KW - 1):]], axis=-1)
            return x, nh[:, A_HIST - (A_KW - 1):], nr, ck, cv, new_f

        mk, mv, mk_b, mv_b = _memkv(mem_prompt, _row(mem_norm_g[l]), w_xk[l].astype(BF16), w_xv[l].astype(BF16),
                                    _row(xkn_g[l]))
        attend_p = lambda qn, kn, vb: _attn_prompt(qn, kn, vb, bias_p, lp, subln, bsz=bp, t=tp, lam_init=lam_init)
        xp, ca, rs, kn_, vn_, cf = block(xp, bp, tp, 256, zero_a, zero_r, cos_p, sin_p, attend_p, mk_b, mv_b,
                                         zero_f, 256)
        outs["p_ca"].append(ca); outs["p_rs"].append(rs); outs["p_k"].append(kn_); outs["p_v"].append(vn_)
        outs["p_mk"].append(mk.reshape(bp, MEM_LEN, M_HEADS, M_DH))
        outs["p_mv"].append(mv.reshape(bp, MEM_LEN, M_HEADS, M_DH))
        outs["p_cf"].append(cf)

        attend_s = lambda qn, kn, vb: _attn_sample(qn, kn, vb, cache_diff_k, cache_diff_v, bias_s_past,
                                                   bias_s_new, lp, subln, l,
                                                   bsz=bs, t=ts, lam_init=lam_init)
        smk = cache_mem_k[l].reshape(bs, MEM_LEN, M_HEADS * M_DH).astype(BF16)
        smv = cache_mem_v[l].reshape(bs, MEM_LEN, M_HEADS * M_DH).astype(BF16)
        xs, sca, srs, skn, svn, scf = block(xs, bs, ts, ts, _pad_rows(state_conv_a[l], A_HIST), state_ret[l],
                                            cos_s, sin_s, attend_s, smk, smv, _pad_rows(state_conv_f[l], F_HIST), 256)
        outs["s_ca"].append(sca); outs["s_rs"].append(srs); outs["s_k"].append(skn); outs["s_v"].append(svn)
        outs["s_cf"].append(scf)

    st = lambda k: jnp.stack(outs[k])
    return (xp.reshape(bp, tp, D_MODEL), xs.reshape(bs, ts, D_MODEL),
            st("p_ca"), st("p_rs"), st("p_k"), st("p_v"), st("p_mk"), st("p_mv"), st("p_cf"),
            st("s_ca"), st("s_rs"), st("s_k"), st("s_v"), st("s_cf"))
```

```python
import functools
import math

import numpy as np
import jax
import jax.numpy as jnp
from jax import lax
from jax.experimental import pallas as pl
from jax.experimental.pallas import tpu as pltpu

F32 = jnp.float32
BF16 = jnp.bfloat16
EPS = 1e-6
NEG_INF = -1e30

D_MODEL = 2048
CHUNK = 64
A_WIDTH = 512
A_KW = 31
A_HIST = 32
B_HEADS = 4
B_DK = 128
B_DV = 256
ROPE_BASE = 10000.0
C_HEADS = 4
C_DQK = 64
C_DV = 128
REL_BUCKETS = 32
REL_MAX_DIST = 128
M_HEADS = 4
M_DH = 128
MEM_LEN = 256
D_FF = 5632
F_KW = 3
F_HIST = 8

A_COLS = 2 * A_WIDTH
B_COLS = B_HEADS * (2 * B_DK + 2 * B_DV)
C_COLS = C_HEADS * (4 * C_DQK + C_DV)
IN_COLS = A_COLS + B_COLS + C_COLS
B_OFF = A_COLS
C_OFF = A_COLS + B_COLS

VMEM_LIMIT_BYTES = 56 * 1024 * 1024
ATT_TILE = 512
ATT_ROWS = 256
ATT_LANES = 128
FFN_CHAINS = 2


def _params(*sem):
    return pltpu.CompilerParams(dimension_semantics=sem, vmem_limit_bytes=VMEM_LIMIT_BYTES)


def _rms(x, g):
    return x * lax.rsqrt(jnp.mean(x * x, axis=-1, keepdims=True) + EPS) * g


def _dot(a, b):
    return jnp.dot(a, b, preferred_element_type=F32)


def _dot_nt(a, b):
    return lax.dot_general(a, b, (((1,), (1,)), ((), ())), preferred_element_type=F32)


def _dot_tn(a, b):
    return lax.dot_general(a, b, (((0,), (0,)), ((), ())), preferred_element_type=F32)


def _in_proj_body(x_ref, g_ref, w_ref, o_ref, h_ref):
    @pl.when(pl.program_id(1) == 0)
    def _():
        h_ref[...] = _rms(x_ref[...], g_ref[...]).astype(BF16)

    o_ref[...] = _dot(h_ref[...], w_ref[...])


def _in_proj(x, g, w, layer, *, tm=1024, tn=512):
    n = x.shape[0]
    cols = w.shape[2]
    return pl.pallas_call(
        _in_proj_body,
        grid=(n // tm, cols // tn),
        in_specs=[
            pl.BlockSpec((tm, D_MODEL), lambda i, j: (i, 0)),
            pl.BlockSpec((1, D_MODEL), lambda i, j: (0, 0)),
            pl.BlockSpec((None, D_MODEL, tn), lambda i, j: (layer, 0, j)),
        ],
        out_specs=pl.BlockSpec((tm, tn), lambda i, j: (i, j)),
        out_shape=jax.ShapeDtypeStruct((n, cols), F32),
        scratch_shapes=[pltpu.VMEM((tm, D_MODEL), BF16)],
        compiler_params=_params("parallel", "arbitrary"),
        name="in_proj",
    )(x, g, w)


def _mixer_body(proj_ref, hist_ref, state_ref, cos_ref, sin_ref, cw_ref, cb_ref, lng_ref, lnb_ref,
                gng_ref, qg_ref, kg_ref,
                a_ref, bo_ref, qn_ref, kn_ref, vb_ref, ck_ref, cv_ref, nh_ref, nr_ref,
                aext, sret, *, tt, log_gammas):
    t = pl.program_id(1)
    nt = pl.num_programs(1)

    @pl.when(t == 0)
    def _():
        aext[0:A_HIST, :] = hist_ref[0]
        sret[...] = state_ref[0]

    glu = proj_ref[:, 0:A_WIDTH] * jax.nn.sigmoid(proj_ref[:, A_WIDTH:A_COLS])
    aext[A_HIST:A_HIST + tt, :] = glu
    first = A_HIST - (A_KW - 1)
    acc = jnp.zeros((tt, A_WIDTH), F32) + cb_ref[...]
    for k in range(A_KW):
        acc = acc + aext[first + k:first + k + tt, :] * cw_ref[k:k + 1, :]
    mu = jnp.mean(acc, axis=-1, keepdims=True)
    xc = acc - mu
    var = jnp.mean(xc * xc, axis=-1, keepdims=True)
    ln = xc * lax.rsqrt(var + EPS) * lng_ref[...] + lnb_ref[...]
    a_ref[...] = (ln * jax.nn.sigmoid(ln)).astype(BF16)

    @pl.when(t == nt - 1)
    def _():
        nh_ref[0] = aext[tt:tt + A_HIST, :]

    aext[0:A_HIST, :] = aext[tt:tt + A_HIST, :]

    cos = cos_ref[...]
    sin = sin_ref[...]
    ri = lax.broadcasted_iota(jnp.int32, (tt, tt), 0)
    ci = lax.broadcasted_iota(jnp.int32, (tt, tt), 1)
    dij = (ri - ci).astype(F32)
    causal = ri >= ci
    rowf = lax.broadcasted_iota(jnp.int32, (tt, 1), 0).astype(F32)
    for h in range(B_HEADS):
        lg = log_gammas[h]
        q = proj_ref[:, B_OFF + h * B_DK:B_OFF + (h + 1) * B_DK]
        k = proj_ref[:, B_OFF + B_HEADS * B_DK + h * B_DK:B_OFF + B_HEADS * B_DK + (h + 1) * B_DK]
        voff = B_OFF + 2 * B_HEADS * B_DK
        v = proj_ref[:, voff + h * B_DV:voff + (h + 1) * B_DV]
        goff = voff + B_HEADS * B_DV
        g = proj_ref[:, goff + h * B_DV:goff + (h + 1) * B_DV]
        qr = q * cos + pltpu.roll(q, B_DK // 2, 1) * sin
        kr = (k * cos + pltpu.roll(k, B_DK // 2, 1) * sin) * (B_DK ** -0.5)
        qb = qr.astype(BF16)
        vb = v.astype(BF16)
        decay = jnp.where(causal, jnp.exp(lg * jnp.maximum(dij, 0.0)), 0.0)
        scores = _dot_nt(qb, kr.astype(BF16)) * decay
        inner = _dot(scores.astype(BF16), vb)
        s_old = sret[h]
        cross = _dot(qb, s_old.astype(BF16)) * jnp.exp(lg * (rowf + 1.0))
        o = inner + cross
        kd = kr * jnp.exp(lg * (tt - 1.0 - rowf))
        sret[h] = s_old * math.exp(lg * tt) + _dot_tn(kd.astype(BF16), vb)
        y = _rms(o, gng_ref[:, h * B_DV:(h + 1) * B_DV])
        bo_ref[:, h * B_DV:(h + 1) * B_DV] = (y * (g * jax.nn.sigmoid(g))).astype(BF16)

    @pl.when(t == nt - 1)
    def _():
        nr_ref[0] = sret[...]

    lane = lax.broadcasted_iota(jnp.int32, (tt, 2 * C_DQK), 1)
    lo = lane < C_DQK

    def qk_norm(x, g2):
        sq = x * x
        s_lo = jnp.sum(jnp.where(lo, sq, 0.0), axis=-1, keepdims=True)
        s_hi = jnp.sum(jnp.where(lo, 0.0, sq), axis=-1, keepdims=True)
        ms = jnp.where(lo, s_lo, s_hi) * (1.0 / C_DQK)
        return x * lax.rsqrt(ms + EPS) * g2

    w = 2 * C_DQK
    for h in range(C_HEADS):
        cq = proj_ref[:, C_OFF + h * w:C_OFF + (h + 1) * w]
        ck = proj_ref[:, C_OFF + C_HEADS * w + h * w:C_OFF + C_HEADS * w + (h + 1) * w]
        qn = qk_norm(cq, qg_ref[...])
        kn = qk_norm(ck, kg_ref[...])
        qn_ref[:, h * w:(h + 1) * w] = (qn * (C_DQK ** -0.5)).astype(BF16)
        kn_ref[:, h * w:(h + 1) * w] = kn.astype(BF16)
        ck_ref[:, h, :] = kn
    cv = proj_ref[:, C_OFF + 2 * C_HEADS * w:IN_COLS]
    ones = jnp.ones((tt, C_DV), BF16)
    for h in range(C_HEADS):
        cv_ref[:, h, :] = cv[:, h * C_DV:(h + 1) * C_DV]
        vb_ref[:, 2 * h * C_DV:(2 * h + 1) * C_DV] = cv[:, h * C_DV:(h + 1) * C_DV].astype(BF16)
        vb_ref[:, (2 * h + 1) * C_DV:(2 * h + 2) * C_DV] = ones


def _mixer(proj, hist, state, cos2, sin2, cw, cb, lng, lnb, gng, qg2, kg2, *, bsz, t, tt):
    n = bsz * t
    nt = t // tt
    log_gammas = tuple(math.log(1.0 - 2.0 ** (-5.0 - h)) for h in range(B_HEADS))
    tok = lambda cols: pl.BlockSpec((tt, cols), lambda b, i: (b * nt + i, 0))
    const = lambda r, c: pl.BlockSpec((r, c), lambda b, i: (0, 0))
    cw_dim = C_HEADS * 2 * C_DQK
    return pl.pallas_call(
        functools.partial(_mixer_body, tt=tt, log_gammas=log_gammas),
        grid=(bsz, nt),
        in_specs=[
            tok(IN_COLS),
            pl.BlockSpec((1, A_HIST, A_WIDTH), lambda b, i: (b, 0, 0)),
            pl.BlockSpec((1, B_HEADS, B_DK, B_DV), lambda b, i: (b, 0, 0, 0)),
            pl.BlockSpec((tt, B_DK), lambda b, i: (i, 0)),
            pl.BlockSpec((tt, B_DK), lambda b, i: (i, 0)),
            const(A_KW, A_WIDTH), const(1, A_WIDTH), const(1, A_WIDTH), const(1, A_WIDTH),
            const(1, B_HEADS * B_DV), const(1, 2 * C_DQK), const(1, 2 * C_DQK),
        ],
        out_specs=[
            tok(A_WIDTH), tok(B_HEADS * B_DV), tok(cw_dim), tok(cw_dim), tok(2 * C_HEADS * C_DV),
            pl.BlockSpec((None, tt, C_HEADS, 2 * C_DQK), lambda b, i: (b, i, 0, 0)),
            pl.BlockSpec((None, tt, C_HEADS, C_DV), lambda b, i: (b, i, 0, 0)),
            pl.BlockSpec((1, A_HIST, A_WIDTH), lambda b, i: (b, 0, 0)),
            pl.BlockSpec((1, B_HEADS, B_DK, B_DV), lambda b, i: (b, 0, 0, 0)),
        ],
        out_shape=[
            jax.ShapeDtypeStruct((n, A_WIDTH), BF16),
            jax.ShapeDtypeStruct((n, B_HEADS * B_DV), BF16),
            jax.ShapeDtypeStruct((n, cw_dim), BF16),
            jax.ShapeDtypeStruct((n, cw_dim), BF16),
            jax.ShapeDtypeStruct((n, 2 * C_HEADS * C_DV), BF16),
            jax.ShapeDtypeStruct((bsz, t, C_HEADS, 2 * C_DQK), F32),
            jax.ShapeDtypeStruct((bsz, t, C_HEADS, C_DV), F32),
            jax.ShapeDtypeStruct((bsz, A_HIST, A_WIDTH), F32),
            jax.ShapeDtypeStruct((bsz, B_HEADS, B_DK, B_DV), F32),
        ],
        scratch_shapes=[pltpu.VMEM((A_HIST + tt, A_WIDTH), F32), pltpu.VMEM((B_HEADS, B_DK, B_DV), F32)],
        compiler_params=_params("parallel", "arbitrary"),
        name="mixer",
    )(proj, hist, state, cos2, sin2, cw, cb, lng, lnb, gng, qg2, kg2)


def _t5_bucket(rel):
    half = REL_BUCKETS // 2
    exact = half // 2
    n = np.abs(rel)
    large = exact + (np.log(np.maximum(n, 1).astype(np.float32) / exact) / math.log(REL_MAX_DIST / exact)
                     * (half - exact)).astype(np.int32)
    large = np.minimum(large, half - 1)
    return (np.where(rel > 0, half, 0) + np.where(n < exact, n, large)).astype(np.int32)


def _bias_body(rb_ref, idx_ref, vis_ref, o_ref):
    h = pl.program_id(0)
    idx = idx_ref[...]
    far = rb_ref[REL_BUCKETS // 2 - 1, h]
    acc = jnp.zeros(idx.shape, F32)
    for b in range(REL_BUCKETS):
        acc = jnp.where(idx == b, rb_ref[b, h], acc)
    o_ref[0] = jnp.where(vis_ref[...] != 0, acc - far, NEG_INF)


def _bias_tiles(rel_bias, idx, vis):
    r, c = idx.shape
    return pl.pallas_call(
        _bias_body,
        grid=(C_HEADS,),
        in_specs=[
            pl.BlockSpec(memory_space=pltpu.SMEM),
            pl.BlockSpec((r, c), lambda h: (0, 0)),
            pl.BlockSpec((r, c), lambda h: (0, 0)),
        ],
        out_specs=pl.BlockSpec((1, r, c), lambda h: (h, 0, 0)),
        out_shape=jax.ShapeDtypeStruct((C_HEADS, r, c), F32),
        compiler_params=_params("arbitrary"),
        name="bias_tiles",
    )(rel_bias, idx, vis)


def _lambda(lp_ref, lam_init):
    lp = lp_ref[...]
    e1 = jnp.exp(jnp.sum(lp[0:1] * lp[1:2], axis=-1, keepdims=True))
    e2 = jnp.exp(jnp.sum(lp[2:3] * lp[3:4], axis=-1, keepdims=True))
    return e1 - e2 + lam_init


def _stack_maps(q):
    lane = lax.broadcasted_iota(jnp.int32, q.shape, 1)
    zero = jnp.zeros_like(q)
    return jnp.concatenate([jnp.where(lane < C_DQK, q, zero), jnp.where(lane < C_DQK, zero, q)], axis=0)


def _attn_finish(acc, l, lam, g, lam_init, tq):
    o = acc / l
    o = o[0:tq] - lam * o[tq:2 * tq]
    return _rms(o, g) * (1.0 - lam_init)


def _attn_prompt_body(q_ref, k_ref, v_ref, bias_ref, lp_ref, g_ref, o_ref, qq_scr, m_scr, acc_scr, *, tq, lam_init):
    qi = pl.program_id(2)
    qq_scr[...] = _stack_maps(q_ref[...])
    m_scr[...] = jnp.full(m_scr.shape, NEG_INF, F32)
    acc_scr[...] = jnp.zeros(acc_scr.shape, F32)
    nl = tq // ATT_LANES

    def step(ki, bias_idx):
        start = pl.multiple_of(ki * tq, tq)
        kb = k_ref[pl.ds(start, tq), :]
        vb = v_ref[pl.ds(start, tq), :]
        for c in range(2 * tq // ATT_ROWS):
            rows = slice(c * ATT_ROWS, (c + 1) * ATT_ROWS)
            s = _dot_nt(qq_scr[rows, :], kb)
            if bias_idx is not None:
                s = s + bias_ref[0, bias_idx, pl.ds((c * ATT_ROWS) % tq, ATT_ROWS), :]
            slabs = [s[:, j * ATT_LANES:(j + 1) * ATT_LANES] for j in range(nl)]
            m_old = m_scr[rows, :]
            m_new = jnp.maximum(m_old, jnp.max(functools.reduce(jnp.maximum, slabs), axis=-1, keepdims=True))
            alpha = jnp.exp(m_old - m_new)
            p = jnp.concatenate([jnp.exp(sl - m_new) for sl in slabs], axis=-1).astype(BF16)
            acc_scr[rows, :] = jnp.concatenate([alpha, alpha], axis=-1) * acc_scr[rows, :] + _dot(p, vb)
            m_scr[rows, :] = m_new

    n_far = jnp.maximum(qi - 1, 0)

    def far_body(ki, c):
        step(ki, None)
        return c

    lax.fori_loop(0, n_far, far_body, 0)

    def near_body(ki, c):
        step(ki, qi - ki)
        return c

    lax.fori_loop(n_far, qi + 1, near_body, 0)

    lam = _lambda(lp_ref, lam_init)
    acc = acc_scr[...]
    o_ref[...] = _attn_finish(acc[:, 0:C_DV], acc[:, C_DV:2 * C_DV], lam, g_ref[...], lam_init, tq).astype(BF16)


def _attn_prompt(qn, kn, vb, bias, lp, g, *, bsz, t, lam_init):
    tq = ATT_TILE
    nq = t // tq
    n = bsz * t
    w = 2 * C_DQK
    return pl.pallas_call(
        functools.partial(_attn_prompt_body, tq=tq, lam_init=lam_init),
        grid=(bsz, C_HEADS, nq),
        in_specs=[
            pl.BlockSpec((tq, w), lambda b, h, i: (b * nq + i, h)),
            pl.BlockSpec((t, w), lambda b, h, i: (b, h)),
            pl.BlockSpec((t, 2 * C_DV), lambda b, h, i: (b, h)),
            pl.BlockSpec((1, 2, tq, tq), lambda b, h, i: (h, 0, 0, 0)),
            pl.BlockSpec((4, C_DQK), lambda b, h, i: (0, 0)),
            pl.BlockSpec((1, C_DV), lambda b, h, i: (0, 0)),
        ],
        out_specs=pl.BlockSpec((tq, C_DV), lambda b, h, i: (b * nq + i, h)),
        out_shape=jax.ShapeDtypeStruct((n, C_HEADS * C_DV), BF16),
        scratch_shapes=[pltpu.VMEM((2 * tq, w), BF16), pltpu.VMEM((2 * tq, ATT_LANES), F32),
                        pltpu.VMEM((2 * tq, 2 * C_DV), F32)],
        compiler_params=_params("parallel", "parallel", "arbitrary"),
        name="attn_prompt",
    )(qn, kn, vb, bias, lp, g)


def _attn_sample_body(q_ref, k_ref, v_ref, pk_ref, pv_ref, bp_ref, bn_ref, lp_ref, g_ref, o_ref, *, tq, lam_init):
    lam = _lambda(lp_ref, lam_init)
    w = 2 * C_DQK
    for h in range(C_HEADS):
        qq = _stack_maps(q_ref[:, h * w:(h + 1) * w])
        bp = bp_ref[h]
        bn = bn_ref[h]
        s_p = _dot_nt(qq, pk_ref[:, h, :].astype(BF16)) + jnp.concatenate([bp, bp], axis=0)
        s_n = _dot_nt(qq, k_ref[:, h * w:(h + 1) * w]) + jnp.concatenate([bn, bn], axis=0)
        m = jnp.maximum(jnp.max(s_p, axis=-1, keepdims=True), jnp.max(s_n, axis=-1, keepdims=True))
        p_p = jnp.exp(s_p - m)
        p_n = jnp.exp(s_n - m)
        l = jnp.sum(p_p, axis=-1, keepdims=True) + jnp.sum(p_n, axis=-1, keepdims=True)
        acc = (_dot(p_p.astype(BF16), pv_ref[:, h, :].astype(BF16))
               + _dot(p_n.astype(BF16), v_ref[:, 2 * h * C_DV:(2 * h + 1) * C_DV]))
        o_ref[:, h * C_DV:(h + 1) * C_DV] = _attn_finish(acc, l, lam, g_ref[...], lam_init, tq).astype(BF16)


def _attn_sample(qn, kn, vb, past_k, past_v, bias_p, bias_n, lp, g, layer, *, bsz, t, lam_init):
    n = bsz * t
    w = 2 * C_DQK
    past = past_k.shape[2]
    full = lambda a: pl.BlockSpec(a.shape, lambda b: (0,) * a.ndim)
    return pl.pallas_call(
        functools.partial(_attn_sample_body, tq=t, lam_init=lam_init),
        grid=(bsz,),
        in_specs=[
            pl.BlockSpec((t, C_HEADS * w), lambda b: (b, 0)),
            pl.BlockSpec((t, C_HEADS * w), lambda b: (b, 0)),
            pl.BlockSpec((t, 2 * C_HEADS * C_DV), lambda b: (b, 0)),
            pl.BlockSpec((None, None, past, C_HEADS, w), lambda b: (layer, b, 0, 0, 0)),
            pl.BlockSpec((None, None, past, C_HEADS, C_DV), lambda b: (layer, b, 0, 0, 0)),
            full(bias_p), full(bias_n),
            pl.BlockSpec((4, C_DQK), lambda b: (0, 0)),
            pl.BlockSpec((1, C_DV), lambda b: (0, 0)),
        ],
        out_specs=pl.BlockSpec((t, C_HEADS * C_DV), lambda b: (b, 0)),
        out_shape=jax.ShapeDtypeStruct((n, C_HEADS * C_DV), BF16),
        compiler_params=_params("parallel"),
        name="attn_sample",
    )(qn, kn, vb, past_k, past_v, bias_p, bias_n, lp, g)


def _memkv_body(mem_ref, g_ref, wk_ref, wv_ref, kg_ref, k_ref, v_ref, kb_ref, vb_ref):
    hm = _rms(mem_ref[0], g_ref[...]).astype(BF16)
    k = _dot(hm, wk_ref[...])
    v = _dot(hm, wv_ref[...])
    for h in range(M_HEADS):
        kn = _rms(k[:, h * M_DH:(h + 1) * M_DH], kg_ref[...])
        k_ref[0, :, h * M_DH:(h + 1) * M_DH] = kn
        kb_ref[0, :, h * M_DH:(h + 1) * M_DH] = kn.astype(BF16)
    v_ref[0] = v
    vb_ref[0] = v.astype(BF16)


def _memkv(mem, g, wk, wv, kg):
    bsz, m, _ = mem.shape
    w = M_HEADS * M_DH
    blk = pl.BlockSpec((1, m, w), lambda b: (b, 0, 0))
    return pl.pallas_call(
        _memkv_body,
        grid=(bsz,),
        in_specs=[
            pl.BlockSpec((1, m, D_MODEL), lambda b: (b, 0, 0)),
            pl.BlockSpec((1, D_MODEL), lambda b: (0, 0)),
            pl.BlockSpec((D_MODEL, w), lambda b: (0, 0)),
            pl.BlockSpec((D_MODEL, w), lambda b: (0, 0)),
            pl.BlockSpec((1, M_DH), lambda b: (0, 0)),
        ],
        out_specs=[blk, blk, blk, blk],
        out_shape=[jax.ShapeDtypeStruct((bsz, m, w), F32), jax.ShapeDtypeStruct((bsz, m, w), F32),
                   jax.ShapeDtypeStruct((bsz, m, w), BF16), jax.ShapeDtypeStruct((bsz, m, w), BF16)],
        compiler_params=_params("parallel"),
        name="memkv",
    )(mem, g, wk, wv, kg)


def _post_body(x_ref, a_ref, bo_ref, co_ref, wo_ref, g_ref, wq_ref, qg_ref, mk_ref, mv_ref, wxo_ref,
               o_ref, att_scr, *, nseq, rps):
    y = (_dot(a_ref[...], wo_ref[0:A_WIDTH, :])
         + _dot(bo_ref[...], wo_ref[A_WIDTH:A_WIDTH + B_HEADS * B_DV, :])
         + _dot(co_ref[...], wo_ref[A_WIDTH + B_HEADS * B_DV:, :]))
    x1 = x_ref[...] + y
    q = _dot(_rms(x1, g_ref[...]).astype(BF16), wq_ref[...])
    for h in range(M_HEADS):
        sl = slice(h * M_DH, (h + 1) * M_DH)
        qn = _rms(q[:, sl], qg_ref[...]).astype(BF16)
        for s in range(nseq):
            rows = slice(s * rps, (s + 1) * rps)
            logits = _dot_nt(qn[rows], mk_ref[s, :, sl]) * (M_DH ** -0.5)
            m = jnp.max(logits, axis=-1, keepdims=True)
            p = jnp.exp(logits - m)
            l = jnp.sum(p, axis=-1, keepdims=True)
            o = _dot(p.astype(BF16), mv_ref[s, :, sl]) / l
            att_scr[rows, sl] = o.astype(BF16)
    o_ref[...] = x1 + _dot(att_scr[...], wxo_ref[...])


def _post(x, a, bo, co, wo, g, wq, qg, mk, mv, wxo, layer, *, t, tm):
    n = x.shape[0]
    w = M_HEADS * M_DH
    if t >= tm:
        nseq, rps = 1, tm
        per = t // tm
        mem_map = lambda i: (i // per, 0, 0)
    else:
        nseq, rps = tm // t, t
        mem_map = lambda i: (i, 0, 0)
    tok = lambda cols: pl.BlockSpec((tm, cols), lambda i: (i, 0))
    const = lambda r, c: pl.BlockSpec((r, c), lambda i: (0, 0), pipeline_mode=pl.Buffered(1))
    stacked = lambda r, c: pl.BlockSpec((None, r, c), lambda i: (layer, 0, 0), pipeline_mode=pl.Buffered(1))
    return pl.pallas_call(
        functools.partial(_post_body, nseq=nseq, rps=rps),
        grid=(n // tm,),
        in_specs=[
            tok(D_MODEL), tok(A_WIDTH), tok(B_HEADS * B_DV), tok(C_HEADS * C_DV),
            stacked(D_MODEL, D_MODEL), const(1, D_MODEL), stacked(D_MODEL, w), const(1, M_DH),
            pl.BlockSpec((nseq, MEM_LEN, w), mem_map),
            pl.BlockSpec((nseq, MEM_LEN, w), mem_map),
            stacked(w, D_MODEL),
        ],
        out_specs=tok(D_MODEL),
        out_shape=jax.ShapeDtypeStruct((n, D_MODEL), F32),
        scratch_shapes=[pltpu.VMEM((tm, w), BF16)],
        compiler_params=_params("parallel"),
        name="post",
    )(x, a, bo, co, wo, g, wq, qg, mk, mv, wxo)


def _ffn_body(x_ref, g_ref, wv_ref, wg_ref, cwv_ref, cwg_ref, cbv_ref, cbg_ref, hv_ref, hg_ref, wd_ref,
              o_ref, nv_ref, ng_ref, h_scr, ubuf, tail, *, nseq, rps, per):
    i = pl.program_id(0)
    j = pl.program_id(1)

    @pl.when(j == 0)
    def _():
        x = x_ref[...]
        h_scr[...] = _rms(x, g_ref[...]).astype(BF16)
        o_ref[...] = x

    if per > 1:
        @pl.when((i == 0) & (j == 0))
        def _():
            tail[...] = jnp.zeros(tail.shape, F32)

    seq_start = (i % per) == 0
    stride = rps + F_HIST
    tm = x_ref.shape[0]
    rc = tm // FFN_CHAINS
    assert rc % rps == 0 or rps % rc == 0

    def conv(half, u, r0, cw_ref, cb_ref, hist_ref, new_ref):
        outs = []
        seg = min(rc, rps)
        for q in range(rc // seg):
            s, o = divmod(r0 + q * seg, rps)
            base = s * stride
            if o == 0:
                if per == 1:
                    prev = hist_ref[s]
                else:
                    prev = jnp.where(seq_start, hist_ref[s], tail[half, j])
                ubuf[half, base:base + F_HIST, :] = prev
            us = u[q * seg:(q + 1) * seg]
            ubuf[half, base + F_HIST + o:base + F_HIST + o + seg, :] = us
            c = cb_ref[...] + us * cw_ref[F_KW - 1:F_KW, :]
            for k in range(F_KW - 1):
                off = base + F_HIST + o - (F_KW - 1) + k
                c = c + ubuf[half, off:off + seg, :] * cw_ref[k:k + 1, :]
            outs.append(c)
            if o + seg == rps:
                last = ubuf[half, base + rps:base + stride, :]
                new_ref[s] = last
                if per > 1:
                    tail[half, j] = last
        return outs[0] if len(outs) == 1 else jnp.concatenate(outs, axis=0)

    chains = [slice(r * rc, (r + 1) * rc) for r in range(FFN_CHAINS)]
    us = [(_dot(h_scr[rows, :], wv_ref[...]), _dot(h_scr[rows, :], wg_ref[...])) for rows in chains]
    for rows, (uv, ug) in zip(chains, us):
        val = conv(0, uv, rows.start, cwv_ref, cbv_ref, hv_ref, nv_ref)
        gate = conv(1, ug, rows.start, cwg_ref, cbg_ref, hg_ref, ng_ref)
        act = (gate * jax.nn.sigmoid(gate) * val).astype(BF16)
        o_ref[rows, :] += _dot(act, wd_ref[...])


def _ffn(x, g, w_up, cw, cb, hist, w_down, layer, *, t, tm=1024, tn=512):
    n = x.shape[0]
    nj = D_FF // tn
    if t >= tm:
        nseq, rps, per = 1, tm, t // tm
        seq_map = lambda i: i // per
    else:
        nseq, rps, per = tm // t, t, 1
        seq_map = lambda i: i
    bsz = hist.shape[0]
    hist_spec = lambda off: pl.BlockSpec((nseq, F_HIST, tn), lambda i, j: (seq_map(i), 0, j + off))
    new_spec = pl.BlockSpec((nseq, F_HIST, tn), lambda i, j: (i, 0, j))
    x_out, tail_v, tail_g = pl.pallas_call(
        functools.partial(_ffn_body, nseq=nseq, rps=rps, per=per),
        grid=(n // tm, nj),
        in_specs=[
            pl.BlockSpec((tm, D_MODEL), lambda i, j: (i, 0), pipeline_mode=pl.Buffered(1)),
            pl.BlockSpec((1, D_MODEL), lambda i, j: (0, 0)),
            pl.BlockSpec((None, D_MODEL, tn), lambda i, j: (layer, 0, j)),
            pl.BlockSpec((None, D_MODEL, tn), lambda i, j: (layer, 0, j + nj)),
            pl.BlockSpec((F_KW, tn), lambda i, j: (0, j)),
            pl.BlockSpec((F_KW, tn), lambda i, j: (0, j + nj)),
            pl.BlockSpec((1, tn), lambda i, j: (0, j)),
            pl.BlockSpec((1, tn), lambda i, j: (0, j + nj)),
            hist_spec(0), hist_spec(nj),
            pl.BlockSpec((None, tn, D_MODEL), lambda i, j: (layer, j, 0)),
        ],
        out_specs=[pl.BlockSpec((tm, D_MODEL), lambda i, j: (i, 0)), new_spec, new_spec],
        out_shape=[jax.ShapeDtypeStruct((n, D_MODEL), F32),
                   jax.ShapeDtypeStruct((bsz * per, F_HIST, D_FF), F32),
                   jax.ShapeDtypeStruct((bsz * per, F_HIST, D_FF), F32)],
        scratch_shapes=[
            pltpu.VMEM((tm, D_MODEL), BF16),
            pltpu.VMEM((2, nseq * (rps + F_HIST), tn), F32),
            pltpu.VMEM((2, nj, F_HIST, tn), F32),
        ],
        compiler_params=_params("arbitrary", "arbitrary"),
        name="ffn",
    )(x, g, w_up, w_up, cw, cw, cb, cb, hist, hist, w_down)
    last = lambda a: a.reshape(bsz, per, F_HIST, D_FF)[:, per - 1]
    return x_out, last(tail_v), last(tail_g)


def _rope_tables(pos0, t):
    half = B_DK // 2
    inv = 1.0 / (ROPE_BASE ** (jnp.arange(half, dtype=F32) / half))
    ang = (pos0 + jnp.arange(t, dtype=jnp.int32)).astype(F32)[:, None] * inv[None, :]
    cos = jnp.cos(ang)
    sin = jnp.sin(ang)
    return jnp.concatenate([cos, cos], axis=-1), jnp.concatenate([-sin, sin], axis=-1)


def _pad_rows(h, rows):
    return jnp.pad(h, ((0, 0), (rows - h.shape[1], 0), (0, 0)))


def _row(v):
    return v.reshape(1, -1).astype(F32)


def kernel(x_prompt, x_sample, mem_prompt, state_conv_a, state_ret, cache_diff_k, cache_diff_v, cache_mem_k,
           cache_mem_v, state_conv_f, norm1_g, w_in, conv_a_w, conv_a_b, ln_a_g, ln_a_b, ret_gn_g, diff_qn_g,
           diff_kn_g, diff_lq1, diff_lk1, diff_lq2, diff_lk2, diff_subln_g, w_out, rel_bias, norm2_g, mem_norm_g,
           w_xq, w_xk, w_xv, xqn_g, xkn_g, w_xo, norm3_g, w_up, conv_f_w, conv_f_b, w_down):
    bp, tp, _ = x_prompt.shape
    bs, ts, _ = x_sample.shape
    depth = w_in.shape[0]
    past = cache_diff_k.shape[2]
    cw_dim = C_HEADS * 2 * C_DQK

    tq = ATT_TILE
    r = np.arange(tq)
    rel_diag = r[None, :] - r[:, None]
    idx_p = np.stack([_t5_bucket(rel_diag), _t5_bucket(rel_diag - tq)]).reshape(2 * tq, tq)
    vis_p = np.stack([(r[None, :] // CHUNK) <= (r[:, None] // CHUNK), np.ones((tq, tq), bool)])
    vis_p = vis_p.reshape(2 * tq, tq).astype(np.int32)
    bias_p = _bias_tiles(rel_bias, jnp.asarray(idx_p), jnp.asarray(vis_p)).reshape(C_HEADS, 2, tq, tq)
    rel_s = np.arange(past + ts)[None, :] - (past + np.arange(ts))[:, None]
    bias_s = _bias_tiles(rel_bias, jnp.asarray(_t5_bucket(rel_s)), jnp.ones(rel_s.shape, jnp.int32))
    bias_s_past, bias_s_new = bias_s[:, :, :past], bias_s[:, :, past:]

    cos_p, sin_p = _rope_tables(0, tp)
    cos_s, sin_s = _rope_tables(past, ts)

    xp = x_prompt.reshape(bp * tp, D_MODEL)
    xs = x_sample.reshape(bs * ts, D_MODEL)
    zero_a = jnp.zeros((bp, A_HIST, A_WIDTH), F32)
    zero_r = jnp.zeros((bp, B_HEADS, B_DK, B_DV), F32)
    zero_f = jnp.zeros((bp, F_HIST, 2 * D_FF), F32)

    outs = {k: [] for k in ("p_ca", "p_rs", "p_k", "p_v", "p_mk", "p_mv", "p_cf", "s_ca", "s_rs", "s_k", "s_v", "s_cf")}
    w_in_b = w_in.astype(BF16)
    w_out_b = w_out.astype(BF16)
    w_xq_b = w_xq.astype(BF16)
    w_xo_b = w_xo.astype(BF16)
    w_up_b = w_up.astype(BF16)
    w_down_b = w_down.astype(BF16)
    for l in range(depth):
        lam_init = 0.8 - 0.6 * math.exp(-0.3 * l)
        lp = jnp.stack([diff_lq1[l], diff_lk1[l], diff_lq2[l], diff_lk2[l]]).astype(F32)
        qg2 = _row(jnp.concatenate([diff_qn_g[l], diff_qn_g[l]]))
        kg2 = _row(jnp.concatenate([diff_kn_g[l], diff_kn_g[l]]))
        mix_args = (conv_a_w[l], _row(conv_a_b[l]), _row(ln_a_g[l]), _row(ln_a_b[l]), _row(ret_gn_g[l]), qg2, kg2)
        subln = _row(diff_subln_g[l])

        def block(x, bsz, t, tt, hist_a, state_r, cos2, sin2, attend, mk_b, mv_b, hist_f, tm_post):
            proj = _in_proj(x, _row(norm1_g[l]), w_in_b, l)
            a, bo, qn, kn, vb, ck, cv, nh, nr = _mixer(proj, hist_a, state_r, cos2, sin2, *mix_args,
                                                        bsz=bsz, t=t, tt=tt)
            co = attend(qn, kn, vb)
            x = _post(x, a, bo, co, w_out_b, _row(norm2_g[l]), w_xq_b, _row(xqn_g[l]), mk_b, mv_b, w_xo_b,
                      l, t=t, tm=tm_post)
            x, nfv, nfg = _ffn(x, _row(norm3_g[l]), w_up_b, conv_f_w[l], _row(conv_f_b[l]), hist_f, w_down_b, l,
                               t=t)
            new_f = jnp.concatenate([nfv[:, F_HIST - (F_KW - 1):], nfg[:, F_HIST - (F_KW - 1):]], axis=-1)
            return x, nh[:, A_HIST - (A_KW - 1):], nr, ck, cv, new_f

        mk, mv, mk_b, mv_b = _memkv(mem_prompt, _row(mem_norm_g[l]), w_xk[l].astype(BF16), w_xv[l].astype(BF16),
                                    _row(xkn_g[l]))
        attend_p = lambda qn, kn, vb: _attn_prompt(qn, kn, vb, bias_p, lp, subln, bsz=bp, t=tp, lam_init=lam_init)
        xp, ca, rs, kn_, vn_, cf = block(xp, bp, tp, 256, zero_a, zero_r, cos_p, sin_p, attend_p, mk_b, mv_b,
                                         zero_f, 256)
        outs["p_ca"].append(ca); outs["p_rs"].append(rs); outs["p_k"].append(kn_); outs["p_v"].append(vn_)
        outs["p_mk"].append(mk.reshape(bp, MEM_LEN, M_HEADS, M_DH))
        outs["p_mv"].append(mv.reshape(bp, MEM_LEN, M_HEADS, M_DH))
        outs["p_cf"].append(cf)

        attend_s = lambda qn, kn, vb: _attn_sample(qn, kn, vb, cache_diff_k, cache_diff_v, bias_s_past,
                                                   bias_s_new, lp, subln, l,
                                                   bsz=bs, t=ts, lam_init=lam_init)
        smk = cache_mem_k[l].reshape(bs, MEM_LEN, M_HEADS * M_DH).astype(BF16)
        smv = cache_mem_v[l].reshape(bs, MEM_LEN, M_HEADS * M_DH).astype(BF16)
        xs, sca, srs, skn, svn, scf = block(xs, bs, ts, ts, _pad_rows(state_conv_a[l], A_HIST), state_ret[l],
                                            cos_s, sin_s, attend_s, smk, smv, _pad_rows(state_conv_f[l], F_HIST), 256)
        outs["s_ca"].append(sca); outs["s_rs"].append(srs); outs["s_k"].append(skn); outs["s_v"].append(svn)
        outs["s_cf"].append(scf)

    st = lambda k: jnp.stack(outs[k])
    return (xp.reshape(bp, tp, D_MODEL), xs.reshape(bs, ts, D_MODEL),
            st("p_ca"), st("p_rs"), st("p_k"), st("p_v"), st("p_mk"), st("p_mv"), st("p_cf"),
            st("s_ca"), st("s_rs"), st("s_k"), st("s_v"), st("s_cf"))
```

```python
import functools
import math

import numpy as np
import jax
import jax.numpy as jnp
from jax import lax
from jax.experimental import pallas as pl
from jax.experimental.pallas import tpu as pltpu

F32 = jnp.float32
BF16 = jnp.bfloat16
EPS = 1e-6
NEG_INF = -1e30

D_MODEL = 2048
CHUNK = 64
A_WIDTH = 512
A_KW = 31
A_HIST = 32
B_HEADS = 4
B_DK = 128
B_DV = 256
ROPE_BASE = 10000.0
C_HEADS = 4
C_DQK = 64
C_DV = 128
REL_BUCKETS = 32
REL_MAX_DIST = 128
M_HEADS = 4
M_DH = 128
MEM_LEN = 256
D_FF = 5632
F_KW = 3
F_HIST = 8

A_COLS = 2 * A_WIDTH
B_COLS = B_HEADS * (2 * B_DK + 2 * B_DV)
C_COLS = C_HEADS * (4 * C_DQK + C_DV)
IN_COLS = A_COLS + B_COLS + C_COLS
B_OFF = A_COLS
C_OFF = A_COLS + B_COLS
PROJ_GROUPS = (0, A_WIDTH, A_COLS, B_OFF + 2 * B_HEADS * B_DK, B_OFF + 2 * B_HEADS * B_DK + B_HEADS * B_DV, C_OFF,
               C_OFF + 2 * C_HEADS * 2 * C_DQK, IN_COLS)

VMEM_LIMIT_BYTES = 56 * 1024 * 1024
ATT_TILE = 512
ATT_ROWS = 256
ATT_LANES = 128
FFN_CHAINS = 2
FUSED_PROJ_MIN_ROWS = 128


def _params(*sem):
    return pltpu.CompilerParams(dimension_semantics=sem, vmem_limit_bytes=VMEM_LIMIT_BYTES)


def _rms(x, g):
    return x * lax.rsqrt(jnp.mean(x * x, axis=-1, keepdims=True) + EPS) * g


def _dot(a, b):
    return jnp.dot(a, b, preferred_element_type=F32)


def _dot_nt(a, b):
    return lax.dot_general(a, b, (((1,), (1,)), ((), ())), preferred_element_type=F32)


def _dot_tn(a, b):
    return lax.dot_general(a, b, (((0,), (0,)), ((), ())), preferred_element_type=F32)


def _in_proj_body(x_ref, g_ref, w_ref, o_ref, h_ref):
    @pl.when(pl.program_id(1) == 0)
    def _():
        h_ref[...] = _rms(x_ref[...], g_ref[...]).astype(BF16)

    o_ref[...] = _dot(h_ref[...], w_ref[...])


def _in_proj(x, g, w, layer, *, tm=1024, tn=512):
    n = x.shape[0]
    cols = w.shape[2]
    return pl.pallas_call(
        _in_proj_body,
        grid=(n // tm, cols // tn),
        in_specs=[
            pl.BlockSpec((tm, D_MODEL), lambda i, j: (i, 0)),
            pl.BlockSpec((1, D_MODEL), lambda i, j: (0, 0)),
            pl.BlockSpec((None, D_MODEL, tn), lambda i, j: (layer, 0, j)),
        ],
        out_specs=pl.BlockSpec((tm, tn), lambda i, j: (i, j)),
        out_shape=jax.ShapeDtypeStruct((n, cols), F32),
        scratch_shapes=[pltpu.VMEM((tm, D_MODEL), BF16)],
        compiler_params=_params("parallel", "arbitrary"),
        name="in_proj",
    )(x, g, w)


def _projected_columns(x_ref, g_ref, w_ref):
    h = _rms(x_ref[...], g_ref[...]).astype(BF16)
    cache = {}

    def cols(c0, c1):
        g0, g1 = next((a, b) for a, b in zip(PROJ_GROUPS[:-1], PROJ_GROUPS[1:]) if a <= c0 and c1 <= b)
        if g0 not in cache:
            cache[g0] = _dot(h, w_ref[:, g0:g1])
        return cache[g0][:, c0 - g0:c1 - g0]

    return cols


def _mixer_body(*refs, tt, log_gammas, fused):
    if fused:
        cols = _projected_columns(*refs[:3])
        refs = refs[3:]
    else:
        proj_ref = refs[0]
        cols = lambda c0, c1: proj_ref[:, c0:c1]
        refs = refs[1:]
    (hist_ref, state_ref, cos_ref, sin_ref, cw_ref, cb_ref, lng_ref, lnb_ref, gng_ref, qg_ref, kg_ref,
     a_ref, bo_ref, qn_ref, kn_ref, vb_ref, ck_ref, cv_ref, nh_ref, nr_ref, aext, zbuf, sret) = refs
    t = pl.program_id(1)

    @pl.when(t == 0)
    def _():
        aext[0:A_HIST, :] = hist_ref[0]
        sret[...] = state_ref[0]

    glu = cols(0, A_WIDTH) * jax.nn.sigmoid(cols(A_WIDTH, A_COLS))
    aext[A_HIST:A_HIST + tt, :] = glu
    first = A_HIST - (A_KW - 1)
    acc = jnp.zeros((tt, A_WIDTH), F32) + cb_ref[...]
    for r in range(8):
        rows = tt if r == 0 else tt + 8
        z = None
        for off in range(r, first + A_KW, 8):
            if off < first:
                continue
            term = aext[off - r:off - r + rows, :] * cw_ref[off - first:off - first + 1, :]
            z = term if z is None else z + term
        if r == 0:
            acc = acc + z
        else:
            zbuf[r - 1] = z
            acc = acc + zbuf[r - 1, r:r + tt, :]
    mu = jnp.mean(acc, axis=-1, keepdims=True)
    xc = acc - mu
    var = jnp.mean(xc * xc, axis=-1, keepdims=True)
    ln = xc * lax.rsqrt(var + EPS) * lng_ref[...] + lnb_ref[...]
    a_ref[...] = (ln * jax.nn.sigmoid(ln)).astype(BF16)

    last_rows = aext[tt:tt + A_HIST, :]
    nh_ref[0] = last_rows
    aext[0:A_HIST, :] = last_rows

    cos = cos_ref[...]
    sin = sin_ref[...]
    ri = lax.broadcasted_iota(jnp.int32, (tt, tt), 0)
    ci = lax.broadcasted_iota(jnp.int32, (tt, tt), 1)
    dij = (ri - ci).astype(F32)
    causal = ri >= ci
    rowf = lax.broadcasted_iota(jnp.int32, (tt, 1), 0).astype(F32)
    for h in range(B_HEADS):
        lg = log_gammas[h]
        q = cols(B_OFF + h * B_DK, B_OFF + (h + 1) * B_DK)
        k = cols(B_OFF + B_HEADS * B_DK + h * B_DK, B_OFF + B_HEADS * B_DK + (h + 1) * B_DK)
        voff = B_OFF + 2 * B_HEADS * B_DK
        v = cols(voff + h * B_DV, voff + (h + 1) * B_DV)
        goff = voff + B_HEADS * B_DV
        g = cols(goff + h * B_DV, goff + (h + 1) * B_DV)
        qr = q * cos + pltpu.roll(q, B_DK // 2, 1) * sin
        kr = (k * cos + pltpu.roll(k, B_DK // 2, 1) * sin) * (B_DK ** -0.5)
        qb = qr.astype(BF16)
        vb = v.astype(BF16)
        decay = jnp.where(causal, jnp.exp(lg * jnp.maximum(dij, 0.0)), 0.0)
        scores = _dot_nt(qb, kr.astype(BF16)) * decay
        inner = _dot(scores.astype(BF16), vb)
        s_old = sret[h]
        cross = _dot(qb, s_old.astype(BF16)) * jnp.exp(lg * (rowf + 1.0))
        o = inner + cross
        kd = kr * jnp.exp(lg * (tt - 1.0 - rowf))
        s_new = s_old * math.exp(lg * tt) + _dot_tn(kd.astype(BF16), vb)
        sret[h] = s_new
        nr_ref[0, h] = s_new
        y = _rms(o, gng_ref[:, h * B_DV:(h + 1) * B_DV])
        bo_ref[:, h * B_DV:(h + 1) * B_DV] = (y * (g * jax.nn.sigmoid(g))).astype(BF16)

    lane = lax.broadcasted_iota(jnp.int32, (tt, 2 * C_DQK), 1)
    lo = lane < C_DQK

    def qk_norm(x, g2):
        sq = x * x
        s_lo = jnp.sum(jnp.where(lo, sq, 0.0), axis=-1, keepdims=True)
        s_hi = jnp.sum(jnp.where(lo, 0.0, sq), axis=-1, keepdims=True)
        ms = jnp.where(lo, s_lo, s_hi) * (1.0 / C_DQK)
        return x * lax.rsqrt(ms + EPS) * g2

    w = 2 * C_DQK
    for h in range(C_HEADS):
        cq = cols(C_OFF + h * w, C_OFF + (h + 1) * w)
        ck = cols(C_OFF + C_HEADS * w + h * w, C_OFF + C_HEADS * w + (h + 1) * w)
        qn = qk_norm(cq, qg_ref[...])
        kn = qk_norm(ck, kg_ref[...])
        qn_ref[:, h * w:(h + 1) * w] = (qn * (C_DQK ** -0.5)).astype(BF16)
        kn_ref[:, h * w:(h + 1) * w] = kn.astype(BF16)
        ck_ref[:, h, :] = kn
    cv = cols(C_OFF + 2 * C_HEADS * w, IN_COLS)
    ones = jnp.ones((tt, C_DV), BF16)
    for h in range(C_HEADS):
        cv_ref[:, h, :] = cv[:, h * C_DV:(h + 1) * C_DV]
        vb_ref[:, 2 * h * C_DV:(2 * h + 1) * C_DV] = cv[:, h * C_DV:(h + 1) * C_DV].astype(BF16)
        vb_ref[:, (2 * h + 1) * C_DV:(2 * h + 2) * C_DV] = ones


def _mixer(src, hist, state, cos2, sin2, cw, cb, lng, lnb, gng, qg2, kg2, *, bsz, t, tt):
    n = bsz * t
    nt = t // tt
    log_gammas = tuple(math.log(1.0 - 2.0 ** (-5.0 - h)) for h in range(B_HEADS))
    tok = lambda cols: pl.BlockSpec((tt, cols), lambda b, i: (b * nt + i, 0))
    const = lambda r, c: pl.BlockSpec((r, c), lambda b, i: (0, 0))
    cw_dim = C_HEADS * 2 * C_DQK
    fused = isinstance(src, tuple)
    if fused:
        x, g1, w_in, layer = src
        src_args = (x, g1, w_in)
        src_specs = [tok(D_MODEL), const(1, D_MODEL),
                     pl.BlockSpec((None, D_MODEL, IN_COLS), lambda b, i: (layer, 0, 0), pipeline_mode=pl.Buffered(1))]
    else:
        src_args = (src,)
        src_specs = [tok(IN_COLS)]
    return pl.pallas_call(
        functools.partial(_mixer_body, tt=tt, log_gammas=log_gammas, fused=fused),
        grid=(bsz, nt),
        in_specs=src_specs + [
            pl.BlockSpec((1, A_HIST, A_WIDTH), lambda b, i: (b, 0, 0)),
            pl.BlockSpec((1, B_HEADS, B_DK, B_DV), lambda b, i: (b, 0, 0, 0)),
            pl.BlockSpec((tt, B_DK), lambda b, i: (i, 0)),
            pl.BlockSpec((tt, B_DK), lambda b, i: (i, 0)),
            const(A_KW, A_WIDTH), const(1, A_WIDTH), const(1, A_WIDTH), const(1, A_WIDTH),
            const(1, B_HEADS * B_DV), const(1, 2 * C_DQK), const(1, 2 * C_DQK),
        ],
        out_specs=[
            tok(A_WIDTH), tok(B_HEADS * B_DV), tok(cw_dim), tok(cw_dim), tok(2 * C_HEADS * C_DV),
            pl.BlockSpec((None, tt, C_HEADS, 2 * C_DQK), lambda b, i: (b, i, 0, 0)),
            pl.BlockSpec((None, tt, C_HEADS, C_DV), lambda b, i: (b, i, 0, 0)),
            pl.BlockSpec((1, A_HIST, A_WIDTH), lambda b, i: (b, 0, 0)),
            pl.BlockSpec((1, B_HEADS, B_DK, B_DV), lambda b, i: (b, 0, 0, 0)),
        ],
        out_shape=[
            jax.ShapeDtypeStruct((n, A_WIDTH), BF16),
            jax.ShapeDtypeStruct((n, B_HEADS * B_DV), BF16),
            jax.ShapeDtypeStruct((n, cw_dim), BF16),
            jax.ShapeDtypeStruct((n, cw_dim), BF16),
            jax.ShapeDtypeStruct((n, 2 * C_HEADS * C_DV), BF16),
            jax.ShapeDtypeStruct((bsz, t, C_HEADS, 2 * C_DQK), F32),
            jax.ShapeDtypeStruct((bsz, t, C_HEADS, C_DV), F32),
            jax.ShapeDtypeStruct((bsz, A_HIST, A_WIDTH), F32),
            jax.ShapeDtypeStruct((bsz, B_HEADS, B_DK, B_DV), F32),
        ],
        scratch_shapes=[pltpu.VMEM((A_HIST + tt, A_WIDTH), F32), pltpu.VMEM((7, tt + 8, A_WIDTH), F32),
                        pltpu.VMEM((B_HEADS, B_DK, B_DV), F32)],
        compiler_params=_params("parallel", "arbitrary"),
        name="mixer",
    )(*src_args, hist, state, cos2, sin2, cw, cb, lng, lnb, gng, qg2, kg2)


def _t5_bucket(rel):
    half = REL_BUCKETS // 2
    exact = half // 2
    n = np.abs(rel)
    large = exact + (np.log(np.maximum(n, 1).astype(np.float32) / exact) / math.log(REL_MAX_DIST / exact)
                     * (half - exact)).astype(np.int32)
    large = np.minimum(large, half - 1)
    return (np.where(rel > 0, half, 0) + np.where(n < exact, n, large)).astype(np.int32)


def _bias_body(rb_ref, idx_ref, vis_ref, o_ref):
    h = pl.program_id(0)
    idx = idx_ref[...]
    far = rb_ref[REL_BUCKETS // 2 - 1, h]
    acc = jnp.zeros(idx.shape, F32)
    for b in range(REL_BUCKETS):
        acc = jnp.where(idx == b, rb_ref[b, h], acc)
    o_ref[0] = jnp.where(vis_ref[...] != 0, acc - far, NEG_INF)


def _bias_tiles(rel_bias, idx, vis):
    r, c = idx.shape
    return pl.pallas_call(
        _bias_body,
        grid=(C_HEADS,),
        in_specs=[
            pl.BlockSpec(memory_space=pltpu.SMEM),
            pl.BlockSpec((r, c), lambda h: (0, 0)),
            pl.BlockSpec((r, c), lambda h: (0, 0)),
        ],
        out_specs=pl.BlockSpec((1, r, c), lambda h: (h, 0, 0)),
        out_shape=jax.ShapeDtypeStruct((C_HEADS, r, c), F32),
        compiler_params=_params("arbitrary"),
        name="bias_tiles",
    )(rel_bias, idx, vis)


def _lambda(lp_ref, lam_init):
    lp = lp_ref[...]
    e1 = jnp.exp(jnp.sum(lp[0:1] * lp[1:2], axis=-1, keepdims=True))
    e2 = jnp.exp(jnp.sum(lp[2:3] * lp[3:4], axis=-1, keepdims=True))
    return e1 - e2 + lam_init


def _stack_maps(q):
    lane = lax.broadcasted_iota(jnp.int32, q.shape, 1)
    zero = jnp.zeros_like(q)
    return jnp.concatenate([jnp.where(lane < C_DQK, q, zero), jnp.where(lane < C_DQK, zero, q)], axis=0)


def _attn_finish(acc, l, lam, g, lam_init, tq):
    o = acc / l
    o = o[0:tq] - lam * o[tq:2 * tq]
    return _rms(o, g) * (1.0 - lam_init)


def _attn_prompt_body(q_ref, k_ref, v_ref, bias_ref, lp_ref, g_ref, o_ref, qq_scr, m_scr, acc_scr, *, tq, lam_init):
    qi = pl.program_id(2)
    qq_scr[...] = _stack_maps(q_ref[...])
    m_scr[...] = jnp.full(m_scr.shape, NEG_INF, F32)
    acc_scr[...] = jnp.zeros(acc_scr.shape, F32)
    nl = tq // ATT_LANES

    def step(ki, bias_idx):
        start = pl.multiple_of(ki * tq, tq)
        kb = k_ref[pl.ds(start, tq), :]
        vb = v_ref[pl.ds(start, tq), :]
        for c in range(2 * tq // ATT_ROWS):
            rows = slice(c * ATT_ROWS, (c + 1) * ATT_ROWS)
            s = _dot_nt(qq_scr[rows, :], kb)
            if bias_idx is not None:
                s = s + bias_ref[0, bias_idx, pl.ds((c * ATT_ROWS) % tq, ATT_ROWS), :]
            slabs = [s[:, j * ATT_LANES:(j + 1) * ATT_LANES] for j in range(nl)]
            m_old = m_scr[rows, :]
            m_new = jnp.maximum(m_old, jnp.max(functools.reduce(jnp.maximum, slabs), axis=-1, keepdims=True))
            alpha = jnp.exp(m_old - m_new)
            p = jnp.concatenate([jnp.exp(sl - m_new) for sl in slabs], axis=-1).astype(BF16)
            acc_scr[rows, :] = jnp.concatenate([alpha, alpha], axis=-1) * acc_scr[rows, :] + _dot(p, vb)
            m_scr[rows, :] = m_new

    n_far = jnp.maximum(qi - 1, 0)

    def far_body(ki, c):
        step(ki, None)
        return c

    lax.fori_loop(0, n_far, far_body, 0)

    def near_body(ki, c):
        step(ki, qi - ki)
        return c

    lax.fori_loop(n_far, qi + 1, near_body, 0)

    lam = _lambda(lp_ref, lam_init)
    acc = acc_scr[...]
    o_ref[...] = _attn_finish(acc[:, 0:C_DV], acc[:, C_DV:2 * C_DV], lam, g_ref[...], lam_init, tq).astype(BF16)


def _attn_prompt(qn, kn, vb, bias, lp, g, *, bsz, t, lam_init):
    tq = ATT_TILE
    nq = t // tq
    n = bsz * t
    w = 2 * C_DQK
    return pl.pallas_call(
        functools.partial(_attn_prompt_body, tq=tq, lam_init=lam_init),
        grid=(bsz, C_HEADS, nq),
        in_specs=[
            pl.BlockSpec((tq, w), lambda b, h, i: (b * nq + i, h)),
            pl.BlockSpec((t, w), lambda b, h, i: (b, h)),
            pl.BlockSpec((t, 2 * C_DV), lambda b, h, i: (b, h)),
            pl.BlockSpec((1, 2, tq, tq), lambda b, h, i: (h, 0, 0, 0)),
            pl.BlockSpec((4, C_DQK), lambda b, h, i: (0, 0)),
            pl.BlockSpec((1, C_DV), lambda b, h, i: (0, 0)),
        ],
        out_specs=pl.BlockSpec((tq, C_DV), lambda b, h, i: (b * nq + i, h)),
        out_shape=jax.ShapeDtypeStruct((n, C_HEADS * C_DV), BF16),
        scratch_shapes=[pltpu.VMEM((2 * tq, w), BF16), pltpu.VMEM((2 * tq, ATT_LANES), F32),
                        pltpu.VMEM((2 * tq, 2 * C_DV), F32)],
        compiler_params=_params("parallel", "parallel", "arbitrary"),
        name="attn_prompt",
    )(qn, kn, vb, bias, lp, g)


def _attn_sample_body(q_ref, k_ref, v_ref, pk_ref, pv_ref, bp_ref, bn_ref, lp_ref, g_ref, o_ref, *, tq, lam_init):
    lam = _lambda(lp_ref, lam_init)
    w = 2 * C_DQK
    for h in range(C_HEADS):
        qq = _stack_maps(q_ref[:, h * w:(h + 1) * w])
        bp = bp_ref[h]
        bn = bn_ref[h]
        s_p = _dot_nt(qq, pk_ref[:, h, :].astype(BF16)) + jnp.concatenate([bp, bp], axis=0)
        s_n = _dot_nt(qq, k_ref[:, h * w:(h + 1) * w]) + jnp.concatenate([bn, bn], axis=0)
        m = jnp.maximum(jnp.max(s_p, axis=-1, keepdims=True), jnp.max(s_n, axis=-1, keepdims=True))
        p_p = jnp.exp(s_p - m)
        p_n = jnp.exp(s_n - m)
        l = jnp.sum(p_p, axis=-1, keepdims=True) + jnp.sum(p_n, axis=-1, keepdims=True)
        acc = (_dot(p_p.astype(BF16), pv_ref[:, h, :].astype(BF16))
               + _dot(p_n.astype(BF16), v_ref[:, 2 * h * C_DV:(2 * h + 1) * C_DV]))
        o_ref[:, h * C_DV:(h + 1) * C_DV] = _attn_finish(acc, l, lam, g_ref[...], lam_init, tq).astype(BF16)


def _attn_sample(qn, kn, vb, past_k, past_v, bias_p, bias_n, lp, g, layer, *, bsz, t, lam_init):
    n = bsz * t
    w = 2 * C_DQK
    past = past_k.shape[2]
    full = lambda a: pl.BlockSpec(a.shape, lambda b: (0,) * a.ndim)
    return pl.pallas_call(
        functools.partial(_attn_sample_body, tq=t, lam_init=lam_init),
        grid=(bsz,),
        in_specs=[
            pl.BlockSpec((t, C_HEADS * w), lambda b: (b, 0)),
            pl.BlockSpec((t, C_HEADS * w), lambda b: (b, 0)),
            pl.BlockSpec((t, 2 * C_HEADS * C_DV), lambda b: (b, 0)),
            pl.BlockSpec((None, None, past, C_HEADS, w), lambda b: (layer, b, 0, 0, 0)),
            pl.BlockSpec((None, None, past, C_HEADS, C_DV), lambda b: (layer, b, 0, 0, 0)),
            full(bias_p), full(bias_n),
            pl.BlockSpec((4, C_DQK), lambda b: (0, 0)),
            pl.BlockSpec((1, C_DV), lambda b: (0, 0)),
        ],
        out_specs=pl.BlockSpec((t, C_HEADS * C_DV), lambda b: (b, 0)),
        out_shape=jax.ShapeDtypeStruct((n, C_HEADS * C_DV), BF16),
        compiler_params=_params("parallel"),
        name="attn_sample",
    )(qn, kn, vb, past_k, past_v, bias_p, bias_n, lp, g)


def _memkv_body(mem_ref, g_ref, wk_ref, wv_ref, kg_ref, k_ref, v_ref, kb_ref, vb_ref):
    hm = _rms(mem_ref[0], g_ref[...]).astype(BF16)
    k = _dot(hm, wk_ref[...])
    v = _dot(hm, wv_ref[...])
    for h in range(M_HEADS):
        kn = _rms(k[:, h * M_DH:(h + 1) * M_DH], kg_ref[...])
        k_ref[0, :, h * M_DH:(h + 1) * M_DH] = kn
        kb_ref[0, :, h * M_DH:(h + 1) * M_DH] = kn.astype(BF16)
    v_ref[0] = v
    vb_ref[0] = v.astype(BF16)


def _memkv(mem, g, wk, wv, kg):
    bsz, m, _ = mem.shape
    w = M_HEADS * M_DH
    blk = pl.BlockSpec((1, m, w), lambda b: (b, 0, 0))
    return pl.pallas_call(
        _memkv_body,
        grid=(bsz,),
        in_specs=[
            pl.BlockSpec((1, m, D_MODEL), lambda b: (b, 0, 0)),
            pl.BlockSpec((1, D_MODEL), lambda b: (0, 0)),
            pl.BlockSpec((D_MODEL, w), lambda b: (0, 0)),
            pl.BlockSpec((D_MODEL, w), lambda b: (0, 0)),
            pl.BlockSpec((1, M_DH), lambda b: (0, 0)),
        ],
        out_specs=[blk, blk, blk, blk],
        out_shape=[jax.ShapeDtypeStruct((bsz, m, w), F32), jax.ShapeDtypeStruct((bsz, m, w), F32),
                   jax.ShapeDtypeStruct((bsz, m, w), BF16), jax.ShapeDtypeStruct((bsz, m, w), BF16)],
        compiler_params=_params("parallel"),
        name="memkv",
    )(mem, g, wk, wv, kg)


def _post_body(x_ref, a_ref, bo_ref, co_ref, wo_ref, g_ref, wq_ref, qg_ref, mk_ref, mv_ref, wxo_ref,
               o_ref, att_scr, *, nseq, rps):
    y = (_dot(a_ref[...], wo_ref[0:A_WIDTH, :])
         + _dot(bo_ref[...], wo_ref[A_WIDTH:A_WIDTH + B_HEADS * B_DV, :])
         + _dot(co_ref[...], wo_ref[A_WIDTH + B_HEADS * B_DV:, :]))
    x1 = x_ref[...] + y
    q = _dot(_rms(x1, g_ref[...]).astype(BF16), wq_ref[...])
    for h in range(M_HEADS):
        sl = slice(h * M_DH, (h + 1) * M_DH)
        qn = _rms(q[:, sl], qg_ref[...]).astype(BF16)
        for s in range(nseq):
            rows = slice(s * rps, (s + 1) * rps)
            logits = _dot_nt(qn[rows], mk_ref[s, :, sl]) * (M_DH ** -0.5)
            m = jnp.max(logits, axis=-1, keepdims=True)
            p = jnp.exp(logits - m)
            l = jnp.sum(p, axis=-1, keepdims=True)
            o = _dot(p.astype(BF16), mv_ref[s, :, sl]) / l
            att_scr[rows, sl] = o.astype(BF16)
    o_ref[...] = x1 + _dot(att_scr[...], wxo_ref[...])


def _post(x, a, bo, co, wo, g, wq, qg, mk, mv, wxo, layer, *, t, tm):
    n = x.shape[0]
    w = M_HEADS * M_DH
    if t >= tm:
        nseq, rps = 1, tm
        per = t // tm
        mem_map = lambda i: (i // per, 0, 0)
    else:
        nseq, rps = tm // t, t
        mem_map = lambda i: (i, 0, 0)
    tok = lambda cols: pl.BlockSpec((tm, cols), lambda i: (i, 0))
    const = lambda r, c: pl.BlockSpec((r, c), lambda i: (0, 0), pipeline_mode=pl.Buffered(1))
    stacked = lambda r, c: pl.BlockSpec((None, r, c), lambda i: (layer, 0, 0), pipeline_mode=pl.Buffered(1))
    return pl.pallas_call(
        functools.partial(_post_body, nseq=nseq, rps=rps),
        grid=(n // tm,),
        in_specs=[
            tok(D_MODEL), tok(A_WIDTH), tok(B_HEADS * B_DV), tok(C_HEADS * C_DV),
            stacked(D_MODEL, D_MODEL), const(1, D_MODEL), stacked(D_MODEL, w), const(1, M_DH),
            pl.BlockSpec((nseq, MEM_LEN, w), mem_map),
            pl.BlockSpec((nseq, MEM_LEN, w), mem_map),
            stacked(w, D_MODEL),
        ],
        out_specs=tok(D_MODEL),
        out_shape=jax.ShapeDtypeStruct((n, D_MODEL), F32),
        scratch_shapes=[pltpu.VMEM((tm, w), BF16)],
        compiler_params=_params("parallel"),
        name="post",
    )(x, a, bo, co, wo, g, wq, qg, mk, mv, wxo)


def _ffn_body(x_ref, g_ref, wv_ref, wg_ref, cwv_ref, cwg_ref, cbv_ref, cbg_ref, hv_ref, hg_ref, wd_ref,
              o_ref, nv_ref, ng_ref, h_scr, ubuf, tail, *, nseq, rps, per):
    i = pl.program_id(0)
    j = pl.program_id(1)

    @pl.when(j == 0)
    def _():
        x = x_ref[...]
        h_scr[...] = _rms(x, g_ref[...]).astype(BF16)
        o_ref[...] = x

    if per > 1:
        @pl.when((i == 0) & (j == 0))
        def _():
            tail[...] = jnp.zeros(tail.shape, F32)

    seq_start = (i % per) == 0
    stride = rps + F_HIST
    tm = x_ref.shape[0]
    rc = tm // FFN_CHAINS
    assert rc % rps == 0 or rps % rc == 0

    def conv(half, u, r0, cw_ref, cb_ref, hist_ref, new_ref):
        outs = []
        seg = min(rc, rps)
        for q in range(rc // seg):
            s, o = divmod(r0 + q * seg, rps)
            base = s * stride
            if o == 0:
                if per == 1:
                    prev = hist_ref[s]
                else:
                    prev = jnp.where(seq_start, hist_ref[s], tail[half, j])
                ubuf[half, base:base + F_HIST, :] = prev
            us = u[q * seg:(q + 1) * seg]
            ubuf[half, base + F_HIST + o:base + F_HIST + o + seg, :] = us
            c = cb_ref[...] + us * cw_ref[F_KW - 1:F_KW, :]
            for k in range(F_KW - 1):
                off = base + F_HIST + o - (F_KW - 1) + k
                c = c + ubuf[half, off:off + seg, :] * cw_ref[k:k + 1, :]
            outs.append(c)
            if o + seg == rps:
                last = ubuf[half, base + rps:base + stride, :]
                new_ref[s] = last
                if per > 1:
                    tail[half, j] = last
        return outs[0] if len(outs) == 1 else jnp.concatenate(outs, axis=0)

    chains = [slice(r * rc, (r + 1) * rc) for r in range(FFN_CHAINS)]
    us = [(_dot(h_scr[rows, :], wv_ref[...]), _dot(h_scr[rows, :], wg_ref[...])) for rows in chains]
    for rows, (uv, ug) in zip(chains, us):
        val = conv(0, uv, rows.start, cwv_ref, cbv_ref, hv_ref, nv_ref)
        gate = conv(1, ug, rows.start, cwg_ref, cbg_ref, hg_ref, ng_ref)
        act = (gate * jax.nn.sigmoid(gate) * val).astype(BF16)
        o_ref[rows, :] += _dot(act, wd_ref[...])


def _ffn(x, g, w_up, cw, cb, hist, w_down, layer, *, t, tm=1024, tn=512):
    n = x.shape[0]
    nj = D_FF // tn
    if t >= tm:
        nseq, rps, per = 1, tm, t // tm
        seq_map = lambda i: i // per
    else:
        nseq, rps, per = tm // t, t, 1
        seq_map = lambda i: i
    bsz = hist.shape[0]
    hist_spec = lambda off: pl.BlockSpec((nseq, F_HIST, tn), lambda i, j: (seq_map(i), 0, j + off))
    new_spec = pl.BlockSpec((nseq, F_HIST, tn), lambda i, j: (i, 0, j))
    x_out, tail_v, tail_g = pl.pallas_call(
        functools.partial(_ffn_body, nseq=nseq, rps=rps, per=per),
        grid=(n // tm, nj),
        in_specs=[
            pl.BlockSpec((tm, D_MODEL), lambda i, j: (i, 0), pipeline_mode=pl.Buffered(1)),
            pl.BlockSpec((1, D_MODEL), lambda i, j: (0, 0)),
            pl.BlockSpec((None, D_MODEL, tn), lambda i, j: (layer, 0, j)),
            pl.BlockSpec((None, D_MODEL, tn), lambda i, j: (layer, 0, j + nj)),
            pl.BlockSpec((F_KW, tn), lambda i, j: (0, j)),
            pl.BlockSpec((F_KW, tn), lambda i, j: (0, j + nj)),
            pl.BlockSpec((1, tn), lambda i, j: (0, j)),
            pl.BlockSpec((1, tn), lambda i, j: (0, j + nj)),
            hist_spec(0), hist_spec(nj),
            pl.BlockSpec((None, tn, D_MODEL), lambda i, j: (layer, j, 0)),
        ],
        out_specs=[pl.BlockSpec((tm, D_MODEL), lambda i, j: (i, 0)), new_spec, new_spec],
        out_shape=[jax.ShapeDtypeStruct((n, D_MODEL), F32),
                   jax.ShapeDtypeStruct((bsz * per, F_HIST, D_FF), F32),
                   jax.ShapeDtypeStruct((bsz * per, F_HIST, D_FF), F32)],
        scratch_shapes=[
            pltpu.VMEM((tm, D_MODEL), BF16),
            pltpu.VMEM((2, nseq * (rps + F_HIST), tn), F32),
            pltpu.VMEM((2, nj, F_HIST, tn), F32),
        ],
        compiler_params=_params("arbitrary", "arbitrary"),
        name="ffn",
    )(x, g, w_up, w_up, cw, cw, cb, cb, hist, hist, w_down)
    last = lambda a: a.reshape(bsz, per, F_HIST, D_FF)[:, per - 1]
    return x_out, last(tail_v), last(tail_g)


def _rope_tables(pos0, t):
    half = B_DK // 2
    inv = 1.0 / (ROPE_BASE ** (jnp.arange(half, dtype=F32) / half))
    ang = (pos0 + jnp.arange(t, dtype=jnp.int32)).astype(F32)[:, None] * inv[None, :]
    cos = jnp.cos(ang)
    sin = jnp.sin(ang)
    return jnp.concatenate([cos, cos], axis=-1), jnp.concatenate([-sin, sin], axis=-1)


def _pad_rows(h, rows):
    return jnp.pad(h, ((0, 0), (rows - h.shape[1], 0), (0, 0)))


def _row(v):
    return v.reshape(1, -1).astype(F32)


def kernel(x_prompt, x_sample, mem_prompt, state_conv_a, state_ret, cache_diff_k, cache_diff_v, cache_mem_k,
           cache_mem_v, state_conv_f, norm1_g, w_in, conv_a_w, conv_a_b, ln_a_g, ln_a_b, ret_gn_g, diff_qn_g,
           diff_kn_g, diff_lq1, diff_lk1, diff_lq2, diff_lk2, diff_subln_g, w_out, rel_bias, norm2_g, mem_norm_g,
           w_xq, w_xk, w_xv, xqn_g, xkn_g, w_xo, norm3_g, w_up, conv_f_w, conv_f_b, w_down):
    bp, tp, _ = x_prompt.shape
    bs, ts, _ = x_sample.shape
    depth = w_in.shape[0]
    past = cache_diff_k.shape[2]
    cw_dim = C_HEADS * 2 * C_DQK

    tq = ATT_TILE
    r = np.arange(tq)
    rel_diag = r[None, :] - r[:, None]
    idx_p = np.stack([_t5_bucket(rel_diag), _t5_bucket(rel_diag - tq)]).reshape(2 * tq, tq)
    vis_p = np.stack([(r[None, :] // CHUNK) <= (r[:, None] // CHUNK), np.ones((tq, tq), bool)])
    vis_p = vis_p.reshape(2 * tq, tq).astype(np.int32)
    bias_p = _bias_tiles(rel_bias, jnp.asarray(idx_p), jnp.asarray(vis_p)).reshape(C_HEADS, 2, tq, tq)
    rel_s = np.arange(past + ts)[None, :] - (past + np.arange(ts))[:, None]
    bias_s = _bias_tiles(rel_bias, jnp.asarray(_t5_bucket(rel_s)), jnp.ones(rel_s.shape, jnp.int32))
    bias_s_past, bias_s_new = bias_s[:, :, :past], bias_s[:, :, past:]

    cos_p, sin_p = _rope_tables(0, tp)
    cos_s, sin_s = _rope_tables(past, ts)

    xp = x_prompt.reshape(bp * tp, D_MODEL)
    xs = x_sample.reshape(bs * ts, D_MODEL)
    zero_a = jnp.zeros((bp, A_HIST, A_WIDTH), F32)
    zero_r = jnp.zeros((bp, B_HEADS, B_DK, B_DV), F32)
    zero_f = jnp.zeros((bp, F_HIST, 2 * D_FF), F32)

    outs = {k: [] for k in ("p_ca", "p_rs", "p_k", "p_v", "p_mk", "p_mv", "p_cf", "s_ca", "s_rs", "s_k", "s_v", "s_cf")}
    w_in_b = w_in.astype(BF16)
    w_out_b = w_out.astype(BF16)
    w_xq_b = w_xq.astype(BF16)
    w_xo_b = w_xo.astype(BF16)
    w_up_b = w_up.astype(BF16)
    w_down_b = w_down.astype(BF16)
    for l in range(depth):
        lam_init = 0.8 - 0.6 * math.exp(-0.3 * l)
        lp = jnp.stack([diff_lq1[l], diff_lk1[l], diff_lq2[l], diff_lk2[l]]).astype(F32)
        qg2 = _row(jnp.concatenate([diff_qn_g[l], diff_qn_g[l]]))
        kg2 = _row(jnp.concatenate([diff_kn_g[l], diff_kn_g[l]]))
        mix_args = (conv_a_w[l], _row(conv_a_b[l]), _row(ln_a_g[l]), _row(ln_a_b[l]), _row(ret_gn_g[l]), qg2, kg2)
        subln = _row(diff_subln_g[l])

        def block(x, bsz, t, tt, hist_a, state_r, cos2, sin2, attend, mk_b, mv_b, hist_f, tm_post):
            if tt >= FUSED_PROJ_MIN_ROWS:
                src = (x, _row(norm1_g[l]), w_in_b, l)
            else:
                src = _in_proj(x, _row(norm1_g[l]), w_in_b, l)
            a, bo, qn, kn, vb, ck, cv, nh, nr = _mixer(src, hist_a, state_r, cos2, sin2, *mix_args,
                                                        bsz=bsz, t=t, tt=tt)
            co = attend(qn, kn, vb)
            x = _post(x, a, bo, co, w_out_b, _row(norm2_g[l]), w_xq_b, _row(xqn_g[l]), mk_b, mv_b, w_xo_b,
                      l, t=t, tm=tm_post)
            x, nfv, nfg = _ffn(x, _row(norm3_g[l]), w_up_b, conv_f_w[l], _row(conv_f_b[l]), hist_f, w_down_b, l,
                               t=t)
            new_f = jnp.concatenate([nfv[:, F_HIST - (F_KW - 1):], nfg[:, F_HIST - (F_KW - 1):]], axis=-1)
            return x, nh[:, A_HIST - (A_KW - 1):], nr, ck, cv, new_f

        mk, mv, mk_b, mv_b = _memkv(mem_prompt, _row(mem_norm_g[l]), w_xk[l].astype(BF16), w_xv[l].astype(BF16),
                                    _row(xkn_g[l]))
        attend_p = lambda qn, kn, vb: _attn_prompt(qn, kn, vb, bias_p, lp, subln, bsz=bp, t=tp, lam_init=lam_init)
        xp, ca, rs, kn_, vn_, cf = block(xp, bp, tp, 256, zero_a, zero_r, cos_p, sin_p, attend_p, mk_b, mv_b,
                                         zero_f, 256)
        outs["p_ca"].append(ca); outs["p_rs"].append(rs); outs["p_k"].append(kn_); outs["p_v"].append(vn_)
        outs["p_mk"].append(mk.reshape(bp, MEM_LEN, M_HEADS, M_DH))
        outs["p_mv"].append(mv.reshape(bp, MEM_LEN, M_HEADS, M_DH))
        outs["p_cf"].append(cf)

        attend_s = lambda qn, kn, vb: _attn_sample(qn, kn, vb, cache_diff_k, cache_diff_v, bias_s_past,
                                                   bias_s_new, lp, subln, l,
                                                   bsz=bs, t=ts, lam_init=lam_init)
        smk = cache_mem_k[l].reshape(bs, MEM_LEN, M_HEADS * M_DH).astype(BF16)
        smv = cache_mem_v[l].reshape(bs, MEM_LEN, M_HEADS * M_DH).astype(BF16)
        xs, sca, srs, skn, svn, scf = block(xs, bs, ts, ts, _pad_rows(state_conv_a[l], A_HIST), state_ret[l],
                                            cos_s, sin_s, attend_s, smk, smv, _pad_rows(state_conv_f[l], F_HIST), 256)
        outs["s_ca"].append(sca); outs["s_rs"].append(srs); outs["s_k"].append(skn); outs["s_v"].append(svn)
        outs["s_cf"].append(scf)

    st = lambda k: jnp.stack(outs[k])
    return (xp.reshape(bp, tp, D_MODEL), xs.reshape(bs, ts, D_MODEL),
            st("p_ca"), st("p_rs"), st("p_k"), st("p_v"), st("p_mk"), st("p_mv"), st("p_cf"),
            st("s_ca"), st("s_rs"), st("s_k"), st("s_v"), st("s_cf"))
```

```python
import functools
import math

import numpy as np
import jax
import jax.numpy as jnp
from jax import lax
from jax.experimental import pallas as pl
from jax.experimental.pallas import tpu as pltpu

F32 = jnp.float32
BF16 = jnp.bfloat16
EPS = 1e-6
NEG_INF = -1e30

D_MODEL = 2048
CHUNK = 64
A_WIDTH = 512
A_KW = 31
A_HIST = 32
B_HEADS = 4
B_DK = 128
B_DV = 256
ROPE_BASE = 10000.0
C_HEADS = 4
C_DQK = 64
C_DV = 128
REL_BUCKETS = 32
REL_MAX_DIST = 128
M_HEADS = 4
M_DH = 128
MEM_LEN = 256
D_FF = 5632
F_KW = 3
F_HIST = 8

A_COLS = 2 * A_WIDTH
B_COLS = B_HEADS * (2 * B_DK + 2 * B_DV)
C_COLS = C_HEADS * (4 * C_DQK + C_DV)
IN_COLS = A_COLS + B_COLS + C_COLS
B_OFF = A_COLS
C_OFF = A_COLS + B_COLS
PROJ_GROUPS = (0, A_WIDTH, A_COLS, B_OFF + 2 * B_HEADS * B_DK, B_OFF + 2 * B_HEADS * B_DK + B_HEADS * B_DV, C_OFF,
               C_OFF + 2 * C_HEADS * 2 * C_DQK, IN_COLS)

VMEM_LIMIT_BYTES = 56 * 1024 * 1024
ATT_TILE = 512
ATT_ROWS = 256
ATT_LANES = 128
FFN_CHAINS = 2
POST_TILE = 512
POST_CHAINS = 2
FUSED_PROJ_MIN_ROWS = 128


def _params(*sem):
    return pltpu.CompilerParams(dimension_semantics=sem, vmem_limit_bytes=VMEM_LIMIT_BYTES)


def _rms(x, g):
    return x * lax.rsqrt(jnp.mean(x * x, axis=-1, keepdims=True) + EPS) * g


def _dot(a, b):
    return jnp.dot(a, b, preferred_element_type=F32)


def _dot_nt(a, b):
    return lax.dot_general(a, b, (((1,), (1,)), ((), ())), preferred_element_type=F32)


def _dot_tn(a, b):
    return lax.dot_general(a, b, (((0,), (0,)), ((), ())), preferred_element_type=F32)


def _in_proj_body(x_ref, g_ref, w_ref, o_ref, h_ref):
    @pl.when(pl.program_id(1) == 0)
    def _():
        h_ref[...] = _rms(x_ref[...], g_ref[...]).astype(BF16)

    o_ref[...] = _dot(h_ref[...], w_ref[...])


def _in_proj(x, g, w, layer, *, tm=1024, tn=512):
    n = x.shape[0]
    cols = w.shape[2]
    return pl.pallas_call(
        _in_proj_body,
        grid=(n // tm, cols // tn),
        in_specs=[
            pl.BlockSpec((tm, D_MODEL), lambda i, j: (i, 0)),
            pl.BlockSpec((1, D_MODEL), lambda i, j: (0, 0)),
            pl.BlockSpec((None, D_MODEL, tn), lambda i, j: (layer, 0, j)),
        ],
        out_specs=pl.BlockSpec((tm, tn), lambda i, j: (i, j)),
        out_shape=jax.ShapeDtypeStruct((n, cols), F32),
        scratch_shapes=[pltpu.VMEM((tm, D_MODEL), BF16)],
        compiler_params=_params("parallel", "arbitrary"),
        name="in_proj",
    )(x, g, w)


def _projected_columns(x_ref, g_ref, w_ref):
    h = _rms(x_ref[...], g_ref[...]).astype(BF16)
    cache = {}

    def cols(c0, c1):
        g0, g1 = next((a, b) for a, b in zip(PROJ_GROUPS[:-1], PROJ_GROUPS[1:]) if a <= c0 and c1 <= b)
        if g0 not in cache:
            cache[g0] = _dot(h, w_ref[:, g0:g1])
        return cache[g0][:, c0 - g0:c1 - g0]

    return cols


def _mixer_body(*refs, tt, log_gammas, fused):
    if fused:
        cols = _projected_columns(*refs[:3])
        refs = refs[3:]
    else:
        proj_ref = refs[0]
        cols = lambda c0, c1: proj_ref[:, c0:c1]
        refs = refs[1:]
    (hist_ref, state_ref, cos_ref, sin_ref, cw_ref, cb_ref, lng_ref, lnb_ref, gng_ref, qg_ref, kg_ref,
     a_ref, bo_ref, qn_ref, kn_ref, vb_ref, ck_ref, cv_ref, nh_ref, nr_ref, aext, zbuf, sret) = refs
    t = pl.program_id(1)

    @pl.when(t == 0)
    def _():
        aext[0:A_HIST, :] = hist_ref[0]
        sret[...] = state_ref[0]

    glu = cols(0, A_WIDTH) * jax.nn.sigmoid(cols(A_WIDTH, A_COLS))
    aext[A_HIST:A_HIST + tt, :] = glu
    first = A_HIST - (A_KW - 1)
    acc = jnp.zeros((tt, A_WIDTH), F32) + cb_ref[...]
    for r in range(8):
        rows = tt if r == 0 else tt + 8
        z = None
        for off in range(r, first + A_KW, 8):
            if off < first:
                continue
            term = aext[off - r:off - r + rows, :] * cw_ref[off - first:off - first + 1, :]
            z = term if z is None else z + term
        if r == 0:
            acc = acc + z
        else:
            zbuf[r - 1] = z
            acc = acc + zbuf[r - 1, r:r + tt, :]
    mu = jnp.mean(acc, axis=-1, keepdims=True)
    xc = acc - mu
    var = jnp.mean(xc * xc, axis=-1, keepdims=True)
    ln = xc * lax.rsqrt(var + EPS) * lng_ref[...] + lnb_ref[...]
    a_ref[...] = (ln * jax.nn.sigmoid(ln)).astype(BF16)

    last_rows = aext[tt:tt + A_HIST, :]
    nh_ref[0] = last_rows
    aext[0:A_HIST, :] = last_rows

    cos = cos_ref[...]
    sin = sin_ref[...]
    ri = lax.broadcasted_iota(jnp.int32, (tt, tt), 0)
    ci = lax.broadcasted_iota(jnp.int32, (tt, tt), 1)
    dij = (ri - ci).astype(F32)
    causal = ri >= ci
    rowf = lax.broadcasted_iota(jnp.int32, (tt, 1), 0).astype(F32)
    for h in range(B_HEADS):
        lg = log_gammas[h]
        q = cols(B_OFF + h * B_DK, B_OFF + (h + 1) * B_DK)
        k = cols(B_OFF + B_HEADS * B_DK + h * B_DK, B_OFF + B_HEADS * B_DK + (h + 1) * B_DK)
        voff = B_OFF + 2 * B_HEADS * B_DK
        v = cols(voff + h * B_DV, voff + (h + 1) * B_DV)
        goff = voff + B_HEADS * B_DV
        g = cols(goff + h * B_DV, goff + (h + 1) * B_DV)
        qr = q * cos + pltpu.roll(q, B_DK // 2, 1) * sin
        kr = (k * cos + pltpu.roll(k, B_DK // 2, 1) * sin) * (B_DK ** -0.5)
        qb = qr.astype(BF16)
        vb = v.astype(BF16)
        decay = jnp.where(causal, jnp.exp(lg * jnp.maximum(dij, 0.0)), 0.0)
        scores = _dot_nt(qb, kr.astype(BF16)) * decay
        inner = _dot(scores.astype(BF16), vb)
        s_old = sret[h]
        cross = _dot(qb, s_old.astype(BF16)) * jnp.exp(lg * (rowf + 1.0))
        o = inner + cross
        kd = kr * jnp.exp(lg * (tt - 1.0 - rowf))
        s_new = s_old * math.exp(lg * tt) + _dot_tn(kd.astype(BF16), vb)
        sret[h] = s_new
        nr_ref[0, h] = s_new
        y = _rms(o, gng_ref[:, h * B_DV:(h + 1) * B_DV])
        bo_ref[:, h * B_DV:(h + 1) * B_DV] = (y * (g * jax.nn.sigmoid(g))).astype(BF16)

    lane = lax.broadcasted_iota(jnp.int32, (tt, 2 * C_DQK), 1)
    lo = lane < C_DQK

    def qk_norm(x, g2):
        sq = x * x
        s_lo = jnp.sum(jnp.where(lo, sq, 0.0), axis=-1, keepdims=True)
        s_hi = jnp.sum(jnp.where(lo, 0.0, sq), axis=-1, keepdims=True)
        ms = jnp.where(lo, s_lo, s_hi) * (1.0 / C_DQK)
        return x * lax.rsqrt(ms + EPS) * g2

    w = 2 * C_DQK
    for h in range(C_HEADS):
        cq = cols(C_OFF + h * w, C_OFF + (h + 1) * w)
        ck = cols(C_OFF + C_HEADS * w + h * w, C_OFF + C_HEADS * w + (h + 1) * w)
        qn = qk_norm(cq, qg_ref[...])
        kn = qk_norm(ck, kg_ref[...])
        qn_ref[:, h * w:(h + 1) * w] = (qn * (C_DQK ** -0.5)).astype(BF16)
        kn_ref[:, h * w:(h + 1) * w] = kn.astype(BF16)
        ck_ref[pl.ds(h, tt, stride=C_HEADS), :] = kn
    cv = cols(C_OFF + 2 * C_HEADS * w, IN_COLS)
    ones = jnp.ones((tt, C_DV), BF16)
    for h in range(C_HEADS):
        cv_ref[pl.ds(h, tt, stride=C_HEADS), :] = cv[:, h * C_DV:(h + 1) * C_DV]
        vb_ref[:, 2 * h * C_DV:(2 * h + 1) * C_DV] = cv[:, h * C_DV:(h + 1) * C_DV].astype(BF16)
        vb_ref[:, (2 * h + 1) * C_DV:(2 * h + 2) * C_DV] = ones


def _mixer(src, hist, state, cos2, sin2, cw, cb, lng, lnb, gng, qg2, kg2, *, bsz, t, tt):
    n = bsz * t
    nt = t // tt
    log_gammas = tuple(math.log(1.0 - 2.0 ** (-5.0 - h)) for h in range(B_HEADS))
    tok = lambda cols: pl.BlockSpec((tt, cols), lambda b, i: (b * nt + i, 0))
    const = lambda r, c: pl.BlockSpec((r, c), lambda b, i: (0, 0))
    cw_dim = C_HEADS * 2 * C_DQK
    fused = isinstance(src, tuple)
    if fused:
        x, g1, w_in, layer = src
        src_args = (x, g1, w_in)
        src_specs = [tok(D_MODEL), const(1, D_MODEL),
                     pl.BlockSpec((None, D_MODEL, IN_COLS), lambda b, i: (layer, 0, 0), pipeline_mode=pl.Buffered(1))]
    else:
        src_args = (src,)
        src_specs = [tok(IN_COLS)]
    return pl.pallas_call(
        functools.partial(_mixer_body, tt=tt, log_gammas=log_gammas, fused=fused),
        grid=(bsz, nt),
        in_specs=src_specs + [
            pl.BlockSpec((1, A_HIST, A_WIDTH), lambda b, i: (b, 0, 0)),
            pl.BlockSpec((1, B_HEADS, B_DK, B_DV), lambda b, i: (b, 0, 0, 0)),
            pl.BlockSpec((tt, B_DK), lambda b, i: (i, 0)),
            pl.BlockSpec((tt, B_DK), lambda b, i: (i, 0)),
            const(A_KW, A_WIDTH), const(1, A_WIDTH), const(1, A_WIDTH), const(1, A_WIDTH),
            const(1, B_HEADS * B_DV), const(1, 2 * C_DQK), const(1, 2 * C_DQK),
        ],
        out_specs=[
            tok(A_WIDTH), tok(B_HEADS * B_DV), tok(cw_dim), tok(cw_dim), tok(2 * C_HEADS * C_DV),
            pl.BlockSpec((None, tt * C_HEADS, 2 * C_DQK), lambda b, i: (b, i, 0)),
            pl.BlockSpec((None, tt * C_HEADS, C_DV), lambda b, i: (b, i, 0)),
            pl.BlockSpec((1, A_HIST, A_WIDTH), lambda b, i: (b, 0, 0)),
            pl.BlockSpec((1, B_HEADS, B_DK, B_DV), lambda b, i: (b, 0, 0, 0)),
        ],
        out_shape=[
            jax.ShapeDtypeStruct((n, A_WIDTH), BF16),
            jax.ShapeDtypeStruct((n, B_HEADS * B_DV), BF16),
            jax.ShapeDtypeStruct((n, cw_dim), BF16),
            jax.ShapeDtypeStruct((n, cw_dim), BF16),
            jax.ShapeDtypeStruct((n, 2 * C_HEADS * C_DV), BF16),
            jax.ShapeDtypeStruct((bsz, t * C_HEADS, 2 * C_DQK), F32),
            jax.ShapeDtypeStruct((bsz, t * C_HEADS, C_DV), F32),
            jax.ShapeDtypeStruct((bsz, A_HIST, A_WIDTH), F32),
            jax.ShapeDtypeStruct((bsz, B_HEADS, B_DK, B_DV), F32),
        ],
        scratch_shapes=[pltpu.VMEM((A_HIST + tt, A_WIDTH), F32), pltpu.VMEM((7, tt + 8, A_WIDTH), F32),
                        pltpu.VMEM((B_HEADS, B_DK, B_DV), F32)],
        compiler_params=_params("parallel", "arbitrary"),
        name="mixer",
    )(*src_args, hist, state, cos2, sin2, cw, cb, lng, lnb, gng, qg2, kg2)


def _t5_bucket(rel):
    half = REL_BUCKETS // 2
    exact = half // 2
    n = np.abs(rel)
    large = exact + (np.log(np.maximum(n, 1).astype(np.float32) / exact) / math.log(REL_MAX_DIST / exact)
                     * (half - exact)).astype(np.int32)
    large = np.minimum(large, half - 1)
    return (np.where(rel > 0, half, 0) + np.where(n < exact, n, large)).astype(np.int32)


def _bias_body(rb_ref, idx_ref, vis_ref, o_ref):
    h = pl.program_id(0)
    idx = idx_ref[...]
    far = rb_ref[REL_BUCKETS // 2 - 1, h]
    acc = jnp.zeros(idx.shape, F32)
    for b in range(REL_BUCKETS):
        acc = jnp.where(idx == b, rb_ref[b, h], acc)
    o_ref[0] = jnp.where(vis_ref[...] != 0, acc - far, NEG_INF)


def _bias_tiles(rel_bias, idx, vis):
    r, c = idx.shape
    return pl.pallas_call(
        _bias_body,
        grid=(C_HEADS,),
        in_specs=[
            pl.BlockSpec(memory_space=pltpu.SMEM),
            pl.BlockSpec((r, c), lambda h: (0, 0)),
            pl.BlockSpec((r, c), lambda h: (0, 0)),
        ],
        out_specs=pl.BlockSpec((1, r, c), lambda h: (h, 0, 0)),
        out_shape=jax.ShapeDtypeStruct((C_HEADS, r, c), F32),
        compiler_params=_params("arbitrary"),
        name="bias_tiles",
    )(rel_bias, idx, vis)


def _lambda(lp_ref, lam_init):
    lp = lp_ref[...]
    e1 = jnp.exp(jnp.sum(lp[0:1] * lp[1:2], axis=-1, keepdims=True))
    e2 = jnp.exp(jnp.sum(lp[2:3] * lp[3:4], axis=-1, keepdims=True))
    return e1 - e2 + lam_init


def _stack_maps(q):
    lane = lax.broadcasted_iota(jnp.int32, q.shape, 1)
    zero = jnp.zeros_like(q)
    return jnp.concatenate([jnp.where(lane < C_DQK, q, zero), jnp.where(lane < C_DQK, zero, q)], axis=0)


def _attn_finish(acc, l, lam, g, lam_init, tq):
    o = acc / l
    o = o[0:tq] - lam * o[tq:2 * tq]
    return _rms(o, g) * (1.0 - lam_init)


def _attn_prompt_body(q_ref, k_ref, v_ref, bias_ref, lp_ref, g_ref, o_ref, qq_scr, m_scr, alpha_scr, p_scr, acc_scr,
                      *, tq, lam_init):
    qi = pl.program_id(2)
    qq_scr[...] = _stack_maps(q_ref[...])
    m_scr[...] = jnp.full(m_scr.shape, NEG_INF, F32)
    acc_scr[...] = jnp.zeros(acc_scr.shape, F32)

    def tile(ref, idx):
        start = idx * tq if isinstance(idx, int) else pl.multiple_of(idx * tq, tq)
        return ref[pl.ds(start, tq), :]

    nl = tq // ATT_LANES
    chunks = [slice(c * ATT_ROWS, (c + 1) * ATT_ROWS) for c in range(2 * tq // ATT_ROWS)]

    def accumulate(rows, vb):
        alpha = alpha_scr[rows, :]
        acc_scr[rows, :] = jnp.concatenate([alpha, alpha], axis=-1) * acc_scr[rows, :] + _dot(p_scr[rows, :], vb)

    def step(ki, bias_idx, first=False):
        kb = tile(k_ref, ki)
        if not first:
            v_prev = tile(v_ref, ki - 1)
        for rows in chunks:
            if not first:
                accumulate(rows, v_prev)
            s = _dot_nt(qq_scr[rows, :], kb)
            if bias_idx is not None:
                s = s + bias_ref[0, bias_idx, pl.ds(rows.start % tq, ATT_ROWS), :]
            slabs = [s[:, j * ATT_LANES:(j + 1) * ATT_LANES] for j in range(nl)]
            m_old = m_scr[rows, :]
            m_new = jnp.maximum(m_old, jnp.max(functools.reduce(jnp.maximum, slabs), axis=-1, keepdims=True))
            alpha_scr[rows, :] = jnp.exp(m_old - m_new)
            p_scr[rows, :] = jnp.concatenate([jnp.exp(sl - m_new) for sl in slabs], axis=-1).astype(BF16)
            m_scr[rows, :] = m_new

    n_far = jnp.maximum(qi - 1, 0)
    step(0, jnp.minimum(qi, 2), first=True)

    def far_body(ki, c):
        step(ki, None)
        return c

    lax.fori_loop(1, n_far, far_body, 0)

    def near_body(ki, c):
        step(ki, qi - ki)
        return c

    lax.fori_loop(jnp.maximum(n_far, 1), qi + 1, near_body, 0)

    v_last = tile(v_ref, qi)
    for rows in chunks:
        accumulate(rows, v_last)
    lam = _lambda(lp_ref, lam_init)
    acc = acc_scr[...]
    o_ref[...] = _attn_finish(acc[:, 0:C_DV], acc[:, C_DV:2 * C_DV], lam, g_ref[...], lam_init, tq).astype(BF16)


def _attn_prompt(qn, kn, vb, bias, lp, g, *, bsz, t, lam_init):
    tq = ATT_TILE
    nq = t // tq
    n = bsz * t
    w = 2 * C_DQK
    return pl.pallas_call(
        functools.partial(_attn_prompt_body, tq=tq, lam_init=lam_init),
        grid=(bsz, C_HEADS, nq),
        in_specs=[
            pl.BlockSpec((tq, w), lambda b, h, i: (b * nq + i, h)),
            pl.BlockSpec((t, w), lambda b, h, i: (b, h)),
            pl.BlockSpec((t, 2 * C_DV), lambda b, h, i: (b, h)),
            pl.BlockSpec((1, 3, tq, tq), lambda b, h, i: (h, 0, 0, 0)),
            pl.BlockSpec((4, C_DQK), lambda b, h, i: (0, 0)),
            pl.BlockSpec((1, C_DV), lambda b, h, i: (0, 0)),
        ],
        out_specs=pl.BlockSpec((tq, C_DV), lambda b, h, i: (b * nq + i, h)),
        out_shape=jax.ShapeDtypeStruct((n, C_HEADS * C_DV), BF16),
        scratch_shapes=[pltpu.VMEM((2 * tq, w), BF16), pltpu.VMEM((2 * tq, ATT_LANES), F32),
                        pltpu.VMEM((2 * tq, ATT_LANES), F32), pltpu.VMEM((2 * tq, tq), BF16),
                        pltpu.VMEM((2 * tq, 2 * C_DV), F32)],
        compiler_params=_params("parallel", "parallel", "arbitrary"),
        name="attn_prompt",
    )(qn, kn, vb, bias, lp, g)


def _attn_sample_body(q_ref, k_ref, v_ref, pk_ref, pv_ref, bp_ref, bn_ref, lp_ref, g_ref, o_ref, *, tq, lam_init):
    lam = _lambda(lp_ref, lam_init)
    w = 2 * C_DQK
    for h in range(C_HEADS):
        qq = _stack_maps(q_ref[:, h * w:(h + 1) * w])
        bp = bp_ref[h]
        bn = bn_ref[h]
        past = pk_ref.shape[0] // C_HEADS
        head_rows = pl.ds(h, past, stride=C_HEADS)
        s_p = _dot_nt(qq, pk_ref[head_rows, :].astype(BF16)) + jnp.concatenate([bp, bp], axis=0)
        s_n = _dot_nt(qq, k_ref[:, h * w:(h + 1) * w]) + jnp.concatenate([bn, bn], axis=0)
        m = jnp.maximum(jnp.max(s_p, axis=-1, keepdims=True), jnp.max(s_n, axis=-1, keepdims=True))
        p_p = jnp.exp(s_p - m)
        p_n = jnp.exp(s_n - m)
        l = jnp.sum(p_p, axis=-1, keepdims=True) + jnp.sum(p_n, axis=-1, keepdims=True)
        acc = (_dot(p_p.astype(BF16), pv_ref[head_rows, :].astype(BF16))
               + _dot(p_n.astype(BF16), v_ref[:, 2 * h * C_DV:(2 * h + 1) * C_DV]))
        o_ref[:, h * C_DV:(h + 1) * C_DV] = _attn_finish(acc, l, lam, g_ref[...], lam_init, tq).astype(BF16)


def _attn_sample(qn, kn, vb, past_k, past_v, bias_p, bias_n, lp, g, layer, *, bsz, t, lam_init):
    n = bsz * t
    w = 2 * C_DQK
    depth, _, past = past_k.shape[:3]
    past_k = past_k.reshape(depth, bsz, past * C_HEADS, w)
    past_v = past_v.reshape(depth, bsz, past * C_HEADS, C_DV)
    full = lambda a: pl.BlockSpec(a.shape, lambda b: (0,) * a.ndim)
    return pl.pallas_call(
        functools.partial(_attn_sample_body, tq=t, lam_init=lam_init),
        grid=(bsz,),
        in_specs=[
            pl.BlockSpec((t, C_HEADS * w), lambda b: (b, 0)),
            pl.BlockSpec((t, C_HEADS * w), lambda b: (b, 0)),
            pl.BlockSpec((t, 2 * C_HEADS * C_DV), lambda b: (b, 0)),
            pl.BlockSpec((None, None, past * C_HEADS, w), lambda b: (layer, b, 0, 0)),
            pl.BlockSpec((None, None, past * C_HEADS, C_DV), lambda b: (layer, b, 0, 0)),
            full(bias_p), full(bias_n),
            pl.BlockSpec((4, C_DQK), lambda b: (0, 0)),
            pl.BlockSpec((1, C_DV), lambda b: (0, 0)),
        ],
        out_specs=pl.BlockSpec((t, C_HEADS * C_DV), lambda b: (b, 0)),
        out_shape=jax.ShapeDtypeStruct((n, C_HEADS * C_DV), BF16),
        compiler_params=_params("parallel"),
        name="attn_sample",
    )(qn, kn, vb, past_k, past_v, bias_p, bias_n, lp, g)


def _memkv_body(mem_ref, g_ref, wk_ref, wv_ref, kg_ref, k_ref, v_ref, kb_ref, vb_ref):
    hm = _rms(mem_ref[0], g_ref[...]).astype(BF16)
    k = _dot(hm, wk_ref[...])
    v = _dot(hm, wv_ref[...])
    for h in range(M_HEADS):
        kn = _rms(k[:, h * M_DH:(h + 1) * M_DH], kg_ref[...])
        k_ref[0, :, h * M_DH:(h + 1) * M_DH] = kn
        kb_ref[0, :, h * M_DH:(h + 1) * M_DH] = kn.astype(BF16)
    v_ref[0] = v
    vb_ref[0] = v.astype(BF16)


def _memkv(mem, g, wk, wv, kg):
    bsz, m, _ = mem.shape
    w = M_HEADS * M_DH
    blk = pl.BlockSpec((1, m, w), lambda b: (b, 0, 0))
    return pl.pallas_call(
        _memkv_body,
        grid=(bsz,),
        in_specs=[
            pl.BlockSpec((1, m, D_MODEL), lambda b: (b, 0, 0)),
            pl.BlockSpec((1, D_MODEL), lambda b: (0, 0)),
            pl.BlockSpec((D_MODEL, w), lambda b: (0, 0)),
            pl.BlockSpec((D_MODEL, w), lambda b: (0, 0)),
            pl.BlockSpec((1, M_DH), lambda b: (0, 0)),
        ],
        out_specs=[blk, blk, blk, blk],
        out_shape=[jax.ShapeDtypeStruct((bsz, m, w), F32), jax.ShapeDtypeStruct((bsz, m, w), F32),
                   jax.ShapeDtypeStruct((bsz, m, w), BF16), jax.ShapeDtypeStruct((bsz, m, w), BF16)],
        compiler_params=_params("parallel"),
        name="memkv",
    )(mem, g, wk, wv, kg)


def _post_body(x_ref, a_ref, bo_ref, co_ref, wo_ref, g_ref, wq_ref, qg_ref, mk_ref, mv_ref, wxo_ref,
               o_ref, att_scr, *, nseq, rps):
    rc = x_ref.shape[0] // POST_CHAINS
    seg = min(rc, rps)
    for c in range(POST_CHAINS):
        rows = slice(c * rc, (c + 1) * rc)
        y = (_dot(a_ref[rows, :], wo_ref[0:A_WIDTH, :])
             + _dot(bo_ref[rows, :], wo_ref[A_WIDTH:A_WIDTH + B_HEADS * B_DV, :])
             + _dot(co_ref[rows, :], wo_ref[A_WIDTH + B_HEADS * B_DV:, :]))
        x1 = x_ref[rows, :] + y
        q = _dot(_rms(x1, g_ref[...]).astype(BF16), wq_ref[...])
        for h in range(M_HEADS):
            sl = slice(h * M_DH, (h + 1) * M_DH)
            qn = _rms(q[:, sl], qg_ref[...]).astype(BF16)
            for u in range(rc // seg):
                r0 = c * rc + u * seg
                s = r0 // rps
                logits = _dot_nt(qn[u * seg:(u + 1) * seg], mk_ref[s, :, sl]) * (M_DH ** -0.5)
                m = jnp.max(logits, axis=-1, keepdims=True)
                p = jnp.exp(logits - m)
                l = jnp.sum(p, axis=-1, keepdims=True)
                o = _dot(p.astype(BF16), mv_ref[s, :, sl]) / l
                att_scr[r0:r0 + seg, sl] = o.astype(BF16)
        o_ref[rows, :] = x1 + _dot(att_scr[rows, :], wxo_ref[...])


def _post(x, a, bo, co, wo, g, wq, qg, mk, mv, wxo, layer, *, t, tm):
    n = x.shape[0]
    w = M_HEADS * M_DH
    if t >= tm:
        nseq, rps = 1, tm
        per = t // tm
        mem_map = lambda i: (i // per, 0, 0)
    else:
        nseq, rps = tm // t, t
        mem_map = lambda i: (i, 0, 0)
    tok = lambda cols: pl.BlockSpec((tm, cols), lambda i: (i, 0))
    const = lambda r, c: pl.BlockSpec((r, c), lambda i: (0, 0), pipeline_mode=pl.Buffered(1))
    stacked = lambda r, c: pl.BlockSpec((None, r, c), lambda i: (layer, 0, 0), pipeline_mode=pl.Buffered(1))
    return pl.pallas_call(
        functools.partial(_post_body, nseq=nseq, rps=rps),
        grid=(n // tm,),
        in_specs=[
            tok(D_MODEL), tok(A_WIDTH), tok(B_HEADS * B_DV), tok(C_HEADS * C_DV),
            stacked(D_MODEL, D_MODEL), const(1, D_MODEL), stacked(D_MODEL, w), const(1, M_DH),
            pl.BlockSpec((nseq, MEM_LEN, w), mem_map),
            pl.BlockSpec((nseq, MEM_LEN, w), mem_map),
            stacked(w, D_MODEL),
        ],
        out_specs=tok(D_MODEL),
        out_shape=jax.ShapeDtypeStruct((n, D_MODEL), F32),
        scratch_shapes=[pltpu.VMEM((tm, w), BF16)],
        compiler_params=_params("parallel"),
        name="post",
    )(x, a, bo, co, wo, g, wq, qg, mk, mv, wxo)


def _ffn_body(x_ref, g_ref, wv_ref, wg_ref, cwv_ref, cwg_ref, cbv_ref, cbg_ref, hv_ref, hg_ref, wd_ref,
              o_ref, nv_ref, ng_ref, h_scr, ubuf, tail, *, nseq, rps, per):
    i = pl.program_id(0)
    j = pl.program_id(1)

    @pl.when(j == 0)
    def _():
        x = x_ref[...]
        h_scr[...] = _rms(x, g_ref[...]).astype(BF16)
        o_ref[...] = x

    if per > 1:
        @pl.when((i == 0) & (j == 0))
        def _():
            tail[...] = jnp.zeros(tail.shape, F32)

    seq_start = (i % per) == 0
    stride = rps + F_HIST
    tm = x_ref.shape[0]
    rc = tm // FFN_CHAINS
    assert rc % rps == 0 or rps % rc == 0

    def conv(half, u, r0, cw_ref, cb_ref, hist_ref, new_ref):
        outs = []
        seg = min(rc, rps)
        for q in range(rc // seg):
            s, o = divmod(r0 + q * seg, rps)
            base = s * stride
            if o == 0:
                if per == 1:
                    prev = hist_ref[s]
                else:
                    prev = jnp.where(seq_start, hist_ref[s], tail[half, j])
                ubuf[half, base:base + F_HIST, :] = prev
            us = u[q * seg:(q + 1) * seg]
            ubuf[half, base + F_HIST + o:base + F_HIST + o + seg, :] = us
            c = cb_ref[...] + us * cw_ref[F_KW - 1:F_KW, :]
            for k in range(F_KW - 1):
                off = base + F_HIST + o - (F_KW - 1) + k
                c = c + ubuf[half, off:off + seg, :] * cw_ref[k:k + 1, :]
            outs.append(c)
            if o + seg == rps:
                last = ubuf[half, base + rps:base + stride, :]
                new_ref[s] = last
                if per > 1:
                    tail[half, j] = last
        return outs[0] if len(outs) == 1 else jnp.concatenate(outs, axis=0)

    chains = [slice(r * rc, (r + 1) * rc) for r in range(FFN_CHAINS)]
    us = [(_dot(h_scr[rows, :], wv_ref[...]), _dot(h_scr[rows, :], wg_ref[...])) for rows in chains]
    for rows, (uv, ug) in zip(chains, us):
        val = conv(0, uv, rows.start, cwv_ref, cbv_ref, hv_ref, nv_ref)
        gate = conv(1, ug, rows.start, cwg_ref, cbg_ref, hg_ref, ng_ref)
        act = (gate * jax.nn.sigmoid(gate) * val).astype(BF16)
        o_ref[rows, :] += _dot(act, wd_ref[...])


def _ffn(x, g, w_up, cw, cb, hist, w_down, layer, *, t, tm=1024, tn=512):
    n = x.shape[0]
    nj = D_FF // tn
    if t >= tm:
        nseq, rps, per = 1, tm, t // tm
        seq_map = lambda i: i // per
    else:
        nseq, rps, per = tm // t, t, 1
        seq_map = lambda i: i
    bsz = hist.shape[0]
    hist_spec = lambda off: pl.BlockSpec((nseq, F_HIST, tn), lambda i, j: (seq_map(i), 0, j + off))
    new_spec = pl.BlockSpec((nseq, F_HIST, tn), lambda i, j: (i, 0, j))
    x_out, tail_v, tail_g = pl.pallas_call(
        functools.partial(_ffn_body, nseq=nseq, rps=rps, per=per),
        grid=(n // tm, nj),
        in_specs=[
            pl.BlockSpec((tm, D_MODEL), lambda i, j: (i, 0), pipeline_mode=pl.Buffered(1)),
            pl.BlockSpec((1, D_MODEL), lambda i, j: (0, 0)),
            pl.BlockSpec((None, D_MODEL, tn), lambda i, j: (layer, 0, j)),
            pl.BlockSpec((None, D_MODEL, tn), lambda i, j: (layer, 0, j + nj)),
            pl.BlockSpec((F_KW, tn), lambda i, j: (0, j)),
            pl.BlockSpec((F_KW, tn), lambda i, j: (0, j + nj)),
            pl.BlockSpec((1, tn), lambda i, j: (0, j)),
            pl.BlockSpec((1, tn), lambda i, j: (0, j + nj)),
            hist_spec(0), hist_spec(nj),
            pl.BlockSpec((None, tn, D_MODEL), lambda i, j: (layer, j, 0)),
        ],
        out_specs=[pl.BlockSpec((tm, D_MODEL), lambda i, j: (i, 0)), new_spec, new_spec],
        out_shape=[jax.ShapeDtypeStruct((n, D_MODEL), F32),
                   jax.ShapeDtypeStruct((bsz * per, F_HIST, D_FF), F32),
                   jax.ShapeDtypeStruct((bsz * per, F_HIST, D_FF), F32)],
        scratch_shapes=[
            pltpu.VMEM((tm, D_MODEL), BF16),
            pltpu.VMEM((2, nseq * (rps + F_HIST), tn), F32),
            pltpu.VMEM((2, nj, F_HIST, tn), F32),
        ],
        compiler_params=_params("arbitrary", "arbitrary"),
        name="ffn",
    )(x, g, w_up, w_up, cw, cw, cb, cb, hist, hist, w_down)
    last = lambda a: a.reshape(bsz, per, F_HIST, D_FF)[:, per - 1]
    return x_out, last(tail_v), last(tail_g)


def _rope_tables(pos0, t):
    half = B_DK // 2
    inv = 1.0 / (ROPE_BASE ** (jnp.arange(half, dtype=F32) / half))
    ang = (pos0 + jnp.arange(t, dtype=jnp.int32)).astype(F32)[:, None] * inv[None, :]
    cos = jnp.cos(ang)
    sin = jnp.sin(ang)
    return jnp.concatenate([cos, cos], axis=-1), jnp.concatenate([-sin, sin], axis=-1)


def _pad_rows(h, rows):
    return jnp.pad(h, ((0, 0), (rows - h.shape[1], 0), (0, 0)))


def _row(v):
    return v.reshape(1, -1).astype(F32)


def kernel(x_prompt, x_sample, mem_prompt, state_conv_a, state_ret, cache_diff_k, cache_diff_v, cache_mem_k,
           cache_mem_v, state_conv_f, norm1_g, w_in, conv_a_w, conv_a_b, ln_a_g, ln_a_b, ret_gn_g, diff_qn_g,
           diff_kn_g, diff_lq1, diff_lk1, diff_lq2, diff_lk2, diff_subln_g, w_out, rel_bias, norm2_g, mem_norm_g,
           w_xq, w_xk, w_xv, xqn_g, xkn_g, w_xo, norm3_g, w_up, conv_f_w, conv_f_b, w_down):
    bp, tp, _ = x_prompt.shape
    bs, ts, _ = x_sample.shape
    depth = w_in.shape[0]
    past = cache_diff_k.shape[2]
    cw_dim = C_HEADS * 2 * C_DQK

    tq = ATT_TILE
    r = np.arange(tq)
    rel_diag = r[None, :] - r[:, None]
    idx_p = np.stack([_t5_bucket(rel_diag - d * tq) for d in range(3)]).reshape(3 * tq, tq)
    vis_p = np.stack([(r[None, :] // CHUNK) <= (r[:, None] // CHUNK), np.ones((tq, tq), bool), np.ones((tq, tq), bool)])
    vis_p = vis_p.reshape(3 * tq, tq).astype(np.int32)
    bias_p = _bias_tiles(rel_bias, jnp.asarray(idx_p), jnp.asarray(vis_p)).reshape(C_HEADS, 3, tq, tq)
    rel_s = np.arange(past + ts)[None, :] - (past + np.arange(ts))[:, None]
    bias_s = _bias_tiles(rel_bias, jnp.asarray(_t5_bucket(rel_s)), jnp.ones(rel_s.shape, jnp.int32))
    bias_s_past, bias_s_new = bias_s[:, :, :past], bias_s[:, :, past:]

    cos_p, sin_p = _rope_tables(0, tp)
    cos_s, sin_s = _rope_tables(past, ts)

    xp = x_prompt.reshape(bp * tp, D_MODEL)
    xs = x_sample.reshape(bs * ts, D_MODEL)
    zero_a = jnp.zeros((bp, A_HIST, A_WIDTH), F32)
    zero_r = jnp.zeros((bp, B_HEADS, B_DK, B_DV), F32)
    zero_f = jnp.zeros((bp, F_HIST, 2 * D_FF), F32)

    outs = {k: [] for k in ("p_ca", "p_rs", "p_k", "p_v", "p_mk", "p_mv", "p_cf", "s_ca", "s_rs", "s_k", "s_v", "s_cf")}
    w_in_b = w_in.astype(BF16)
    w_out_b = w_out.astype(BF16)
    w_xq_b = w_xq.astype(BF16)
    w_xo_b = w_xo.astype(BF16)
    w_up_b = w_up.astype(BF16)
    w_down_b = w_down.astype(BF16)
    for l in range(depth):
        lam_init = 0.8 - 0.6 * math.exp(-0.3 * l)
        lp = jnp.stack([diff_lq1[l], diff_lk1[l], diff_lq2[l], diff_lk2[l]]).astype(F32)
        qg2 = _row(jnp.concatenate([diff_qn_g[l], diff_qn_g[l]]))
        kg2 = _row(jnp.concatenate([diff_kn_g[l], diff_kn_g[l]]))
        mix_args = (conv_a_w[l], _row(conv_a_b[l]), _row(ln_a_g[l]), _row(ln_a_b[l]), _row(ret_gn_g[l]), qg2, kg2)
        subln = _row(diff_subln_g[l])

        def block(x, bsz, t, tt, hist_a, state_r, cos2, sin2, attend, mk_b, mv_b, hist_f, tm_post):
            if tt >= FUSED_PROJ_MIN_ROWS:
                src = (x, _row(norm1_g[l]), w_in_b, l)
            else:
                src = _in_proj(x, _row(norm1_g[l]), w_in_b, l)
            a, bo, qn, kn, vb, ck, cv, nh, nr = _mixer(src, hist_a, state_r, cos2, sin2, *mix_args,
                                                        bsz=bsz, t=t, tt=tt)
            co = attend(qn, kn, vb)
            x = _post(x, a, bo, co, w_out_b, _row(norm2_g[l]), w_xq_b, _row(xqn_g[l]), mk_b, mv_b, w_xo_b,
                      l, t=t, tm=tm_post)
            x, nfv, nfg = _ffn(x, _row(norm3_g[l]), w_up_b, conv_f_w[l], _row(conv_f_b[l]), hist_f, w_down_b, l,
                               t=t)
            new_f = jnp.concatenate([nfv[:, F_HIST - (F_KW - 1):], nfg[:, F_HIST - (F_KW - 1):]], axis=-1)
            return (x, nh[:, A_HIST - (A_KW - 1):], nr, ck.reshape(bsz, t, C_HEADS, 2 * C_DQK),
                    cv.reshape(bsz, t, C_HEADS, C_DV), new_f)

        mk, mv, mk_b, mv_b = _memkv(mem_prompt, _row(mem_norm_g[l]), w_xk[l].astype(BF16), w_xv[l].astype(BF16),
                                    _row(xkn_g[l]))
        attend_p = lambda qn, kn, vb: _attn_prompt(qn, kn, vb, bias_p, lp, subln, bsz=bp, t=tp, lam_init=lam_init)
        xp, ca, rs, kn_, vn_, cf = block(xp, bp, tp, 256, zero_a, zero_r, cos_p, sin_p, attend_p, mk_b, mv_b,
                                         zero_f, POST_TILE)
        outs["p_ca"].append(ca); outs["p_rs"].append(rs); outs["p_k"].append(kn_); outs["p_v"].append(vn_)
        outs["p_mk"].append(mk.reshape(bp, MEM_LEN, M_HEADS, M_DH))
        outs["p_mv"].append(mv.reshape(bp, MEM_LEN, M_HEADS, M_DH))
        outs["p_cf"].append(cf)

        attend_s = lambda qn, kn, vb: _attn_sample(qn, kn, vb, cache_diff_k, cache_diff_v, bias_s_past,
                                                   bias_s_new, lp, subln, l,
                                                   bsz=bs, t=ts, lam_init=lam_init)
        smk = cache_mem_k[l].reshape(bs, MEM_LEN, M_HEADS * M_DH).astype(BF16)
        smv = cache_mem_v[l].reshape(bs, MEM_LEN, M_HEADS * M_DH).astype(BF16)
        xs, sca, srs, skn, svn, scf = block(xs, bs, ts, ts, _pad_rows(state_conv_a[l], A_HIST), state_ret[l],
                                            cos_s, sin_s, attend_s, smk, smv, _pad_rows(state_conv_f[l], F_HIST),
                                            POST_TILE)
        outs["s_ca"].append(sca); outs["s_rs"].append(srs); outs["s_k"].append(skn); outs["s_v"].append(svn)
        outs["s_cf"].append(scf)

    st = lambda k: jnp.stack(outs[k])
    return (xp.reshape(bp, tp, D_MODEL), xs.reshape(bs, ts, D_MODEL),
            st("p_ca"), st("p_rs"), st("p_k"), st("p_v"), st("p_mk"), st("p_mv"), st("p_cf"),
            st("s_ca"), st("s_rs"), st("s_k"), st("s_v"), st("s_cf"))
```

```python
import functools
import math

import numpy as np
import jax
import jax.numpy as jnp
from jax import lax
from jax.experimental import pallas as pl
from jax.experimental.pallas import tpu as pltpu

F32 = jnp.float32
BF16 = jnp.bfloat16
EPS = 1e-6
NEG_INF = -1e30

D_MODEL = 2048
CHUNK = 64
A_WIDTH = 512
A_KW = 31
A_HIST = 32
B_HEADS = 4
B_DK = 128
B_DV = 256
ROPE_BASE = 10000.0
C_HEADS = 4
C_DQK = 64
C_DV = 128
REL_BUCKETS = 32
REL_MAX_DIST = 128
M_HEADS = 4
M_DH = 128
MEM_LEN = 256
D_FF = 5632
F_KW = 3
F_HIST = 8

A_COLS = 2 * A_WIDTH
B_COLS = B_HEADS * (2 * B_DK + 2 * B_DV)
C_COLS = C_HEADS * (4 * C_DQK + C_DV)
IN_COLS = A_COLS + B_COLS + C_COLS
B_OFF = A_COLS
C_OFF = A_COLS + B_COLS
PROJ_GROUPS = (0, A_WIDTH, A_COLS, B_OFF + 2 * B_HEADS * B_DK, B_OFF + 2 * B_HEADS * B_DK + B_HEADS * B_DV, C_OFF,
               C_OFF + 2 * C_HEADS * 2 * C_DQK, IN_COLS)

VMEM_LIMIT_BYTES = 56 * 1024 * 1024
ATT_TILE = 512
ATT_ROWS = 256
ATT_LANES = 128
FFN_CHAINS = 2
POST_TILE = 512
POST_CHAINS = 2
FUSED_PROJ_MIN_ROWS = 128


def _params(*sem):
    return pltpu.CompilerParams(dimension_semantics=sem, vmem_limit_bytes=VMEM_LIMIT_BYTES)


def _rms(x, g):
    return x * lax.rsqrt(jnp.mean(x * x, axis=-1, keepdims=True) + EPS) * g


def _dot(a, b):
    return jnp.dot(a, b, preferred_element_type=F32)


def _dot_nt(a, b):
    return lax.dot_general(a, b, (((1,), (1,)), ((), ())), preferred_element_type=F32)


def _dot_tn(a, b):
    return lax.dot_general(a, b, (((0,), (0,)), ((), ())), preferred_element_type=F32)


def _in_proj_body(x_ref, g_ref, w_ref, o_ref, h_ref):
    @pl.when(pl.program_id(1) == 0)
    def _():
        h_ref[...] = _rms(x_ref[...], g_ref[...]).astype(BF16)

    o_ref[...] = _dot(h_ref[...], w_ref[...])


def _in_proj(x, g, w, layer, *, tm=1024, tn=512):
    n = x.shape[0]
    cols = w.shape[2]
    return pl.pallas_call(
        _in_proj_body,
        grid=(n // tm, cols // tn),
        in_specs=[
            pl.BlockSpec((tm, D_MODEL), lambda i, j: (i, 0)),
            pl.BlockSpec((1, D_MODEL), lambda i, j: (0, 0)),
            pl.BlockSpec((None, D_MODEL, tn), lambda i, j: (layer, 0, j)),
        ],
        out_specs=pl.BlockSpec((tm, tn), lambda i, j: (i, j)),
        out_shape=jax.ShapeDtypeStruct((n, cols), F32),
        scratch_shapes=[pltpu.VMEM((tm, D_MODEL), BF16)],
        compiler_params=_params("parallel", "arbitrary"),
        name="in_proj",
    )(x, g, w)


def _projected_columns(x_ref, g_ref, w_ref):
    h = _rms(x_ref[...], g_ref[...]).astype(BF16)
    cache = {}

    def cols(c0, c1):
        g0, g1 = next((a, b) for a, b in zip(PROJ_GROUPS[:-1], PROJ_GROUPS[1:]) if a <= c0 and c1 <= b)
        if g0 not in cache:
            cache[g0] = _dot(h, w_ref[:, g0:g1])
        return cache[g0][:, c0 - g0:c1 - g0]

    return cols


def _mixer_body(*refs, tt, log_gammas, fused):
    if fused:
        cols = _projected_columns(*refs[:3])
        refs = refs[3:]
    else:
        proj_ref = refs[0]
        cols = lambda c0, c1: proj_ref[:, c0:c1]
        refs = refs[1:]
    (hist_ref, state_ref, cos_ref, sin_ref, cw_ref, cb_ref, lng_ref, lnb_ref, gng_ref, qg_ref, kg_ref,
     a_ref, bo_ref, qn_ref, kn_ref, vb_ref, ck_ref, cv_ref, nh_ref, nr_ref, aext, zbuf, sret) = refs
    t = pl.program_id(1)

    @pl.when(t == 0)
    def _():
        aext[0:A_HIST, :] = hist_ref[0]
        sret[...] = state_ref[0]

    glu = cols(0, A_WIDTH) * jax.nn.sigmoid(cols(A_WIDTH, A_COLS))
    aext[A_HIST:A_HIST + tt, :] = glu
    first = A_HIST - (A_KW - 1)
    acc = jnp.zeros((tt, A_WIDTH), F32) + cb_ref[...]
    for r in range(8):
        rows = tt if r == 0 else tt + 8
        z = None
        for off in range(r, first + A_KW, 8):
            if off < first:
                continue
            term = aext[off - r:off - r + rows, :] * cw_ref[off - first:off - first + 1, :]
            z = term if z is None else z + term
        if r == 0:
            acc = acc + z
        else:
            zbuf[r - 1] = z
            acc = acc + zbuf[r - 1, r:r + tt, :]
    mu = jnp.mean(acc, axis=-1, keepdims=True)
    xc = acc - mu
    var = jnp.mean(xc * xc, axis=-1, keepdims=True)
    ln = xc * lax.rsqrt(var + EPS) * lng_ref[...] + lnb_ref[...]
    a_ref[...] = (ln * jax.nn.sigmoid(ln)).astype(BF16)

    last_rows = aext[tt:tt + A_HIST, :]
    nh_ref[0] = last_rows
    aext[0:A_HIST, :] = last_rows

    cos = cos_ref[...]
    sin = sin_ref[...]
    ri = lax.broadcasted_iota(jnp.int32, (tt, tt), 0)
    ci = lax.broadcasted_iota(jnp.int32, (tt, tt), 1)
    dij = (ri - ci).astype(F32)
    causal = ri >= ci
    rowf = lax.broadcasted_iota(jnp.int32, (tt, 1), 0).astype(F32)
    for h in range(B_HEADS):
        lg = log_gammas[h]
        q = cols(B_OFF + h * B_DK, B_OFF + (h + 1) * B_DK)
        k = cols(B_OFF + B_HEADS * B_DK + h * B_DK, B_OFF + B_HEADS * B_DK + (h + 1) * B_DK)
        voff = B_OFF + 2 * B_HEADS * B_DK
        v = cols(voff + h * B_DV, voff + (h + 1) * B_DV)
        goff = voff + B_HEADS * B_DV
        g = cols(goff + h * B_DV, goff + (h + 1) * B_DV)
        qr = q * cos + pltpu.roll(q, B_DK // 2, 1) * sin
        kr = (k * cos + pltpu.roll(k, B_DK // 2, 1) * sin) * (B_DK ** -0.5)
        qb = qr.astype(BF16)
        vb = v.astype(BF16)
        decay = jnp.where(causal, jnp.exp(lg * jnp.maximum(dij, 0.0)), 0.0)
        scores = _dot_nt(qb, kr.astype(BF16)) * decay
        inner = _dot(scores.astype(BF16), vb)
        s_old = sret[h]
        cross = _dot(qb, s_old.astype(BF16)) * jnp.exp(lg * (rowf + 1.0))
        o = inner + cross
        kd = kr * jnp.exp(lg * (tt - 1.0 - rowf))
        s_new = s_old * math.exp(lg * tt) + _dot_tn(kd.astype(BF16), vb)
        sret[h] = s_new
        nr_ref[0, h] = s_new
        y = _rms(o, gng_ref[:, h * B_DV:(h + 1) * B_DV])
        bo_ref[:, h * B_DV:(h + 1) * B_DV] = (y * (g * jax.nn.sigmoid(g))).astype(BF16)

    lane = lax.broadcasted_iota(jnp.int32, (tt, 2 * C_DQK), 1)
    lo = lane < C_DQK

    def qk_norm(x, g2):
        sq = x * x
        s_lo = jnp.sum(jnp.where(lo, sq, 0.0), axis=-1, keepdims=True)
        s_hi = jnp.sum(jnp.where(lo, 0.0, sq), axis=-1, keepdims=True)
        ms = jnp.where(lo, s_lo, s_hi) * (1.0 / C_DQK)
        return x * lax.rsqrt(ms + EPS) * g2

    w = 2 * C_DQK
    for h in range(C_HEADS):
        cq = cols(C_OFF + h * w, C_OFF + (h + 1) * w)
        ck = cols(C_OFF + C_HEADS * w + h * w, C_OFF + C_HEADS * w + (h + 1) * w)
        qn = qk_norm(cq, qg_ref[...])
        kn = qk_norm(ck, kg_ref[...])
        qn_ref[:, h * w:(h + 1) * w] = (qn * (C_DQK ** -0.5)).astype(BF16)
        kn_ref[:, h * w:(h + 1) * w] = kn.astype(BF16)
        ck_ref[pl.ds(h, tt, stride=C_HEADS), :] = kn
    cv = cols(C_OFF + 2 * C_HEADS * w, IN_COLS)
    ones = jnp.ones((tt, C_DV), BF16)
    for h in range(C_HEADS):
        cv_ref[pl.ds(h, tt, stride=C_HEADS), :] = cv[:, h * C_DV:(h + 1) * C_DV]
        vb_ref[:, 2 * h * C_DV:(2 * h + 1) * C_DV] = cv[:, h * C_DV:(h + 1) * C_DV].astype(BF16)
        vb_ref[:, (2 * h + 1) * C_DV:(2 * h + 2) * C_DV] = ones


def _mixer(src, hist, state, cos2, sin2, cw, cb, lng, lnb, gng, qg2, kg2, *, bsz, t, tt):
    n = bsz * t
    nt = t // tt
    log_gammas = tuple(math.log(1.0 - 2.0 ** (-5.0 - h)) for h in range(B_HEADS))
    tok = lambda cols: pl.BlockSpec((tt, cols), lambda b, i: (b * nt + i, 0))
    const = lambda r, c: pl.BlockSpec((r, c), lambda b, i: (0, 0))
    cw_dim = C_HEADS * 2 * C_DQK
    fused = isinstance(src, tuple)
    if fused:
        x, g1, w_in, layer = src
        src_args = (x, g1, w_in)
        src_specs = [tok(D_MODEL), const(1, D_MODEL),
                     pl.BlockSpec((None, D_MODEL, IN_COLS), lambda b, i: (layer, 0, 0), pipeline_mode=pl.Buffered(1))]
    else:
        src_args = (src,)
        src_specs = [tok(IN_COLS)]
    return pl.pallas_call(
        functools.partial(_mixer_body, tt=tt, log_gammas=log_gammas, fused=fused),
        grid=(bsz, nt),
        in_specs=src_specs + [
            pl.BlockSpec((1, A_HIST, A_WIDTH), lambda b, i: (b, 0, 0)),
            pl.BlockSpec((1, B_HEADS, B_DK, B_DV), lambda b, i: (b, 0, 0, 0)),
            pl.BlockSpec((tt, B_DK), lambda b, i: (i, 0)),
            pl.BlockSpec((tt, B_DK), lambda b, i: (i, 0)),
            const(A_KW, A_WIDTH), const(1, A_WIDTH), const(1, A_WIDTH), const(1, A_WIDTH),
            const(1, B_HEADS * B_DV), const(1, 2 * C_DQK), const(1, 2 * C_DQK),
        ],
        out_specs=[
            tok(A_WIDTH), tok(B_HEADS * B_DV), tok(cw_dim), tok(cw_dim), tok(2 * C_HEADS * C_DV),
            pl.BlockSpec((None, tt * C_HEADS, 2 * C_DQK), lambda b, i: (b, i, 0)),
            pl.BlockSpec((None, tt * C_HEADS, C_DV), lambda b, i: (b, i, 0)),
            pl.BlockSpec((1, A_HIST, A_WIDTH), lambda b, i: (b, 0, 0)),
            pl.BlockSpec((1, B_HEADS, B_DK, B_DV), lambda b, i: (b, 0, 0, 0)),
        ],
        out_shape=[
            jax.ShapeDtypeStruct((n, A_WIDTH), BF16),
            jax.ShapeDtypeStruct((n, B_HEADS * B_DV), BF16),
            jax.ShapeDtypeStruct((n, cw_dim), BF16),
            jax.ShapeDtypeStruct((n, cw_dim), BF16),
            jax.ShapeDtypeStruct((n, 2 * C_HEADS * C_DV), BF16),
            jax.ShapeDtypeStruct((bsz, t * C_HEADS, 2 * C_DQK), F32),
            jax.ShapeDtypeStruct((bsz, t * C_HEADS, C_DV), F32),
            jax.ShapeDtypeStruct((bsz, A_HIST, A_WIDTH), F32),
            jax.ShapeDtypeStruct((bsz, B_HEADS, B_DK, B_DV), F32),
        ],
        scratch_shapes=[pltpu.VMEM((A_HIST + tt, A_WIDTH), F32), pltpu.VMEM((7, tt + 8, A_WIDTH), F32),
                        pltpu.VMEM((B_HEADS, B_DK, B_DV), F32)],
        compiler_params=_params("parallel", "arbitrary"),
        name="mixer",
    )(*src_args, hist, state, cos2, sin2, cw, cb, lng, lnb, gng, qg2, kg2)


def _t5_bucket(rel):
    half = REL_BUCKETS // 2
    exact = half // 2
    n = np.abs(rel)
    large = exact + (np.log(np.maximum(n, 1).astype(np.float32) / exact) / math.log(REL_MAX_DIST / exact)
                     * (half - exact)).astype(np.int32)
    large = np.minimum(large, half - 1)
    return (np.where(rel > 0, half, 0) + np.where(n < exact, n, large)).astype(np.int32)


def _bias_body(rb_ref, idx_ref, vis_ref, o_ref):
    h = pl.program_id(0)
    idx = idx_ref[...]
    far = rb_ref[REL_BUCKETS // 2 - 1, h]
    acc = jnp.zeros(idx.shape, F32)
    for b in range(REL_BUCKETS):
        acc = jnp.where(idx == b, rb_ref[b, h], acc)
    o_ref[0] = jnp.where(vis_ref[...] != 0, acc - far, NEG_INF)


def _bias_tiles(rel_bias, idx, vis):
    r, c = idx.shape
    return pl.pallas_call(
        _bias_body,
        grid=(C_HEADS,),
        in_specs=[
            pl.BlockSpec(memory_space=pltpu.SMEM),
            pl.BlockSpec((r, c), lambda h: (0, 0)),
            pl.BlockSpec((r, c), lambda h: (0, 0)),
        ],
        out_specs=pl.BlockSpec((1, r, c), lambda h: (h, 0, 0)),
        out_shape=jax.ShapeDtypeStruct((C_HEADS, r, c), F32),
        compiler_params=_params("arbitrary"),
        name="bias_tiles",
    )(rel_bias, idx, vis)


def _lambda(lp_ref, lam_init):
    lp = lp_ref[...]
    e1 = jnp.exp(jnp.sum(lp[0:1] * lp[1:2], axis=-1, keepdims=True))
    e2 = jnp.exp(jnp.sum(lp[2:3] * lp[3:4], axis=-1, keepdims=True))
    return e1 - e2 + lam_init


def _stack_maps(q):
    lane = lax.broadcasted_iota(jnp.int32, q.shape, 1)
    zero = jnp.zeros_like(q)
    return jnp.concatenate([jnp.where(lane < C_DQK, q, zero), jnp.where(lane < C_DQK, zero, q)], axis=0)


def _attn_finish(acc, l, lam, g, lam_init, tq):
    o = acc / l
    o = o[0:tq] - lam * o[tq:2 * tq]
    return _rms(o, g) * (1.0 - lam_init)


def _attn_prompt_body(q_ref, k_ref, v_ref, bias_ref, lp_ref, g_ref, o_ref, qq_scr, m_scr, alpha_scr, p_scr, acc_scr,
                      *, tq, lam_init):
    qi = pl.program_id(2)
    qq_scr[...] = _stack_maps(q_ref[...])
    m_scr[...] = jnp.full(m_scr.shape, NEG_INF, F32)
    acc_scr[...] = jnp.zeros(acc_scr.shape, F32)

    def tile(ref, idx):
        start = idx * tq if isinstance(idx, int) else pl.multiple_of(idx * tq, tq)
        return ref[pl.ds(start, tq), :]

    nl = tq // ATT_LANES
    chunks = [slice(c * ATT_ROWS, (c + 1) * ATT_ROWS) for c in range(2 * tq // ATT_ROWS)]

    def accumulate(rows, vb):
        alpha = alpha_scr[rows, :]
        acc_scr[rows, :] = jnp.concatenate([alpha, alpha], axis=-1) * acc_scr[rows, :] + _dot(p_scr[rows, :], vb)

    def step(ki, bias_idx, first=False):
        kb = tile(k_ref, ki)
        if not first:
            v_prev = tile(v_ref, ki - 1)
        for rows in chunks:
            if not first:
                accumulate(rows, v_prev)
            s = _dot_nt(qq_scr[rows, :], kb)
            if bias_idx is not None:
                s = s + bias_ref[0, bias_idx, pl.ds(rows.start % tq, ATT_ROWS), :]
            slabs = [s[:, j * ATT_LANES:(j + 1) * ATT_LANES] for j in range(nl)]
            m_old = m_scr[rows, :]
            m_new = jnp.maximum(m_old, jnp.max(functools.reduce(jnp.maximum, slabs), axis=-1, keepdims=True))
            alpha_scr[rows, :] = jnp.exp(m_old - m_new)
            p_scr[rows, :] = jnp.concatenate([jnp.exp(sl - m_new) for sl in slabs], axis=-1).astype(BF16)
            m_scr[rows, :] = m_new

    n_far = jnp.maximum(qi - 1, 0)
    step(0, jnp.minimum(qi, 2), first=True)

    def far_body(ki, c):
        step(ki, None)
        return c

    lax.fori_loop(1, n_far, far_body, 0)

    def near_body(ki, c):
        step(ki, qi - ki)
        return c

    lax.fori_loop(jnp.maximum(n_far, 1), qi + 1, near_body, 0)

    v_last = tile(v_ref, qi)
    for rows in chunks:
        accumulate(rows, v_last)
    lam = _lambda(lp_ref, lam_init)
    acc = acc_scr[...]
    o_ref[...] = _attn_finish(acc[:, 0:C_DV], acc[:, C_DV:2 * C_DV], lam, g_ref[...], lam_init, tq).astype(BF16)


def _attn_prompt(qn, kn, vb, bias, lp, g, *, bsz, t, lam_init):
    tq = ATT_TILE
    nq = t // tq
    n = bsz * t
    w = 2 * C_DQK
    return pl.pallas_call(
        functools.partial(_attn_prompt_body, tq=tq, lam_init=lam_init),
        grid=(bsz, C_HEADS, nq),
        in_specs=[
            pl.BlockSpec((tq, w), lambda b, h, i: (b * nq + i, h)),
            pl.BlockSpec((t, w), lambda b, h, i: (b, h)),
            pl.BlockSpec((t, 2 * C_DV), lambda b, h, i: (b, h)),
            pl.BlockSpec((1, 3, tq, tq), lambda b, h, i: (h, 0, 0, 0)),
            pl.BlockSpec((4, C_DQK), lambda b, h, i: (0, 0)),
            pl.BlockSpec((1, C_DV), lambda b, h, i: (0, 0)),
        ],
        out_specs=pl.BlockSpec((tq, C_DV), lambda b, h, i: (b * nq + i, h)),
        out_shape=jax.ShapeDtypeStruct((n, C_HEADS * C_DV), BF16),
        scratch_shapes=[pltpu.VMEM((2 * tq, w), BF16), pltpu.VMEM((2 * tq, ATT_LANES), F32),
                        pltpu.VMEM((2 * tq, ATT_LANES), F32), pltpu.VMEM((2 * tq, tq), BF16),
                        pltpu.VMEM((2 * tq, 2 * C_DV), F32)],
        compiler_params=_params("parallel", "parallel", "arbitrary"),
        name="attn_prompt",
    )(qn, kn, vb, bias, lp, g)


def _attn_sample_body(q_ref, k_ref, v_ref, pk_ref, pv_ref, bp_ref, bn_ref, lp_ref, g_ref, o_ref, *, tq, lam_init):
    lam = _lambda(lp_ref, lam_init)
    w = 2 * C_DQK
    for h in range(C_HEADS):
        qq = _stack_maps(q_ref[:, h * w:(h + 1) * w])
        bp = bp_ref[h]
        bn = bn_ref[h]
        past = pk_ref.shape[0] // C_HEADS
        head_rows = pl.ds(h, past, stride=C_HEADS)
        s_p = _dot_nt(qq, pk_ref[head_rows, :].astype(BF16)) + jnp.concatenate([bp, bp], axis=0)
        s_n = _dot_nt(qq, k_ref[:, h * w:(h + 1) * w]) + jnp.concatenate([bn, bn], axis=0)
        m = jnp.maximum(jnp.max(s_p, axis=-1, keepdims=True), jnp.max(s_n, axis=-1, keepdims=True))
        p_p = jnp.exp(s_p - m)
        p_n = jnp.exp(s_n - m)
        l = jnp.sum(p_p, axis=-1, keepdims=True) + jnp.sum(p_n, axis=-1, keepdims=True)
        acc = (_dot(p_p.astype(BF16), pv_ref[head_rows, :].astype(BF16))
               + _dot(p_n.astype(BF16), v_ref[:, 2 * h * C_DV:(2 * h + 1) * C_DV]))
        o_ref[:, h * C_DV:(h + 1) * C_DV] = _attn_finish(acc, l, lam, g_ref[...], lam_init, tq).astype(BF16)


def _attn_sample(qn, kn, vb, past_k, past_v, bias_p, bias_n, lp, g, layer, *, bsz, t, lam_init):
    n = bsz * t
    w = 2 * C_DQK
    depth, _, past = past_k.shape[:3]
    past_k = past_k.reshape(depth, bsz, past * C_HEADS, w)
    past_v = past_v.reshape(depth, bsz, past * C_HEADS, C_DV)
    full = lambda a: pl.BlockSpec(a.shape, lambda b: (0,) * a.ndim)
    return pl.pallas_call(
        functools.partial(_attn_sample_body, tq=t, lam_init=lam_init),
        grid=(bsz,),
        in_specs=[
            pl.BlockSpec((t, C_HEADS * w), lambda b: (b, 0)),
            pl.BlockSpec((t, C_HEADS * w), lambda b: (b, 0)),
            pl.BlockSpec((t, 2 * C_HEADS * C_DV), lambda b: (b, 0)),
            pl.BlockSpec((None, None, past * C_HEADS, w), lambda b: (layer, b, 0, 0)),
            pl.BlockSpec((None, None, past * C_HEADS, C_DV), lambda b: (layer, b, 0, 0)),
            full(bias_p), full(bias_n),
            pl.BlockSpec((4, C_DQK), lambda b: (0, 0)),
            pl.BlockSpec((1, C_DV), lambda b: (0, 0)),
        ],
        out_specs=pl.BlockSpec((t, C_HEADS * C_DV), lambda b: (b, 0)),
        out_shape=jax.ShapeDtypeStruct((n, C_HEADS * C_DV), BF16),
        compiler_params=_params("parallel"),
        name="attn_sample",
    )(qn, kn, vb, past_k, past_v, bias_p, bias_n, lp, g)


def _memkv_body(mem_ref, g_ref, wk_ref, wv_ref, kg_ref, k_ref, v_ref, kb_ref, vb_ref):
    hm = _rms(mem_ref[0], g_ref[...]).astype(BF16)
    k = _dot(hm, wk_ref[...])
    v = _dot(hm, wv_ref[...])
    for h in range(M_HEADS):
        kn = _rms(k[:, h * M_DH:(h + 1) * M_DH], kg_ref[...])
        k_ref[0, :, h * M_DH:(h + 1) * M_DH] = kn
        kb_ref[0, :, h * M_DH:(h + 1) * M_DH] = kn.astype(BF16)
    v_ref[0] = v
    vb_ref[0] = v.astype(BF16)


def _memkv(mem, g, wk, wv, kg):
    bsz, m, _ = mem.shape
    w = M_HEADS * M_DH
    blk = pl.BlockSpec((1, m, w), lambda b: (b, 0, 0))
    return pl.pallas_call(
        _memkv_body,
        grid=(bsz,),
        in_specs=[
            pl.BlockSpec((1, m, D_MODEL), lambda b: (b, 0, 0)),
            pl.BlockSpec((1, D_MODEL), lambda b: (0, 0)),
            pl.BlockSpec((D_MODEL, w), lambda b: (0, 0)),
            pl.BlockSpec((D_MODEL, w), lambda b: (0, 0)),
            pl.BlockSpec((1, M_DH), lambda b: (0, 0)),
        ],
        out_specs=[blk, blk, blk, blk],
        out_shape=[jax.ShapeDtypeStruct((bsz, m, w), F32), jax.ShapeDtypeStruct((bsz, m, w), F32),
                   jax.ShapeDtypeStruct((bsz, m, w), BF16), jax.ShapeDtypeStruct((bsz, m, w), BF16)],
        compiler_params=_params("parallel"),
        name="memkv",
    )(mem, g, wk, wv, kg)


def _post_body(x_ref, a_ref, bo_ref, co_ref, wo_ref, g_ref, wq_ref, qg_ref, mk_ref, mv_ref, wxo_ref,
               o_ref, att_scr, *, nseq, rps):
    rc = x_ref.shape[0] // POST_CHAINS
    seg = min(rc, rps)
    for c in range(POST_CHAINS):
        rows = slice(c * rc, (c + 1) * rc)
        y = (_dot(a_ref[rows, :], wo_ref[0:A_WIDTH, :])
             + _dot(bo_ref[rows, :], wo_ref[A_WIDTH:A_WIDTH + B_HEADS * B_DV, :])
             + _dot(co_ref[rows, :], wo_ref[A_WIDTH + B_HEADS * B_DV:, :]))
        x1 = x_ref[rows, :] + y
        q = _dot(_rms(x1, g_ref[...]).astype(BF16), wq_ref[...])
        for h in range(M_HEADS):
            sl = slice(h * M_DH, (h + 1) * M_DH)
            qn = _rms(q[:, sl], qg_ref[...]).astype(BF16)
            for u in range(rc // seg):
                r0 = c * rc + u * seg
                s = r0 // rps
                logits = _dot_nt(qn[u * seg:(u + 1) * seg], mk_ref[s, :, sl]) * (M_DH ** -0.5)
                m = jnp.max(logits, axis=-1, keepdims=True)
                p = jnp.exp(logits - m)
                l = jnp.sum(p, axis=-1, keepdims=True)
                o = _dot(p.astype(BF16), mv_ref[s, :, sl]) / l
                att_scr[r0:r0 + seg, sl] = o.astype(BF16)
        o_ref[rows, :] = x1 + _dot(att_scr[rows, :], wxo_ref[...])


def _post(x, a, bo, co, wo, g, wq, qg, mk, mv, wxo, layer, *, t, tm):
    n = x.shape[0]
    w = M_HEADS * M_DH
    if t >= tm:
        nseq, rps = 1, tm
        per = t // tm
        mem_map = lambda i: (i // per, 0, 0)
    else:
        nseq, rps = tm // t, t
        mem_map = lambda i: (i, 0, 0)
    tok = lambda cols: pl.BlockSpec((tm, cols), lambda i: (i, 0))
    const = lambda r, c: pl.BlockSpec((r, c), lambda i: (0, 0), pipeline_mode=pl.Buffered(1))
    stacked = lambda r, c: pl.BlockSpec((None, r, c), lambda i: (layer, 0, 0), pipeline_mode=pl.Buffered(1))
    return pl.pallas_call(
        functools.partial(_post_body, nseq=nseq, rps=rps),
        grid=(n // tm,),
        in_specs=[
            tok(D_MODEL), tok(A_WIDTH), tok(B_HEADS * B_DV), tok(C_HEADS * C_DV),
            stacked(D_MODEL, D_MODEL), const(1, D_MODEL), stacked(D_MODEL, w), const(1, M_DH),
            pl.BlockSpec((nseq, MEM_LEN, w), mem_map),
            pl.BlockSpec((nseq, MEM_LEN, w), mem_map),
            stacked(w, D_MODEL),
        ],
        out_specs=tok(D_MODEL),
        out_shape=jax.ShapeDtypeStruct((n, D_MODEL), F32),
        scratch_shapes=[pltpu.VMEM((tm, w), BF16)],
        compiler_params=_params("parallel"),
        name="post",
    )(x, a, bo, co, wo, g, wq, qg, mk, mv, wxo)


def _ffn_body(x_ref, g_ref, wv_ref, wg_ref, cwv_ref, cwg_ref, cbv_ref, cbg_ref, hv_ref, hg_ref, wd_ref,
              o_ref, nv_ref, ng_ref, h_scr, ubuf, tail, *, nseq, rps, per):
    i = pl.program_id(0)
    j = pl.program_id(1)

    @pl.when(j == 0)
    def _():
        x = x_ref[...]
        h_scr[...] = _rms(x, g_ref[...]).astype(BF16)
        o_ref[...] = x

    if per > 1:
        @pl.when((i == 0) & (j == 0))
        def _():
            tail[...] = jnp.zeros(tail.shape, F32)

    seq_start = (i % per) == 0
    stride = rps + F_HIST
    tm = x_ref.shape[0]
    rc = tm // FFN_CHAINS
    assert rc % rps == 0 or rps % rc == 0

    seg = min(rc, rps)
    chains = [slice(r * rc, (r + 1) * rc) for r in range(FFN_CHAINS)]

    def up(rows):
        return _dot(h_scr[rows, :], wv_ref[...]), _dot(h_scr[rows, :], wg_ref[...])

    def keep(us):
        for rows, pair in zip(chains, us):
            for half, u in enumerate(pair):
                for q in range(rc // seg):
                    s, o = divmod(rows.start + q * seg, rps)
                    dst = s * stride + F_HIST + o
                    ubuf[half, dst:dst + seg, :] = u[q * seg:(q + 1) * seg]

    def conv(half, r0, cw_ref, cb_ref, hist_ref, new_ref):
        outs = []
        for q in range(rc // seg):
            s, o = divmod(r0 + q * seg, rps)
            base = s * stride
            if o == 0:
                if per == 1:
                    prev = hist_ref[s]
                else:
                    prev = jnp.where(seq_start, hist_ref[s], tail[half, j])
                ubuf[half, base:base + F_HIST, :] = prev
            c = cb_ref[...]
            for k in range(F_KW):
                off = base + F_HIST + o - (F_KW - 1) + k
                c = c + ubuf[half, off:off + seg, :] * cw_ref[k:k + 1, :]
            outs.append(c)
            if o + seg == rps:
                last = ubuf[half, base + rps:base + stride, :]
                new_ref[s] = last
                if per > 1:
                    tail[half, j] = last
        return outs[0] if len(outs) == 1 else jnp.concatenate(outs, axis=0)

    def down(rows):
        val = conv(0, rows.start, cwv_ref, cbv_ref, hv_ref, nv_ref)
        gate = conv(1, rows.start, cwg_ref, cbg_ref, hg_ref, ng_ref)
        act = (gate * jax.nn.sigmoid(gate) * val).astype(BF16)
        o_ref[rows, :] += _dot(act, wd_ref[...])

    keep([up(rows) for rows in chains])
    for rows in chains:
        down(rows)


def _ffn(x, g, w_up, cw, cb, hist, w_down, layer, *, t, tm=1024, tn=512):
    n = x.shape[0]
    nj = D_FF // tn
    if t >= tm:
        nseq, rps, per = 1, tm, t // tm
        seq_map = lambda i: i // per
    else:
        nseq, rps, per = tm // t, t, 1
        seq_map = lambda i: i
    bsz = hist.shape[0]
    hist_spec = lambda off: pl.BlockSpec((nseq, F_HIST, tn), lambda i, j: (seq_map(i), 0, j + off))
    new_spec = pl.BlockSpec((nseq, F_HIST, tn), lambda i, j: (i, 0, j))
    x_out, tail_v, tail_g = pl.pallas_call(
        functools.partial(_ffn_body, nseq=nseq, rps=rps, per=per),
        grid=(n // tm, nj),
        in_specs=[
            pl.BlockSpec((tm, D_MODEL), lambda i, j: (i, 0), pipeline_mode=pl.Buffered(1)),
            pl.BlockSpec((1, D_MODEL), lambda i, j: (0, 0)),
            pl.BlockSpec((None, D_MODEL, tn), lambda i, j: (layer, 0, j)),
            pl.BlockSpec((None, D_MODEL, tn), lambda i, j: (layer, 0, j + nj)),
            pl.BlockSpec((F_KW, tn), lambda i, j: (0, j)),
            pl.BlockSpec((F_KW, tn), lambda i, j: (0, j + nj)),
            pl.BlockSpec((1, tn), lambda i, j: (0, j)),
            pl.BlockSpec((1, tn), lambda i, j: (0, j + nj)),
            hist_spec(0), hist_spec(nj),
            pl.BlockSpec((None, tn, D_MODEL), lambda i, j: (layer, j, 0)),
        ],
        out_specs=[pl.BlockSpec((tm, D_MODEL), lambda i, j: (i, 0)), new_spec, new_spec],
        out_shape=[jax.ShapeDtypeStruct((n, D_MODEL), F32),
                   jax.ShapeDtypeStruct((bsz * per, F_HIST, D_FF), F32),
                   jax.ShapeDtypeStruct((bsz * per, F_HIST, D_FF), F32)],
        scratch_shapes=[
            pltpu.VMEM((tm, D_MODEL), BF16),
            pltpu.VMEM((2, nseq * (rps + F_HIST), tn), F32),
            pltpu.VMEM((2, nj, F_HIST, tn), F32),
        ],
        compiler_params=_params("arbitrary", "arbitrary"),
        name="ffn",
    )(x, g, w_up, w_up, cw, cw, cb, cb, hist, hist, w_down)
    last = lambda a: a.reshape(bsz, per, F_HIST, D_FF)[:, per - 1]
    return x_out, last(tail_v), last(tail_g)


def _rope_tables(pos0, t):
    half = B_DK // 2
    inv = 1.0 / (ROPE_BASE ** (jnp.arange(half, dtype=F32) / half))
    ang = (pos0 + jnp.arange(t, dtype=jnp.int32)).astype(F32)[:, None] * inv[None, :]
    cos = jnp.cos(ang)
    sin = jnp.sin(ang)
    return jnp.concatenate([cos, cos], axis=-1), jnp.concatenate([-sin, sin], axis=-1)


def _pad_rows(h, rows):
    return jnp.pad(h, ((0, 0), (rows - h.shape[1], 0), (0, 0)))


def _row(v):
    return v.reshape(1, -1).astype(F32)


def kernel(x_prompt, x_sample, mem_prompt, state_conv_a, state_ret, cache_diff_k, cache_diff_v, cache_mem_k,
           cache_mem_v, state_conv_f, norm1_g, w_in, conv_a_w, conv_a_b, ln_a_g, ln_a_b, ret_gn_g, diff_qn_g,
           diff_kn_g, diff_lq1, diff_lk1, diff_lq2, diff_lk2, diff_subln_g, w_out, rel_bias, norm2_g, mem_norm_g,
           w_xq, w_xk, w_xv, xqn_g, xkn_g, w_xo, norm3_g, w_up, conv_f_w, conv_f_b, w_down):
    bp, tp, _ = x_prompt.shape
    bs, ts, _ = x_sample.shape
    depth = w_in.shape[0]
    past = cache_diff_k.shape[2]
    cw_dim = C_HEADS * 2 * C_DQK

    tq = ATT_TILE
    r = np.arange(tq)
    rel_diag = r[None, :] - r[:, None]
    idx_p = np.stack([_t5_bucket(rel_diag - d * tq) for d in range(3)]).reshape(3 * tq, tq)
    vis_p = np.stack([(r[None, :] // CHUNK) <= (r[:, None] // CHUNK), np.ones((tq, tq), bool), np.ones((tq, tq), bool)])
    vis_p = vis_p.reshape(3 * tq, tq).astype(np.int32)
    bias_p = _bias_tiles(rel_bias, jnp.asarray(idx_p), jnp.asarray(vis_p)).reshape(C_HEADS, 3, tq, tq)
    rel_s = np.arange(past + ts)[None, :] - (past + np.arange(ts))[:, None]
    bias_s = _bias_tiles(rel_bias, jnp.asarray(_t5_bucket(rel_s)), jnp.ones(rel_s.shape, jnp.int32))
    bias_s_past, bias_s_new = bias_s[:, :, :past], bias_s[:, :, past:]

    cos_p, sin_p = _rope_tables(0, tp)
    cos_s, sin_s = _rope_tables(past, ts)

    xp = x_prompt.reshape(bp * tp, D_MODEL)
    xs = x_sample.reshape(bs * ts, D_MODEL)
    zero_a = jnp.zeros((bp, A_HIST, A_WIDTH), F32)
    zero_r = jnp.zeros((bp, B_HEADS, B_DK, B_DV), F32)
    zero_f = jnp.zeros((bp, F_HIST, 2 * D_FF), F32)

    outs = {k: [] for k in ("p_ca", "p_rs", "p_k", "p_v", "p_mk", "p_mv", "p_cf", "s_ca", "s_rs", "s_k", "s_v", "s_cf")}
    w_in_b = w_in.astype(BF16)
    w_out_b = w_out.astype(BF16)
    w_xq_b = w_xq.astype(BF16)
    w_xo_b = w_xo.astype(BF16)
    w_up_b = w_up.astype(BF16)
    w_down_b = w_down.astype(BF16)
    for l in range(depth):
        lam_init = 0.8 - 0.6 * math.exp(-0.3 * l)
        lp = jnp.stack([diff_lq1[l], diff_lk1[l], diff_lq2[l], diff_lk2[l]]).astype(F32)
        qg2 = _row(jnp.concatenate([diff_qn_g[l], diff_qn_g[l]]))
        kg2 = _row(jnp.concatenate([diff_kn_g[l], diff_kn_g[l]]))
        mix_args = (conv_a_w[l], _row(conv_a_b[l]), _row(ln_a_g[l]), _row(ln_a_b[l]), _row(ret_gn_g[l]), qg2, kg2)
        subln = _row(diff_subln_g[l])

        def block(x, bsz, t, tt, hist_a, state_r, cos2, sin2, attend, mk_b, mv_b, hist_f, tm_post):
            if tt >= FUSED_PROJ_MIN_ROWS:
                src = (x, _row(norm1_g[l]), w_in_b, l)
            else:
                src = _in_proj(x, _row(norm1_g[l]), w_in_b, l)
            a, bo, qn, kn, vb, ck, cv, nh, nr = _mixer(src, hist_a, state_r, cos2, sin2, *mix_args,
                                                        bsz=bsz, t=t, tt=tt)
            co = attend(qn, kn, vb)
            x = _post(x, a, bo, co, w_out_b, _row(norm2_g[l]), w_xq_b, _row(xqn_g[l]), mk_b, mv_b, w_xo_b,
                      l, t=t, tm=tm_post)
            x, nfv, nfg = _ffn(x, _row(norm3_g[l]), w_up_b, conv_f_w[l], _row(conv_f_b[l]), hist_f, w_down_b, l,
                               t=t)
            new_f = jnp.concatenate([nfv[:, F_HIST - (F_KW - 1):], nfg[:, F_HIST - (F_KW - 1):]], axis=-1)
            return (x, nh[:, A_HIST - (A_KW - 1):], nr, ck.reshape(bsz, t, C_HEADS, 2 * C_DQK),
                    cv.reshape(bsz, t, C_HEADS, C_DV), new_f)

        mk, mv, mk_b, mv_b = _memkv(mem_prompt, _row(mem_norm_g[l]), w_xk[l].astype(BF16), w_xv[l].astype(BF16),
                                    _row(xkn_g[l]))
        attend_p = lambda qn, kn, vb: _attn_prompt(qn, kn, vb, bias_p, lp, subln, bsz=bp, t=tp, lam_init=lam_init)
        xp, ca, rs, kn_, vn_, cf = block(xp, bp, tp, 256, zero_a, zero_r, cos_p, sin_p, attend_p, mk_b, mv_b,
                                         zero_f, POST_TILE)
        outs["p_ca"].append(ca); outs["p_rs"].append(rs); outs["p_k"].append(kn_); outs["p_v"].append(vn_)
        outs["p_mk"].append(mk.reshape(bp, MEM_LEN, M_HEADS, M_DH))
        outs["p_mv"].append(mv.reshape(bp, MEM_LEN, M_HEADS, M_DH))
        outs["p_cf"].append(cf)

        attend_s = lambda qn, kn, vb: _attn_sample(qn, kn, vb, cache_diff_k, cache_diff_v, bias_s_past,
                                                   bias_s_new, lp, subln, l,
                                                   bsz=bs, t=ts, lam_init=lam_init)
        smk = cache_mem_k[l].reshape(bs, MEM_LEN, M_HEADS * M_DH).astype(BF16)
        smv = cache_mem_v[l].reshape(bs, MEM_LEN, M_HEADS * M_DH).astype(BF16)
        xs, sca, srs, skn, svn, scf = block(xs, bs, ts, ts, _pad_rows(state_conv_a[l], A_HIST), state_ret[l],
                                            cos_s, sin_s, attend_s, smk, smv, _pad_rows(state_conv_f[l], F_HIST),
                                            POST_TILE)
        outs["s_ca"].append(sca); outs["s_rs"].append(srs); outs["s_k"].append(skn); outs["s_v"].append(svn)
        outs["s_cf"].append(scf)

    st = lambda k: jnp.stack(outs[k])
    return (xp.reshape(bp, tp, D_MODEL), xs.reshape(bs, ts, D_MODEL),
            st("p_ca"), st("p_rs"), st("p_k"), st("p_v"), st("p_mk"), st("p_mv"), st("p_cf"),
            st("s_ca"), st("s_rs"), st("s_k"), st("s_v"), st("s_cf"))
```

```python
import functools
import math

import numpy as np
import jax
import jax.numpy as jnp
from jax import lax
from jax.experimental import pallas as pl
from jax.experimental.pallas import tpu as pltpu

F32 = jnp.float32
BF16 = jnp.bfloat16
EPS = 1e-6
NEG_INF = -1e30
LOG2E = math.log2(math.e)

D_MODEL = 2048
CHUNK = 64
A_WIDTH = 512
A_KW = 31
A_HIST = 32
B_HEADS = 4
B_DK = 128
B_DV = 256
ROPE_BASE = 10000.0
C_HEADS = 4
C_DQK = 64
C_DV = 128
REL_BUCKETS = 32
REL_MAX_DIST = 128
M_HEADS = 4
M_DH = 128
MEM_LEN = 256
D_FF = 5632
F_KW = 3
F_HIST = 8

A_COLS = 2 * A_WIDTH
B_COLS = B_HEADS * (2 * B_DK + 2 * B_DV)
C_COLS = C_HEADS * (4 * C_DQK + C_DV)
IN_COLS = A_COLS + B_COLS + C_COLS
B_OFF = A_COLS
C_OFF = A_COLS + B_COLS
PROJ_GROUPS = (0, A_WIDTH, A_COLS, B_OFF + 2 * B_HEADS * B_DK, B_OFF + 2 * B_HEADS * B_DK + B_HEADS * B_DV, C_OFF,
               C_OFF + 2 * C_HEADS * 2 * C_DQK, IN_COLS)

VMEM_LIMIT_BYTES = 56 * 1024 * 1024
ATT_TILE = 512
ATT_ROWS = 128
ATT_LANES = 128
FFN_CHAINS = 2
POST_TILE = 512
POST_CHAINS = 2
FUSED_PROJ_MIN_ROWS = 128


def _params(*sem):
    return pltpu.CompilerParams(dimension_semantics=sem, vmem_limit_bytes=VMEM_LIMIT_BYTES)


def _rms(x, g):
    return x * lax.rsqrt(jnp.mean(x * x, axis=-1, keepdims=True) + EPS) * g


def _dot(a, b):
    return jnp.dot(a, b, preferred_element_type=F32)


def _dot_nt(a, b):
    return lax.dot_general(a, b, (((1,), (1,)), ((), ())), preferred_element_type=F32)


def _dot_tn(a, b):
    return lax.dot_general(a, b, (((0,), (0,)), ((), ())), preferred_element_type=F32)


def _in_proj_body(x_ref, g_ref, w_ref, o_ref, h_ref):
    @pl.when(pl.program_id(1) == 0)
    def _():
        h_ref[...] = _rms(x_ref[...], g_ref[...]).astype(BF16)

    o_ref[...] = _dot(h_ref[...], w_ref[...])


def _in_proj(x, g, w, layer, *, tm=1024, tn=512):
    n = x.shape[0]
    cols = w.shape[2]
    return pl.pallas_call(
        _in_proj_body,
        grid=(n // tm, cols // tn),
        in_specs=[
            pl.BlockSpec((tm, D_MODEL), lambda i, j: (i, 0)),
            pl.BlockSpec((1, D_MODEL), lambda i, j: (0, 0)),
            pl.BlockSpec((None, D_MODEL, tn), lambda i, j: (layer, 0, j)),
        ],
        out_specs=pl.BlockSpec((tm, tn), lambda i, j: (i, j)),
        out_shape=jax.ShapeDtypeStruct((n, cols), F32),
        scratch_shapes=[pltpu.VMEM((tm, D_MODEL), BF16)],
        compiler_params=_params("parallel", "arbitrary"),
        name="in_proj",
    )(x, g, w)


def _projected_columns(x_ref, g_ref, w_ref):
    h = _rms(x_ref[...], g_ref[...]).astype(BF16)
    cache = {}

    def cols(c0, c1):
        g0, g1 = next((a, b) for a, b in zip(PROJ_GROUPS[:-1], PROJ_GROUPS[1:]) if a <= c0 and c1 <= b)
        if g0 not in cache:
            cache[g0] = _dot(h, w_ref[:, g0:g1])
        return cache[g0][:, c0 - g0:c1 - g0]

    return cols


def _mixer_body(*refs, tt, log_gammas, fused):
    if fused:
        cols = _projected_columns(*refs[:3])
        refs = refs[3:]
    else:
        proj_ref = refs[0]
        cols = lambda c0, c1: proj_ref[:, c0:c1]
        refs = refs[1:]
    (hist_ref, state_ref, cos_ref, sin_ref, cw_ref, cb_ref, lng_ref, lnb_ref, gng_ref, qg_ref, kg_ref,
     a_ref, bo_ref, qn_ref, kn_ref, vb_ref, ck_ref, cv_ref, nh_ref, nr_ref, aext, zbuf, sret) = refs
    t = pl.program_id(1)

    @pl.when(t == 0)
    def _():
        aext[0:A_HIST, :] = hist_ref[0]
        sret[...] = state_ref[0]

    glu = cols(0, A_WIDTH) * jax.nn.sigmoid(cols(A_WIDTH, A_COLS))
    aext[A_HIST:A_HIST + tt, :] = glu
    first = A_HIST - (A_KW - 1)
    acc = jnp.zeros((tt, A_WIDTH), F32) + cb_ref[...]
    for r in range(8):
        rows = tt if r == 0 else tt + 8
        z = None
        for off in range(r, first + A_KW, 8):
            if off < first:
                continue
            term = aext[off - r:off - r + rows, :] * cw_ref[off - first:off - first + 1, :]
            z = term if z is None else z + term
        if r == 0:
            acc = acc + z
        else:
            zbuf[r - 1] = z
            acc = acc + zbuf[r - 1, r:r + tt, :]
    mu = jnp.mean(acc, axis=-1, keepdims=True)
    xc = acc - mu
    var = jnp.mean(xc * xc, axis=-1, keepdims=True)
    ln = xc * lax.rsqrt(var + EPS) * lng_ref[...] + lnb_ref[...]
    a_ref[...] = (ln * jax.nn.sigmoid(ln)).astype(BF16)

    last_rows = aext[tt:tt + A_HIST, :]
    nh_ref[0] = last_rows
    aext[0:A_HIST, :] = last_rows

    cos = cos_ref[...]
    sin = sin_ref[...]
    ri = lax.broadcasted_iota(jnp.int32, (tt, tt), 0)
    ci = lax.broadcasted_iota(jnp.int32, (tt, tt), 1)
    dij = (ri - ci).astype(F32)
    causal = ri >= ci
    rowf = lax.broadcasted_iota(jnp.int32, (tt, 1), 0).astype(F32)
    for h in range(B_HEADS):
        lg = log_gammas[h]
        q = cols(B_OFF + h * B_DK, B_OFF + (h + 1) * B_DK)
        k = cols(B_OFF + B_HEADS * B_DK + h * B_DK, B_OFF + B_HEADS * B_DK + (h + 1) * B_DK)
        voff = B_OFF + 2 * B_HEADS * B_DK
        v = cols(voff + h * B_DV, voff + (h + 1) * B_DV)
        goff = voff + B_HEADS * B_DV
        g = cols(goff + h * B_DV, goff + (h + 1) * B_DV)
        qr = q * cos + pltpu.roll(q, B_DK // 2, 1) * sin
        kr = (k * cos + pltpu.roll(k, B_DK // 2, 1) * sin) * (B_DK ** -0.5)
        qb = qr.astype(BF16)
        vb = v.astype(BF16)
        decay = jnp.where(causal, jnp.exp(lg * jnp.maximum(dij, 0.0)), 0.0)
        scores = _dot_nt(qb, kr.astype(BF16)) * decay
        inner = _dot(scores.astype(BF16), vb)
        s_old = sret[h]
        cross = _dot(qb, s_old.astype(BF16)) * jnp.exp(lg * (rowf + 1.0))
        o = inner + cross
        kd = kr * jnp.exp(lg * (tt - 1.0 - rowf))
        s_new = s_old * math.exp(lg * tt) + _dot_tn(kd.astype(BF16), vb)
        sret[h] = s_new
        nr_ref[0, h] = s_new
        y = _rms(o, gng_ref[:, h * B_DV:(h + 1) * B_DV])
        bo_ref[:, h * B_DV:(h + 1) * B_DV] = (y * (g * jax.nn.sigmoid(g))).astype(BF16)

    lane = lax.broadcasted_iota(jnp.int32, (tt, 2 * C_DQK), 1)
    lo = lane < C_DQK

    def qk_norm(x, g2):
        sq = x * x
        s_lo = jnp.sum(jnp.where(lo, sq, 0.0), axis=-1, keepdims=True)
        s_hi = jnp.sum(jnp.where(lo, 0.0, sq), axis=-1, keepdims=True)
        ms = jnp.where(lo, s_lo, s_hi) * (1.0 / C_DQK)
        return x * lax.rsqrt(ms + EPS) * g2

    w = 2 * C_DQK
    for h in range(C_HEADS):
        cq = cols(C_OFF + h * w, C_OFF + (h + 1) * w)
        ck = cols(C_OFF + C_HEADS * w + h * w, C_OFF + C_HEADS * w + (h + 1) * w)
        qn = qk_norm(cq, qg_ref[...])
        kn = qk_norm(ck, kg_ref[...])
        qn_ref[:, h * w:(h + 1) * w] = (qn * (C_DQK ** -0.5 * LOG2E)).astype(BF16)
        kn_ref[:, h * w:(h + 1) * w] = kn.astype(BF16)
        ck_ref[pl.ds(h, tt, stride=C_HEADS), :] = kn
    cv = cols(C_OFF + 2 * C_HEADS * w, IN_COLS)
    ones = jnp.ones((tt, C_DV), BF16)
    for h in range(C_HEADS):
        cv_ref[pl.ds(h, tt, stride=C_HEADS), :] = cv[:, h * C_DV:(h + 1) * C_DV]
        vb_ref[:, 2 * h * C_DV:(2 * h + 1) * C_DV] = cv[:, h * C_DV:(h + 1) * C_DV].astype(BF16)
        vb_ref[:, (2 * h + 1) * C_DV:(2 * h + 2) * C_DV] = ones


def _mixer(src, hist, state, cos2, sin2, cw, cb, lng, lnb, gng, qg2, kg2, *, bsz, t, tt):
    n = bsz * t
    nt = t // tt
    log_gammas = tuple(math.log(1.0 - 2.0 ** (-5.0 - h)) for h in range(B_HEADS))
    tok = lambda cols: pl.BlockSpec((tt, cols), lambda b, i: (b * nt + i, 0))
    const = lambda r, c: pl.BlockSpec((r, c), lambda b, i: (0, 0))
    cw_dim = C_HEADS * 2 * C_DQK
    fused = isinstance(src, tuple)
    if fused:
        x, g1, w_in, layer = src
        src_args = (x, g1, w_in)
        src_specs = [tok(D_MODEL), const(1, D_MODEL),
                     pl.BlockSpec((None, D_MODEL, IN_COLS), lambda b, i: (layer, 0, 0), pipeline_mode=pl.Buffered(1))]
    else:
        src_args = (src,)
        src_specs = [tok(IN_COLS)]
    return pl.pallas_call(
        functools.partial(_mixer_body, tt=tt, log_gammas=log_gammas, fused=fused),
        grid=(bsz, nt),
        in_specs=src_specs + [
            pl.BlockSpec((1, A_HIST, A_WIDTH), lambda b, i: (b, 0, 0)),
            pl.BlockSpec((1, B_HEADS, B_DK, B_DV), lambda b, i: (b, 0, 0, 0)),
            pl.BlockSpec((tt, B_DK), lambda b, i: (i, 0)),
            pl.BlockSpec((tt, B_DK), lambda b, i: (i, 0)),
            const(A_KW, A_WIDTH), const(1, A_WIDTH), const(1, A_WIDTH), const(1, A_WIDTH),
            const(1, B_HEADS * B_DV), const(1, 2 * C_DQK), const(1, 2 * C_DQK),
        ],
        out_specs=[
            tok(A_WIDTH), tok(B_HEADS * B_DV), tok(cw_dim), tok(cw_dim), tok(2 * C_HEADS * C_DV),
            pl.BlockSpec((None, tt * C_HEADS, 2 * C_DQK), lambda b, i: (b, i, 0)),
            pl.BlockSpec((None, tt * C_HEADS, C_DV), lambda b, i: (b, i, 0)),
            pl.BlockSpec((1, A_HIST, A_WIDTH), lambda b, i: (b, 0, 0)),
            pl.BlockSpec((1, B_HEADS, B_DK, B_DV), lambda b, i: (b, 0, 0, 0)),
        ],
        out_shape=[
            jax.ShapeDtypeStruct((n, A_WIDTH), BF16),
            jax.ShapeDtypeStruct((n, B_HEADS * B_DV), BF16),
            jax.ShapeDtypeStruct((n, cw_dim), BF16),
            jax.ShapeDtypeStruct((n, cw_dim), BF16),
            jax.ShapeDtypeStruct((n, 2 * C_HEADS * C_DV), BF16),
            jax.ShapeDtypeStruct((bsz, t * C_HEADS, 2 * C_DQK), F32),
            jax.ShapeDtypeStruct((bsz, t * C_HEADS, C_DV), F32),
            jax.ShapeDtypeStruct((bsz, A_HIST, A_WIDTH), F32),
            jax.ShapeDtypeStruct((bsz, B_HEADS, B_DK, B_DV), F32),
        ],
        scratch_shapes=[pltpu.VMEM((A_HIST + tt, A_WIDTH), F32), pltpu.VMEM((7, tt + 8, A_WIDTH), F32),
                        pltpu.VMEM((B_HEADS, B_DK, B_DV), F32)],
        compiler_params=_params("parallel", "arbitrary"),
        name="mixer",
    )(*src_args, hist, state, cos2, sin2, cw, cb, lng, lnb, gng, qg2, kg2)


def _t5_bucket(rel):
    half = REL_BUCKETS // 2
    exact = half // 2
    n = np.abs(rel)
    large = exact + (np.log(np.maximum(n, 1).astype(np.float32) / exact) / math.log(REL_MAX_DIST / exact)
                     * (half - exact)).astype(np.int32)
    large = np.minimum(large, half - 1)
    return (np.where(rel > 0, half, 0) + np.where(n < exact, n, large)).astype(np.int32)


def _bias_body(rb_ref, idx_ref, vis_ref, o_ref):
    h = pl.program_id(0)
    idx = idx_ref[...]
    far = rb_ref[REL_BUCKETS // 2 - 1, h]
    acc = jnp.zeros(idx.shape, F32)
    for b in range(REL_BUCKETS):
        acc = jnp.where(idx == b, rb_ref[b, h], acc)
    o_ref[0] = jnp.where(vis_ref[...] != 0, (acc - far) * LOG2E, NEG_INF)


def _bias_tiles(rel_bias, idx, vis):
    r, c = idx.shape
    return pl.pallas_call(
        _bias_body,
        grid=(C_HEADS,),
        in_specs=[
            pl.BlockSpec(memory_space=pltpu.SMEM),
            pl.BlockSpec((r, c), lambda h: (0, 0)),
            pl.BlockSpec((r, c), lambda h: (0, 0)),
        ],
        out_specs=pl.BlockSpec((1, r, c), lambda h: (h, 0, 0)),
        out_shape=jax.ShapeDtypeStruct((C_HEADS, r, c), F32),
        compiler_params=_params("arbitrary"),
        name="bias_tiles",
    )(rel_bias, idx, vis)


def _lambda(lp_ref, lam_init):
    lp = lp_ref[...]
    e1 = jnp.exp(jnp.sum(lp[0:1] * lp[1:2], axis=-1, keepdims=True))
    e2 = jnp.exp(jnp.sum(lp[2:3] * lp[3:4], axis=-1, keepdims=True))
    return e1 - e2 + lam_init


def _stack_maps(q):
    lane = lax.broadcasted_iota(jnp.int32, q.shape, 1)
    zero = jnp.zeros_like(q)
    return jnp.concatenate([jnp.where(lane < C_DQK, q, zero), jnp.where(lane < C_DQK, zero, q)], axis=0)


def _attn_finish(acc, l, lam, g, lam_init, tq):
    o = acc / l
    o = o[0:tq] - lam * o[tq:2 * tq]
    return _rms(o, g) * (1.0 - lam_init)


def _attn_prompt_body(q_ref, k_ref, v_ref, bias_ref, lp_ref, g_ref, o_ref, qq_scr, m_scr, alpha_scr, p_scr, acc_scr,
                      *, tq, lam_init):
    qi = pl.program_id(2)
    qq_scr[...] = _stack_maps(q_ref[...])
    m_scr[...] = jnp.full(m_scr.shape, NEG_INF, F32)
    acc_scr[...] = jnp.zeros(acc_scr.shape, F32)

    def tile(ref, idx):
        start = idx * tq if isinstance(idx, int) else pl.multiple_of(idx * tq, tq)
        return ref[pl.ds(start, tq), :]

    nl = tq // ATT_LANES
    chunks = [slice(c * ATT_ROWS, (c + 1) * ATT_ROWS) for c in range(2 * tq // ATT_ROWS)]

    def accumulate(rows, vb):
        alpha = alpha_scr[rows, :]
        acc_scr[rows, :] = jnp.concatenate([alpha, alpha], axis=-1) * acc_scr[rows, :] + _dot(p_scr[rows, :], vb)

    def step(ki, bias_idx, first=False):
        kb = tile(k_ref, ki)
        if not first:
            v_prev = tile(v_ref, ki - 1)
        for rows in chunks:
            if not first:
                accumulate(rows, v_prev)
            s = _dot_nt(qq_scr[rows, :], kb)
            if bias_idx is not None:
                s = s + bias_ref[0, bias_idx, pl.ds(rows.start % tq, ATT_ROWS), :]
            slabs = [s[:, j * ATT_LANES:(j + 1) * ATT_LANES] for j in range(nl)]
            m_old = m_scr[rows, :]
            m_new = jnp.maximum(m_old, jnp.max(functools.reduce(jnp.maximum, slabs), axis=-1, keepdims=True))
            alpha_scr[rows, :] = jnp.exp2(m_old - m_new)
            p_scr[rows, :] = jnp.concatenate([jnp.exp2(sl - m_new) for sl in slabs], axis=-1).astype(BF16)
            m_scr[rows, :] = m_new

    n_far = jnp.maximum(qi - 1, 0)
    step(0, jnp.minimum(qi, 2), first=True)

    def far_body(ki, c):
        step(ki, None)
        return c

    lax.fori_loop(1, n_far, far_body, 0)

    def near_body(ki, c):
        step(ki, qi - ki)
        return c

    lax.fori_loop(jnp.maximum(n_far, 1), qi + 1, near_body, 0)

    v_last = tile(v_ref, qi)
    for rows in chunks:
        accumulate(rows, v_last)
    lam = _lambda(lp_ref, lam_init)
    acc = acc_scr[...]
    o_ref[...] = _attn_finish(acc[:, 0:C_DV], acc[:, C_DV:2 * C_DV], lam, g_ref[...], lam_init, tq).astype(BF16)


def _attn_prompt(qn, kn, vb, bias, lp, g, *, bsz, t, lam_init):
    tq = ATT_TILE
    nq = t // tq
    n = bsz * t
    w = 2 * C_DQK
    return pl.pallas_call(
        functools.partial(_attn_prompt_body, tq=tq, lam_init=lam_init),
        grid=(bsz, C_HEADS, nq),
        in_specs=[
            pl.BlockSpec((tq, w), lambda b, h, i: (b * nq + i, h)),
            pl.BlockSpec((t, w), lambda b, h, i: (b, h)),
            pl.BlockSpec((t, 2 * C_DV), lambda b, h, i: (b, h)),
            pl.BlockSpec((1, 3, tq, tq), lambda b, h, i: (h, 0, 0, 0)),
            pl.BlockSpec((4, C_DQK), lambda b, h, i: (0, 0)),
            pl.BlockSpec((1, C_DV), lambda b, h, i: (0, 0)),
        ],
        out_specs=pl.BlockSpec((tq, C_DV), lambda b, h, i: (b * nq + i, h)),
        out_shape=jax.ShapeDtypeStruct((n, C_HEADS * C_DV), BF16),
        scratch_shapes=[pltpu.VMEM((2 * tq, w), BF16), pltpu.VMEM((2 * tq, ATT_LANES), F32),
                        pltpu.VMEM((2 * tq, ATT_LANES), F32), pltpu.VMEM((2 * tq, tq), BF16),
                        pltpu.VMEM((2 * tq, 2 * C_DV), F32)],
        compiler_params=_params("parallel", "parallel", "arbitrary"),
        name="attn_prompt",
    )(qn, kn, vb, bias, lp, g)


def _attn_sample_body(q_ref, k_ref, v_ref, pk_ref, pv_ref, bp_ref, bn_ref, lp_ref, g_ref, o_ref, *, tq, lam_init):
    lam = _lambda(lp_ref, lam_init)
    w = 2 * C_DQK
    for h in range(C_HEADS):
        qq = _stack_maps(q_ref[:, h * w:(h + 1) * w])
        bp = bp_ref[h]
        bn = bn_ref[h]
        past = pk_ref.shape[0] // C_HEADS
        head_rows = pl.ds(h, past, stride=C_HEADS)
        s_p = _dot_nt(qq, pk_ref[head_rows, :].astype(BF16)) + jnp.concatenate([bp, bp], axis=0)
        s_n = _dot_nt(qq, k_ref[:, h * w:(h + 1) * w]) + jnp.concatenate([bn, bn], axis=0)
        m = jnp.maximum(jnp.max(s_p, axis=-1, keepdims=True), jnp.max(s_n, axis=-1, keepdims=True))
        p_p = jnp.exp2(s_p - m)
        p_n = jnp.exp2(s_n - m)
        l = jnp.sum(p_p, axis=-1, keepdims=True) + jnp.sum(p_n, axis=-1, keepdims=True)
        acc = (_dot(p_p.astype(BF16), pv_ref[head_rows, :].astype(BF16))
               + _dot(p_n.astype(BF16), v_ref[:, 2 * h * C_DV:(2 * h + 1) * C_DV]))
        o_ref[:, h * C_DV:(h + 1) * C_DV] = _attn_finish(acc, l, lam, g_ref[...], lam_init, tq).astype(BF16)


def _attn_sample(qn, kn, vb, past_k, past_v, bias_p, bias_n, lp, g, layer, *, bsz, t, lam_init):
    n = bsz * t
    w = 2 * C_DQK
    depth, _, past = past_k.shape[:3]
    past_k = past_k.reshape(depth, bsz, past * C_HEADS, w)
    past_v = past_v.reshape(depth, bsz, past * C_HEADS, C_DV)
    full = lambda a: pl.BlockSpec(a.shape, lambda b: (0,) * a.ndim)
    return pl.pallas_call(
        functools.partial(_attn_sample_body, tq=t, lam_init=lam_init),
        grid=(bsz,),
        in_specs=[
            pl.BlockSpec((t, C_HEADS * w), lambda b: (b, 0)),
            pl.BlockSpec((t, C_HEADS * w), lambda b: (b, 0)),
            pl.BlockSpec((t, 2 * C_HEADS * C_DV), lambda b: (b, 0)),
            pl.BlockSpec((None, None, past * C_HEADS, w), lambda b: (layer, b, 0, 0)),
            pl.BlockSpec((None, None, past * C_HEADS, C_DV), lambda b: (layer, b, 0, 0)),
            full(bias_p), full(bias_n),
            pl.BlockSpec((4, C_DQK), lambda b: (0, 0)),
            pl.BlockSpec((1, C_DV), lambda b: (0, 0)),
        ],
        out_specs=pl.BlockSpec((t, C_HEADS * C_DV), lambda b: (b, 0)),
        out_shape=jax.ShapeDtypeStruct((n, C_HEADS * C_DV), BF16),
        compiler_params=_params("parallel"),
        name="attn_sample",
    )(qn, kn, vb, past_k, past_v, bias_p, bias_n, lp, g)


def _memkv_body(mem_ref, g_ref, wk_ref, wv_ref, kg_ref, k_ref, v_ref, kb_ref, vb_ref):
    hm = _rms(mem_ref[0], g_ref[...]).astype(BF16)
    k = _dot(hm, wk_ref[...])
    v = _dot(hm, wv_ref[...])
    for h in range(M_HEADS):
        kn = _rms(k[:, h * M_DH:(h + 1) * M_DH], kg_ref[...])
        k_ref[0, :, h * M_DH:(h + 1) * M_DH] = kn
        kb_ref[0, :, h * M_DH:(h + 1) * M_DH] = kn.astype(BF16)
    v_ref[0] = v
    vb_ref[0] = v.astype(BF16)


def _memkv(mem, g, wk, wv, kg):
    bsz, m, _ = mem.shape
    w = M_HEADS * M_DH
    blk = pl.BlockSpec((1, m, w), lambda b: (b, 0, 0))
    return pl.pallas_call(
        _memkv_body,
        grid=(bsz,),
        in_specs=[
            pl.BlockSpec((1, m, D_MODEL), lambda b: (b, 0, 0)),
            pl.BlockSpec((1, D_MODEL), lambda b: (0, 0)),
            pl.BlockSpec((D_MODEL, w), lambda b: (0, 0)),
            pl.BlockSpec((D_MODEL, w), lambda b: (0, 0)),
            pl.BlockSpec((1, M_DH), lambda b: (0, 0)),
        ],
        out_specs=[blk, blk, blk, blk],
        out_shape=[jax.ShapeDtypeStruct((bsz, m, w), F32), jax.ShapeDtypeStruct((bsz, m, w), F32),
                   jax.ShapeDtypeStruct((bsz, m, w), BF16), jax.ShapeDtypeStruct((bsz, m, w), BF16)],
        compiler_params=_params("parallel"),
        name="memkv",
    )(mem, g, wk, wv, kg)


def _post_body(x_ref, a_ref, bo_ref, co_ref, wo_ref, g_ref, wq_ref, qg_ref, mk_ref, mv_ref, wxo_ref,
               o_ref, att_scr, *, nseq, rps):
    rc = x_ref.shape[0] // POST_CHAINS
    seg = min(rc, rps)
    for c in range(POST_CHAINS):
        rows = slice(c * rc, (c + 1) * rc)
        y = (_dot(a_ref[rows, :], wo_ref[0:A_WIDTH, :])
             + _dot(bo_ref[rows, :], wo_ref[A_WIDTH:A_WIDTH + B_HEADS * B_DV, :])
             + _dot(co_ref[rows, :], wo_ref[A_WIDTH + B_HEADS * B_DV:, :]))
        x1 = x_ref[rows, :] + y
        q = _dot(_rms(x1, g_ref[...]).astype(BF16), wq_ref[...])
        for h in range(M_HEADS):
            sl = slice(h * M_DH, (h + 1) * M_DH)
            qn = _rms(q[:, sl], qg_ref[...]).astype(BF16)
            for u in range(rc // seg):
                r0 = c * rc + u * seg
                s = r0 // rps
                logits = _dot_nt(qn[u * seg:(u + 1) * seg], mk_ref[s, :, sl]) * (M_DH ** -0.5)
                m = jnp.max(logits, axis=-1, keepdims=True)
                p = jnp.exp(logits - m)
                l = jnp.sum(p, axis=-1, keepdims=True)
                o = _dot(p.astype(BF16), mv_ref[s, :, sl]) / l
                att_scr[r0:r0 + seg, sl] = o.astype(BF16)
        o_ref[rows, :] = x1 + _dot(att_scr[rows, :], wxo_ref[...])


def _post(x, a, bo, co, wo, g, wq, qg, mk, mv, wxo, layer, *, t, tm):
    n = x.shape[0]
    w = M_HEADS * M_DH
    assert n % tm == 0 and (t % tm == 0 or tm % t == 0), (n, t, tm)
    if t >= tm:
        nseq, rps = 1, tm
        per = t // tm
        mem_map = lambda i: (i // per, 0, 0)
    else:
        nseq, rps = tm // t, t
        mem_map = lambda i: (i, 0, 0)
    tok = lambda cols: pl.BlockSpec((tm, cols), lambda i: (i, 0))
    const = lambda r, c: pl.BlockSpec((r, c), lambda i: (0, 0), pipeline_mode=pl.Buffered(1))
    stacked = lambda r, c: pl.BlockSpec((None, r, c), lambda i: (layer, 0, 0), pipeline_mode=pl.Buffered(1))
    return pl.pallas_call(
        functools.partial(_post_body, nseq=nseq, rps=rps),
        grid=(n // tm,),
        in_specs=[
            tok(D_MODEL), tok(A_WIDTH), tok(B_HEADS * B_DV), tok(C_HEADS * C_DV),
            stacked(D_MODEL, D_MODEL), const(1, D_MODEL), stacked(D_MODEL, w), const(1, M_DH),
            pl.BlockSpec((nseq, MEM_LEN, w), mem_map),
            pl.BlockSpec((nseq, MEM_LEN, w), mem_map),
            stacked(w, D_MODEL),
        ],
        out_specs=tok(D_MODEL),
        out_shape=jax.ShapeDtypeStruct((n, D_MODEL), F32),
        scratch_shapes=[pltpu.VMEM((tm, w), BF16)],
        compiler_params=_params("parallel"),
        name="post",
    )(x, a, bo, co, wo, g, wq, qg, mk, mv, wxo)


def _ffn_body(x_ref, g_ref, wv_ref, wg_ref, cwv_ref, cwg_ref, cbv_ref, cbg_ref, hv_ref, hg_ref, wd_ref,
              o_ref, nv_ref, ng_ref, h_scr, ubuf, tail, *, nseq, rps, per):
    i = pl.program_id(0)
    j = pl.program_id(1)

    @pl.when(j == 0)
    def _():
        x = x_ref[...]
        h_scr[...] = _rms(x, g_ref[...]).astype(BF16)
        o_ref[...] = x

    if per > 1:
        @pl.when((i == 0) & (j == 0))
        def _():
            tail[...] = jnp.zeros(tail.shape, F32)

    seq_start = (i % per) == 0
    stride = rps + F_HIST
    tm = x_ref.shape[0]
    rc = tm // FFN_CHAINS
    assert rc % rps == 0 or rps % rc == 0

    seg = min(rc, rps)
    chains = [slice(r * rc, (r + 1) * rc) for r in range(FFN_CHAINS)]

    def up(rows):
        return _dot(h_scr[rows, :], wv_ref[...]), _dot(h_scr[rows, :], wg_ref[...])

    def keep(us):
        for rows, pair in zip(chains, us):
            for half, u in enumerate(pair):
                for q in range(rc // seg):
                    s, o = divmod(rows.start + q * seg, rps)
                    dst = s * stride + F_HIST + o
                    ubuf[half, dst:dst + seg, :] = u[q * seg:(q + 1) * seg]

    def conv(half, r0, cw_ref, cb_ref, hist_ref, new_ref):
        outs = []
        for q in range(rc // seg):
            s, o = divmod(r0 + q * seg, rps)
            base = s * stride
            if o == 0:
                if per == 1:
                    prev = hist_ref[s]
                else:
                    prev = jnp.where(seq_start, hist_ref[s], tail[half, j])
                ubuf[half, base:base + F_HIST, :] = prev
            c = cb_ref[...]
            for k in range(F_KW):
                off = base + F_HIST + o - (F_KW - 1) + k
                c = c + ubuf[half, off:off + seg, :] * cw_ref[k:k + 1, :]
            outs.append(c)
            if o + seg == rps:
                last = ubuf[half, base + rps:base + stride, :]
                new_ref[s] = last
                if per > 1:
                    tail[half, j] = last
        return outs[0] if len(outs) == 1 else jnp.concatenate(outs, axis=0)

    def down(rows):
        val = conv(0, rows.start, cwv_ref, cbv_ref, hv_ref, nv_ref)
        gate = conv(1, rows.start, cwg_ref, cbg_ref, hg_ref, ng_ref)
        act = (gate * jax.nn.sigmoid(gate) * val).astype(BF16)
        o_ref[rows, :] += _dot(act, wd_ref[...])

    keep([up(rows) for rows in chains])
    for rows in chains:
        down(rows)


def _ffn(x, g, w_up, cw, cb, hist, w_down, layer, *, t, tm=1024, tn=512):
    n = x.shape[0]
    nj = D_FF // tn
    assert n % tm == 0 and (t % tm == 0 or tm % t == 0), (n, t, tm)
    if t >= tm:
        nseq, rps, per = 1, tm, t // tm
        seq_map = lambda i: i // per
    else:
        nseq, rps, per = tm // t, t, 1
        seq_map = lambda i: i
    bsz = hist.shape[0]
    hist_spec = lambda off: pl.BlockSpec((nseq, F_HIST, tn), lambda i, j: (seq_map(i), 0, j + off))
    new_spec = pl.BlockSpec((nseq, F_HIST, tn), lambda i, j: (i, 0, j))
    x_out, tail_v, tail_g = pl.pallas_call(
        functools.partial(_ffn_body, nseq=nseq, rps=rps, per=per),
        grid=(n // tm, nj),
        in_specs=[
            pl.BlockSpec((tm, D_MODEL), lambda i, j: (i, 0), pipeline_mode=pl.Buffered(1)),
            pl.BlockSpec((1, D_MODEL), lambda i, j: (0, 0)),
            pl.BlockSpec((None, D_MODEL, tn), lambda i, j: (layer, 0, j)),
            pl.BlockSpec((None, D_MODEL, tn), lambda i, j: (layer, 0, j + nj)),
            pl.BlockSpec((F_KW, tn), lambda i, j: (0, j)),
            pl.BlockSpec((F_KW, tn), lambda i, j: (0, j + nj)),
            pl.BlockSpec((1, tn), lambda i, j: (0, j)),
            pl.BlockSpec((1, tn), lambda i, j: (0, j + nj)),
            hist_spec(0), hist_spec(nj),
            pl.BlockSpec((None, tn, D_MODEL), lambda i, j: (layer, j, 0)),
        ],
        out_specs=[pl.BlockSpec((tm, D_MODEL), lambda i, j: (i, 0)), new_spec, new_spec],
        out_shape=[jax.ShapeDtypeStruct((n, D_MODEL), F32),
                   jax.ShapeDtypeStruct((bsz * per, F_HIST, D_FF), F32),
                   jax.ShapeDtypeStruct((bsz * per, F_HIST, D_FF), F32)],
        scratch_shapes=[
            pltpu.VMEM((tm, D_MODEL), BF16),
            pltpu.VMEM((2, nseq * (rps + F_HIST), tn), F32),
            pltpu.VMEM((2, nj, F_HIST, tn), F32),
        ],
        compiler_params=_params("arbitrary", "arbitrary"),
        name="ffn",
    )(x, g, w_up, w_up, cw, cw, cb, cb, hist, hist, w_down)
    last = lambda a: a.reshape(bsz, per, F_HIST, D_FF)[:, per - 1]
    return x_out, last(tail_v), last(tail_g)


def _rope_tables(pos0, t):
    half = B_DK // 2
    inv = 1.0 / (ROPE_BASE ** (jnp.arange(half, dtype=F32) / half))
    ang = (pos0 + jnp.arange(t, dtype=jnp.int32)).astype(F32)[:, None] * inv[None, :]
    cos = jnp.cos(ang)
    sin = jnp.sin(ang)
    return jnp.concatenate([cos, cos], axis=-1), jnp.concatenate([-sin, sin], axis=-1)


def _pad_rows(h, rows):
    return jnp.pad(h, ((0, 0), (rows - h.shape[1], 0), (0, 0)))


def _row(v):
    return v.reshape(1, -1).astype(F32)


def kernel(x_prompt, x_sample, mem_prompt, state_conv_a, state_ret, cache_diff_k, cache_diff_v, cache_mem_k,
           cache_mem_v, state_conv_f, norm1_g, w_in, conv_a_w, conv_a_b, ln_a_g, ln_a_b, ret_gn_g, diff_qn_g,
           diff_kn_g, diff_lq1, diff_lk1, diff_lq2, diff_lk2, diff_subln_g, w_out, rel_bias, norm2_g, mem_norm_g,
           w_xq, w_xk, w_xv, xqn_g, xkn_g, w_xo, norm3_g, w_up, conv_f_w, conv_f_b, w_down):
    bp, tp, _ = x_prompt.shape
    bs, ts, _ = x_sample.shape
    depth = w_in.shape[0]
    past = cache_diff_k.shape[2]
    cw_dim = C_HEADS * 2 * C_DQK

    tq = ATT_TILE
    r = np.arange(tq)
    rel_diag = r[None, :] - r[:, None]
    idx_p = np.stack([_t5_bucket(rel_diag - d * tq) for d in range(3)]).reshape(3 * tq, tq)
    vis_p = np.stack([(r[None, :] // CHUNK) <= (r[:, None] // CHUNK), np.ones((tq, tq), bool), np.ones((tq, tq), bool)])
    vis_p = vis_p.reshape(3 * tq, tq).astype(np.int32)
    bias_p = _bias_tiles(rel_bias, jnp.asarray(idx_p), jnp.asarray(vis_p)).reshape(C_HEADS, 3, tq, tq)
    rel_s = np.arange(past + ts)[None, :] - (past + np.arange(ts))[:, None]
    bias_s = _bias_tiles(rel_bias, jnp.asarray(_t5_bucket(rel_s)), jnp.ones(rel_s.shape, jnp.int32))
    bias_s_past, bias_s_new = bias_s[:, :, :past], bias_s[:, :, past:]

    cos_p, sin_p = _rope_tables(0, tp)
    cos_s, sin_s = _rope_tables(past, ts)

    xp = x_prompt.reshape(bp * tp, D_MODEL)
    xs = x_sample.reshape(bs * ts, D_MODEL)
    zero_a = jnp.zeros((bp, A_HIST, A_WIDTH), F32)
    zero_r = jnp.zeros((bp, B_HEADS, B_DK, B_DV), F32)
    zero_f = jnp.zeros((bp, F_HIST, 2 * D_FF), F32)

    outs = {k: [] for k in ("p_ca", "p_rs", "p_k", "p_v", "p_mk", "p_mv", "p_cf", "s_ca", "s_rs", "s_k", "s_v", "s_cf")}
    w_in_b = w_in.astype(BF16)
    w_out_b = w_out.astype(BF16)
    w_xq_b = w_xq.astype(BF16)
    w_xo_b = w_xo.astype(BF16)
    w_up_b = w_up.astype(BF16)
    w_down_b = w_down.astype(BF16)
    for l in range(depth):
        lam_init = 0.8 - 0.6 * math.exp(-0.3 * l)
        lp = jnp.stack([diff_lq1[l], diff_lk1[l], diff_lq2[l], diff_lk2[l]]).astype(F32)
        qg2 = _row(jnp.concatenate([diff_qn_g[l], diff_qn_g[l]]))
        kg2 = _row(jnp.concatenate([diff_kn_g[l], diff_kn_g[l]]))
        mix_args = (conv_a_w[l], _row(conv_a_b[l]), _row(ln_a_g[l]), _row(ln_a_b[l]), _row(ret_gn_g[l]), qg2, kg2)
        subln = _row(diff_subln_g[l])

        def block(x, bsz, t, tt, hist_a, state_r, cos2, sin2, attend, mk_b, mv_b, hist_f, tm_post):
            if tt >= FUSED_PROJ_MIN_ROWS:
                src = (x, _row(norm1_g[l]), w_in_b, l)
            else:
                src = _in_proj(x, _row(norm1_g[l]), w_in_b, l)
            a, bo, qn, kn, vb, ck, cv, nh, nr = _mixer(src, hist_a, state_r, cos2, sin2, *mix_args,
                                                        bsz=bsz, t=t, tt=tt)
            co = attend(qn, kn, vb)
            x = _post(x, a, bo, co, w_out_b, _row(norm2_g[l]), w_xq_b, _row(xqn_g[l]), mk_b, mv_b, w_xo_b,
                      l, t=t, tm=tm_post)
            x, nfv, nfg = _ffn(x, _row(norm3_g[l]), w_up_b, conv_f_w[l], _row(conv_f_b[l]), hist_f, w_down_b, l,
                               t=t)
            new_f = jnp.concatenate([nfv[:, F_HIST - (F_KW - 1):], nfg[:, F_HIST - (F_KW - 1):]], axis=-1)
            return (x, nh[:, A_HIST - (A_KW - 1):], nr, ck.reshape(bsz, t, C_HEADS, 2 * C_DQK),
                    cv.reshape(bsz, t, C_HEADS, C_DV), new_f)

        mk, mv, mk_b, mv_b = _memkv(mem_prompt, _row(mem_norm_g[l]), w_xk[l].astype(BF16), w_xv[l].astype(BF16),
                                    _row(xkn_g[l]))
        attend_p = lambda qn, kn, vb: _attn_prompt(qn, kn, vb, bias_p, lp, subln, bsz=bp, t=tp, lam_init=lam_init)
        xp, ca, rs, kn_, vn_, cf = block(xp, bp, tp, 256, zero_a, zero_r, cos_p, sin_p, attend_p, mk_b, mv_b,
                                         zero_f, POST_TILE)
        outs["p_ca"].append(ca); outs["p_rs"].append(rs); outs["p_k"].append(kn_); outs["p_v"].append(vn_)
        outs["p_mk"].append(mk.reshape(bp, MEM_LEN, M_HEADS, M_DH))
        outs["p_mv"].append(mv.reshape(bp, MEM_LEN, M_HEADS, M_DH))
        outs["p_cf"].append(cf)

        attend_s = lambda qn, kn, vb: _attn_sample(qn, kn, vb, cache_diff_k, cache_diff_v, bias_s_past,
                                                   bias_s_new, lp, subln, l,
                                                   bsz=bs, t=ts, lam_init=lam_init)
        smk = cache_mem_k[l].reshape(bs, MEM_LEN, M_HEADS * M_DH).astype(BF16)
        smv = cache_mem_v[l].reshape(bs, MEM_LEN, M_HEADS * M_DH).astype(BF16)
        xs, sca, srs, skn, svn, scf = block(xs, bs, ts, ts, _pad_rows(state_conv_a[l], A_HIST), state_ret[l],
                                            cos_s, sin_s, attend_s, smk, smv, _pad_rows(state_conv_f[l], F_HIST),
                                            POST_TILE)
        outs["s_ca"].append(sca); outs["s_rs"].append(srs); outs["s_k"].append(skn); outs["s_v"].append(svn)
        outs["s_cf"].append(scf)

    st = lambda k: jnp.stack(outs[k])
    return (xp.reshape(bp, tp, D_MODEL), xs.reshape(bs, ts, D_MODEL),
            st("p_ca"), st("p_rs"), st("p_k"), st("p_v"), st("p_mk"), st("p_mv"), st("p_cf"),
            st("s_ca"), st("s_rs"), st("s_k"), st("s_v"), st("s_cf"))
```

```python
import functools
import math

import numpy as np
import jax
import jax.numpy as jnp
from jax import lax
from jax.experimental import pallas as pl
from jax.experimental.pallas import tpu as pltpu

F32 = jnp.float32
BF16 = jnp.bfloat16
EPS = 1e-6
NEG_INF = -1e30

D_MODEL = 2048
CHUNK = 64
A_WIDTH = 512
A_KW = 31
A_HIST = 32
B_HEADS = 4
B_DK = 128
B_DV = 256
ROPE_BASE = 10000.0
C_HEADS = 4
C_DQK = 64
C_DV = 128
REL_BUCKETS = 32
REL_MAX_DIST = 128
M_HEADS = 4
M_DH = 128
MEM_LEN = 256
D_FF = 5632
F_KW = 3
F_HIST = 8

A_COLS = 2 * A_WIDTH
B_COLS = B_HEADS * (2 * B_DK + 2 * B_DV)
C_COLS = C_HEADS * (4 * C_DQK + C_DV)
IN_COLS = A_COLS + B_COLS + C_COLS
B_OFF = A_COLS
C_OFF = A_COLS + B_COLS
PROJ_GROUPS = (0, A_WIDTH, A_COLS, B_OFF + 2 * B_HEADS * B_DK, B_OFF + 2 * B_HEADS * B_DK + B_HEADS * B_DV, C_OFF,
               C_OFF + 2 * C_HEADS * 2 * C_DQK, IN_COLS)

VMEM_LIMIT_BYTES = 56 * 1024 * 1024
LANES = 128
BF16_SUBLANES = 16
ATT_TILE = 512
ATT_ROWS = 256
ATT_LANES = 128
FFN_CHAINS = 2
POST_TILE = 512
POST_CHAINS = 2
FUSED_PROJ_MIN_ROWS = 128


def _params(*sem):
    return pltpu.CompilerParams(dimension_semantics=sem, vmem_limit_bytes=VMEM_LIMIT_BYTES)


def _rms(x, g):
    return x * lax.rsqrt(jnp.mean(x * x, axis=-1, keepdims=True) + EPS) * g


def _dot(a, b):
    return jnp.dot(a, b, preferred_element_type=F32)


def _dot_nt(a, b):
    return lax.dot_general(a, b, (((1,), (1,)), ((), ())), preferred_element_type=F32)


def _dot_tn(a, b):
    return lax.dot_general(a, b, (((0,), (0,)), ((), ())), preferred_element_type=F32)


def _in_proj_body(x_ref, g_ref, w_ref, o_ref, h_ref):
    @pl.when(pl.program_id(1) == 0)
    def _():
        h_ref[...] = _rms(x_ref[...], g_ref[...]).astype(BF16)

    o_ref[...] = _dot(h_ref[...], w_ref[...])


def _in_proj(x, g, w, layer, *, tm=1024, tn=512):
    n = x.shape[0]
    cols = w.shape[2]
    return pl.pallas_call(
        _in_proj_body,
        grid=(n // tm, cols // tn),
        in_specs=[
            pl.BlockSpec((tm, D_MODEL), lambda i, j: (i, 0)),
            pl.BlockSpec((1, D_MODEL), lambda i, j: (0, 0)),
            pl.BlockSpec((None, D_MODEL, tn), lambda i, j: (layer, 0, j)),
        ],
        out_specs=pl.BlockSpec((tm, tn), lambda i, j: (i, j)),
        out_shape=jax.ShapeDtypeStruct((n, cols), F32),
        scratch_shapes=[pltpu.VMEM((tm, D_MODEL), BF16)],
        compiler_params=_params("parallel", "arbitrary"),
        name="in_proj",
    )(x, g, w)


def _projected_columns(x_ref, g_ref, w_ref):
    h = _rms(x_ref[...], g_ref[...]).astype(BF16)
    cache = {}

    def cols(c0, c1):
        g0, g1 = next((a, b) for a, b in zip(PROJ_GROUPS[:-1], PROJ_GROUPS[1:]) if a <= c0 and c1 <= b)
        if g0 not in cache:
            cache[g0] = _dot(h, w_ref[:, g0:g1])
        return cache[g0][:, c0 - g0:c1 - g0]

    return cols


def _mixer_body(*refs, tt, log_gammas, fused):
    if fused:
        cols = _projected_columns(*refs[:3])
        refs = refs[3:]
    else:
        proj_ref = refs[0]
        cols = lambda c0, c1: proj_ref[:, c0:c1]
        refs = refs[1:]
    (hist_ref, state_ref, cos_ref, sin_ref, cw_ref, cb_ref, lng_ref, lnb_ref, gng_ref, qg_ref, kg_ref,
     a_ref, bo_ref, qn_ref, kn_ref, vb_ref, ck_ref, cv_ref, nh_ref, nr_ref, aext, zbuf, sret) = refs
    t = pl.program_id(1)

    @pl.when(t == 0)
    def _():
        aext[0:A_HIST, :] = hist_ref[0]
        sret[...] = state_ref[0]

    glu = cols(0, A_WIDTH) * jax.nn.sigmoid(cols(A_WIDTH, A_COLS))
    aext[A_HIST:A_HIST + tt, :] = glu
    first = A_HIST - (A_KW - 1)
    acc = jnp.zeros((tt, A_WIDTH), F32) + cb_ref[...]
    for r in range(8):
        rows = tt if r == 0 else tt + 8
        z = None
        for off in range(r, first + A_KW, 8):
            if off < first:
                continue
            term = aext[off - r:off - r + rows, :] * cw_ref[off - first:off - first + 1, :]
            z = term if z is None else z + term
        if r == 0:
            acc = acc + z
        else:
            zbuf[r - 1] = z
            acc = acc + zbuf[r - 1, r:r + tt, :]
    mu = jnp.mean(acc, axis=-1, keepdims=True)
    xc = acc - mu
    var = jnp.mean(xc * xc, axis=-1, keepdims=True)
    ln = xc * lax.rsqrt(var + EPS) * lng_ref[...] + lnb_ref[...]
    a_ref[...] = (ln * jax.nn.sigmoid(ln)).astype(BF16)

    last_rows = aext[tt:tt + A_HIST, :]
    nh_ref[0] = last_rows
    aext[0:A_HIST, :] = last_rows

    cos = cos_ref[...]
    sin = sin_ref[...]
    ri = lax.broadcasted_iota(jnp.int32, (tt, tt), 0)
    ci = lax.broadcasted_iota(jnp.int32, (tt, tt), 1)
    dij = (ri - ci).astype(F32)
    causal = ri >= ci
    rowf = lax.broadcasted_iota(jnp.int32, (tt, 1), 0).astype(F32)
    for h in range(B_HEADS):
        lg = log_gammas[h]
        q = cols(B_OFF + h * B_DK, B_OFF + (h + 1) * B_DK)
        k = cols(B_OFF + B_HEADS * B_DK + h * B_DK, B_OFF + B_HEADS * B_DK + (h + 1) * B_DK)
        voff = B_OFF + 2 * B_HEADS * B_DK
        v = cols(voff + h * B_DV, voff + (h + 1) * B_DV)
        goff = voff + B_HEADS * B_DV
        g = cols(goff + h * B_DV, goff + (h + 1) * B_DV)
        qr = q * cos + pltpu.roll(q, B_DK // 2, 1) * sin
        kr = (k * cos + pltpu.roll(k, B_DK // 2, 1) * sin) * (B_DK ** -0.5)
        qb = qr.astype(BF16)
        vb = v.astype(BF16)
        decay = jnp.where(causal, jnp.exp(lg * jnp.maximum(dij, 0.0)), 0.0)
        scores = _dot_nt(qb, kr.astype(BF16)) * decay
        inner = _dot(scores.astype(BF16), vb)
        s_old = sret[h]
        cross = _dot(qb, s_old.astype(BF16)) * jnp.exp(lg * (rowf + 1.0))
        o = inner + cross
        kd = kr * jnp.exp(lg * (tt - 1.0 - rowf))
        s_new = s_old * math.exp(lg * tt) + _dot_tn(kd.astype(BF16), vb)
        sret[h] = s_new
        nr_ref[0, h] = s_new
        y = _rms(o, gng_ref[:, h * B_DV:(h + 1) * B_DV])
        bo_ref[:, h * B_DV:(h + 1) * B_DV] = (y * (g * jax.nn.sigmoid(g))).astype(BF16)

    lane = lax.broadcasted_iota(jnp.int32, (tt, 2 * C_DQK), 1)
    lo = lane < C_DQK

    def qk_norm(x, g2):
        sq = x * x
        s_lo = jnp.sum(jnp.where(lo, sq, 0.0), axis=-1, keepdims=True)
        s_hi = jnp.sum(jnp.where(lo, 0.0, sq), axis=-1, keepdims=True)
        ms = jnp.where(lo, s_lo, s_hi) * (1.0 / C_DQK)
        return x * lax.rsqrt(ms + EPS) * g2

    w = 2 * C_DQK
    for h in range(C_HEADS):
        cq = cols(C_OFF + h * w, C_OFF + (h + 1) * w)
        ck = cols(C_OFF + C_HEADS * w + h * w, C_OFF + C_HEADS * w + (h + 1) * w)
        qn = qk_norm(cq, qg_ref[...])
        kn = qk_norm(ck, kg_ref[...])
        qn_ref[:, h * w:(h + 1) * w] = (qn * (C_DQK ** -0.5)).astype(BF16)
        kn_ref[:, h * w:(h + 1) * w] = kn.astype(BF16)
        ck_ref[pl.ds(h, tt, stride=C_HEADS), :] = kn
    cv = cols(C_OFF + 2 * C_HEADS * w, IN_COLS)
    ones = jnp.ones((tt, C_DV), BF16)
    for h in range(C_HEADS):
        cv_ref[pl.ds(h, tt, stride=C_HEADS), :] = cv[:, h * C_DV:(h + 1) * C_DV]
        vb_ref[:, 2 * h * C_DV:(2 * h + 1) * C_DV] = cv[:, h * C_DV:(h + 1) * C_DV].astype(BF16)
        vb_ref[:, (2 * h + 1) * C_DV:(2 * h + 2) * C_DV] = ones


def _mixer(src, hist, state, cos2, sin2, cw, cb, lng, lnb, gng, qg2, kg2, *, bsz, t, tt):
    n = bsz * t
    nt = t // tt
    log_gammas = tuple(math.log(1.0 - 2.0 ** (-5.0 - h)) for h in range(B_HEADS))
    tok = lambda cols: pl.BlockSpec((tt, cols), lambda b, i: (b * nt + i, 0))
    const = lambda r, c: pl.BlockSpec((r, c), lambda b, i: (0, 0))
    cw_dim = C_HEADS * 2 * C_DQK
    fused = isinstance(src, tuple)
    if fused:
        x, g1, w_in, layer = src
        src_args = (x, g1, w_in)
        src_specs = [tok(D_MODEL), const(1, D_MODEL),
                     pl.BlockSpec((None, D_MODEL, IN_COLS), lambda b, i: (layer, 0, 0), pipeline_mode=pl.Buffered(1))]
    else:
        src_args = (src,)
        src_specs = [tok(IN_COLS)]
    return pl.pallas_call(
        functools.partial(_mixer_body, tt=tt, log_gammas=log_gammas, fused=fused),
        grid=(bsz, nt),
        in_specs=src_specs + [
            pl.BlockSpec((1, A_HIST, A_WIDTH), lambda b, i: (b, 0, 0)),
            pl.BlockSpec((1, B_HEADS, B_DK, B_DV), lambda b, i: (b, 0, 0, 0)),
            pl.BlockSpec((tt, B_DK), lambda b, i: (i, 0)),
            pl.BlockSpec((tt, B_DK), lambda b, i: (i, 0)),
            const(A_KW, A_WIDTH), const(1, A_WIDTH), const(1, A_WIDTH), const(1, A_WIDTH),
            const(1, B_HEADS * B_DV), const(1, 2 * C_DQK), const(1, 2 * C_DQK),
        ],
        out_specs=[
            tok(A_WIDTH), tok(B_HEADS * B_DV), tok(cw_dim), tok(cw_dim), tok(2 * C_HEADS * C_DV),
            pl.BlockSpec((None, tt * C_HEADS, 2 * C_DQK), lambda b, i: (b, i, 0)),
            pl.BlockSpec((None, tt * C_HEADS, C_DV), lambda b, i: (b, i, 0)),
            pl.BlockSpec((1, A_HIST, A_WIDTH), lambda b, i: (b, 0, 0)),
            pl.BlockSpec((1, B_HEADS, B_DK, B_DV), lambda b, i: (b, 0, 0, 0)),
        ],
        out_shape=[
            jax.ShapeDtypeStruct((n, A_WIDTH), BF16),
            jax.ShapeDtypeStruct((n, B_HEADS * B_DV), BF16),
            jax.ShapeDtypeStruct((n, cw_dim), BF16),
            jax.ShapeDtypeStruct((n, cw_dim), BF16),
            jax.ShapeDtypeStruct((n, 2 * C_HEADS * C_DV), BF16),
            jax.ShapeDtypeStruct((bsz, t * C_HEADS, 2 * C_DQK), F32),
            jax.ShapeDtypeStruct((bsz, t * C_HEADS, C_DV), F32),
            jax.ShapeDtypeStruct((bsz, A_HIST, A_WIDTH), F32),
            jax.ShapeDtypeStruct((bsz, B_HEADS, B_DK, B_DV), F32),
        ],
        scratch_shapes=[pltpu.VMEM((A_HIST + tt, A_WIDTH), F32), pltpu.VMEM((7, tt + 8, A_WIDTH), F32),
                        pltpu.VMEM((B_HEADS, B_DK, B_DV), F32)],
        compiler_params=_params("parallel", "arbitrary"),
        name="mixer",
    )(*src_args, hist, state, cos2, sin2, cw, cb, lng, lnb, gng, qg2, kg2)


def _t5_bucket(rel):
    half = REL_BUCKETS // 2
    exact = half // 2
    n = np.abs(rel)
    large = exact + (np.log(np.maximum(n, 1).astype(np.float32) / exact) / math.log(REL_MAX_DIST / exact)
                     * (half - exact)).astype(np.int32)
    large = np.minimum(large, half - 1)
    return (np.where(rel > 0, half, 0) + np.where(n < exact, n, large)).astype(np.int32)


def _bias_body(rb_ref, idx_ref, vis_ref, o_ref):
    h = pl.program_id(0)
    idx = idx_ref[...]
    far = rb_ref[REL_BUCKETS // 2 - 1, h]
    acc = jnp.zeros(idx.shape, F32)
    for b in range(REL_BUCKETS):
        acc = jnp.where(idx == b, rb_ref[b, h], acc)
    o_ref[0] = jnp.where(vis_ref[...] != 0, acc - far, NEG_INF)


def _bias_tiles(rel_bias, idx, vis):
    r, c = idx.shape
    return pl.pallas_call(
        _bias_body,
        grid=(C_HEADS,),
        in_specs=[
            pl.BlockSpec(memory_space=pltpu.SMEM),
            pl.BlockSpec((r, c), lambda h: (0, 0)),
            pl.BlockSpec((r, c), lambda h: (0, 0)),
        ],
        out_specs=pl.BlockSpec((1, r, c), lambda h: (h, 0, 0)),
        out_shape=jax.ShapeDtypeStruct((C_HEADS, r, c), F32),
        compiler_params=_params("arbitrary"),
        name="bias_tiles",
    )(rel_bias, idx, vis)


def _lambda(lp_ref, lam_init):
    lp = lp_ref[...]
    e1 = jnp.exp(jnp.sum(lp[0:1] * lp[1:2], axis=-1, keepdims=True))
    e2 = jnp.exp(jnp.sum(lp[2:3] * lp[3:4], axis=-1, keepdims=True))
    return e1 - e2 + lam_init


def _stack_maps(q):
    lane = lax.broadcasted_iota(jnp.int32, q.shape, 1)
    zero = jnp.zeros_like(q)
    return jnp.concatenate([jnp.where(lane < C_DQK, q, zero), jnp.where(lane < C_DQK, zero, q)], axis=0)


def _attn_finish(acc, l, lam, g, lam_init, tq):
    o = acc / l
    o = o[0:tq] - lam * o[tq:2 * tq]
    return _rms(o, g) * (1.0 - lam_init)


def _attn_prompt_body(q_ref, k_ref, v_ref, bias_ref, lp_ref, g_ref, o_ref, qq_scr, m_scr, alpha_scr, p_scr, acc_scr,
                      *, tq, lam_init):
    qi = pl.program_id(2)
    qq_scr[...] = _stack_maps(q_ref[...])
    m_scr[...] = jnp.full(m_scr.shape, NEG_INF, F32)
    acc_scr[...] = jnp.zeros(acc_scr.shape, F32)

    def tile(ref, idx):
        start = idx * tq if isinstance(idx, int) else pl.multiple_of(idx * tq, tq)
        return ref[pl.ds(start, tq), :]

    nl = tq // ATT_LANES
    chunks = [slice(c * ATT_ROWS, (c + 1) * ATT_ROWS) for c in range(2 * tq // ATT_ROWS)]

    def accumulate(rows, vb):
        alpha = alpha_scr[rows, :]
        acc_scr[rows, :] = jnp.concatenate([alpha, alpha], axis=-1) * acc_scr[rows, :] + _dot(p_scr[rows, :], vb)

    def step(ki, bias_idx, first=False):
        kb = tile(k_ref, ki)
        if not first:
            v_prev = tile(v_ref, ki - 1)
        for rows in chunks:
            if not first:
                accumulate(rows, v_prev)
            s = _dot_nt(qq_scr[rows, :], kb)
            if bias_idx is not None:
                s = s + bias_ref[0, bias_idx, pl.ds(rows.start % tq, ATT_ROWS), :]
            slabs = [s[:, j * ATT_LANES:(j + 1) * ATT_LANES] for j in range(nl)]
            m_old = m_scr[rows, :]
            m_new = jnp.maximum(m_old, jnp.max(functools.reduce(jnp.maximum, slabs), axis=-1, keepdims=True))
            alpha_scr[rows, :] = jnp.exp(m_old - m_new)
            p_scr[rows, :] = jnp.concatenate([jnp.exp(sl - m_new) for sl in slabs], axis=-1).astype(BF16)
            m_scr[rows, :] = m_new

    n_far = jnp.maximum(qi - 1, 0)
    step(0, jnp.minimum(qi, 2), first=True)

    def far_body(ki, c):
        step(ki, None)
        return c

    lax.fori_loop(1, n_far, far_body, 0)

    def near_body(ki, c):
        step(ki, qi - ki)
        return c

    lax.fori_loop(jnp.maximum(n_far, 1), qi + 1, near_body, 0)

    v_last = tile(v_ref, qi)
    for rows in chunks:
        accumulate(rows, v_last)
    lam = _lambda(lp_ref, lam_init)
    acc = acc_scr[...]
    o_ref[...] = _attn_finish(acc[:, 0:C_DV], acc[:, C_DV:2 * C_DV], lam, g_ref[...], lam_init, tq).astype(BF16)


def _attn_prompt(qn, kn, vb, bias, lp, g, *, bsz, t, lam_init):
    tq = ATT_TILE
    nq = t // tq
    n = bsz * t
    w = 2 * C_DQK
    return pl.pallas_call(
        functools.partial(_attn_prompt_body, tq=tq, lam_init=lam_init),
        grid=(bsz, C_HEADS, nq),
        in_specs=[
            pl.BlockSpec((tq, w), lambda b, h, i: (b * nq + i, h)),
            pl.BlockSpec((t, w), lambda b, h, i: (b, h)),
            pl.BlockSpec((t, 2 * C_DV), lambda b, h, i: (b, h)),
            pl.BlockSpec((1, 3, tq, tq), lambda b, h, i: (h, 0, 0, 0)),
            pl.BlockSpec((4, C_DQK), lambda b, h, i: (0, 0)),
            pl.BlockSpec((1, C_DV), lambda b, h, i: (0, 0)),
        ],
        out_specs=pl.BlockSpec((tq, C_DV), lambda b, h, i: (b * nq + i, h)),
        out_shape=jax.ShapeDtypeStruct((n, C_HEADS * C_DV), BF16),
        scratch_shapes=[pltpu.VMEM((2 * tq, w), BF16), pltpu.VMEM((2 * tq, ATT_LANES), F32),
                        pltpu.VMEM((2 * tq, ATT_LANES), F32), pltpu.VMEM((2 * tq, tq), BF16),
                        pltpu.VMEM((2 * tq, 2 * C_DV), F32)],
        compiler_params=_params("parallel", "parallel", "arbitrary"),
        name="attn_prompt",
    )(qn, kn, vb, bias, lp, g)


def _attn_sample_body(q_ref, k_ref, v_ref, pk_ref, pv_ref, bp_ref, bn_ref, lp_ref, g_ref, o_ref, *, tq, lam_init):
    lam = _lambda(lp_ref, lam_init)
    w = 2 * C_DQK
    for h in range(C_HEADS):
        qq = _stack_maps(q_ref[:, h * w:(h + 1) * w])
        bp = bp_ref[h]
        bn = bn_ref[h]
        past = pk_ref.shape[0] // C_HEADS
        head_rows = pl.ds(h, past, stride=C_HEADS)
        s_p = _dot_nt(qq, pk_ref[head_rows, :].astype(BF16)) + jnp.concatenate([bp, bp], axis=0)
        s_n = _dot_nt(qq, k_ref[:, h * w:(h + 1) * w]) + jnp.concatenate([bn, bn], axis=0)
        m = jnp.maximum(jnp.max(s_p, axis=-1, keepdims=True), jnp.max(s_n, axis=-1, keepdims=True))
        p_p = jnp.exp(s_p - m)
        p_n = jnp.exp(s_n - m)
        l = jnp.sum(p_p, axis=-1, keepdims=True) + jnp.sum(p_n, axis=-1, keepdims=True)
        acc = (_dot(p_p.astype(BF16), pv_ref[head_rows, :].astype(BF16))
               + _dot(p_n.astype(BF16), v_ref[:, 2 * h * C_DV:(2 * h + 1) * C_DV]))
        o_ref[:, h * C_DV:(h + 1) * C_DV] = _attn_finish(acc, l, lam, g_ref[...], lam_init, tq).astype(BF16)


def _attn_sample(qn, kn, vb, past_k, past_v, bias_p, bias_n, lp, g, layer, *, bsz, t, lam_init):
    n = bsz * t
    w = 2 * C_DQK
    depth, _, past = past_k.shape[:3]
    past_k = past_k.reshape(depth, bsz, past * C_HEADS, w)
    past_v = past_v.reshape(depth, bsz, past * C_HEADS, C_DV)
    full = lambda a: pl.BlockSpec(a.shape, lambda b: (0,) * a.ndim)
    return pl.pallas_call(
        functools.partial(_attn_sample_body, tq=t, lam_init=lam_init),
        grid=(bsz,),
        in_specs=[
            pl.BlockSpec((t, C_HEADS * w), lambda b: (b, 0)),
            pl.BlockSpec((t, C_HEADS * w), lambda b: (b, 0)),
            pl.BlockSpec((t, 2 * C_HEADS * C_DV), lambda b: (b, 0)),
            pl.BlockSpec((None, None, past * C_HEADS, w), lambda b: (layer, b, 0, 0)),
            pl.BlockSpec((None, None, past * C_HEADS, C_DV), lambda b: (layer, b, 0, 0)),
            full(bias_p), full(bias_n),
            pl.BlockSpec((4, C_DQK), lambda b: (0, 0)),
            pl.BlockSpec((1, C_DV), lambda b: (0, 0)),
        ],
        out_specs=pl.BlockSpec((t, C_HEADS * C_DV), lambda b: (b, 0)),
        out_shape=jax.ShapeDtypeStruct((n, C_HEADS * C_DV), BF16),
        compiler_params=_params("parallel"),
        name="attn_sample",
    )(qn, kn, vb, past_k, past_v, bias_p, bias_n, lp, g)


def _memkv_body(mem_ref, g_ref, wk_ref, wv_ref, kg_ref, k_ref, v_ref, kb_ref, vb_ref):
    hm = _rms(mem_ref[0], g_ref[...]).astype(BF16)
    k = _dot(hm, wk_ref[...])
    v = _dot(hm, wv_ref[...])
    for h in range(M_HEADS):
        kn = _rms(k[:, h * M_DH:(h + 1) * M_DH], kg_ref[...])
        k_ref[0, :, h * M_DH:(h + 1) * M_DH] = kn
        kb_ref[0, :, h * M_DH:(h + 1) * M_DH] = kn.astype(BF16)
    v_ref[0] = v
    vb_ref[0] = v.astype(BF16)


def _memkv(mem, g, wk, wv, kg):
    bsz, m, _ = mem.shape
    w = M_HEADS * M_DH
    blk = pl.BlockSpec((1, m, w), lambda b: (b, 0, 0))
    return pl.pallas_call(
        _memkv_body,
        grid=(bsz,),
        in_specs=[
            pl.BlockSpec((1, m, D_MODEL), lambda b: (b, 0, 0)),
            pl.BlockSpec((1, D_MODEL), lambda b: (0, 0)),
            pl.BlockSpec((D_MODEL, w), lambda b: (0, 0)),
            pl.BlockSpec((D_MODEL, w), lambda b: (0, 0)),
            pl.BlockSpec((1, M_DH), lambda b: (0, 0)),
        ],
        out_specs=[blk, blk, blk, blk],
        out_shape=[jax.ShapeDtypeStruct((bsz, m, w), F32), jax.ShapeDtypeStruct((bsz, m, w), F32),
                   jax.ShapeDtypeStruct((bsz, m, w), BF16), jax.ShapeDtypeStruct((bsz, m, w), BF16)],
        compiler_params=_params("parallel"),
        name="memkv",
    )(mem, g, wk, wv, kg)


def _post_body(x_ref, a_ref, bo_ref, co_ref, wo_ref, g_ref, wq_ref, qg_ref, mk_ref, mv_ref, wxo_ref,
               o_ref, att_scr, *, nseq, rps):
    rc = x_ref.shape[0] // POST_CHAINS
    seg = min(rc, rps)
    for c in range(POST_CHAINS):
        rows = slice(c * rc, (c + 1) * rc)
        y = (_dot(a_ref[rows, :], wo_ref[0:A_WIDTH, :])
             + _dot(bo_ref[rows, :], wo_ref[A_WIDTH:A_WIDTH + B_HEADS * B_DV, :])
             + _dot(co_ref[rows, :], wo_ref[A_WIDTH + B_HEADS * B_DV:, :]))
        x1 = x_ref[rows, :] + y
        q = _dot(_rms(x1, g_ref[...]).astype(BF16), wq_ref[...])
        for h in range(M_HEADS):
            sl = slice(h * M_DH, (h + 1) * M_DH)
            qn = _rms(q[:, sl], qg_ref[...]).astype(BF16)
            for u in range(rc // seg):
                r0 = c * rc + u * seg
                s = r0 // rps
                logits = _dot_nt(qn[u * seg:(u + 1) * seg], mk_ref[s, :, sl]) * (M_DH ** -0.5)
                m = jnp.max(logits, axis=-1, keepdims=True)
                p = jnp.exp(logits - m)
                l = jnp.sum(p, axis=-1, keepdims=True)
                o = _dot(p.astype(BF16), mv_ref[s, :, sl]) / l
                att_scr[r0:r0 + seg, sl] = o.astype(BF16)
        o_ref[rows, :] = x1 + _dot(att_scr[rows, :], wxo_ref[...])


def _post(x, a, bo, co, wo, g, wq, qg, mk, mv, wxo, layer, *, t, tm):
    n = x.shape[0]
    w = M_HEADS * M_DH
    assert n % tm == 0 and (t % tm == 0 or tm % t == 0), (n, t, tm)
    if t >= tm:
        nseq, rps = 1, tm
        per = t // tm
        mem_map = lambda i: (i // per, 0, 0)
    else:
        nseq, rps = tm // t, t
        mem_map = lambda i: (i, 0, 0)
    tok = lambda cols: pl.BlockSpec((tm, cols), lambda i: (i, 0))
    const = lambda r, c: pl.BlockSpec((r, c), lambda i: (0, 0), pipeline_mode=pl.Buffered(1))
    stacked = lambda r, c: pl.BlockSpec((None, r, c), lambda i: (layer, 0, 0), pipeline_mode=pl.Buffered(1))
    return pl.pallas_call(
        functools.partial(_post_body, nseq=nseq, rps=rps),
        grid=(n // tm,),
        in_specs=[
            tok(D_MODEL), tok(A_WIDTH), tok(B_HEADS * B_DV), tok(C_HEADS * C_DV),
            stacked(D_MODEL, D_MODEL), const(1, D_MODEL), stacked(D_MODEL, w), const(1, M_DH),
            pl.BlockSpec((nseq, MEM_LEN, w), mem_map),
            pl.BlockSpec((nseq, MEM_LEN, w), mem_map),
            stacked(w, D_MODEL),
        ],
        out_specs=tok(D_MODEL),
        out_shape=jax.ShapeDtypeStruct((n, D_MODEL), F32),
        scratch_shapes=[pltpu.VMEM((tm, w), BF16)],
        compiler_params=_params("parallel"),
        name="post",
    )(x, a, bo, co, wo, g, wq, qg, mk, mv, wxo)


def _ffn_body(*refs, nseq, rps, per, cast_next):
    if cast_next:
        (x_ref, g_ref, wv_ref, wg_ref, cwv_ref, cwg_ref, cbv_ref, cbg_ref, hv_ref, hg_ref, wd_ref, upf_ref, dnf_ref,
         o_ref, nv_ref, ng_ref, upb_ref, dnb_ref, h_scr, ubuf, tail) = refs
        upb_ref[...] = upf_ref[...].astype(BF16)
        dnb_ref[...] = dnf_ref[...].astype(BF16)
    else:
        (x_ref, g_ref, wv_ref, wg_ref, cwv_ref, cwg_ref, cbv_ref, cbg_ref, hv_ref, hg_ref, wd_ref,
         o_ref, nv_ref, ng_ref, h_scr, ubuf, tail) = refs
    i = pl.program_id(0)
    j = pl.program_id(1)

    @pl.when(j == 0)
    def _():
        x = x_ref[...]
        h_scr[...] = _rms(x, g_ref[...]).astype(BF16)
        o_ref[...] = x

    if per > 1:
        @pl.when((i == 0) & (j == 0))
        def _():
            tail[...] = jnp.zeros(tail.shape, F32)

    seq_start = (i % per) == 0
    stride = rps + F_HIST
    tm = x_ref.shape[0]
    rc = tm // FFN_CHAINS
    assert rc % rps == 0 or rps % rc == 0

    seg = min(rc, rps)
    chains = [slice(r * rc, (r + 1) * rc) for r in range(FFN_CHAINS)]

    def up(rows):
        return _dot(h_scr[rows, :], wv_ref[...]), _dot(h_scr[rows, :], wg_ref[...])

    def keep(us):
        for rows, pair in zip(chains, us):
            for half, u in enumerate(pair):
                for q in range(rc // seg):
                    s, o = divmod(rows.start + q * seg, rps)
                    dst = s * stride + F_HIST + o
                    ubuf[half, dst:dst + seg, :] = u[q * seg:(q + 1) * seg]

    def conv(half, r0, cw_ref, cb_ref, hist_ref, new_ref):
        outs = []
        for q in range(rc // seg):
            s, o = divmod(r0 + q * seg, rps)
            base = s * stride
            if o == 0:
                if per == 1:
                    prev = hist_ref[s]
                else:
                    prev = jnp.where(seq_start, hist_ref[s], tail[half, j])
                ubuf[half, base:base + F_HIST, :] = prev
            c = cb_ref[...]
            for k in range(F_KW):
                off = base + F_HIST + o - (F_KW - 1) + k
                c = c + ubuf[half, off:off + seg, :] * cw_ref[k:k + 1, :]
            outs.append(c)
            if o + seg == rps:
                last = ubuf[half, base + rps:base + stride, :]
                new_ref[s] = last
                if per > 1:
                    tail[half, j] = last
        return outs[0] if len(outs) == 1 else jnp.concatenate(outs, axis=0)

    def down(rows):
        val = conv(0, rows.start, cwv_ref, cbv_ref, hv_ref, nv_ref)
        gate = conv(1, rows.start, cwg_ref, cbg_ref, hg_ref, ng_ref)
        act = (gate * jax.nn.sigmoid(gate) * val).astype(BF16)
        o_ref[rows, :] += _dot(act, wd_ref[...])

    keep([up(rows) for rows in chains])
    for rows in chains:
        down(rows)


def _ffn_cast_slabs(n, tm=1024, tn=512):
    steps = (n // tm) * (D_FF // tn)
    cu, cd = 2 * D_FF // steps, D_FF // steps
    ok = cu * steps == 2 * D_FF and cd * steps == D_FF and cu % LANES == 0 and cd % BF16_SUBLANES == 0
    return (cu, cd) if ok else None


def _ffn(x, g, w_up, cw, cb, hist, w_down, layer, *, t, cast_next=None, tm=1024, tn=512):
    n = x.shape[0]
    nj = D_FF // tn
    assert n % tm == 0 and (t % tm == 0 or tm % t == 0), (n, t, tm)
    if t >= tm:
        nseq, rps, per = 1, tm, t // tm
        seq_map = lambda i: i // per
    else:
        nseq, rps, per = tm // t, t, 1
        seq_map = lambda i: i
    bsz = hist.shape[0]
    hist_spec = lambda off: pl.BlockSpec((nseq, F_HIST, tn), lambda i, j: (seq_map(i), 0, j + off))
    new_spec = pl.BlockSpec((nseq, F_HIST, tn), lambda i, j: (i, 0, j))
    cast_args, cast_in, cast_out, cast_shapes = (), [], [], []
    if cast_next is not None:
        upf, dnf, nxt = cast_next
        cu, cd = _ffn_cast_slabs(n, tm, tn)
        cast_args = (upf, dnf)
        cast_in = [pl.BlockSpec((None, D_MODEL, cu), lambda i, j: (nxt, 0, i * nj + j)),
                   pl.BlockSpec((None, cd, D_MODEL), lambda i, j: (nxt, i * nj + j, 0))]
        cast_out = [pl.BlockSpec((None, D_MODEL, cu), lambda i, j: (0, 0, i * nj + j)),
                    pl.BlockSpec((None, cd, D_MODEL), lambda i, j: (0, i * nj + j, 0))]
        cast_shapes = [jax.ShapeDtypeStruct((1, D_MODEL, 2 * D_FF), BF16), jax.ShapeDtypeStruct((1, D_FF, D_MODEL), BF16)]
    results = pl.pallas_call(
        functools.partial(_ffn_body, nseq=nseq, rps=rps, per=per, cast_next=cast_next is not None),
        grid=(n // tm, nj),
        in_specs=[
            pl.BlockSpec((tm, D_MODEL), lambda i, j: (i, 0), pipeline_mode=pl.Buffered(1)),
            pl.BlockSpec((1, D_MODEL), lambda i, j: (0, 0)),
            pl.BlockSpec((None, D_MODEL, tn), lambda i, j: (layer, 0, j)),
            pl.BlockSpec((None, D_MODEL, tn), lambda i, j: (layer, 0, j + nj)),
            pl.BlockSpec((F_KW, tn), lambda i, j: (0, j)),
            pl.BlockSpec((F_KW, tn), lambda i, j: (0, j + nj)),
            pl.BlockSpec((1, tn), lambda i, j: (0, j)),
            pl.BlockSpec((1, tn), lambda i, j: (0, j + nj)),
            hist_spec(0), hist_spec(nj),
            pl.BlockSpec((None, tn, D_MODEL), lambda i, j: (layer, j, 0)),
        ] + cast_in,
        out_specs=[pl.BlockSpec((tm, D_MODEL), lambda i, j: (i, 0)), new_spec, new_spec] + cast_out,
        out_shape=[jax.ShapeDtypeStruct((n, D_MODEL), F32),
                   jax.ShapeDtypeStruct((bsz * per, F_HIST, D_FF), F32),
                   jax.ShapeDtypeStruct((bsz * per, F_HIST, D_FF), F32)] + cast_shapes,
        scratch_shapes=[
            pltpu.VMEM((tm, D_MODEL), BF16),
            pltpu.VMEM((2, nseq * (rps + F_HIST), tn), F32),
            pltpu.VMEM((2, nj, F_HIST, tn), F32),
        ],
        compiler_params=_params("arbitrary", "arbitrary"),
        name="ffn",
    )(x, g, w_up, w_up, cw, cw, cb, cb, hist, hist, w_down, *cast_args)
    x_out, tail_v, tail_g = results[:3]
    last = lambda a: a.reshape(bsz, per, F_HIST, D_FF)[:, per - 1]
    return x_out, last(tail_v), last(tail_g), tuple(results[3:])


def _rope_tables(pos0, t):
    half = B_DK // 2
    inv = 1.0 / (ROPE_BASE ** (jnp.arange(half, dtype=F32) / half))
    ang = (pos0 + jnp.arange(t, dtype=jnp.int32)).astype(F32)[:, None] * inv[None, :]
    cos = jnp.cos(ang)
    sin = jnp.sin(ang)
    return jnp.concatenate([cos, cos], axis=-1), jnp.concatenate([-sin, sin], axis=-1)


def _pad_rows(h, rows):
    return jnp.pad(h, ((0, 0), (rows - h.shape[1], 0), (0, 0)))


def _row(v):
    return v.reshape(1, -1).astype(F32)


def kernel(x_prompt, x_sample, mem_prompt, state_conv_a, state_ret, cache_diff_k, cache_diff_v, cache_mem_k,
           cache_mem_v, state_conv_f, norm1_g, w_in, conv_a_w, conv_a_b, ln_a_g, ln_a_b, ret_gn_g, diff_qn_g,
           diff_kn_g, diff_lq1, diff_lk1, diff_lq2, diff_lk2, diff_subln_g, w_out, rel_bias, norm2_g, mem_norm_g,
           w_xq, w_xk, w_xv, xqn_g, xkn_g, w_xo, norm3_g, w_up, conv_f_w, conv_f_b, w_down):
    bp, tp, _ = x_prompt.shape
    bs, ts, _ = x_sample.shape
    depth = w_in.shape[0]
    past = cache_diff_k.shape[2]
    cw_dim = C_HEADS * 2 * C_DQK

    tq = ATT_TILE
    r = np.arange(tq)
    rel_diag = r[None, :] - r[:, None]
    idx_p = np.stack([_t5_bucket(rel_diag - d * tq) for d in range(3)]).reshape(3 * tq, tq)
    vis_p = np.stack([(r[None, :] // CHUNK) <= (r[:, None] // CHUNK), np.ones((tq, tq), bool), np.ones((tq, tq), bool)])
    vis_p = vis_p.reshape(3 * tq, tq).astype(np.int32)
    bias_p = _bias_tiles(rel_bias, jnp.asarray(idx_p), jnp.asarray(vis_p)).reshape(C_HEADS, 3, tq, tq)
    rel_s = np.arange(past + ts)[None, :] - (past + np.arange(ts))[:, None]
    bias_s = _bias_tiles(rel_bias, jnp.asarray(_t5_bucket(rel_s)), jnp.ones(rel_s.shape, jnp.int32))
    bias_s_past, bias_s_new = bias_s[:, :, :past], bias_s[:, :, past:]

    cos_p, sin_p = _rope_tables(0, tp)
    cos_s, sin_s = _rope_tables(past, ts)

    xp = x_prompt.reshape(bp * tp, D_MODEL)
    xs = x_sample.reshape(bs * ts, D_MODEL)
    zero_a = jnp.zeros((bp, A_HIST, A_WIDTH), F32)
    zero_r = jnp.zeros((bp, B_HEADS, B_DK, B_DV), F32)
    zero_f = jnp.zeros((bp, F_HIST, 2 * D_FF), F32)

    outs = {k: [] for k in ("p_ca", "p_rs", "p_k", "p_v", "p_mk", "p_mv", "p_cf", "s_ca", "s_rs", "s_k", "s_v", "s_cf")}
    w_in_b = w_in.astype(BF16)
    w_out_b = w_out.astype(BF16)
    w_xq_b = w_xq.astype(BF16)
    w_xo_b = w_xo.astype(BF16)
    ride_along = _ffn_cast_slabs(bp * tp) is not None
    cast_layer = lambda k: (w_up[k:k + 1].astype(BF16), w_down[k:k + 1].astype(BF16))
    ffn_w = cast_layer(0)
    for l in range(depth):
        lam_init = 0.8 - 0.6 * math.exp(-0.3 * l)
        w_up_b, w_down_b = ffn_w
        has_next = l + 1 < depth
        lp = jnp.stack([diff_lq1[l], diff_lk1[l], diff_lq2[l], diff_lk2[l]]).astype(F32)
        qg2 = _row(jnp.concatenate([diff_qn_g[l], diff_qn_g[l]]))
        kg2 = _row(jnp.concatenate([diff_kn_g[l], diff_kn_g[l]]))
        mix_args = (conv_a_w[l], _row(conv_a_b[l]), _row(ln_a_g[l]), _row(ln_a_b[l]), _row(ret_gn_g[l]), qg2, kg2)
        subln = _row(diff_subln_g[l])

        def block(x, bsz, t, tt, hist_a, state_r, cos2, sin2, attend, mk_b, mv_b, hist_f, tm_post, cast_next=None):
            if tt >= FUSED_PROJ_MIN_ROWS:
                src = (x, _row(norm1_g[l]), w_in_b, l)
            else:
                src = _in_proj(x, _row(norm1_g[l]), w_in_b, l)
            a, bo, qn, kn, vb, ck, cv, nh, nr = _mixer(src, hist_a, state_r, cos2, sin2, *mix_args,
                                                        bsz=bsz, t=t, tt=tt)
            co = attend(qn, kn, vb)
            x = _post(x, a, bo, co, w_out_b, _row(norm2_g[l]), w_xq_b, _row(xqn_g[l]), mk_b, mv_b, w_xo_b,
                      l, t=t, tm=tm_post)
            x, nfv, nfg, next_w = _ffn(x, _row(norm3_g[l]), w_up_b, conv_f_w[l], _row(conv_f_b[l]), hist_f, w_down_b,
                                       0, t=t, cast_next=cast_next)
            new_f = jnp.concatenate([nfv[:, F_HIST - (F_KW - 1):], nfg[:, F_HIST - (F_KW - 1):]], axis=-1)
            return (x, nh[:, A_HIST - (A_KW - 1):], nr, ck.reshape(bsz, t, C_HEADS, 2 * C_DQK),
                    cv.reshape(bsz, t, C_HEADS, C_DV), new_f, next_w)

        mk, mv, mk_b, mv_b = _memkv(mem_prompt, _row(mem_norm_g[l]), w_xk[l].astype(BF16), w_xv[l].astype(BF16),
                                    _row(xkn_g[l]))
        attend_p = lambda qn, kn, vb: _attn_prompt(qn, kn, vb, bias_p, lp, subln, bsz=bp, t=tp, lam_init=lam_init)
        xp, ca, rs, kn_, vn_, cf, next_w = block(xp, bp, tp, 256, zero_a, zero_r, cos_p, sin_p, attend_p, mk_b, mv_b,
                                                 zero_f, POST_TILE,
                                                 cast_next=(w_up, w_down, l + 1) if has_next and ride_along else None)
        if has_next:
            ffn_w = next_w if ride_along else cast_layer(l + 1)
        outs["p_ca"].append(ca); outs["p_rs"].append(rs); outs["p_k"].append(kn_); outs["p_v"].append(vn_)
        outs["p_mk"].append(mk.reshape(bp, MEM_LEN, M_HEADS, M_DH))
        outs["p_mv"].append(mv.reshape(bp, MEM_LEN, M_HEADS, M_DH))
        outs["p_cf"].append(cf)

        attend_s = lambda qn, kn, vb: _attn_sample(qn, kn, vb, cache_diff_k, cache_diff_v, bias_s_past,
                                                   bias_s_new, lp, subln, l,
                                                   bsz=bs, t=ts, lam_init=lam_init)
        smk = cache_mem_k[l].reshape(bs, MEM_LEN, M_HEADS * M_DH).astype(BF16)
        smv = cache_mem_v[l].reshape(bs, MEM_LEN, M_HEADS * M_DH).astype(BF16)
        xs, sca, srs, skn, svn, scf, _ = block(xs, bs, ts, ts, _pad_rows(state_conv_a[l], A_HIST), state_ret[l],
                                            cos_s, sin_s, attend_s, smk, smv, _pad_rows(state_conv_f[l], F_HIST),
                                            POST_TILE)
        outs["s_ca"].append(sca); outs["s_rs"].append(srs); outs["s_k"].append(skn); outs["s_v"].append(svn)
        outs["s_cf"].append(scf)

    st = lambda k: jnp.stack(outs[k])
    return (xp.reshape(bp, tp, D_MODEL), xs.reshape(bs, ts, D_MODEL),
            st("p_ca"), st("p_rs"), st("p_k"), st("p_v"), st("p_mk"), st("p_mv"), st("p_cf"),
            st("s_ca"), st("s_rs"), st("s_k"), st("s_v"), st("s_cf"))
```

```python
import functools
import math

import numpy as np
import jax
import jax.numpy as jnp
from jax import lax
from jax.experimental import pallas as pl
from jax.experimental.pallas import tpu as pltpu

F32 = jnp.float32
BF16 = jnp.bfloat16
EPS = 1e-6
NEG_INF = -1e30

D_MODEL = 2048
CHUNK = 64
A_WIDTH = 512
A_KW = 31
A_HIST = 32
B_HEADS = 4
B_DK = 128
B_DV = 256
ROPE_BASE = 10000.0
C_HEADS = 4
C_DQK = 64
C_DV = 128
REL_BUCKETS = 32
REL_MAX_DIST = 128
M_HEADS = 4
M_DH = 128
MEM_LEN = 256
D_FF = 5632
F_KW = 3
F_HIST = 8

A_COLS = 2 * A_WIDTH
B_COLS = B_HEADS * (2 * B_DK + 2 * B_DV)
C_COLS = C_HEADS * (4 * C_DQK + C_DV)
IN_COLS = A_COLS + B_COLS + C_COLS
B_OFF = A_COLS
C_OFF = A_COLS + B_COLS
PROJ_GROUPS = (0, A_WIDTH, A_COLS, B_OFF + 2 * B_HEADS * B_DK, B_OFF + 2 * B_HEADS * B_DK + B_HEADS * B_DV, C_OFF,
               C_OFF + 2 * C_HEADS * 2 * C_DQK, IN_COLS)

VMEM_LIMIT_BYTES = 56 * 1024 * 1024
LANES = 128
BF16_SUBLANES = 16
ATT_TILE = 512
ATT_ROWS = 256
ATT_LANES = 128
FFN_CHAINS = 2
POST_TILE = 512
POST_CHAINS = 2
FUSED_PROJ_MIN_ROWS = 128


def _params(*sem):
    return pltpu.CompilerParams(dimension_semantics=sem, vmem_limit_bytes=VMEM_LIMIT_BYTES)


def _rms(x, g):
    return x * lax.rsqrt(jnp.mean(x * x, axis=-1, keepdims=True) + EPS) * g


def _dot(a, b):
    return jnp.dot(a, b, preferred_element_type=F32)


def _dot_nt(a, b):
    return lax.dot_general(a, b, (((1,), (1,)), ((), ())), preferred_element_type=F32)


def _dot_tn(a, b):
    return lax.dot_general(a, b, (((0,), (0,)), ((), ())), preferred_element_type=F32)


def _in_proj_body(x_ref, g_ref, w_ref, o_ref, h_ref):
    @pl.when(pl.program_id(1) == 0)
    def _():
        h_ref[...] = _rms(x_ref[...], g_ref[...]).astype(BF16)

    o_ref[...] = _dot(h_ref[...], w_ref[...])


def _in_proj(x, g, w, layer, *, tm=1024, tn=512):
    n = x.shape[0]
    cols = w.shape[2]
    return pl.pallas_call(
        _in_proj_body,
        grid=(n // tm, cols // tn),
        in_specs=[
            pl.BlockSpec((tm, D_MODEL), lambda i, j: (i, 0)),
            pl.BlockSpec((1, D_MODEL), lambda i, j: (0, 0)),
            pl.BlockSpec((None, D_MODEL, tn), lambda i, j: (layer, 0, j)),
        ],
        out_specs=pl.BlockSpec((tm, tn), lambda i, j: (i, j)),
        out_shape=jax.ShapeDtypeStruct((n, cols), F32),
        scratch_shapes=[pltpu.VMEM((tm, D_MODEL), BF16)],
        compiler_params=_params("parallel", "arbitrary"),
        name="in_proj",
    )(x, g, w)


def _projected_columns(x_ref, g_ref, w_ref):
    h = _rms(x_ref[...], g_ref[...]).astype(BF16)
    cache = {}

    def cols(c0, c1):
        g0, g1 = next((a, b) for a, b in zip(PROJ_GROUPS[:-1], PROJ_GROUPS[1:]) if a <= c0 and c1 <= b)
        if g0 not in cache:
            cache[g0] = _dot(h, w_ref[:, g0:g1])
        return cache[g0][:, c0 - g0:c1 - g0]

    return cols


def _mixer_body(*refs, tt, log_gammas, fused):
    if fused:
        cols = _projected_columns(*refs[:3])
        refs = refs[3:]
    else:
        proj_ref = refs[0]
        cols = lambda c0, c1: proj_ref[:, c0:c1]
        refs = refs[1:]
    (hist_ref, state_ref, cos_ref, sin_ref, cw_ref, cb_ref, lng_ref, lnb_ref, gng_ref, qg_ref, kg_ref,
     a_ref, bo_ref, qn_ref, kn_ref, vb_ref, ck_ref, cv_ref, nh_ref, nr_ref, aext, zbuf, sret) = refs
    t = pl.program_id(1)

    @pl.when(t == 0)
    def _():
        aext[0:A_HIST, :] = hist_ref[0]
        sret[...] = state_ref[0]

    glu = cols(0, A_WIDTH) * jax.nn.sigmoid(cols(A_WIDTH, A_COLS))
    aext[A_HIST:A_HIST + tt, :] = glu
    first = A_HIST - (A_KW - 1)
    acc = jnp.zeros((tt, A_WIDTH), F32) + cb_ref[...]
    for r in range(8):
        rows = tt if r == 0 else tt + 8
        z = None
        for off in range(r, first + A_KW, 8):
            if off < first:
                continue
            term = aext[off - r:off - r + rows, :] * cw_ref[off - first:off - first + 1, :]
            z = term if z is None else z + term
        if r == 0:
            acc = acc + z
        else:
            zbuf[r - 1] = z
            acc = acc + zbuf[r - 1, r:r + tt, :]
    mu = jnp.mean(acc, axis=-1, keepdims=True)
    xc = acc - mu
    var = jnp.mean(xc * xc, axis=-1, keepdims=True)
    ln = xc * lax.rsqrt(var + EPS) * lng_ref[...] + lnb_ref[...]
    a_ref[...] = (ln * jax.nn.sigmoid(ln)).astype(BF16)

    last_rows = aext[tt:tt + A_HIST, :]
    nh_ref[0] = last_rows
    aext[0:A_HIST, :] = last_rows

    cos = cos_ref[...]
    sin = sin_ref[...]
    ri = lax.broadcasted_iota(jnp.int32, (tt, tt), 0)
    ci = lax.broadcasted_iota(jnp.int32, (tt, tt), 1)
    dij = (ri - ci).astype(F32)
    causal = ri >= ci
    rowf = lax.broadcasted_iota(jnp.int32, (tt, 1), 0).astype(F32)
    for h in range(B_HEADS):
        lg = log_gammas[h]
        q = cols(B_OFF + h * B_DK, B_OFF + (h + 1) * B_DK)
        k = cols(B_OFF + B_HEADS * B_DK + h * B_DK, B_OFF + B_HEADS * B_DK + (h + 1) * B_DK)
        voff = B_OFF + 2 * B_HEADS * B_DK
        v = cols(voff + h * B_DV, voff + (h + 1) * B_DV)
        goff = voff + B_HEADS * B_DV
        g = cols(goff + h * B_DV, goff + (h + 1) * B_DV)
        qr = q * cos + pltpu.roll(q, B_DK // 2, 1) * sin
        kr = (k * cos + pltpu.roll(k, B_DK // 2, 1) * sin) * (B_DK ** -0.5)
        qb = qr.astype(BF16)
        vb = v.astype(BF16)
        decay = jnp.where(causal, jnp.exp(lg * jnp.maximum(dij, 0.0)), 0.0)
        scores = _dot_nt(qb, kr.astype(BF16)) * decay
        inner = _dot(scores.astype(BF16), vb)
        s_old = sret[h]
        cross = _dot(qb, s_old.astype(BF16)) * jnp.exp(lg * (rowf + 1.0))
        o = inner + cross
        kd = kr * jnp.exp(lg * (tt - 1.0 - rowf))
        s_new = s_old * math.exp(lg * tt) + _dot_tn(kd.astype(BF16), vb)
        sret[h] = s_new
        nr_ref[0, h] = s_new
        y = _rms(o, gng_ref[:, h * B_DV:(h + 1) * B_DV])
        bo_ref[:, h * B_DV:(h + 1) * B_DV] = (y * (g * jax.nn.sigmoid(g))).astype(BF16)

    lane = lax.broadcasted_iota(jnp.int32, (tt, 2 * C_DQK), 1)
    lo = lane < C_DQK

    def qk_norm(x, g2):
        sq = x * x
        s_lo = jnp.sum(jnp.where(lo, sq, 0.0), axis=-1, keepdims=True)
        s_hi = jnp.sum(jnp.where(lo, 0.0, sq), axis=-1, keepdims=True)
        ms = jnp.where(lo, s_lo, s_hi) * (1.0 / C_DQK)
        return x * lax.rsqrt(ms + EPS) * g2

    w = 2 * C_DQK
    for h in range(C_HEADS):
        cq = cols(C_OFF + h * w, C_OFF + (h + 1) * w)
        ck = cols(C_OFF + C_HEADS * w + h * w, C_OFF + C_HEADS * w + (h + 1) * w)
        qn = qk_norm(cq, qg_ref[...])
        kn = qk_norm(ck, kg_ref[...])
        qn_ref[:, h * w:(h + 1) * w] = (qn * (C_DQK ** -0.5)).astype(BF16)
        kn_ref[:, h * w:(h + 1) * w] = kn.astype(BF16)
        ck_ref[pl.ds(h, tt, stride=C_HEADS), :] = kn
    cv = cols(C_OFF + 2 * C_HEADS * w, IN_COLS)
    ones = jnp.ones((tt, C_DV), BF16)
    for h in range(C_HEADS):
        cv_ref[pl.ds(h, tt, stride=C_HEADS), :] = cv[:, h * C_DV:(h + 1) * C_DV]
        vb_ref[:, 2 * h * C_DV:(2 * h + 1) * C_DV] = cv[:, h * C_DV:(h + 1) * C_DV].astype(BF16)
        vb_ref[:, (2 * h + 1) * C_DV:(2 * h + 2) * C_DV] = ones


def _mixer(src, hist, state, cos2, sin2, cw, cb, lng, lnb, gng, qg2, kg2, *, bsz, t, tt):
    n = bsz * t
    nt = t // tt
    log_gammas = tuple(math.log(1.0 - 2.0 ** (-5.0 - h)) for h in range(B_HEADS))
    tok = lambda cols: pl.BlockSpec((tt, cols), lambda b, i: (b * nt + i, 0))
    const = lambda r, c: pl.BlockSpec((r, c), lambda b, i: (0, 0))
    cw_dim = C_HEADS * 2 * C_DQK
    fused = isinstance(src, tuple)
    if fused:
        x, g1, w_in, layer = src
        src_args = (x, g1, w_in)
        src_specs = [tok(D_MODEL), const(1, D_MODEL),
                     pl.BlockSpec((None, D_MODEL, IN_COLS), lambda b, i: (layer, 0, 0), pipeline_mode=pl.Buffered(1))]
    else:
        src_args = (src,)
        src_specs = [tok(IN_COLS)]
    return pl.pallas_call(
        functools.partial(_mixer_body, tt=tt, log_gammas=log_gammas, fused=fused),
        grid=(bsz, nt),
        in_specs=src_specs + [
            pl.BlockSpec((1, A_HIST, A_WIDTH), lambda b, i: (b, 0, 0)),
            pl.BlockSpec((1, B_HEADS, B_DK, B_DV), lambda b, i: (b, 0, 0, 0)),
            pl.BlockSpec((tt, B_DK), lambda b, i: (i, 0)),
            pl.BlockSpec((tt, B_DK), lambda b, i: (i, 0)),
            const(A_KW, A_WIDTH), const(1, A_WIDTH), const(1, A_WIDTH), const(1, A_WIDTH),
            const(1, B_HEADS * B_DV), const(1, 2 * C_DQK), const(1, 2 * C_DQK),
        ],
        out_specs=[
            tok(A_WIDTH), tok(B_HEADS * B_DV), tok(cw_dim), tok(cw_dim), tok(2 * C_HEADS * C_DV),
            pl.BlockSpec((None, tt * C_HEADS, 2 * C_DQK), lambda b, i: (b, i, 0)),
            pl.BlockSpec((None, tt * C_HEADS, C_DV), lambda b, i: (b, i, 0)),
            pl.BlockSpec((1, A_HIST, A_WIDTH), lambda b, i: (b, 0, 0)),
            pl.BlockSpec((1, B_HEADS, B_DK, B_DV), lambda b, i: (b, 0, 0, 0)),
        ],
        out_shape=[
            jax.ShapeDtypeStruct((n, A_WIDTH), BF16),
            jax.ShapeDtypeStruct((n, B_HEADS * B_DV), BF16),
            jax.ShapeDtypeStruct((n, cw_dim), BF16),
            jax.ShapeDtypeStruct((n, cw_dim), BF16),
            jax.ShapeDtypeStruct((n, 2 * C_HEADS * C_DV), BF16),
            jax.ShapeDtypeStruct((bsz, t * C_HEADS, 2 * C_DQK), F32),
            jax.ShapeDtypeStruct((bsz, t * C_HEADS, C_DV), F32),
            jax.ShapeDtypeStruct((bsz, A_HIST, A_WIDTH), F32),
            jax.ShapeDtypeStruct((bsz, B_HEADS, B_DK, B_DV), F32),
        ],
        scratch_shapes=[pltpu.VMEM((A_HIST + tt, A_WIDTH), F32), pltpu.VMEM((7, tt + 8, A_WIDTH), F32),
                        pltpu.VMEM((B_HEADS, B_DK, B_DV), F32)],
        compiler_params=_params("parallel", "arbitrary"),
        name="mixer",
    )(*src_args, hist, state, cos2, sin2, cw, cb, lng, lnb, gng, qg2, kg2)


def _t5_bucket(rel):
    half = REL_BUCKETS // 2
    exact = half // 2
    n = np.abs(rel)
    large = exact + (np.log(np.maximum(n, 1).astype(np.float32) / exact) / math.log(REL_MAX_DIST / exact)
                     * (half - exact)).astype(np.int32)
    large = np.minimum(large, half - 1)
    return (np.where(rel > 0, half, 0) + np.where(n < exact, n, large)).astype(np.int32)


def _bias_body(rb_ref, idx_ref, vis_ref, o_ref):
    h = pl.program_id(0)
    idx = idx_ref[...]
    far = rb_ref[REL_BUCKETS // 2 - 1, h]
    acc = jnp.zeros(idx.shape, F32)
    for b in range(REL_BUCKETS):
        acc = jnp.where(idx == b, rb_ref[b, h], acc)
    o_ref[0] = jnp.where(vis_ref[...] != 0, acc - far, NEG_INF)


def _bias_tiles(rel_bias, idx, vis):
    r, c = idx.shape
    return pl.pallas_call(
        _bias_body,
        grid=(C_HEADS,),
        in_specs=[
            pl.BlockSpec(memory_space=pltpu.SMEM),
            pl.BlockSpec((r, c), lambda h: (0, 0)),
            pl.BlockSpec((r, c), lambda h: (0, 0)),
        ],
        out_specs=pl.BlockSpec((1, r, c), lambda h: (h, 0, 0)),
        out_shape=jax.ShapeDtypeStruct((C_HEADS, r, c), F32),
        compiler_params=_params("arbitrary"),
        name="bias_tiles",
    )(rel_bias, idx, vis)


def _lambda(lp_ref, lam_init):
    lp = lp_ref[...]
    e1 = jnp.exp(jnp.sum(lp[0:1] * lp[1:2], axis=-1, keepdims=True))
    e2 = jnp.exp(jnp.sum(lp[2:3] * lp[3:4], axis=-1, keepdims=True))
    return e1 - e2 + lam_init


def _stack_maps(q):
    lane = lax.broadcasted_iota(jnp.int32, q.shape, 1)
    zero = jnp.zeros_like(q)
    return jnp.concatenate([jnp.where(lane < C_DQK, q, zero), jnp.where(lane < C_DQK, zero, q)], axis=0)


def _attn_finish(acc, l, lam, g, lam_init, tq):
    o = acc / l
    o = o[0:tq] - lam * o[tq:2 * tq]
    return _rms(o, g) * (1.0 - lam_init)


def _attn_prompt_body(*refs, tq, lam_init, cast_ride):
    if cast_ride:
        (q_ref, k_ref, v_ref, bias_ref, lp_ref, g_ref, upf_ref, dnf_ref, o_ref, upb_ref, dnb_ref,
         qq_scr, m_scr, alpha_scr, p_scr, acc_scr) = refs
        _cast_slabs(upf_ref, dnf_ref, upb_ref, dnb_ref)
    else:
        q_ref, k_ref, v_ref, bias_ref, lp_ref, g_ref, o_ref, qq_scr, m_scr, alpha_scr, p_scr, acc_scr = refs
    qi = pl.program_id(2)
    qq_scr[...] = _stack_maps(q_ref[...])
    m_scr[...] = jnp.full(m_scr.shape, NEG_INF, F32)
    acc_scr[...] = jnp.zeros(acc_scr.shape, F32)

    def tile(ref, idx):
        start = idx * tq if isinstance(idx, int) else pl.multiple_of(idx * tq, tq)
        return ref[pl.ds(start, tq), :]

    nl = tq // ATT_LANES
    chunks = [slice(c * ATT_ROWS, (c + 1) * ATT_ROWS) for c in range(2 * tq // ATT_ROWS)]

    def accumulate(rows, vb):
        alpha = alpha_scr[rows, :]
        acc_scr[rows, :] = jnp.concatenate([alpha, alpha], axis=-1) * acc_scr[rows, :] + _dot(p_scr[rows, :], vb)

    def step(ki, bias_idx, first=False):
        kb = tile(k_ref, ki)
        if not first:
            v_prev = tile(v_ref, ki - 1)
        for rows in chunks:
            if not first:
                accumulate(rows, v_prev)
            s = _dot_nt(qq_scr[rows, :], kb)
            if bias_idx is not None:
                s = s + bias_ref[0, bias_idx, pl.ds(rows.start % tq, ATT_ROWS), :]
            slabs = [s[:, j * ATT_LANES:(j + 1) * ATT_LANES] for j in range(nl)]
            m_old = m_scr[rows, :]
            m_new = jnp.maximum(m_old, jnp.max(functools.reduce(jnp.maximum, slabs), axis=-1, keepdims=True))
            alpha_scr[rows, :] = jnp.exp(m_old - m_new)
            p_scr[rows, :] = jnp.concatenate([jnp.exp(sl - m_new) for sl in slabs], axis=-1).astype(BF16)
            m_scr[rows, :] = m_new

    n_far = jnp.maximum(qi - 1, 0)
    step(0, jnp.minimum(qi, 2), first=True)

    def far_body(ki, c):
        step(ki, None)
        return c

    lax.fori_loop(1, n_far, far_body, 0)

    def near_body(ki, c):
        step(ki, qi - ki)
        return c

    lax.fori_loop(jnp.maximum(n_far, 1), qi + 1, near_body, 0)

    v_last = tile(v_ref, qi)
    for rows in chunks:
        accumulate(rows, v_last)
    lam = _lambda(lp_ref, lam_init)
    acc = acc_scr[...]
    o_ref[...] = _attn_finish(acc[:, 0:C_DV], acc[:, C_DV:2 * C_DV], lam, g_ref[...], lam_init, tq).astype(BF16)


def _attn_prompt(qn, kn, vb, bias, lp, g, *, bsz, t, lam_init, cast_ride=None):
    tq = ATT_TILE
    nq = t // tq
    n = bsz * t
    w = 2 * C_DQK
    cast_args, cast_in, cast_out, cast_shapes = (), [], [], []
    if cast_ride is not None:
        cast_args, cast_in, cast_out, cast_shapes = _weight_cast_ride(
            *cast_ride, bsz * C_HEADS * nq, lambda b, h, i: (b * C_HEADS + h) * nq + i)
    results = pl.pallas_call(
        functools.partial(_attn_prompt_body, tq=tq, lam_init=lam_init, cast_ride=cast_ride is not None),
        grid=(bsz, C_HEADS, nq),
        in_specs=[
            pl.BlockSpec((tq, w), lambda b, h, i: (b * nq + i, h)),
            pl.BlockSpec((t, w), lambda b, h, i: (b, h)),
            pl.BlockSpec((t, 2 * C_DV), lambda b, h, i: (b, h)),
            pl.BlockSpec((1, 3, tq, tq), lambda b, h, i: (h, 0, 0, 0)),
            pl.BlockSpec((4, C_DQK), lambda b, h, i: (0, 0)),
            pl.BlockSpec((1, C_DV), lambda b, h, i: (0, 0)),
        ] + cast_in,
        out_specs=[pl.BlockSpec((tq, C_DV), lambda b, h, i: (b * nq + i, h))] + cast_out,
        out_shape=[jax.ShapeDtypeStruct((n, C_HEADS * C_DV), BF16)] + cast_shapes,
        scratch_shapes=[pltpu.VMEM((2 * tq, w), BF16), pltpu.VMEM((2 * tq, ATT_LANES), F32),
                        pltpu.VMEM((2 * tq, ATT_LANES), F32), pltpu.VMEM((2 * tq, tq), BF16),
                        pltpu.VMEM((2 * tq, 2 * C_DV), F32)],
        compiler_params=_params("arbitrary", "arbitrary", "arbitrary"),
        name="attn_prompt",
    )(qn, kn, vb, bias, lp, g, *cast_args)
    return results[0], tuple(results[1:])


def _attn_sample_body(q_ref, k_ref, v_ref, pk_ref, pv_ref, bp_ref, bn_ref, lp_ref, g_ref, o_ref, *, tq, lam_init):
    lam = _lambda(lp_ref, lam_init)
    w = 2 * C_DQK
    for h in range(C_HEADS):
        qq = _stack_maps(q_ref[:, h * w:(h + 1) * w])
        bp = bp_ref[h]
        bn = bn_ref[h]
        past = pk_ref.shape[0] // C_HEADS
        head_rows = pl.ds(h, past, stride=C_HEADS)
        s_p = _dot_nt(qq, pk_ref[head_rows, :].astype(BF16)) + jnp.concatenate([bp, bp], axis=0)
        s_n = _dot_nt(qq, k_ref[:, h * w:(h + 1) * w]) + jnp.concatenate([bn, bn], axis=0)
        m = jnp.maximum(jnp.max(s_p, axis=-1, keepdims=True), jnp.max(s_n, axis=-1, keepdims=True))
        p_p = jnp.exp(s_p - m)
        p_n = jnp.exp(s_n - m)
        l = jnp.sum(p_p, axis=-1, keepdims=True) + jnp.sum(p_n, axis=-1, keepdims=True)
        acc = (_dot(p_p.astype(BF16), pv_ref[head_rows, :].astype(BF16))
               + _dot(p_n.astype(BF16), v_ref[:, 2 * h * C_DV:(2 * h + 1) * C_DV]))
        o_ref[:, h * C_DV:(h + 1) * C_DV] = _attn_finish(acc, l, lam, g_ref[...], lam_init, tq).astype(BF16)


def _attn_sample(qn, kn, vb, past_k, past_v, bias_p, bias_n, lp, g, layer, *, bsz, t, lam_init):
    n = bsz * t
    w = 2 * C_DQK
    depth, _, past = past_k.shape[:3]
    past_k = past_k.reshape(depth, bsz, past * C_HEADS, w)
    past_v = past_v.reshape(depth, bsz, past * C_HEADS, C_DV)
    full = lambda a: pl.BlockSpec(a.shape, lambda b: (0,) * a.ndim)
    return pl.pallas_call(
        functools.partial(_attn_sample_body, tq=t, lam_init=lam_init),
        grid=(bsz,),
        in_specs=[
            pl.BlockSpec((t, C_HEADS * w), lambda b: (b, 0)),
            pl.BlockSpec((t, C_HEADS * w), lambda b: (b, 0)),
            pl.BlockSpec((t, 2 * C_HEADS * C_DV), lambda b: (b, 0)),
            pl.BlockSpec((None, None, past * C_HEADS, w), lambda b: (layer, b, 0, 0)),
            pl.BlockSpec((None, None, past * C_HEADS, C_DV), lambda b: (layer, b, 0, 0)),
            full(bias_p), full(bias_n),
            pl.BlockSpec((4, C_DQK), lambda b: (0, 0)),
            pl.BlockSpec((1, C_DV), lambda b: (0, 0)),
        ],
        out_specs=pl.BlockSpec((t, C_HEADS * C_DV), lambda b: (b, 0)),
        out_shape=jax.ShapeDtypeStruct((n, C_HEADS * C_DV), BF16),
        compiler_params=_params("parallel"),
        name="attn_sample",
    )(qn, kn, vb, past_k, past_v, bias_p, bias_n, lp, g)


def _memkv_body(mem_ref, g_ref, wk_ref, wv_ref, kg_ref, k_ref, v_ref, kb_ref, vb_ref):
    hm = _rms(mem_ref[0], g_ref[...]).astype(BF16)
    k = _dot(hm, wk_ref[...])
    v = _dot(hm, wv_ref[...])
    for h in range(M_HEADS):
        kn = _rms(k[:, h * M_DH:(h + 1) * M_DH], kg_ref[...])
        k_ref[0, :, h * M_DH:(h + 1) * M_DH] = kn
        kb_ref[0, :, h * M_DH:(h + 1) * M_DH] = kn.astype(BF16)
    v_ref[0] = v
    vb_ref[0] = v.astype(BF16)


def _memkv(mem, g, wk, wv, kg):
    bsz, m, _ = mem.shape
    w = M_HEADS * M_DH
    blk = pl.BlockSpec((1, m, w), lambda b: (b, 0, 0))
    return pl.pallas_call(
        _memkv_body,
        grid=(bsz,),
        in_specs=[
            pl.BlockSpec((1, m, D_MODEL), lambda b: (b, 0, 0)),
            pl.BlockSpec((1, D_MODEL), lambda b: (0, 0)),
            pl.BlockSpec((D_MODEL, w), lambda b: (0, 0)),
            pl.BlockSpec((D_MODEL, w), lambda b: (0, 0)),
            pl.BlockSpec((1, M_DH), lambda b: (0, 0)),
        ],
        out_specs=[blk, blk, blk, blk],
        out_shape=[jax.ShapeDtypeStruct((bsz, m, w), F32), jax.ShapeDtypeStruct((bsz, m, w), F32),
                   jax.ShapeDtypeStruct((bsz, m, w), BF16), jax.ShapeDtypeStruct((bsz, m, w), BF16)],
        compiler_params=_params("parallel"),
        name="memkv",
    )(mem, g, wk, wv, kg)


def _post_body(x_ref, a_ref, bo_ref, co_ref, wo_ref, g_ref, wq_ref, qg_ref, mk_ref, mv_ref, wxo_ref,
               o_ref, att_scr, *, nseq, rps):
    rc = x_ref.shape[0] // POST_CHAINS
    seg = min(rc, rps)
    for c in range(POST_CHAINS):
        rows = slice(c * rc, (c + 1) * rc)
        y = (_dot(a_ref[rows, :], wo_ref[0:A_WIDTH, :])
             + _dot(bo_ref[rows, :], wo_ref[A_WIDTH:A_WIDTH + B_HEADS * B_DV, :])
             + _dot(co_ref[rows, :], wo_ref[A_WIDTH + B_HEADS * B_DV:, :]))
        x1 = x_ref[rows, :] + y
        q = _dot(_rms(x1, g_ref[...]).astype(BF16), wq_ref[...])
        for h in range(M_HEADS):
            sl = slice(h * M_DH, (h + 1) * M_DH)
            qn = _rms(q[:, sl], qg_ref[...]).astype(BF16)
            for u in range(rc // seg):
                r0 = c * rc + u * seg
                s = r0 // rps
                logits = _dot_nt(qn[u * seg:(u + 1) * seg], mk_ref[s, :, sl]) * (M_DH ** -0.5)
                m = jnp.max(logits, axis=-1, keepdims=True)
                p = jnp.exp(logits - m)
                l = jnp.sum(p, axis=-1, keepdims=True)
                o = _dot(p.astype(BF16), mv_ref[s, :, sl]) / l
                att_scr[r0:r0 + seg, sl] = o.astype(BF16)
        o_ref[rows, :] = x1 + _dot(att_scr[rows, :], wxo_ref[...])


def _post(x, a, bo, co, wo, g, wq, qg, mk, mv, wxo, layer, *, t, tm):
    n = x.shape[0]
    w = M_HEADS * M_DH
    assert n % tm == 0 and (t % tm == 0 or tm % t == 0), (n, t, tm)
    if t >= tm:
        nseq, rps = 1, tm
        per = t // tm
        mem_map = lambda i: (i // per, 0, 0)
    else:
        nseq, rps = tm // t, t
        mem_map = lambda i: (i, 0, 0)
    tok = lambda cols: pl.BlockSpec((tm, cols), lambda i: (i, 0))
    const = lambda r, c: pl.BlockSpec((r, c), lambda i: (0, 0), pipeline_mode=pl.Buffered(1))
    stacked = lambda r, c: pl.BlockSpec((None, r, c), lambda i: (layer, 0, 0), pipeline_mode=pl.Buffered(1))
    return pl.pallas_call(
        functools.partial(_post_body, nseq=nseq, rps=rps),
        grid=(n // tm,),
        in_specs=[
            tok(D_MODEL), tok(A_WIDTH), tok(B_HEADS * B_DV), tok(C_HEADS * C_DV),
            stacked(D_MODEL, D_MODEL), const(1, D_MODEL), stacked(D_MODEL, w), const(1, M_DH),
            pl.BlockSpec((nseq, MEM_LEN, w), mem_map),
            pl.BlockSpec((nseq, MEM_LEN, w), mem_map),
            stacked(w, D_MODEL),
        ],
        out_specs=tok(D_MODEL),
        out_shape=jax.ShapeDtypeStruct((n, D_MODEL), F32),
        scratch_shapes=[pltpu.VMEM((tm, w), BF16)],
        compiler_params=_params("parallel"),
        name="post",
    )(x, a, bo, co, wo, g, wq, qg, mk, mv, wxo)


def _ffn_body(*refs, nseq, rps, per, cast_next):
    if cast_next:
        (x_ref, g_ref, wv_ref, wg_ref, cwv_ref, cwg_ref, cbv_ref, cbg_ref, hv_ref, hg_ref, wd_ref, upf_ref, dnf_ref,
         o_ref, nv_ref, ng_ref, upb_ref, dnb_ref, h_scr, ubuf, tail) = refs
        _cast_slabs(upf_ref, dnf_ref, upb_ref, dnb_ref)
    else:
        (x_ref, g_ref, wv_ref, wg_ref, cwv_ref, cwg_ref, cbv_ref, cbg_ref, hv_ref, hg_ref, wd_ref,
         o_ref, nv_ref, ng_ref, h_scr, ubuf, tail) = refs
    i = pl.program_id(0)
    j = pl.program_id(1)

    @pl.when(j == 0)
    def _():
        x = x_ref[...]
        h_scr[...] = _rms(x, g_ref[...]).astype(BF16)
        o_ref[...] = x

    if per > 1:
        @pl.when((i == 0) & (j == 0))
        def _():
            tail[...] = jnp.zeros(tail.shape, F32)

    seq_start = (i % per) == 0
    stride = rps + F_HIST
    tm = x_ref.shape[0]
    rc = tm // FFN_CHAINS
    assert rc % rps == 0 or rps % rc == 0

    seg = min(rc, rps)
    chains = [slice(r * rc, (r + 1) * rc) for r in range(FFN_CHAINS)]

    def up(rows):
        return _dot(h_scr[rows, :], wv_ref[...]), _dot(h_scr[rows, :], wg_ref[...])

    def keep(us):
        for rows, pair in zip(chains, us):
            for half, u in enumerate(pair):
                for q in range(rc // seg):
                    s, o = divmod(rows.start + q * seg, rps)
                    dst = s * stride + F_HIST + o
                    ubuf[half, dst:dst + seg, :] = u[q * seg:(q + 1) * seg]

    def conv(half, r0, cw_ref, cb_ref, hist_ref, new_ref):
        outs = []
        for q in range(rc // seg):
            s, o = divmod(r0 + q * seg, rps)
            base = s * stride
            if o == 0:
                if per == 1:
                    prev = hist_ref[s]
                else:
                    prev = jnp.where(seq_start, hist_ref[s], tail[half, j])
                ubuf[half, base:base + F_HIST, :] = prev
            c = cb_ref[...]
            for k in range(F_KW):
                off = base + F_HIST + o - (F_KW - 1) + k
                c = c + ubuf[half, off:off + seg, :] * cw_ref[k:k + 1, :]
            outs.append(c)
            if o + seg == rps:
                last = ubuf[half, base + rps:base + stride, :]
                new_ref[s] = last
                if per > 1:
                    tail[half, j] = last
        return outs[0] if len(outs) == 1 else jnp.concatenate(outs, axis=0)

    def down(rows):
        val = conv(0, rows.start, cwv_ref, cbv_ref, hv_ref, nv_ref)
        gate = conv(1, rows.start, cwg_ref, cbg_ref, hg_ref, ng_ref)
        act = (gate * jax.nn.sigmoid(gate) * val).astype(BF16)
        o_ref[rows, :] += _dot(act, wd_ref[...])

    keep([up(rows) for rows in chains])
    for rows in chains:
        down(rows)


def _slab(total, unit, steps):
    return next(unit * k for k in range(1, total // unit + 1) if total % (unit * k) == 0 and total // (unit * k) <= steps)


def _weight_cast_ride(w_up, w_down, layer, steps, step_of):
    cu, cd = _slab(2 * D_FF, LANES, steps), _slab(D_FF, BF16_SUBLANES, steps)
    up_idx = lambda *g: jnp.minimum(step_of(*g), 2 * D_FF // cu - 1)
    dn_idx = lambda *g: jnp.minimum(step_of(*g), D_FF // cd - 1)
    in_specs = [pl.BlockSpec((None, D_MODEL, cu), lambda *g: (layer, 0, up_idx(*g))),
                pl.BlockSpec((None, cd, D_MODEL), lambda *g: (layer, dn_idx(*g), 0))]
    out_specs = [pl.BlockSpec((None, D_MODEL, cu), lambda *g: (0, 0, up_idx(*g))),
                 pl.BlockSpec((None, cd, D_MODEL), lambda *g: (0, dn_idx(*g), 0))]
    out_shapes = [jax.ShapeDtypeStruct((1, D_MODEL, 2 * D_FF), BF16), jax.ShapeDtypeStruct((1, D_FF, D_MODEL), BF16)]
    return (w_up, w_down), in_specs, out_specs, out_shapes


def _cast_slabs(upf_ref, dnf_ref, upb_ref, dnb_ref):
    upb_ref[...] = upf_ref[...].astype(BF16)
    dnb_ref[...] = dnf_ref[...].astype(BF16)


def _ffn(x, g, w_up, cw, cb, hist, w_down, layer, *, t, cast_next=None, tm=1024, tn=512):
    n = x.shape[0]
    nj = D_FF // tn
    assert n % tm == 0 and (t % tm == 0 or tm % t == 0), (n, t, tm)
    if t >= tm:
        nseq, rps, per = 1, tm, t // tm
        seq_map = lambda i: i // per
    else:
        nseq, rps, per = tm // t, t, 1
        seq_map = lambda i: i
    bsz = hist.shape[0]
    hist_spec = lambda off: pl.BlockSpec((nseq, F_HIST, tn), lambda i, j: (seq_map(i), 0, j + off))
    new_spec = pl.BlockSpec((nseq, F_HIST, tn), lambda i, j: (i, 0, j))
    cast_args, cast_in, cast_out, cast_shapes = (), [], [], []
    if cast_next is not None:
        cast_args, cast_in, cast_out, cast_shapes = _weight_cast_ride(*cast_next, (n // tm) * nj,
                                                                      lambda i, j: i * nj + j)
    results = pl.pallas_call(
        functools.partial(_ffn_body, nseq=nseq, rps=rps, per=per, cast_next=cast_next is not None),
        grid=(n // tm, nj),
        in_specs=[
            pl.BlockSpec((tm, D_MODEL), lambda i, j: (i, 0), pipeline_mode=pl.Buffered(1)),
            pl.BlockSpec((1, D_MODEL), lambda i, j: (0, 0)),
            pl.BlockSpec((None, D_MODEL, tn), lambda i, j: (layer, 0, j)),
            pl.BlockSpec((None, D_MODEL, tn), lambda i, j: (layer, 0, j + nj)),
            pl.BlockSpec((F_KW, tn), lambda i, j: (0, j)),
            pl.BlockSpec((F_KW, tn), lambda i, j: (0, j + nj)),
            pl.BlockSpec((1, tn), lambda i, j: (0, j)),
            pl.BlockSpec((1, tn), lambda i, j: (0, j + nj)),
            hist_spec(0), hist_spec(nj),
            pl.BlockSpec((None, tn, D_MODEL), lambda i, j: (layer, j, 0)),
        ] + cast_in,
        out_specs=[pl.BlockSpec((tm, D_MODEL), lambda i, j: (i, 0)), new_spec, new_spec] + cast_out,
        out_shape=[jax.ShapeDtypeStruct((n, D_MODEL), F32),
                   jax.ShapeDtypeStruct((bsz * per, F_HIST, D_FF), F32),
                   jax.ShapeDtypeStruct((bsz * per, F_HIST, D_FF), F32)] + cast_shapes,
        scratch_shapes=[
            pltpu.VMEM((tm, D_MODEL), BF16),
            pltpu.VMEM((2, nseq * (rps + F_HIST), tn), F32),
            pltpu.VMEM((2, nj, F_HIST, tn), F32),
        ],
        compiler_params=_params("arbitrary", "arbitrary"),
        name="ffn",
    )(x, g, w_up, w_up, cw, cw, cb, cb, hist, hist, w_down, *cast_args)
    x_out, tail_v, tail_g = results[:3]
    last = lambda a: a.reshape(bsz, per, F_HIST, D_FF)[:, per - 1]
    return x_out, last(tail_v), last(tail_g), tuple(results[3:])


def _rope_tables(pos0, t):
    half = B_DK // 2
    inv = 1.0 / (ROPE_BASE ** (jnp.arange(half, dtype=F32) / half))
    ang = (pos0 + jnp.arange(t, dtype=jnp.int32)).astype(F32)[:, None] * inv[None, :]
    cos = jnp.cos(ang)
    sin = jnp.sin(ang)
    return jnp.concatenate([cos, cos], axis=-1), jnp.concatenate([-sin, sin], axis=-1)


def _pad_rows(h, rows):
    return jnp.pad(h, ((0, 0), (rows - h.shape[1], 0), (0, 0)))


def _row(v):
    return v.reshape(1, -1).astype(F32)


def kernel(x_prompt, x_sample, mem_prompt, state_conv_a, state_ret, cache_diff_k, cache_diff_v, cache_mem_k,
           cache_mem_v, state_conv_f, norm1_g, w_in, conv_a_w, conv_a_b, ln_a_g, ln_a_b, ret_gn_g, diff_qn_g,
           diff_kn_g, diff_lq1, diff_lk1, diff_lq2, diff_lk2, diff_subln_g, w_out, rel_bias, norm2_g, mem_norm_g,
           w_xq, w_xk, w_xv, xqn_g, xkn_g, w_xo, norm3_g, w_up, conv_f_w, conv_f_b, w_down):
    bp, tp, _ = x_prompt.shape
    bs, ts, _ = x_sample.shape
    depth = w_in.shape[0]
    past = cache_diff_k.shape[2]
    cw_dim = C_HEADS * 2 * C_DQK

    tq = ATT_TILE
    r = np.arange(tq)
    rel_diag = r[None, :] - r[:, None]
    idx_p = np.stack([_t5_bucket(rel_diag - d * tq) for d in range(3)]).reshape(3 * tq, tq)
    vis_p = np.stack([(r[None, :] // CHUNK) <= (r[:, None] // CHUNK), np.ones((tq, tq), bool), np.ones((tq, tq), bool)])
    vis_p = vis_p.reshape(3 * tq, tq).astype(np.int32)
    bias_p = _bias_tiles(rel_bias, jnp.asarray(idx_p), jnp.asarray(vis_p)).reshape(C_HEADS, 3, tq, tq)
    rel_s = np.arange(past + ts)[None, :] - (past + np.arange(ts))[:, None]
    bias_s = _bias_tiles(rel_bias, jnp.asarray(_t5_bucket(rel_s)), jnp.ones(rel_s.shape, jnp.int32))
    bias_s_past, bias_s_new = bias_s[:, :, :past], bias_s[:, :, past:]

    cos_p, sin_p = _rope_tables(0, tp)
    cos_s, sin_s = _rope_tables(past, ts)

    xp = x_prompt.reshape(bp * tp, D_MODEL)
    xs = x_sample.reshape(bs * ts, D_MODEL)
    zero_a = jnp.zeros((bp, A_HIST, A_WIDTH), F32)
    zero_r = jnp.zeros((bp, B_HEADS, B_DK, B_DV), F32)
    zero_f = jnp.zeros((bp, F_HIST, 2 * D_FF), F32)

    outs = {k: [] for k in ("p_ca", "p_rs", "p_k", "p_v", "p_mk", "p_mv", "p_cf", "s_ca", "s_rs", "s_k", "s_v", "s_cf")}
    w_in_b = w_in.astype(BF16)
    w_out_b = w_out.astype(BF16)
    w_xq_b = w_xq.astype(BF16)
    w_xo_b = w_xo.astype(BF16)
    ffn_w = [None]
    for l in range(depth):
        lam_init = 0.8 - 0.6 * math.exp(-0.3 * l)
        has_next = l + 1 < depth
        lp = jnp.stack([diff_lq1[l], diff_lk1[l], diff_lq2[l], diff_lk2[l]]).astype(F32)
        qg2 = _row(jnp.concatenate([diff_qn_g[l], diff_qn_g[l]]))
        kg2 = _row(jnp.concatenate([diff_kn_g[l], diff_kn_g[l]]))
        mix_args = (conv_a_w[l], _row(conv_a_b[l]), _row(ln_a_g[l]), _row(ln_a_b[l]), _row(ret_gn_g[l]), qg2, kg2)
        subln = _row(diff_subln_g[l])

        def block(x, bsz, t, tt, hist_a, state_r, cos2, sin2, attend, mk_b, mv_b, hist_f, tm_post, cast_next=None):
            if tt >= FUSED_PROJ_MIN_ROWS:
                src = (x, _row(norm1_g[l]), w_in_b, l)
            else:
                src = _in_proj(x, _row(norm1_g[l]), w_in_b, l)
            a, bo, qn, kn, vb, ck, cv, nh, nr = _mixer(src, hist_a, state_r, cos2, sin2, *mix_args,
                                                        bsz=bsz, t=t, tt=tt)
            co = attend(qn, kn, vb)
            x = _post(x, a, bo, co, w_out_b, _row(norm2_g[l]), w_xq_b, _row(xqn_g[l]), mk_b, mv_b, w_xo_b,
                      l, t=t, tm=tm_post)
            w_up_b, w_down_b = ffn_w[0]
            x, nfv, nfg, next_w = _ffn(x, _row(norm3_g[l]), w_up_b, conv_f_w[l], _row(conv_f_b[l]), hist_f, w_down_b,
                                       0, t=t, cast_next=cast_next)
            new_f = jnp.concatenate([nfv[:, F_HIST - (F_KW - 1):], nfg[:, F_HIST - (F_KW - 1):]], axis=-1)
            return (x, nh[:, A_HIST - (A_KW - 1):], nr, ck.reshape(bsz, t, C_HEADS, 2 * C_DQK),
                    cv.reshape(bsz, t, C_HEADS, C_DV), new_f, next_w)

        mk, mv, mk_b, mv_b = _memkv(mem_prompt, _row(mem_norm_g[l]), w_xk[l].astype(BF16), w_xv[l].astype(BF16),
                                    _row(xkn_g[l]))
        def attend_p(qn, kn, vb):
            ride = (w_up, w_down, l) if ffn_w[0] is None else None
            co, cast = _attn_prompt(qn, kn, vb, bias_p, lp, subln, bsz=bp, t=tp, lam_init=lam_init, cast_ride=ride)
            if ride is not None:
                ffn_w[0] = cast
            return co

        xp, ca, rs, kn_, vn_, cf, next_w = block(xp, bp, tp, 256, zero_a, zero_r, cos_p, sin_p, attend_p, mk_b, mv_b,
                                                 zero_f, POST_TILE,
                                                 cast_next=(w_up, w_down, l + 1) if has_next else None)
        outs["p_ca"].append(ca); outs["p_rs"].append(rs); outs["p_k"].append(kn_); outs["p_v"].append(vn_)
        outs["p_mk"].append(mk.reshape(bp, MEM_LEN, M_HEADS, M_DH))
        outs["p_mv"].append(mv.reshape(bp, MEM_LEN, M_HEADS, M_DH))
        outs["p_cf"].append(cf)

        attend_s = lambda qn, kn, vb: _attn_sample(qn, kn, vb, cache_diff_k, cache_diff_v, bias_s_past,
                                                   bias_s_new, lp, subln, l,
                                                   bsz=bs, t=ts, lam_init=lam_init)
        smk = cache_mem_k[l].reshape(bs, MEM_LEN, M_HEADS * M_DH).astype(BF16)
        smv = cache_mem_v[l].reshape(bs, MEM_LEN, M_HEADS * M_DH).astype(BF16)
        xs, sca, srs, skn, svn, scf, _ = block(xs, bs, ts, ts, _pad_rows(state_conv_a[l], A_HIST), state_ret[l],
                                            cos_s, sin_s, attend_s, smk, smv, _pad_rows(state_conv_f[l], F_HIST),
                                            POST_TILE)
        outs["s_ca"].append(sca); outs["s_rs"].append(srs); outs["s_k"].append(skn); outs["s_v"].append(svn)
        outs["s_cf"].append(scf)
        if has_next:
            ffn_w[0] = next_w

    st = lambda k: jnp.stack(outs[k])
    return (xp.reshape(bp, tp, D_MODEL), xs.reshape(bs, ts, D_MODEL),
            st("p_ca"), st("p_rs"), st("p_k"), st("p_v"), st("p_mk"), st("p_mv"), st("p_cf"),
            st("s_ca"), st("s_rs"), st("s_k"), st("s_v"), st("s_cf"))
```

```python
import functools
import math

import numpy as np
import jax
import jax.numpy as jnp
from jax import lax
from jax.experimental import pallas as pl
from jax.experimental.pallas import tpu as pltpu

F32 = jnp.float32
BF16 = jnp.bfloat16
EPS = 1e-6
NEG_INF = -1e30

D_MODEL = 2048
CHUNK = 64
A_WIDTH = 512
A_KW = 31
A_HIST = 32
B_HEADS = 4
B_DK = 128
B_DV = 256
ROPE_BASE = 10000.0
C_HEADS = 4
C_DQK = 64
C_DV = 128
REL_BUCKETS = 32
REL_MAX_DIST = 128
M_HEADS = 4
M_DH = 128
MEM_LEN = 256
D_FF = 5632
F_KW = 3
F_HIST = 8

A_COLS = 2 * A_WIDTH
B_COLS = B_HEADS * (2 * B_DK + 2 * B_DV)
C_COLS = C_HEADS * (4 * C_DQK + C_DV)
IN_COLS = A_COLS + B_COLS + C_COLS
B_OFF = A_COLS
C_OFF = A_COLS + B_COLS
PROJ_GROUPS = (0, A_WIDTH, A_COLS, B_OFF + 2 * B_HEADS * B_DK, B_OFF + 2 * B_HEADS * B_DK + B_HEADS * B_DV, C_OFF,
               C_OFF + 2 * C_HEADS * 2 * C_DQK, IN_COLS)

VMEM_LIMIT_BYTES = 56 * 1024 * 1024
LANES = 128
BF16_SUBLANES = 16
TOKEN_TILE = 1024
COL_TILE = 512
MIX_TILE = 256
ATT_TILE = 512
ATT_ROWS = 256
ATT_LANES = 128
FFN_CHAINS = 2
POST_TILE = 512
POST_CHAINS = 2
FUSED_PROJ_MIN_ROWS = 128


def _params(*sem):
    return pltpu.CompilerParams(dimension_semantics=sem, vmem_limit_bytes=VMEM_LIMIT_BYTES)


def _rms(x, g):
    return x * lax.rsqrt(jnp.mean(x * x, axis=-1, keepdims=True) + EPS) * g


def _dot(a, b):
    return jnp.dot(a, b, preferred_element_type=F32)


def _dot_nt(a, b):
    return lax.dot_general(a, b, (((1,), (1,)), ((), ())), preferred_element_type=F32)


def _dot_tn(a, b):
    return lax.dot_general(a, b, (((0,), (0,)), ((), ())), preferred_element_type=F32)


def _in_proj_body(x_ref, g_ref, w_ref, o_ref, h_ref):
    @pl.when(pl.program_id(1) == 0)
    def _():
        h_ref[...] = _rms(x_ref[...], g_ref[...]).astype(BF16)

    o_ref[...] = _dot(h_ref[...], w_ref[...])


def _in_proj(x, g, w, layer, *, tm=TOKEN_TILE, tn=COL_TILE):
    n = x.shape[0]
    cols = w.shape[2]
    return pl.pallas_call(
        _in_proj_body,
        grid=(n // tm, cols // tn),
        in_specs=[
            pl.BlockSpec((tm, D_MODEL), lambda i, j: (i, 0)),
            pl.BlockSpec((1, D_MODEL), lambda i, j: (0, 0)),
            pl.BlockSpec((None, D_MODEL, tn), lambda i, j: (layer, 0, j)),
        ],
        out_specs=pl.BlockSpec((tm, tn), lambda i, j: (i, j)),
        out_shape=jax.ShapeDtypeStruct((n, cols), F32),
        scratch_shapes=[pltpu.VMEM((tm, D_MODEL), BF16)],
        compiler_params=_params("parallel", "arbitrary"),
        name="in_proj",
    )(x, g, w)


def _projected_columns(x_ref, g_ref, w_ref):
    h = _rms(x_ref[...], g_ref[...]).astype(BF16)
    cache = {}

    def cols(c0, c1):
        g0, g1 = next((a, b) for a, b in zip(PROJ_GROUPS[:-1], PROJ_GROUPS[1:]) if a <= c0 and c1 <= b)
        if g0 not in cache:
            cache[g0] = _dot(h, w_ref[:, g0:g1])
        return cache[g0][:, c0 - g0:c1 - g0]

    return cols


def _mixer_body(*refs, tt, log_gammas, fused):
    if fused:
        cols = _projected_columns(*refs[:3])
        refs = refs[3:]
    else:
        proj_ref = refs[0]
        cols = lambda c0, c1: proj_ref[:, c0:c1]
        refs = refs[1:]
    (hist_ref, state_ref, cos_ref, sin_ref, cw_ref, cb_ref, lng_ref, lnb_ref, gng_ref, qg_ref, kg_ref,
     a_ref, bo_ref, qn_ref, kn_ref, vb_ref, ck_ref, cv_ref, nh_ref, nr_ref, aext, zbuf, sret) = refs
    t = pl.program_id(1)

    @pl.when(t == 0)
    def _():
        aext[0:A_HIST, :] = hist_ref[0]
        sret[...] = state_ref[0]

    glu = cols(0, A_WIDTH) * jax.nn.sigmoid(cols(A_WIDTH, A_COLS))
    aext[A_HIST:A_HIST + tt, :] = glu
    first = A_HIST - (A_KW - 1)
    acc = jnp.zeros((tt, A_WIDTH), F32) + cb_ref[...]
    for r in range(8):
        rows = tt if r == 0 else tt + 8
        z = None
        for off in range(r, first + A_KW, 8):
            if off < first:
                continue
            term = aext[off - r:off - r + rows, :] * cw_ref[off - first:off - first + 1, :]
            z = term if z is None else z + term
        if r == 0:
            acc = acc + z
        else:
            zbuf[r - 1] = z
            acc = acc + zbuf[r - 1, r:r + tt, :]
    mu = jnp.mean(acc, axis=-1, keepdims=True)
    xc = acc - mu
    var = jnp.mean(xc * xc, axis=-1, keepdims=True)
    ln = xc * lax.rsqrt(var + EPS) * lng_ref[...] + lnb_ref[...]
    a_ref[...] = (ln * jax.nn.sigmoid(ln)).astype(BF16)

    last_rows = aext[tt:tt + A_HIST, :]
    nh_ref[0] = last_rows
    aext[0:A_HIST, :] = last_rows

    cos = cos_ref[...]
    sin = sin_ref[...]
    ri = lax.broadcasted_iota(jnp.int32, (tt, tt), 0)
    ci = lax.broadcasted_iota(jnp.int32, (tt, tt), 1)
    dij = (ri - ci).astype(F32)
    causal = ri >= ci
    rowf = lax.broadcasted_iota(jnp.int32, (tt, 1), 0).astype(F32)
    for h in range(B_HEADS):
        lg = log_gammas[h]
        q = cols(B_OFF + h * B_DK, B_OFF + (h + 1) * B_DK)
        k = cols(B_OFF + B_HEADS * B_DK + h * B_DK, B_OFF + B_HEADS * B_DK + (h + 1) * B_DK)
        voff = B_OFF + 2 * B_HEADS * B_DK
        v = cols(voff + h * B_DV, voff + (h + 1) * B_DV)
        goff = voff + B_HEADS * B_DV
        g = cols(goff + h * B_DV, goff + (h + 1) * B_DV)
        qr = q * cos + pltpu.roll(q, B_DK // 2, 1) * sin
        kr = (k * cos + pltpu.roll(k, B_DK // 2, 1) * sin) * (B_DK ** -0.5)
        qb = qr.astype(BF16)
        vb = v.astype(BF16)
        decay = jnp.where(causal, jnp.exp(lg * jnp.maximum(dij, 0.0)), 0.0)
        scores = _dot_nt(qb, kr.astype(BF16)) * decay
        inner = _dot(scores.astype(BF16), vb)
        s_old = sret[h]
        cross = _dot(qb, s_old.astype(BF16)) * jnp.exp(lg * (rowf + 1.0))
        o = inner + cross
        kd = kr * jnp.exp(lg * (tt - 1.0 - rowf))
        s_new = s_old * math.exp(lg * tt) + _dot_tn(kd.astype(BF16), vb)
        sret[h] = s_new
        nr_ref[0, h] = s_new
        y = _rms(o, gng_ref[:, h * B_DV:(h + 1) * B_DV])
        bo_ref[:, h * B_DV:(h + 1) * B_DV] = (y * (g * jax.nn.sigmoid(g))).astype(BF16)

    lane = lax.broadcasted_iota(jnp.int32, (tt, 2 * C_DQK), 1)
    lo = lane < C_DQK

    def qk_norm(x, g2):
        sq = x * x
        s_lo = jnp.sum(jnp.where(lo, sq, 0.0), axis=-1, keepdims=True)
        s_hi = jnp.sum(jnp.where(lo, 0.0, sq), axis=-1, keepdims=True)
        ms = jnp.where(lo, s_lo, s_hi) * (1.0 / C_DQK)
        return x * lax.rsqrt(ms + EPS) * g2

    w = 2 * C_DQK
    for h in range(C_HEADS):
        cq = cols(C_OFF + h * w, C_OFF + (h + 1) * w)
        ck = cols(C_OFF + C_HEADS * w + h * w, C_OFF + C_HEADS * w + (h + 1) * w)
        qn = qk_norm(cq, qg_ref[...])
        kn = qk_norm(ck, kg_ref[...])
        qn_ref[:, h * w:(h + 1) * w] = (qn * (C_DQK ** -0.5)).astype(BF16)
        kn_ref[:, h * w:(h + 1) * w] = kn.astype(BF16)
        ck_ref[pl.ds(h, tt, stride=C_HEADS), :] = kn
    cv = cols(C_OFF + 2 * C_HEADS * w, IN_COLS)
    ones = jnp.ones((tt, C_DV), BF16)
    for h in range(C_HEADS):
        cv_ref[pl.ds(h, tt, stride=C_HEADS), :] = cv[:, h * C_DV:(h + 1) * C_DV]
        vb_ref[:, 2 * h * C_DV:(2 * h + 1) * C_DV] = cv[:, h * C_DV:(h + 1) * C_DV].astype(BF16)
        vb_ref[:, (2 * h + 1) * C_DV:(2 * h + 2) * C_DV] = ones


def _mixer(src, hist, state, cos2, sin2, cw, cb, lng, lnb, gng, qg2, kg2, *, bsz, t, tt):
    n = bsz * t
    nt = t // tt
    log_gammas = tuple(math.log(1.0 - 2.0 ** (-5.0 - h)) for h in range(B_HEADS))
    tok = lambda cols: pl.BlockSpec((tt, cols), lambda b, i: (b * nt + i, 0))
    const = lambda r, c: pl.BlockSpec((r, c), lambda b, i: (0, 0))
    cw_dim = C_HEADS * 2 * C_DQK
    fused = isinstance(src, tuple)
    if fused:
        x, g1, w_in, layer = src
        src_args = (x, g1, w_in)
        src_specs = [tok(D_MODEL), const(1, D_MODEL),
                     pl.BlockSpec((None, D_MODEL, IN_COLS), lambda b, i: (layer, 0, 0), pipeline_mode=pl.Buffered(1))]
    else:
        src_args = (src,)
        src_specs = [tok(IN_COLS)]
    return pl.pallas_call(
        functools.partial(_mixer_body, tt=tt, log_gammas=log_gammas, fused=fused),
        grid=(bsz, nt),
        in_specs=src_specs + [
            pl.BlockSpec((1, A_HIST, A_WIDTH), lambda b, i: (b, 0, 0)),
            pl.BlockSpec((1, B_HEADS, B_DK, B_DV), lambda b, i: (b, 0, 0, 0)),
            pl.BlockSpec((tt, B_DK), lambda b, i: (i, 0)),
            pl.BlockSpec((tt, B_DK), lambda b, i: (i, 0)),
            const(A_KW, A_WIDTH), const(1, A_WIDTH), const(1, A_WIDTH), const(1, A_WIDTH),
            const(1, B_HEADS * B_DV), const(1, 2 * C_DQK), const(1, 2 * C_DQK),
        ],
        out_specs=[
            tok(A_WIDTH), tok(B_HEADS * B_DV), tok(cw_dim), tok(cw_dim), tok(2 * C_HEADS * C_DV),
            pl.BlockSpec((None, tt * C_HEADS, 2 * C_DQK), lambda b, i: (b, i, 0)),
            pl.BlockSpec((None, tt * C_HEADS, C_DV), lambda b, i: (b, i, 0)),
            pl.BlockSpec((1, A_HIST, A_WIDTH), lambda b, i: (b, 0, 0)),
            pl.BlockSpec((1, B_HEADS, B_DK, B_DV), lambda b, i: (b, 0, 0, 0)),
        ],
        out_shape=[
            jax.ShapeDtypeStruct((n, A_WIDTH), BF16),
            jax.ShapeDtypeStruct((n, B_HEADS * B_DV), BF16),
            jax.ShapeDtypeStruct((n, cw_dim), BF16),
            jax.ShapeDtypeStruct((n, cw_dim), BF16),
            jax.ShapeDtypeStruct((n, 2 * C_HEADS * C_DV), BF16),
            jax.ShapeDtypeStruct((bsz, t * C_HEADS, 2 * C_DQK), F32),
            jax.ShapeDtypeStruct((bsz, t * C_HEADS, C_DV), F32),
            jax.ShapeDtypeStruct((bsz, A_HIST, A_WIDTH), F32),
            jax.ShapeDtypeStruct((bsz, B_HEADS, B_DK, B_DV), F32),
        ],
        scratch_shapes=[pltpu.VMEM((A_HIST + tt, A_WIDTH), F32), pltpu.VMEM((7, tt + 8, A_WIDTH), F32),
                        pltpu.VMEM((B_HEADS, B_DK, B_DV), F32)],
        compiler_params=_params("parallel", "arbitrary"),
        name="mixer",
    )(*src_args, hist, state, cos2, sin2, cw, cb, lng, lnb, gng, qg2, kg2)


def _t5_bucket(rel):
    half = REL_BUCKETS // 2
    exact = half // 2
    n = np.abs(rel)
    large = exact + (np.log(np.maximum(n, 1).astype(np.float32) / exact) / math.log(REL_MAX_DIST / exact)
                     * (half - exact)).astype(np.int32)
    large = np.minimum(large, half - 1)
    return (np.where(rel > 0, half, 0) + np.where(n < exact, n, large)).astype(np.int32)


def _bias_body(rb_ref, idx_ref, vis_ref, o_ref):
    h = pl.program_id(0)
    idx = idx_ref[...]
    far = rb_ref[REL_BUCKETS // 2 - 1, h]
    acc = jnp.zeros(idx.shape, F32)
    for b in range(REL_BUCKETS):
        acc = jnp.where(idx == b, rb_ref[b, h], acc)
    o_ref[0] = jnp.where(vis_ref[...] != 0, acc - far, NEG_INF)


def _bias_tiles(rel_bias, idx, vis):
    r, c = idx.shape
    return pl.pallas_call(
        _bias_body,
        grid=(C_HEADS,),
        in_specs=[
            pl.BlockSpec(memory_space=pltpu.SMEM),
            pl.BlockSpec((r, c), lambda h: (0, 0)),
            pl.BlockSpec((r, c), lambda h: (0, 0)),
        ],
        out_specs=pl.BlockSpec((1, r, c), lambda h: (h, 0, 0)),
        out_shape=jax.ShapeDtypeStruct((C_HEADS, r, c), F32),
        compiler_params=_params("arbitrary"),
        name="bias_tiles",
    )(rel_bias, idx, vis)


def _lambda(lp_ref, lam_init):
    lp = lp_ref[...]
    e1 = jnp.exp(jnp.sum(lp[0:1] * lp[1:2], axis=-1, keepdims=True))
    e2 = jnp.exp(jnp.sum(lp[2:3] * lp[3:4], axis=-1, keepdims=True))
    return e1 - e2 + lam_init


def _stack_maps(q):
    lane = lax.broadcasted_iota(jnp.int32, q.shape, 1)
    zero = jnp.zeros_like(q)
    return jnp.concatenate([jnp.where(lane < C_DQK, q, zero), jnp.where(lane < C_DQK, zero, q)], axis=0)


def _attn_finish(acc, l, lam, g, lam_init, tq):
    o = acc / l
    o = o[0:tq] - lam * o[tq:2 * tq]
    return _rms(o, g) * (1.0 - lam_init)


def _attn_prompt_body(*refs, tq, lam_init, n_cast):
    _cast_slabs(refs[6:6 + n_cast], refs[7 + n_cast:7 + 2 * n_cast])
    q_ref, k_ref, v_ref, bias_ref, lp_ref, g_ref = refs[:6]
    o_ref = refs[6 + n_cast]
    qq_scr, m_scr, alpha_scr, p_scr, acc_scr = refs[7 + 2 * n_cast:]
    qi = pl.program_id(2)
    qq_scr[...] = _stack_maps(q_ref[...])
    m_scr[...] = jnp.full(m_scr.shape, NEG_INF, F32)
    acc_scr[...] = jnp.zeros(acc_scr.shape, F32)

    def tile(ref, idx):
        start = idx * tq if isinstance(idx, int) else pl.multiple_of(idx * tq, tq)
        return ref[pl.ds(start, tq), :]

    nl = tq // ATT_LANES
    chunks = [slice(c * ATT_ROWS, (c + 1) * ATT_ROWS) for c in range(2 * tq // ATT_ROWS)]

    def accumulate(rows, vb):
        alpha = alpha_scr[rows, :]
        acc_scr[rows, :] = jnp.concatenate([alpha, alpha], axis=-1) * acc_scr[rows, :] + _dot(p_scr[rows, :], vb)

    def step(ki, bias_idx, first=False):
        kb = tile(k_ref, ki)
        if not first:
            v_prev = tile(v_ref, ki - 1)
        for rows in chunks:
            if not first:
                accumulate(rows, v_prev)
            s = _dot_nt(qq_scr[rows, :], kb)
            if bias_idx is not None:
                s = s + bias_ref[0, bias_idx, pl.ds(rows.start % tq, ATT_ROWS), :]
            slabs = [s[:, j * ATT_LANES:(j + 1) * ATT_LANES] for j in range(nl)]
            m_old = m_scr[rows, :]
            m_new = jnp.maximum(m_old, jnp.max(functools.reduce(jnp.maximum, slabs), axis=-1, keepdims=True))
            alpha_scr[rows, :] = jnp.exp(m_old - m_new)
            p_scr[rows, :] = jnp.concatenate([jnp.exp(sl - m_new) for sl in slabs], axis=-1).astype(BF16)
            m_scr[rows, :] = m_new

    n_far = jnp.maximum(qi - 1, 0)
    step(0, jnp.minimum(qi, 2), first=True)

    def far_body(ki, c):
        step(ki, None)
        return c

    lax.fori_loop(1, n_far, far_body, 0)

    def near_body(ki, c):
        step(ki, qi - ki)
        return c

    lax.fori_loop(jnp.maximum(n_far, 1), qi + 1, near_body, 0)

    v_last = tile(v_ref, qi)
    for rows in chunks:
        accumulate(rows, v_last)
    lam = _lambda(lp_ref, lam_init)
    acc = acc_scr[...]
    o_ref[...] = _attn_finish(acc[:, 0:C_DV], acc[:, C_DV:2 * C_DV], lam, g_ref[...], lam_init, tq).astype(BF16)


def _attn_prompt(qn, kn, vb, bias, lp, g, *, bsz, t, lam_init, cast_ride=()):
    tq = ATT_TILE
    nq = t // tq
    n = bsz * t
    w = 2 * C_DQK
    cast_args, cast_in, cast_out, cast_shapes = (), [], [], []
    if cast_ride:
        cast_args, cast_in, cast_out, cast_shapes = _weight_cast_ride(
            cast_ride, bsz * C_HEADS * nq, lambda b, h, i: (b * C_HEADS + h) * nq + i)
    results = pl.pallas_call(
        functools.partial(_attn_prompt_body, tq=tq, lam_init=lam_init, n_cast=len(cast_args)),
        grid=(bsz, C_HEADS, nq),
        in_specs=[
            pl.BlockSpec((tq, w), lambda b, h, i: (b * nq + i, h)),
            pl.BlockSpec((t, w), lambda b, h, i: (b, h)),
            pl.BlockSpec((t, 2 * C_DV), lambda b, h, i: (b, h)),
            pl.BlockSpec((1, 3, tq, tq), lambda b, h, i: (h, 0, 0, 0)),
            pl.BlockSpec((4, C_DQK), lambda b, h, i: (0, 0)),
            pl.BlockSpec((1, C_DV), lambda b, h, i: (0, 0)),
        ] + cast_in,
        out_specs=[pl.BlockSpec((tq, C_DV), lambda b, h, i: (b * nq + i, h))] + cast_out,
        out_shape=[jax.ShapeDtypeStruct((n, C_HEADS * C_DV), BF16)] + cast_shapes,
        scratch_shapes=[pltpu.VMEM((2 * tq, w), BF16), pltpu.VMEM((2 * tq, ATT_LANES), F32),
                        pltpu.VMEM((2 * tq, ATT_LANES), F32), pltpu.VMEM((2 * tq, tq), BF16),
                        pltpu.VMEM((2 * tq, 2 * C_DV), F32)],
        compiler_params=_params("arbitrary", "arbitrary", "arbitrary"),
        name="attn_prompt",
    )(qn, kn, vb, bias, lp, g, *cast_args)
    return results[0], tuple(results[1:])


def _attn_sample_body(q_ref, k_ref, v_ref, pk_ref, pv_ref, bp_ref, bn_ref, lp_ref, g_ref, o_ref, *, tq, lam_init):
    lam = _lambda(lp_ref, lam_init)
    w = 2 * C_DQK
    for h in range(C_HEADS):
        qq = _stack_maps(q_ref[:, h * w:(h + 1) * w])
        bp = bp_ref[h]
        bn = bn_ref[h]
        past = pk_ref.shape[0] // C_HEADS
        head_rows = pl.ds(h, past, stride=C_HEADS)
        s_p = _dot_nt(qq, pk_ref[head_rows, :].astype(BF16)) + jnp.concatenate([bp, bp], axis=0)
        s_n = _dot_nt(qq, k_ref[:, h * w:(h + 1) * w]) + jnp.concatenate([bn, bn], axis=0)
        m = jnp.maximum(jnp.max(s_p, axis=-1, keepdims=True), jnp.max(s_n, axis=-1, keepdims=True))
        p_p = jnp.exp(s_p - m)
        p_n = jnp.exp(s_n - m)
        l = jnp.sum(p_p, axis=-1, keepdims=True) + jnp.sum(p_n, axis=-1, keepdims=True)
        acc = (_dot(p_p.astype(BF16), pv_ref[head_rows, :].astype(BF16))
               + _dot(p_n.astype(BF16), v_ref[:, 2 * h * C_DV:(2 * h + 1) * C_DV]))
        o_ref[:, h * C_DV:(h + 1) * C_DV] = _attn_finish(acc, l, lam, g_ref[...], lam_init, tq).astype(BF16)


def _attn_sample(qn, kn, vb, past_k, past_v, bias_p, bias_n, lp, g, layer, *, bsz, t, lam_init):
    n = bsz * t
    w = 2 * C_DQK
    depth, _, past = past_k.shape[:3]
    past_k = past_k.reshape(depth, bsz, past * C_HEADS, w)
    past_v = past_v.reshape(depth, bsz, past * C_HEADS, C_DV)
    full = lambda a: pl.BlockSpec(a.shape, lambda b: (0,) * a.ndim)
    return pl.pallas_call(
        functools.partial(_attn_sample_body, tq=t, lam_init=lam_init),
        grid=(bsz,),
        in_specs=[
            pl.BlockSpec((t, C_HEADS * w), lambda b: (b, 0)),
            pl.BlockSpec((t, C_HEADS * w), lambda b: (b, 0)),
            pl.BlockSpec((t, 2 * C_HEADS * C_DV), lambda b: (b, 0)),
            pl.BlockSpec((None, None, past * C_HEADS, w), lambda b: (layer, b, 0, 0)),
            pl.BlockSpec((None, None, past * C_HEADS, C_DV), lambda b: (layer, b, 0, 0)),
            full(bias_p), full(bias_n),
            pl.BlockSpec((4, C_DQK), lambda b: (0, 0)),
            pl.BlockSpec((1, C_DV), lambda b: (0, 0)),
        ],
        out_specs=pl.BlockSpec((t, C_HEADS * C_DV), lambda b: (b, 0)),
        out_shape=jax.ShapeDtypeStruct((n, C_HEADS * C_DV), BF16),
        compiler_params=_params("parallel"),
        name="attn_sample",
    )(qn, kn, vb, past_k, past_v, bias_p, bias_n, lp, g)


def _memkv_body(mem_ref, g_ref, wk_ref, wv_ref, kg_ref, k_ref, v_ref, kb_ref, vb_ref):
    hm = _rms(mem_ref[0], g_ref[...]).astype(BF16)
    k = _dot(hm, wk_ref[...])
    v = _dot(hm, wv_ref[...])
    for h in range(M_HEADS):
        kn = _rms(k[:, h * M_DH:(h + 1) * M_DH], kg_ref[...])
        k_ref[0, :, h * M_DH:(h + 1) * M_DH] = kn
        kb_ref[0, :, h * M_DH:(h + 1) * M_DH] = kn.astype(BF16)
    v_ref[0] = v
    vb_ref[0] = v.astype(BF16)


def _memkv(mem, g, wk, wv, kg):
    bsz, m, _ = mem.shape
    w = M_HEADS * M_DH
    blk = pl.BlockSpec((1, m, w), lambda b: (b, 0, 0))
    return pl.pallas_call(
        _memkv_body,
        grid=(bsz,),
        in_specs=[
            pl.BlockSpec((1, m, D_MODEL), lambda b: (b, 0, 0)),
            pl.BlockSpec((1, D_MODEL), lambda b: (0, 0)),
            pl.BlockSpec((D_MODEL, w), lambda b: (0, 0)),
            pl.BlockSpec((D_MODEL, w), lambda b: (0, 0)),
            pl.BlockSpec((1, M_DH), lambda b: (0, 0)),
        ],
        out_specs=[blk, blk, blk, blk],
        out_shape=[jax.ShapeDtypeStruct((bsz, m, w), F32), jax.ShapeDtypeStruct((bsz, m, w), F32),
                   jax.ShapeDtypeStruct((bsz, m, w), BF16), jax.ShapeDtypeStruct((bsz, m, w), BF16)],
        compiler_params=_params("parallel"),
        name="memkv",
    )(mem, g, wk, wv, kg)


def _post_body(x_ref, a_ref, bo_ref, co_ref, wo_ref, g_ref, wq_ref, qg_ref, mk_ref, mv_ref, wxo_ref,
               o_ref, att_scr, *, nseq, rps):
    rc = x_ref.shape[0] // POST_CHAINS
    seg = min(rc, rps)
    for c in range(POST_CHAINS):
        rows = slice(c * rc, (c + 1) * rc)
        y = (_dot(a_ref[rows, :], wo_ref[0:A_WIDTH, :])
             + _dot(bo_ref[rows, :], wo_ref[A_WIDTH:A_WIDTH + B_HEADS * B_DV, :])
             + _dot(co_ref[rows, :], wo_ref[A_WIDTH + B_HEADS * B_DV:, :]))
        x1 = x_ref[rows, :] + y
        q = _dot(_rms(x1, g_ref[...]).astype(BF16), wq_ref[...])
        for h in range(M_HEADS):
            sl = slice(h * M_DH, (h + 1) * M_DH)
            qn = _rms(q[:, sl], qg_ref[...]).astype(BF16)
            for u in range(rc // seg):
                r0 = c * rc + u * seg
                s = r0 // rps
                logits = _dot_nt(qn[u * seg:(u + 1) * seg], mk_ref[s, :, sl]) * (M_DH ** -0.5)
                m = jnp.max(logits, axis=-1, keepdims=True)
                p = jnp.exp(logits - m)
                l = jnp.sum(p, axis=-1, keepdims=True)
                o = _dot(p.astype(BF16), mv_ref[s, :, sl]) / l
                att_scr[r0:r0 + seg, sl] = o.astype(BF16)
        o_ref[rows, :] = x1 + _dot(att_scr[rows, :], wxo_ref[...])


def _post(x, a, bo, co, wo, g, wq, qg, mk, mv, wxo, layer, *, t, tm):
    n = x.shape[0]
    w = M_HEADS * M_DH
    assert n % tm == 0 and (t % tm == 0 or tm % t == 0), (n, t, tm)
    if t >= tm:
        nseq, rps = 1, tm
        per = t // tm
        mem_map = lambda i: (i // per, 0, 0)
    else:
        nseq, rps = tm // t, t
        mem_map = lambda i: (i, 0, 0)
    tok = lambda cols: pl.BlockSpec((tm, cols), lambda i: (i, 0))
    const = lambda r, c: pl.BlockSpec((r, c), lambda i: (0, 0), pipeline_mode=pl.Buffered(1))
    stacked = lambda r, c: pl.BlockSpec((None, r, c), lambda i: (layer, 0, 0), pipeline_mode=pl.Buffered(1))
    return pl.pallas_call(
        functools.partial(_post_body, nseq=nseq, rps=rps),
        grid=(n // tm,),
        in_specs=[
            tok(D_MODEL), tok(A_WIDTH), tok(B_HEADS * B_DV), tok(C_HEADS * C_DV),
            stacked(D_MODEL, D_MODEL), const(1, D_MODEL), stacked(D_MODEL, w), const(1, M_DH),
            pl.BlockSpec((nseq, MEM_LEN, w), mem_map),
            pl.BlockSpec((nseq, MEM_LEN, w), mem_map),
            stacked(w, D_MODEL),
        ],
        out_specs=tok(D_MODEL),
        out_shape=jax.ShapeDtypeStruct((n, D_MODEL), F32),
        scratch_shapes=[pltpu.VMEM((tm, w), BF16)],
        compiler_params=_params("parallel"),
        name="post",
    )(x, a, bo, co, wo, g, wq, qg, mk, mv, wxo)


def _ffn_body(*refs, nseq, rps, per, n_cast):
    _cast_slabs(refs[11:11 + n_cast], refs[14 + n_cast:14 + 2 * n_cast])
    x_ref, g_ref, wv_ref, wg_ref, cwv_ref, cwg_ref, cbv_ref, cbg_ref, hv_ref, hg_ref, wd_ref = refs[:11]
    o_ref, nv_ref, ng_ref = refs[11 + n_cast:14 + n_cast]
    h_scr, ubuf, tail = refs[14 + 2 * n_cast:]
    i = pl.program_id(0)
    j = pl.program_id(1)

    @pl.when(j == 0)
    def _():
        x = x_ref[...]
        h_scr[...] = _rms(x, g_ref[...]).astype(BF16)
        o_ref[...] = x

    if per > 1:
        @pl.when((i == 0) & (j == 0))
        def _():
            tail[...] = jnp.zeros(tail.shape, F32)

    seq_start = (i % per) == 0
    stride = rps + F_HIST
    tm = x_ref.shape[0]
    rc = tm // FFN_CHAINS
    assert rc % rps == 0 or rps % rc == 0

    seg = min(rc, rps)
    chains = [slice(r * rc, (r + 1) * rc) for r in range(FFN_CHAINS)]

    def up(rows):
        return _dot(h_scr[rows, :], wv_ref[...]), _dot(h_scr[rows, :], wg_ref[...])

    def keep(us):
        for rows, pair in zip(chains, us):
            for half, u in enumerate(pair):
                for q in range(rc // seg):
                    s, o = divmod(rows.start + q * seg, rps)
                    dst = s * stride + F_HIST + o
                    ubuf[half, dst:dst + seg, :] = u[q * seg:(q + 1) * seg]

    def conv(half, r0, cw_ref, cb_ref, hist_ref, new_ref):
        outs = []
        for q in range(rc // seg):
            s, o = divmod(r0 + q * seg, rps)
            base = s * stride
            if o == 0:
                if per == 1:
                    prev = hist_ref[s]
                else:
                    prev = jnp.where(seq_start, hist_ref[s], tail[half, j])
                ubuf[half, base:base + F_HIST, :] = prev
            c = cb_ref[...]
            for k in range(F_KW):
                off = base + F_HIST + o - (F_KW - 1) + k
                c = c + ubuf[half, off:off + seg, :] * cw_ref[k:k + 1, :]
            outs.append(c)
            if o + seg == rps:
                last = ubuf[half, base + rps:base + stride, :]
                new_ref[s] = last
                if per > 1:
                    tail[half, j] = last
        return outs[0] if len(outs) == 1 else jnp.concatenate(outs, axis=0)

    def down(rows):
        val = conv(0, rows.start, cwv_ref, cbv_ref, hv_ref, nv_ref)
        gate = conv(1, rows.start, cwg_ref, cbg_ref, hg_ref, ng_ref)
        act = (gate * jax.nn.sigmoid(gate) * val).astype(BF16)
        o_ref[rows, :] += _dot(act, wd_ref[...])

    keep([up(rows) for rows in chains])
    for rows in chains:
        down(rows)


def _slab(total, unit, steps):
    return next(unit * k for k in range(1, total // unit + 1) if total % (unit * k) == 0 and total // (unit * k) <= steps)


def _weight_cast_ride(entries, steps, step_of):
    args, in_specs, out_specs, out_shapes = [], [], [], []
    for w, layer, axis in entries:
        _, k, ncol = w.shape
        size = _slab((k, ncol)[axis], (BF16_SUBLANES, LANES)[axis], steps)
        last = (k, ncol)[axis] // size - 1
        block = (None, size, ncol) if axis == 0 else (None, k, size)

        def index(*g, lead, axis=axis, last=last):
            s = jnp.minimum(step_of(*g), last)
            return (lead, s, 0) if axis == 0 else (lead, 0, s)

        args.append(w)
        in_specs.append(pl.BlockSpec(block, functools.partial(index, lead=layer)))
        out_specs.append(pl.BlockSpec(block, functools.partial(index, lead=0)))
        out_shapes.append(jax.ShapeDtypeStruct((1, k, ncol), BF16))
    return tuple(args), in_specs, out_specs, out_shapes


def _cast_slabs(f32_refs, bf16_refs):
    for src, dst in zip(f32_refs, bf16_refs):
        dst[...] = src[...].astype(BF16)


def _ffn(x, g, w_up, cw, cb, hist, w_down, layer, *, t, cast_next=None, tm=TOKEN_TILE, tn=COL_TILE):
    n = x.shape[0]
    nj = D_FF // tn
    assert n % tm == 0 and (t % tm == 0 or tm % t == 0), (n, t, tm)
    if t >= tm:
        nseq, rps, per = 1, tm, t // tm
        seq_map = lambda i: i // per
    else:
        nseq, rps, per = tm // t, t, 1
        seq_map = lambda i: i
    bsz = hist.shape[0]
    hist_spec = lambda off: pl.BlockSpec((nseq, F_HIST, tn), lambda i, j: (seq_map(i), 0, j + off))
    new_spec = pl.BlockSpec((nseq, F_HIST, tn), lambda i, j: (i, 0, j))
    cast_args, cast_in, cast_out, cast_shapes = (), [], [], []
    if cast_next:
        cast_args, cast_in, cast_out, cast_shapes = _weight_cast_ride(cast_next, (n // tm) * nj,
                                                                      lambda i, j: i * nj + j)
    results = pl.pallas_call(
        functools.partial(_ffn_body, nseq=nseq, rps=rps, per=per, n_cast=len(cast_args)),
        grid=(n // tm, nj),
        in_specs=[
            pl.BlockSpec((tm, D_MODEL), lambda i, j: (i, 0), pipeline_mode=pl.Buffered(1)),
            pl.BlockSpec((1, D_MODEL), lambda i, j: (0, 0)),
            pl.BlockSpec((None, D_MODEL, tn), lambda i, j: (layer, 0, j)),
            pl.BlockSpec((None, D_MODEL, tn), lambda i, j: (layer, 0, j + nj)),
            pl.BlockSpec((F_KW, tn), lambda i, j: (0, j)),
            pl.BlockSpec((F_KW, tn), lambda i, j: (0, j + nj)),
            pl.BlockSpec((1, tn), lambda i, j: (0, j)),
            pl.BlockSpec((1, tn), lambda i, j: (0, j + nj)),
            hist_spec(0), hist_spec(nj),
            pl.BlockSpec((None, tn, D_MODEL), lambda i, j: (layer, j, 0)),
        ] + cast_in,
        out_specs=[pl.BlockSpec((tm, D_MODEL), lambda i, j: (i, 0)), new_spec, new_spec] + cast_out,
        out_shape=[jax.ShapeDtypeStruct((n, D_MODEL), F32),
                   jax.ShapeDtypeStruct((bsz * per, F_HIST, D_FF), F32),
                   jax.ShapeDtypeStruct((bsz * per, F_HIST, D_FF), F32)] + cast_shapes,
        scratch_shapes=[
            pltpu.VMEM((tm, D_MODEL), BF16),
            pltpu.VMEM((2, nseq * (rps + F_HIST), tn), F32),
            pltpu.VMEM((2, nj, F_HIST, tn), F32),
        ],
        compiler_params=_params("arbitrary", "arbitrary"),
        name="ffn",
    )(x, g, w_up, w_up, cw, cw, cb, cb, hist, hist, w_down, *cast_args)
    x_out, tail_v, tail_g = results[:3]
    last = lambda a: a.reshape(bsz, per, F_HIST, D_FF)[:, per - 1]
    return x_out, last(tail_v), last(tail_g), tuple(results[3:])


def _rope_tables(pos0, t):
    half = B_DK // 2
    inv = 1.0 / (ROPE_BASE ** (jnp.arange(half, dtype=F32) / half))
    ang = (pos0 + jnp.arange(t, dtype=jnp.int32)).astype(F32)[:, None] * inv[None, :]
    cos = jnp.cos(ang)
    sin = jnp.sin(ang)
    return jnp.concatenate([cos, cos], axis=-1), jnp.concatenate([-sin, sin], axis=-1)


def _pad_rows(h, rows):
    return jnp.pad(h, ((0, 0), (rows - h.shape[1], 0), (0, 0)))


def _row(v):
    return v.reshape(1, -1).astype(F32)


def kernel(x_prompt, x_sample, mem_prompt, state_conv_a, state_ret, cache_diff_k, cache_diff_v, cache_mem_k,
           cache_mem_v, state_conv_f, norm1_g, w_in, conv_a_w, conv_a_b, ln_a_g, ln_a_b, ret_gn_g, diff_qn_g,
           diff_kn_g, diff_lq1, diff_lk1, diff_lq2, diff_lk2, diff_subln_g, w_out, rel_bias, norm2_g, mem_norm_g,
           w_xq, w_xk, w_xv, xqn_g, xkn_g, w_xo, norm3_g, w_up, conv_f_w, conv_f_b, w_down):
    bp, tp, _ = x_prompt.shape
    bs, ts, _ = x_sample.shape
    depth = w_in.shape[0]
    past = cache_diff_k.shape[2]
    cw_dim = C_HEADS * 2 * C_DQK

    tq = ATT_TILE
    r = np.arange(tq)
    rel_diag = r[None, :] - r[:, None]
    idx_p = np.stack([_t5_bucket(rel_diag - d * tq) for d in range(3)]).reshape(3 * tq, tq)
    vis_p = np.stack([(r[None, :] // CHUNK) <= (r[:, None] // CHUNK), np.ones((tq, tq), bool), np.ones((tq, tq), bool)])
    vis_p = vis_p.reshape(3 * tq, tq).astype(np.int32)
    bias_p = _bias_tiles(rel_bias, jnp.asarray(idx_p), jnp.asarray(vis_p)).reshape(C_HEADS, 3, tq, tq)
    rel_s = np.arange(past + ts)[None, :] - (past + np.arange(ts))[:, None]
    bias_s = _bias_tiles(rel_bias, jnp.asarray(_t5_bucket(rel_s)), jnp.ones(rel_s.shape, jnp.int32))
    bias_s_past, bias_s_new = bias_s[:, :, :past], bias_s[:, :, past:]

    cos_p, sin_p = _rope_tables(0, tp)
    cos_s, sin_s = _rope_tables(past, ts)

    xp = x_prompt.reshape(bp * tp, D_MODEL)
    xs = x_sample.reshape(bs * ts, D_MODEL)
    zero_a = jnp.zeros((bp, A_HIST, A_WIDTH), F32)
    zero_r = jnp.zeros((bp, B_HEADS, B_DK, B_DV), F32)
    zero_f = jnp.zeros((bp, F_HIST, 2 * D_FF), F32)

    outs = {k: [] for k in ("p_ca", "p_rs", "p_k", "p_v", "p_mk", "p_mv", "p_cf", "s_ca", "s_rs", "s_k", "s_v", "s_cf")}
    small = {"in": w_in, "out": w_out, "xq": w_xq, "xo": w_xo}
    cur = {k: w[0:1].astype(BF16) for k, w in small.items()}
    nxt = {}
    ffn_w = [None]
    for l in range(depth):
        lam_init = 0.8 - 0.6 * math.exp(-0.3 * l)
        has_next = l + 1 < depth
        w_in_b, w_out_b, w_xq_b, w_xo_b = cur["in"], cur["out"], cur["xq"], cur["xo"]
        lp = jnp.stack([diff_lq1[l], diff_lk1[l], diff_lq2[l], diff_lk2[l]]).astype(F32)
        qg2 = _row(jnp.concatenate([diff_qn_g[l], diff_qn_g[l]]))
        kg2 = _row(jnp.concatenate([diff_kn_g[l], diff_kn_g[l]]))
        mix_args = (conv_a_w[l], _row(conv_a_b[l]), _row(ln_a_g[l]), _row(ln_a_b[l]), _row(ret_gn_g[l]), qg2, kg2)
        subln = _row(diff_subln_g[l])

        def block(x, bsz, t, tt, hist_a, state_r, cos2, sin2, attend, mk_b, mv_b, hist_f, tm_post, cast_next=None):
            if tt >= FUSED_PROJ_MIN_ROWS:
                src = (x, _row(norm1_g[l]), w_in_b, 0)
            else:
                src = _in_proj(x, _row(norm1_g[l]), w_in_b, 0)
            a, bo, qn, kn, vb, ck, cv, nh, nr = _mixer(src, hist_a, state_r, cos2, sin2, *mix_args,
                                                        bsz=bsz, t=t, tt=tt)
            co = attend(qn, kn, vb)
            x = _post(x, a, bo, co, w_out_b, _row(norm2_g[l]), w_xq_b, _row(xqn_g[l]), mk_b, mv_b, w_xo_b,
                      0, t=t, tm=tm_post)
            w_up_b, w_down_b = ffn_w[0]
            x, nfv, nfg, next_w = _ffn(x, _row(norm3_g[l]), w_up_b, conv_f_w[l], _row(conv_f_b[l]), hist_f, w_down_b,
                                       0, t=t, cast_next=cast_next)
            new_f = jnp.concatenate([nfv[:, F_HIST - (F_KW - 1):], nfg[:, F_HIST - (F_KW - 1):]], axis=-1)
            return (x, nh[:, A_HIST - (A_KW - 1):], nr, ck.reshape(bsz, t, C_HEADS, 2 * C_DQK),
                    cv.reshape(bsz, t, C_HEADS, C_DV), new_f, next_w)

        mk, mv, mk_b, mv_b = _memkv(mem_prompt, _row(mem_norm_g[l]), w_xk[l].astype(BF16), w_xv[l].astype(BF16),
                                    _row(xkn_g[l]))
        def attend_p(qn, kn, vb):
            ride = [(w, l + 1, 1) for w in small.values()] if has_next else []
            if ffn_w[0] is None:
                ride += [(w_up, l, 1), (w_down, l, 0)]
            co, cast = _attn_prompt(qn, kn, vb, bias_p, lp, subln, bsz=bp, t=tp, lam_init=lam_init, cast_ride=ride)
            if has_next:
                nxt.update(zip(small, cast[:len(small)]))
            if ffn_w[0] is None:
                ffn_w[0] = cast[-2:]
            return co

        xp, ca, rs, kn_, vn_, cf, next_w = block(xp, bp, tp, MIX_TILE, zero_a, zero_r, cos_p, sin_p, attend_p, mk_b,
                                                 mv_b, zero_f, POST_TILE,
                                                 cast_next=[(w_up, l + 1, 1), (w_down, l + 1, 0)] if has_next else None)
        outs["p_ca"].append(ca); outs["p_rs"].append(rs); outs["p_k"].append(kn_); outs["p_v"].append(vn_)
        outs["p_mk"].append(mk.reshape(bp, MEM_LEN, M_HEADS, M_DH))
        outs["p_mv"].append(mv.reshape(bp, MEM_LEN, M_HEADS, M_DH))
        outs["p_cf"].append(cf)

        attend_s = lambda qn, kn, vb: _attn_sample(qn, kn, vb, cache_diff_k, cache_diff_v, bias_s_past,
                                                   bias_s_new, lp, subln, l,
                                                   bsz=bs, t=ts, lam_init=lam_init)
        smk = cache_mem_k[l].reshape(bs, MEM_LEN, M_HEADS * M_DH).astype(BF16)
        smv = cache_mem_v[l].reshape(bs, MEM_LEN, M_HEADS * M_DH).astype(BF16)
        xs, sca, srs, skn, svn, scf, _ = block(xs, bs, ts, ts, _pad_rows(state_conv_a[l], A_HIST), state_ret[l],
                                               cos_s, sin_s, attend_s, smk, smv, _pad_rows(state_conv_f[l], F_HIST),
                                               POST_TILE)
        outs["s_ca"].append(sca); outs["s_rs"].append(srs); outs["s_k"].append(skn); outs["s_v"].append(svn)
        outs["s_cf"].append(scf)
        if has_next:
            ffn_w[0] = next_w
            cur = dict(nxt)

    st = lambda k: jnp.stack(outs[k])
    return (xp.reshape(bp, tp, D_MODEL), xs.reshape(bs, ts, D_MODEL),
            st("p_ca"), st("p_rs"), st("p_k"), st("p_v"), st("p_mk"), st("p_mv"), st("p_cf"),
            st("s_ca"), st("s_rs"), st("s_k"), st("s_v"), st("s_cf"))
```

```python
import functools
import math

import numpy as np
import jax
import jax.numpy as jnp
from jax import lax
from jax.experimental import pallas as pl
from jax.experimental.pallas import tpu as pltpu

F32 = jnp.float32
BF16 = jnp.bfloat16
EPS = 1e-6
NEG_INF = -1e30

D_MODEL = 2048
CHUNK = 64
A_WIDTH = 512
A_KW = 31
A_HIST = 32
B_HEADS = 4
B_DK = 128
B_DV = 256
ROPE_BASE = 10000.0
C_HEADS = 4
C_DQK = 64
C_DV = 128
REL_BUCKETS = 32
REL_MAX_DIST = 128
M_HEADS = 4
M_DH = 128
MEM_LEN = 256
D_FF = 5632
F_KW = 3
F_HIST = 8

A_COLS = 2 * A_WIDTH
B_COLS = B_HEADS * (2 * B_DK + 2 * B_DV)
C_COLS = C_HEADS * (4 * C_DQK + C_DV)
IN_COLS = A_COLS + B_COLS + C_COLS
B_OFF = A_COLS
C_OFF = A_COLS + B_COLS
PROJ_GROUPS = (0, A_WIDTH, A_COLS, B_OFF + 2 * B_HEADS * B_DK, B_OFF + 2 * B_HEADS * B_DK + B_HEADS * B_DV, C_OFF,
               C_OFF + 2 * C_HEADS * 2 * C_DQK, IN_COLS)

VMEM_LIMIT_BYTES = 56 * 1024 * 1024
LANES = 128
BF16_SUBLANES = 16
TOKEN_TILE = 1024
COL_TILE = 512
MIX_TILE = 256
ATT_TILE = 512
ATT_ROWS = 256
ATT_LANES = 128
FFN_CHAINS = 2
POST_TILE = 512
POST_CHAINS = 2
CAST_STEPS = 11
FUSED_PROJ_MIN_ROWS = 128


def _params(*sem):
    return pltpu.CompilerParams(dimension_semantics=sem, vmem_limit_bytes=VMEM_LIMIT_BYTES)


def _rms(x, g):
    return x * lax.rsqrt(jnp.mean(x * x, axis=-1, keepdims=True) + EPS) * g


def _dot(a, b):
    return jnp.dot(a, b, preferred_element_type=F32)


def _dot_nt(a, b):
    return lax.dot_general(a, b, (((1,), (1,)), ((), ())), preferred_element_type=F32)


def _dot_tn(a, b):
    return lax.dot_general(a, b, (((0,), (0,)), ((), ())), preferred_element_type=F32)


def _in_proj_body(x_ref, g_ref, w_ref, o_ref, h_ref):
    @pl.when(pl.program_id(1) == 0)
    def _():
        h_ref[...] = _rms(x_ref[...], g_ref[...]).astype(BF16)

    o_ref[...] = _dot(h_ref[...], w_ref[...])


def _in_proj(x, g, w, layer, *, tm=TOKEN_TILE, tn=COL_TILE):
    n = x.shape[0]
    cols = w.shape[2]
    return pl.pallas_call(
        _in_proj_body,
        grid=(n // tm, cols // tn),
        in_specs=[
            pl.BlockSpec((tm, D_MODEL), lambda i, j: (i, 0)),
            pl.BlockSpec((1, D_MODEL), lambda i, j: (0, 0)),
            pl.BlockSpec((None, D_MODEL, tn), lambda i, j: (layer, 0, j)),
        ],
        out_specs=pl.BlockSpec((tm, tn), lambda i, j: (i, j)),
        out_shape=jax.ShapeDtypeStruct((n, cols), F32),
        scratch_shapes=[pltpu.VMEM((tm, D_MODEL), BF16)],
        compiler_params=_params("parallel", "arbitrary"),
        name="in_proj",
    )(x, g, w)


def _projected_columns(x_ref, g_ref, w_ref):
    h = _rms(x_ref[...], g_ref[...]).astype(BF16)
    cache = {}

    def cols(c0, c1):
        g0, g1 = next((a, b) for a, b in zip(PROJ_GROUPS[:-1], PROJ_GROUPS[1:]) if a <= c0 and c1 <= b)
        if g0 not in cache:
            cache[g0] = _dot(h, w_ref[:, g0:g1])
        return cache[g0][:, c0 - g0:c1 - g0]

    return cols


def _mixer_body(*refs, tt, log_gammas, fused):
    if fused:
        cols = _projected_columns(*refs[:3])
        refs = refs[3:]
    else:
        proj_ref = refs[0]
        cols = lambda c0, c1: proj_ref[:, c0:c1]
        refs = refs[1:]
    (hist_ref, state_ref, cos_ref, sin_ref, cw_ref, cb_ref, lng_ref, lnb_ref, gng_ref, qg_ref, kg_ref,
     a_ref, bo_ref, qn_ref, kn_ref, vb_ref, ck_ref, cv_ref, nh_ref, nr_ref, aext, zbuf, sret) = refs
    t = pl.program_id(1)

    @pl.when(t == 0)
    def _():
        aext[0:A_HIST, :] = hist_ref[0]
        sret[...] = state_ref[0]

    glu = cols(0, A_WIDTH) * jax.nn.sigmoid(cols(A_WIDTH, A_COLS))
    aext[A_HIST:A_HIST + tt, :] = glu
    first = A_HIST - (A_KW - 1)
    acc = jnp.zeros((tt, A_WIDTH), F32) + cb_ref[...]
    for r in range(8):
        rows = tt if r == 0 else tt + 8
        z = None
        for off in range(r, first + A_KW, 8):
            if off < first:
                continue
            term = aext[off - r:off - r + rows, :] * cw_ref[off - first:off - first + 1, :]
            z = term if z is None else z + term
        if r == 0:
            acc = acc + z
        else:
            zbuf[r - 1] = z
            acc = acc + zbuf[r - 1, r:r + tt, :]
    mu = jnp.mean(acc, axis=-1, keepdims=True)
    xc = acc - mu
    var = jnp.mean(xc * xc, axis=-1, keepdims=True)
    ln = xc * lax.rsqrt(var + EPS) * lng_ref[...] + lnb_ref[...]
    a_ref[...] = (ln * jax.nn.sigmoid(ln)).astype(BF16)

    last_rows = aext[tt:tt + A_HIST, :]
    nh_ref[0] = last_rows
    aext[0:A_HIST, :] = last_rows

    cos = cos_ref[...]
    sin = sin_ref[...]
    ri = lax.broadcasted_iota(jnp.int32, (tt, tt), 0)
    ci = lax.broadcasted_iota(jnp.int32, (tt, tt), 1)
    dij = (ri - ci).astype(F32)
    causal = ri >= ci
    rowf = lax.broadcasted_iota(jnp.int32, (tt, 1), 0).astype(F32)
    for h in range(B_HEADS):
        lg = log_gammas[h]
        q = cols(B_OFF + h * B_DK, B_OFF + (h + 1) * B_DK)
        k = cols(B_OFF + B_HEADS * B_DK + h * B_DK, B_OFF + B_HEADS * B_DK + (h + 1) * B_DK)
        voff = B_OFF + 2 * B_HEADS * B_DK
        v = cols(voff + h * B_DV, voff + (h + 1) * B_DV)
        goff = voff + B_HEADS * B_DV
        g = cols(goff + h * B_DV, goff + (h + 1) * B_DV)
        qr = q * cos + pltpu.roll(q, B_DK // 2, 1) * sin
        kr = (k * cos + pltpu.roll(k, B_DK // 2, 1) * sin) * (B_DK ** -0.5)
        qb = qr.astype(BF16)
        vb = v.astype(BF16)
        decay = jnp.where(causal, jnp.exp(lg * jnp.maximum(dij, 0.0)), 0.0)
        scores = _dot_nt(qb, kr.astype(BF16)) * decay
        inner = _dot(scores.astype(BF16), vb)
        s_old = sret[h]
        cross = _dot(qb, s_old.astype(BF16)) * jnp.exp(lg * (rowf + 1.0))
        o = inner + cross
        kd = kr * jnp.exp(lg * (tt - 1.0 - rowf))
        s_new = s_old * math.exp(lg * tt) + _dot_tn(kd.astype(BF16), vb)
        sret[h] = s_new
        nr_ref[0, h] = s_new
        y = _rms(o, gng_ref[:, h * B_DV:(h + 1) * B_DV])
        bo_ref[:, h * B_DV:(h + 1) * B_DV] = (y * (g * jax.nn.sigmoid(g))).astype(BF16)

    lane = lax.broadcasted_iota(jnp.int32, (tt, 2 * C_DQK), 1)
    lo = lane < C_DQK

    def qk_norm(x, g2):
        sq = x * x
        s_lo = jnp.sum(jnp.where(lo, sq, 0.0), axis=-1, keepdims=True)
        s_hi = jnp.sum(jnp.where(lo, 0.0, sq), axis=-1, keepdims=True)
        ms = jnp.where(lo, s_lo, s_hi) * (1.0 / C_DQK)
        return x * lax.rsqrt(ms + EPS) * g2

    w = 2 * C_DQK
    for h in range(C_HEADS):
        cq = cols(C_OFF + h * w, C_OFF + (h + 1) * w)
        ck = cols(C_OFF + C_HEADS * w + h * w, C_OFF + C_HEADS * w + (h + 1) * w)
        qn = qk_norm(cq, qg_ref[...])
        kn = qk_norm(ck, kg_ref[...])
        qn_ref[:, h * w:(h + 1) * w] = (qn * (C_DQK ** -0.5)).astype(BF16)
        kn_ref[:, h * w:(h + 1) * w] = kn.astype(BF16)
        ck_ref[pl.ds(h, tt, stride=C_HEADS), :] = kn
    cv = cols(C_OFF + 2 * C_HEADS * w, IN_COLS)
    ones = jnp.ones((tt, C_DV), BF16)
    for h in range(C_HEADS):
        cv_ref[pl.ds(h, tt, stride=C_HEADS), :] = cv[:, h * C_DV:(h + 1) * C_DV]
        vb_ref[:, 2 * h * C_DV:(2 * h + 1) * C_DV] = cv[:, h * C_DV:(h + 1) * C_DV].astype(BF16)
        vb_ref[:, (2 * h + 1) * C_DV:(2 * h + 2) * C_DV] = ones


def _mixer(src, hist, state, cos2, sin2, cw, cb, lng, lnb, gng, qg2, kg2, *, bsz, t, tt):
    n = bsz * t
    nt = t // tt
    log_gammas = tuple(math.log(1.0 - 2.0 ** (-5.0 - h)) for h in range(B_HEADS))
    tok = lambda cols: pl.BlockSpec((tt, cols), lambda b, i: (b * nt + i, 0))
    const = lambda r, c: pl.BlockSpec((r, c), lambda b, i: (0, 0))
    cw_dim = C_HEADS * 2 * C_DQK
    fused = isinstance(src, tuple)
    if fused:
        x, g1, w_in, layer = src
        src_args = (x, g1, w_in)
        src_specs = [tok(D_MODEL), const(1, D_MODEL),
                     pl.BlockSpec((None, D_MODEL, IN_COLS), lambda b, i: (layer, 0, 0), pipeline_mode=pl.Buffered(1))]
    else:
        src_args = (src,)
        src_specs = [tok(IN_COLS)]
    return pl.pallas_call(
        functools.partial(_mixer_body, tt=tt, log_gammas=log_gammas, fused=fused),
        grid=(bsz, nt),
        in_specs=src_specs + [
            pl.BlockSpec((1, A_HIST, A_WIDTH), lambda b, i: (b, 0, 0)),
            pl.BlockSpec((1, B_HEADS, B_DK, B_DV), lambda b, i: (b, 0, 0, 0)),
            pl.BlockSpec((tt, B_DK), lambda b, i: (i, 0)),
            pl.BlockSpec((tt, B_DK), lambda b, i: (i, 0)),
            const(A_KW, A_WIDTH), const(1, A_WIDTH), const(1, A_WIDTH), const(1, A_WIDTH),
            const(1, B_HEADS * B_DV), const(1, 2 * C_DQK), const(1, 2 * C_DQK),
        ],
        out_specs=[
            tok(A_WIDTH), tok(B_HEADS * B_DV), tok(cw_dim), tok(cw_dim), tok(2 * C_HEADS * C_DV),
            pl.BlockSpec((None, tt * C_HEADS, 2 * C_DQK), lambda b, i: (b, i, 0)),
            pl.BlockSpec((None, tt * C_HEADS, C_DV), lambda b, i: (b, i, 0)),
            pl.BlockSpec((1, A_HIST, A_WIDTH), lambda b, i: (b, 0, 0)),
            pl.BlockSpec((1, B_HEADS, B_DK, B_DV), lambda b, i: (b, 0, 0, 0)),
        ],
        out_shape=[
            jax.ShapeDtypeStruct((n, A_WIDTH), BF16),
            jax.ShapeDtypeStruct((n, B_HEADS * B_DV), BF16),
            jax.ShapeDtypeStruct((n, cw_dim), BF16),
            jax.ShapeDtypeStruct((n, cw_dim), BF16),
            jax.ShapeDtypeStruct((n, 2 * C_HEADS * C_DV), BF16),
            jax.ShapeDtypeStruct((bsz, t * C_HEADS, 2 * C_DQK), F32),
            jax.ShapeDtypeStruct((bsz, t * C_HEADS, C_DV), F32),
            jax.ShapeDtypeStruct((bsz, A_HIST, A_WIDTH), F32),
            jax.ShapeDtypeStruct((bsz, B_HEADS, B_DK, B_DV), F32),
        ],
        scratch_shapes=[pltpu.VMEM((A_HIST + tt, A_WIDTH), F32), pltpu.VMEM((7, tt + 8, A_WIDTH), F32),
                        pltpu.VMEM((B_HEADS, B_DK, B_DV), F32)],
        compiler_params=_params("parallel", "arbitrary"),
        name="mixer",
    )(*src_args, hist, state, cos2, sin2, cw, cb, lng, lnb, gng, qg2, kg2)


def _t5_bucket(rel):
    half = REL_BUCKETS // 2
    exact = half // 2
    n = np.abs(rel)
    large = exact + (np.log(np.maximum(n, 1).astype(np.float32) / exact) / math.log(REL_MAX_DIST / exact)
                     * (half - exact)).astype(np.int32)
    large = np.minimum(large, half - 1)
    return (np.where(rel > 0, half, 0) + np.where(n < exact, n, large)).astype(np.int32)


def _bias_body(rb_ref, idx_ref, vis_ref, o_ref, *, row_tile, buckets):
    h = pl.program_id(0)
    far = rb_ref[REL_BUCKETS // 2 - 1, h]
    for k, present in enumerate(buckets):
        rows = slice(k * row_tile, (k + 1) * row_tile)
        idx = idx_ref[rows, :]
        acc = jnp.zeros(idx.shape, F32)
        for b in present:
            acc = jnp.where(idx == b, rb_ref[b, h], acc)
        o_ref[0, rows, :] = jnp.where(vis_ref[rows, :] != 0, acc - far, NEG_INF)


def _bias_tiles(rel_bias, idx, vis, row_tile):
    r, c = idx.shape
    buckets = tuple(tuple(int(b) for b in np.unique(idx[k:k + row_tile])) for k in range(0, r, row_tile))
    idx, vis = jnp.asarray(idx), jnp.asarray(vis)
    return pl.pallas_call(
        functools.partial(_bias_body, row_tile=row_tile, buckets=buckets),
        grid=(C_HEADS,),
        in_specs=[
            pl.BlockSpec(memory_space=pltpu.SMEM),
            pl.BlockSpec((r, c), lambda h: (0, 0)),
            pl.BlockSpec((r, c), lambda h: (0, 0)),
        ],
        out_specs=pl.BlockSpec((1, r, c), lambda h: (h, 0, 0)),
        out_shape=jax.ShapeDtypeStruct((C_HEADS, r, c), F32),
        compiler_params=_params("arbitrary"),
        name="bias_tiles",
    )(rel_bias, idx, vis)


def _lambda(lp_ref, lam_init):
    lp = lp_ref[...]
    e1 = jnp.exp(jnp.sum(lp[0:1] * lp[1:2], axis=-1, keepdims=True))
    e2 = jnp.exp(jnp.sum(lp[2:3] * lp[3:4], axis=-1, keepdims=True))
    return e1 - e2 + lam_init


def _stack_maps(q):
    lane = lax.broadcasted_iota(jnp.int32, q.shape, 1)
    zero = jnp.zeros_like(q)
    return jnp.concatenate([jnp.where(lane < C_DQK, q, zero), jnp.where(lane < C_DQK, zero, q)], axis=0)


def _attn_finish(acc, l, lam, g, lam_init, tq):
    o = acc / l
    o = o[0:tq] - lam * o[tq:2 * tq]
    return _rms(o, g) * (1.0 - lam_init)


def _attn_prompt_body(*refs, tq, lam_init, n_cast):
    _cast_slabs(refs[6:6 + n_cast], refs[7 + n_cast:7 + 2 * n_cast])
    q_ref, k_ref, v_ref, bias_ref, lp_ref, g_ref = refs[:6]
    o_ref = refs[6 + n_cast]
    qq_scr, m_scr, alpha_scr, p_scr, acc_scr = refs[7 + 2 * n_cast:]
    qi = pl.program_id(2)
    qq_scr[...] = _stack_maps(q_ref[...])
    m_scr[...] = jnp.full(m_scr.shape, NEG_INF, F32)
    acc_scr[...] = jnp.zeros(acc_scr.shape, F32)

    def tile(ref, idx):
        start = idx * tq if isinstance(idx, int) else pl.multiple_of(idx * tq, tq)
        return ref[pl.ds(start, tq), :]

    nl = tq // ATT_LANES
    chunks = [slice(c * ATT_ROWS, (c + 1) * ATT_ROWS) for c in range(2 * tq // ATT_ROWS)]

    def accumulate(rows, vb):
        alpha = alpha_scr[rows, :]
        acc_scr[rows, :] = jnp.concatenate([alpha, alpha], axis=-1) * acc_scr[rows, :] + _dot(p_scr[rows, :], vb)

    def step(ki, bias_idx, first=False):
        kb = tile(k_ref, ki)
        if not first:
            v_prev = tile(v_ref, ki - 1)
        for rows in chunks:
            if not first:
                accumulate(rows, v_prev)
            s = _dot_nt(qq_scr[rows, :], kb)
            if bias_idx is not None:
                s = s + bias_ref[0, bias_idx, pl.ds(rows.start % tq, ATT_ROWS), :]
            slabs = [s[:, j * ATT_LANES:(j + 1) * ATT_LANES] for j in range(nl)]
            m_old = m_scr[rows, :]
            m_new = jnp.maximum(m_old, jnp.max(functools.reduce(jnp.maximum, slabs), axis=-1, keepdims=True))
            alpha_scr[rows, :] = jnp.exp(m_old - m_new)
            p_scr[rows, :] = jnp.concatenate([jnp.exp(sl - m_new) for sl in slabs], axis=-1).astype(BF16)
            m_scr[rows, :] = m_new

    n_far = jnp.maximum(qi - 1, 0)
    step(0, jnp.minimum(qi, 2), first=True)

    def far_body(ki, c):
        step(ki, None)
        return c

    lax.fori_loop(1, n_far, far_body, 0)

    def near_body(ki, c):
        step(ki, qi - ki)
        return c

    lax.fori_loop(jnp.maximum(n_far, 1), qi + 1, near_body, 0)

    v_last = tile(v_ref, qi)
    for rows in chunks:
        accumulate(rows, v_last)
    lam = _lambda(lp_ref, lam_init)
    acc = acc_scr[...]
    o_ref[...] = _attn_finish(acc[:, 0:C_DV], acc[:, C_DV:2 * C_DV], lam, g_ref[...], lam_init, tq).astype(BF16)


def _attn_prompt(qn, kn, vb, bias, lp, g, *, bsz, t, lam_init, cast_ride=()):
    tq = ATT_TILE
    nq = t // tq
    n = bsz * t
    w = 2 * C_DQK
    cast_args, cast_in, cast_out, cast_shapes = (), [], [], []
    if cast_ride:
        cast_args, cast_in, cast_out, cast_shapes = _weight_cast_ride(
            cast_ride, bsz * C_HEADS * nq, lambda b, h, i: (b * C_HEADS + h) * nq + i)
    results = pl.pallas_call(
        functools.partial(_attn_prompt_body, tq=tq, lam_init=lam_init, n_cast=len(cast_args)),
        grid=(bsz, C_HEADS, nq),
        in_specs=[
            pl.BlockSpec((tq, w), lambda b, h, i: (b * nq + i, h)),
            pl.BlockSpec((t, w), lambda b, h, i: (b, h)),
            pl.BlockSpec((t, 2 * C_DV), lambda b, h, i: (b, h)),
            pl.BlockSpec((1, 3, tq, tq), lambda b, h, i: (h, 0, 0, 0)),
            pl.BlockSpec((4, C_DQK), lambda b, h, i: (0, 0)),
            pl.BlockSpec((1, C_DV), lambda b, h, i: (0, 0)),
        ] + cast_in,
        out_specs=[pl.BlockSpec((tq, C_DV), lambda b, h, i: (b * nq + i, h))] + cast_out,
        out_shape=[jax.ShapeDtypeStruct((n, C_HEADS * C_DV), BF16)] + cast_shapes,
        scratch_shapes=[pltpu.VMEM((2 * tq, w), BF16), pltpu.VMEM((2 * tq, ATT_LANES), F32),
                        pltpu.VMEM((2 * tq, ATT_LANES), F32), pltpu.VMEM((2 * tq, tq), BF16),
                        pltpu.VMEM((2 * tq, 2 * C_DV), F32)],
        compiler_params=_params("arbitrary", "arbitrary", "arbitrary"),
        name="attn_prompt",
    )(qn, kn, vb, bias, lp, g, *cast_args)
    return results[0], tuple(results[1:])


def _attn_sample_body(q_ref, k_ref, v_ref, pk_ref, pv_ref, bp_ref, bn_ref, lp_ref, g_ref, o_ref, *, tq, lam_init):
    lam = _lambda(lp_ref, lam_init)
    w = 2 * C_DQK
    for h in range(C_HEADS):
        qq = _stack_maps(q_ref[:, h * w:(h + 1) * w])
        bp = bp_ref[h]
        bn = bn_ref[h]
        past = pk_ref.shape[0] // C_HEADS
        head_rows = pl.ds(h, past, stride=C_HEADS)
        s_p = _dot_nt(qq, pk_ref[head_rows, :].astype(BF16)) + jnp.concatenate([bp, bp], axis=0)
        s_n = _dot_nt(qq, k_ref[:, h * w:(h + 1) * w]) + jnp.concatenate([bn, bn], axis=0)
        m = jnp.maximum(jnp.max(s_p, axis=-1, keepdims=True), jnp.max(s_n, axis=-1, keepdims=True))
        p_p = jnp.exp(s_p - m)
        p_n = jnp.exp(s_n - m)
        l = jnp.sum(p_p, axis=-1, keepdims=True) + jnp.sum(p_n, axis=-1, keepdims=True)
        acc = (_dot(p_p.astype(BF16), pv_ref[head_rows, :].astype(BF16))
               + _dot(p_n.astype(BF16), v_ref[:, 2 * h * C_DV:(2 * h + 1) * C_DV]))
        o_ref[:, h * C_DV:(h + 1) * C_DV] = _attn_finish(acc, l, lam, g_ref[...], lam_init, tq).astype(BF16)


def _attn_sample(qn, kn, vb, past_k, past_v, bias_p, bias_n, lp, g, layer, *, bsz, t, lam_init):
    n = bsz * t
    w = 2 * C_DQK
    depth, _, past = past_k.shape[:3]
    past_k = past_k.reshape(depth, bsz, past * C_HEADS, w)
    past_v = past_v.reshape(depth, bsz, past * C_HEADS, C_DV)
    full = lambda a: pl.BlockSpec(a.shape, lambda b: (0,) * a.ndim)
    return pl.pallas_call(
        functools.partial(_attn_sample_body, tq=t, lam_init=lam_init),
        grid=(bsz,),
        in_specs=[
            pl.BlockSpec((t, C_HEADS * w), lambda b: (b, 0)),
            pl.BlockSpec((t, C_HEADS * w), lambda b: (b, 0)),
            pl.BlockSpec((t, 2 * C_HEADS * C_DV), lambda b: (b, 0)),
            pl.BlockSpec((None, None, past * C_HEADS, w), lambda b: (layer, b, 0, 0)),
            pl.BlockSpec((None, None, past * C_HEADS, C_DV), lambda b: (layer, b, 0, 0)),
            full(bias_p), full(bias_n),
            pl.BlockSpec((4, C_DQK), lambda b: (0, 0)),
            pl.BlockSpec((1, C_DV), lambda b: (0, 0)),
        ],
        out_specs=pl.BlockSpec((t, C_HEADS * C_DV), lambda b: (b, 0)),
        out_shape=jax.ShapeDtypeStruct((n, C_HEADS * C_DV), BF16),
        compiler_params=_params("parallel"),
        name="attn_sample",
    )(qn, kn, vb, past_k, past_v, bias_p, bias_n, lp, g)


def _memkv_body(mem_ref, g_ref, wk_ref, wv_ref, kg_ref, k_ref, v_ref, kb_ref, vb_ref):
    hm = _rms(mem_ref[0], g_ref[...]).astype(BF16)
    k = _dot(hm, wk_ref[...])
    v = _dot(hm, wv_ref[...])
    for h in range(M_HEADS):
        kn = _rms(k[:, h * M_DH:(h + 1) * M_DH], kg_ref[...])
        k_ref[0, :, h * M_DH:(h + 1) * M_DH] = kn
        kb_ref[0, :, h * M_DH:(h + 1) * M_DH] = kn.astype(BF16)
    v_ref[0] = v
    vb_ref[0] = v.astype(BF16)


def _memkv(mem, g, wk, wv, kg):
    bsz, m, _ = mem.shape
    w = M_HEADS * M_DH
    blk = pl.BlockSpec((1, m, w), lambda b: (b, 0, 0))
    return pl.pallas_call(
        _memkv_body,
        grid=(bsz,),
        in_specs=[
            pl.BlockSpec((1, m, D_MODEL), lambda b: (b, 0, 0)),
            pl.BlockSpec((1, D_MODEL), lambda b: (0, 0)),
            pl.BlockSpec((D_MODEL, w), lambda b: (0, 0)),
            pl.BlockSpec((D_MODEL, w), lambda b: (0, 0)),
            pl.BlockSpec((1, M_DH), lambda b: (0, 0)),
        ],
        out_specs=[blk, blk, blk, blk],
        out_shape=[jax.ShapeDtypeStruct((bsz, m, w), F32), jax.ShapeDtypeStruct((bsz, m, w), F32),
                   jax.ShapeDtypeStruct((bsz, m, w), BF16), jax.ShapeDtypeStruct((bsz, m, w), BF16)],
        compiler_params=_params("parallel"),
        name="memkv",
    )(mem, g, wk, wv, kg)


def _post_body(x_ref, a_ref, bo_ref, co_ref, wo_ref, g_ref, wq_ref, qg_ref, mk_ref, mv_ref, wxo_ref,
               o_ref, att_scr, *, nseq, rps):
    rc = x_ref.shape[0] // POST_CHAINS
    seg = min(rc, rps)
    for c in range(POST_CHAINS):
        rows = slice(c * rc, (c + 1) * rc)
        y = (_dot(a_ref[rows, :], wo_ref[0:A_WIDTH, :])
             + _dot(bo_ref[rows, :], wo_ref[A_WIDTH:A_WIDTH + B_HEADS * B_DV, :])
             + _dot(co_ref[rows, :], wo_ref[A_WIDTH + B_HEADS * B_DV:, :]))
        x1 = x_ref[rows, :] + y
        q = _dot(_rms(x1, g_ref[...]).astype(BF16), wq_ref[...])
        for h in range(M_HEADS):
            sl = slice(h * M_DH, (h + 1) * M_DH)
            qn = _rms(q[:, sl], qg_ref[...]).astype(BF16)
            for u in range(rc // seg):
                r0 = c * rc + u * seg
                s = r0 // rps
                logits = _dot_nt(qn[u * seg:(u + 1) * seg], mk_ref[s, :, sl]) * (M_DH ** -0.5)
                m = jnp.max(logits, axis=-1, keepdims=True)
                p = jnp.exp(logits - m)
                l = jnp.sum(p, axis=-1, keepdims=True)
                o = _dot(p.astype(BF16), mv_ref[s, :, sl]) / l
                att_scr[r0:r0 + seg, sl] = o.astype(BF16)
        o_ref[rows, :] = x1 + _dot(att_scr[rows, :], wxo_ref[...])


def _post(x, a, bo, co, wo, g, wq, qg, mk, mv, wxo, layer, *, t, tm):
    n = x.shape[0]
    w = M_HEADS * M_DH
    assert n % tm == 0 and (t % tm == 0 or tm % t == 0), (n, t, tm)
    if t >= tm:
        nseq, rps = 1, tm
        per = t // tm
        mem_map = lambda i: (i // per, 0, 0)
    else:
        nseq, rps = tm // t, t
        mem_map = lambda i: (i, 0, 0)
    tok = lambda cols: pl.BlockSpec((tm, cols), lambda i: (i, 0))
    const = lambda r, c: pl.BlockSpec((r, c), lambda i: (0, 0), pipeline_mode=pl.Buffered(1))
    stacked = lambda r, c: pl.BlockSpec((None, r, c), lambda i: (layer, 0, 0), pipeline_mode=pl.Buffered(1))
    return pl.pallas_call(
        functools.partial(_post_body, nseq=nseq, rps=rps),
        grid=(n // tm,),
        in_specs=[
            tok(D_MODEL), tok(A_WIDTH), tok(B_HEADS * B_DV), tok(C_HEADS * C_DV),
            stacked(D_MODEL, D_MODEL), const(1, D_MODEL), stacked(D_MODEL, w), const(1, M_DH),
            pl.BlockSpec((nseq, MEM_LEN, w), mem_map),
            pl.BlockSpec((nseq, MEM_LEN, w), mem_map),
            stacked(w, D_MODEL),
        ],
        out_specs=tok(D_MODEL),
        out_shape=jax.ShapeDtypeStruct((n, D_MODEL), F32),
        scratch_shapes=[pltpu.VMEM((tm, w), BF16)],
        compiler_params=_params("parallel"),
        name="post",
    )(x, a, bo, co, wo, g, wq, qg, mk, mv, wxo)


def _ffn_body(*refs, nseq, rps, per, n_cast):
    _cast_slabs(refs[11:11 + n_cast], refs[14 + n_cast:14 + 2 * n_cast])
    x_ref, g_ref, wv_ref, wg_ref, cwv_ref, cwg_ref, cbv_ref, cbg_ref, hv_ref, hg_ref, wd_ref = refs[:11]
    o_ref, nv_ref, ng_ref = refs[11 + n_cast:14 + n_cast]
    h_scr, ubuf, tail = refs[14 + 2 * n_cast:]
    i = pl.program_id(0)
    j = pl.program_id(1)

    @pl.when(j == 0)
    def _():
        x = x_ref[...]
        h_scr[...] = _rms(x, g_ref[...]).astype(BF16)
        o_ref[...] = x

    if per > 1:
        @pl.when((i == 0) & (j == 0))
        def _():
            tail[...] = jnp.zeros(tail.shape, F32)

    seq_start = (i % per) == 0
    stride = rps + F_HIST
    tm = x_ref.shape[0]
    rc = tm // FFN_CHAINS
    assert rc % rps == 0 or rps % rc == 0

    seg = min(rc, rps)
    chains = [slice(r * rc, (r + 1) * rc) for r in range(FFN_CHAINS)]

    def up(rows):
        return _dot(h_scr[rows, :], wv_ref[...]), _dot(h_scr[rows, :], wg_ref[...])

    def keep(us):
        for rows, pair in zip(chains, us):
            for half, u in enumerate(pair):
                for q in range(rc // seg):
                    s, o = divmod(rows.start + q * seg, rps)
                    dst = s * stride + F_HIST + o
                    ubuf[half, dst:dst + seg, :] = u[q * seg:(q + 1) * seg]

    def conv(half, r0, cw_ref, cb_ref, hist_ref, new_ref):
        outs = []
        for q in range(rc // seg):
            s, o = divmod(r0 + q * seg, rps)
            base = s * stride
            if o == 0:
                if per == 1:
                    prev = hist_ref[s]
                else:
                    prev = jnp.where(seq_start, hist_ref[s], tail[half, j])
                ubuf[half, base:base + F_HIST, :] = prev
            c = cb_ref[...]
            for k in range(F_KW):
                off = base + F_HIST + o - (F_KW - 1) + k
                c = c + ubuf[half, off:off + seg, :] * cw_ref[k:k + 1, :]
            outs.append(c)
            if o + seg == rps:
                last = ubuf[half, base + rps:base + stride, :]
                new_ref[s] = last
                if per > 1:
                    tail[half, j] = last
        return outs[0] if len(outs) == 1 else jnp.concatenate(outs, axis=0)

    def down(rows):
        val = conv(0, rows.start, cwv_ref, cbv_ref, hv_ref, nv_ref)
        gate = conv(1, rows.start, cwg_ref, cbg_ref, hg_ref, ng_ref)
        act = (gate * jax.nn.sigmoid(gate) * val).astype(BF16)
        o_ref[rows, :] += _dot(act, wd_ref[...])

    keep([up(rows) for rows in chains])
    for rows in chains:
        down(rows)


def _slab(total, unit, steps):
    return next(unit * k for k in range(1, total // unit + 1) if total % (unit * k) == 0 and total // (unit * k) <= steps)


def _weight_cast_ride(entries, steps, step_of):
    args, in_specs, out_specs, out_shapes = [], [], [], []
    for w, layer, axis in entries:
        _, k, ncol = w.shape
        size = _slab((k, ncol)[axis], (BF16_SUBLANES, LANES)[axis], steps)
        last = (k, ncol)[axis] // size - 1
        block = (None, size, ncol) if axis == 0 else (None, k, size)

        def index(*g, lead, axis=axis, last=last):
            s = jnp.minimum(step_of(*g), last)
            return (lead, s, 0) if axis == 0 else (lead, 0, s)

        args.append(w)
        in_specs.append(pl.BlockSpec(block, functools.partial(index, lead=layer)))
        out_specs.append(pl.BlockSpec(block, functools.partial(index, lead=0)))
        out_shapes.append(jax.ShapeDtypeStruct((1, k, ncol), BF16))
    return tuple(args), in_specs, out_specs, out_shapes


def _cast_slabs(f32_refs, bf16_refs):
    for src, dst in zip(f32_refs, bf16_refs):
        dst[...] = src[...].astype(BF16)


def _cast_weights(entries, steps=CAST_STEPS):
    args, in_specs, out_specs, out_shapes = _weight_cast_ride(entries, steps, lambda s: s)
    return pl.pallas_call(
        lambda *refs: _cast_slabs(refs[:len(args)], refs[len(args):]),
        grid=(steps,),
        in_specs=in_specs,
        out_specs=out_specs,
        out_shape=out_shapes,
        compiler_params=_params("arbitrary"),
        name="cast_weights",
    )(*args)


def _ffn(x, g, w_up, cw, cb, hist, w_down, layer, *, t, cast_next=None, tm=TOKEN_TILE, tn=COL_TILE):
    n = x.shape[0]
    nj = D_FF // tn
    assert n % tm == 0 and (t % tm == 0 or tm % t == 0), (n, t, tm)
    if t >= tm:
        nseq, rps, per = 1, tm, t // tm
        seq_map = lambda i: i // per
    else:
        nseq, rps, per = tm // t, t, 1
        seq_map = lambda i: i
    bsz = hist.shape[0]
    hist_spec = lambda off: pl.BlockSpec((nseq, F_HIST, tn), lambda i, j: (seq_map(i), 0, j + off))
    new_spec = pl.BlockSpec((nseq, F_HIST, tn), lambda i, j: (i, 0, j))
    cast_args, cast_in, cast_out, cast_shapes = (), [], [], []
    if cast_next:
        cast_args, cast_in, cast_out, cast_shapes = _weight_cast_ride(cast_next, (n // tm) * nj,
                                                                      lambda i, j: i * nj + j)
    results = pl.pallas_call(
        functools.partial(_ffn_body, nseq=nseq, rps=rps, per=per, n_cast=len(cast_args)),
        grid=(n // tm, nj),
        in_specs=[
            pl.BlockSpec((tm, D_MODEL), lambda i, j: (i, 0), pipeline_mode=pl.Buffered(1)),
            pl.BlockSpec((1, D_MODEL), lambda i, j: (0, 0)),
            pl.BlockSpec((None, D_MODEL, tn), lambda i, j: (layer, 0, j)),
            pl.BlockSpec((None, D_MODEL, tn), lambda i, j: (layer, 0, j + nj)),
            pl.BlockSpec((F_KW, tn), lambda i, j: (0, j)),
            pl.BlockSpec((F_KW, tn), lambda i, j: (0, j + nj)),
            pl.BlockSpec((1, tn), lambda i, j: (0, j)),
            pl.BlockSpec((1, tn), lambda i, j: (0, j + nj)),
            hist_spec(0), hist_spec(nj),
            pl.BlockSpec((None, tn, D_MODEL), lambda i, j: (layer, j, 0)),
        ] + cast_in,
        out_specs=[pl.BlockSpec((tm, D_MODEL), lambda i, j: (i, 0)), new_spec, new_spec] + cast_out,
        out_shape=[jax.ShapeDtypeStruct((n, D_MODEL), F32),
                   jax.ShapeDtypeStruct((bsz * per, F_HIST, D_FF), F32),
                   jax.ShapeDtypeStruct((bsz * per, F_HIST, D_FF), F32)] + cast_shapes,
        scratch_shapes=[
            pltpu.VMEM((tm, D_MODEL), BF16),
            pltpu.VMEM((2, nseq * (rps + F_HIST), tn), F32),
            pltpu.VMEM((2, nj, F_HIST, tn), F32),
        ],
        compiler_params=_params("arbitrary", "arbitrary"),
        name="ffn",
    )(x, g, w_up, w_up, cw, cw, cb, cb, hist, hist, w_down, *cast_args)
    x_out, tail_v, tail_g = results[:3]
    last = lambda a: a.reshape(bsz, per, F_HIST, D_FF)[:, per - 1]
    return x_out, last(tail_v), last(tail_g), tuple(results[3:])


def _rope_tables(pos0, t):
    half = B_DK // 2
    inv = 1.0 / (ROPE_BASE ** (jnp.arange(half, dtype=F32) / half))
    ang = (pos0 + jnp.arange(t, dtype=jnp.int32)).astype(F32)[:, None] * inv[None, :]
    cos = jnp.cos(ang)
    sin = jnp.sin(ang)
    return jnp.concatenate([cos, cos], axis=-1), jnp.concatenate([-sin, sin], axis=-1)


def _pad_rows(h, rows):
    return jnp.pad(h, ((0, 0), (rows - h.shape[1], 0), (0, 0)))


def _row(v):
    return v.reshape(1, -1).astype(F32)


def kernel(x_prompt, x_sample, mem_prompt, state_conv_a, state_ret, cache_diff_k, cache_diff_v, cache_mem_k,
           cache_mem_v, state_conv_f, norm1_g, w_in, conv_a_w, conv_a_b, ln_a_g, ln_a_b, ret_gn_g, diff_qn_g,
           diff_kn_g, diff_lq1, diff_lk1, diff_lq2, diff_lk2, diff_subln_g, w_out, rel_bias, norm2_g, mem_norm_g,
           w_xq, w_xk, w_xv, xqn_g, xkn_g, w_xo, norm3_g, w_up, conv_f_w, conv_f_b, w_down):
    bp, tp, _ = x_prompt.shape
    bs, ts, _ = x_sample.shape
    depth = w_in.shape[0]
    past = cache_diff_k.shape[2]
    cw_dim = C_HEADS * 2 * C_DQK

    tq = ATT_TILE
    r = np.arange(tq)
    rel_diag = r[None, :] - r[:, None]
    idx_p = np.stack([_t5_bucket(rel_diag - d * tq) for d in range(3)]).reshape(3 * tq, tq)
    vis_p = np.stack([(r[None, :] // CHUNK) <= (r[:, None] // CHUNK), np.ones((tq, tq), bool), np.ones((tq, tq), bool)])
    vis_p = vis_p.reshape(3 * tq, tq).astype(np.int32)
    bias_p = _bias_tiles(rel_bias, idx_p, vis_p, tq).reshape(C_HEADS, 3, tq, tq)
    rel_s = np.arange(past + ts)[None, :] - (past + np.arange(ts))[:, None]
    bias_s = _bias_tiles(rel_bias, _t5_bucket(rel_s), np.ones(rel_s.shape, np.int32), ts)
    bias_s_past, bias_s_new = bias_s[:, :, :past], bias_s[:, :, past:]

    cos_p, sin_p = _rope_tables(0, tp)
    cos_s, sin_s = _rope_tables(past, ts)

    xp = x_prompt.reshape(bp * tp, D_MODEL)
    xs = x_sample.reshape(bs * ts, D_MODEL)
    zero_a = jnp.zeros((bp, A_HIST, A_WIDTH), F32)
    zero_r = jnp.zeros((bp, B_HEADS, B_DK, B_DV), F32)
    zero_f = jnp.zeros((bp, F_HIST, 2 * D_FF), F32)

    outs = {k: [] for k in ("p_ca", "p_rs", "p_k", "p_v", "p_mk", "p_mv", "p_cf", "s_ca", "s_rs", "s_k", "s_v", "s_cf")}
    small = {"in": w_in, "out": w_out, "xq": w_xq, "xo": w_xo}
    cur = dict(zip(small, _cast_weights([(w, 0, 1) for w in small.values()])))
    nxt = {}
    ffn_w = [None]
    for l in range(depth):
        lam_init = 0.8 - 0.6 * math.exp(-0.3 * l)
        has_next = l + 1 < depth
        w_in_b, w_out_b, w_xq_b, w_xo_b = cur["in"], cur["out"], cur["xq"], cur["xo"]
        lp = jnp.stack([diff_lq1[l], diff_lk1[l], diff_lq2[l], diff_lk2[l]]).astype(F32)
        qg2 = _row(jnp.concatenate([diff_qn_g[l], diff_qn_g[l]]))
        kg2 = _row(jnp.concatenate([diff_kn_g[l], diff_kn_g[l]]))
        mix_args = (conv_a_w[l], _row(conv_a_b[l]), _row(ln_a_g[l]), _row(ln_a_b[l]), _row(ret_gn_g[l]), qg2, kg2)
        subln = _row(diff_subln_g[l])

        def block(x, bsz, t, tt, hist_a, state_r, cos2, sin2, attend, mk_b, mv_b, hist_f, tm_post, cast_next=None):
            if tt >= FUSED_PROJ_MIN_ROWS:
                src = (x, _row(norm1_g[l]), w_in_b, 0)
            else:
                src = _in_proj(x, _row(norm1_g[l]), w_in_b, 0)
            a, bo, qn, kn, vb, ck, cv, nh, nr = _mixer(src, hist_a, state_r, cos2, sin2, *mix_args,
                                                        bsz=bsz, t=t, tt=tt)
            co = attend(qn, kn, vb)
            x = _post(x, a, bo, co, w_out_b, _row(norm2_g[l]), w_xq_b, _row(xqn_g[l]), mk_b, mv_b, w_xo_b,
                      0, t=t, tm=tm_post)
            w_up_b, w_down_b = ffn_w[0]
            x, nfv, nfg, next_w = _ffn(x, _row(norm3_g[l]), w_up_b, conv_f_w[l], _row(conv_f_b[l]), hist_f, w_down_b,
                                       0, t=t, cast_next=cast_next)
            new_f = jnp.concatenate([nfv[:, F_HIST - (F_KW - 1):], nfg[:, F_HIST - (F_KW - 1):]], axis=-1)
            return (x, nh[:, A_HIST - (A_KW - 1):], nr, ck.reshape(bsz, t, C_HEADS, 2 * C_DQK),
                    cv.reshape(bsz, t, C_HEADS, C_DV), new_f, next_w)

        mk, mv, mk_b, mv_b = _memkv(mem_prompt, _row(mem_norm_g[l]), w_xk[l].astype(BF16), w_xv[l].astype(BF16),
                                    _row(xkn_g[l]))
        def attend_p(qn, kn, vb):
            ride = [(w, l + 1, 1) for w in small.values()] if has_next else []
            if ffn_w[0] is None:
                ride += [(w_up, l, 1), (w_down, l, 0)]
            co, cast = _attn_prompt(qn, kn, vb, bias_p, lp, subln, bsz=bp, t=tp, lam_init=lam_init, cast_ride=ride)
            if has_next:
                nxt.update(zip(small, cast[:len(small)]))
            if ffn_w[0] is None:
                ffn_w[0] = cast[-2:]
            return co

        xp, ca, rs, kn_, vn_, cf, next_w = block(xp, bp, tp, MIX_TILE, zero_a, zero_r, cos_p, sin_p, attend_p, mk_b,
                                                 mv_b, zero_f, POST_TILE,
                                                 cast_next=[(w_up, l + 1, 1), (w_down, l + 1, 0)] if has_next else None)
        outs["p_ca"].append(ca); outs["p_rs"].append(rs); outs["p_k"].append(kn_); outs["p_v"].append(vn_)
        outs["p_mk"].append(mk.reshape(bp, MEM_LEN, M_HEADS, M_DH))
        outs["p_mv"].append(mv.reshape(bp, MEM_LEN, M_HEADS, M_DH))
        outs["p_cf"].append(cf)

        attend_s = lambda qn, kn, vb: _attn_sample(qn, kn, vb, cache_diff_k, cache_diff_v, bias_s_past,
                                                   bias_s_new, lp, subln, l,
                                                   bsz=bs, t=ts, lam_init=lam_init)
        smk = cache_mem_k[l].reshape(bs, MEM_LEN, M_HEADS * M_DH).astype(BF16)
        smv = cache_mem_v[l].reshape(bs, MEM_LEN, M_HEADS * M_DH).astype(BF16)
        xs, sca, srs, skn, svn, scf, _ = block(xs, bs, ts, ts, _pad_rows(state_conv_a[l], A_HIST), state_ret[l],
                                               cos_s, sin_s, attend_s, smk, smv, _pad_rows(state_conv_f[l], F_HIST),
                                               POST_TILE)
        outs["s_ca"].append(sca); outs["s_rs"].append(srs); outs["s_k"].append(skn); outs["s_v"].append(svn)
        outs["s_cf"].append(scf)
        if has_next:
            ffn_w[0] = next_w
            cur = dict(nxt)

    st = lambda k: jnp.stack(outs[k])
    return (xp.reshape(bp, tp, D_MODEL), xs.reshape(bs, ts, D_MODEL),
            st("p_ca"), st("p_rs"), st("p_k"), st("p_v"), st("p_mk"), st("p_mv"), st("p_cf"),
            st("s_ca"), st("s_rs"), st("s_k"), st("s_v"), st("s_cf"))
```

```python
import functools
import math

import numpy as np
import jax
import jax.numpy as jnp
from jax import lax
from jax.experimental import pallas as pl
from jax.experimental.pallas import tpu as pltpu

F32 = jnp.float32
BF16 = jnp.bfloat16
EPS = 1e-6
NEG_INF = -1e30

D_MODEL = 2048
CHUNK = 64
A_WIDTH = 512
A_KW = 31
A_HIST = 32
B_HEADS = 4
B_DK = 128
B_DV = 256
ROPE_BASE = 10000.0
C_HEADS = 4
C_DQK = 64
C_DV = 128
REL_BUCKETS = 32
REL_MAX_DIST = 128
M_HEADS = 4
M_DH = 128
MEM_LEN = 256
D_FF = 5632
F_KW = 3
F_HIST = 8

A_COLS = 2 * A_WIDTH
B_COLS = B_HEADS * (2 * B_DK + 2 * B_DV)
C_COLS = C_HEADS * (4 * C_DQK + C_DV)
IN_COLS = A_COLS + B_COLS + C_COLS
B_OFF = A_COLS
C_OFF = A_COLS + B_COLS
PROJ_GROUPS = (0, A_WIDTH, A_COLS, B_OFF + 2 * B_HEADS * B_DK, B_OFF + 2 * B_HEADS * B_DK + B_HEADS * B_DV, C_OFF,
               C_OFF + 2 * C_HEADS * 2 * C_DQK, IN_COLS)

VMEM_LIMIT_BYTES = 56 * 1024 * 1024
LANES = 128
BF16_SUBLANES = 16
TOKEN_TILE = 1024
COL_TILE = 512
MIX_TILE = 256
ATT_TILE = 512
ATT_ROWS = 256
ATT_LANES = 128
FFN_CHAINS = 2
POST_TILE = 512
POST_CHAINS = 2
CAST_STEPS = 11
FUSED_PROJ_MIN_ROWS = 128


def _params(*sem):
    return pltpu.CompilerParams(dimension_semantics=sem, vmem_limit_bytes=VMEM_LIMIT_BYTES)


def _rms(x, g):
    return x * lax.rsqrt(jnp.mean(x * x, axis=-1, keepdims=True) + EPS) * g


def _dot(a, b):
    return jnp.dot(a, b, preferred_element_type=F32)


def _dot_nt(a, b):
    return lax.dot_general(a, b, (((1,), (1,)), ((), ())), preferred_element_type=F32)


def _dot_tn(a, b):
    return lax.dot_general(a, b, (((0,), (0,)), ((), ())), preferred_element_type=F32)


def _in_proj_body(x_ref, g_ref, w_ref, o_ref, h_ref):
    @pl.when(pl.program_id(1) == 0)
    def _():
        h_ref[...] = _rms(x_ref[...], g_ref[...]).astype(BF16)

    o_ref[...] = _dot(h_ref[...], w_ref[...])


def _in_proj(x, g, w, layer, *, tm=TOKEN_TILE, tn=COL_TILE):
    n = x.shape[0]
    cols = w.shape[2]
    return pl.pallas_call(
        _in_proj_body,
        grid=(n // tm, cols // tn),
        in_specs=[
            pl.BlockSpec((tm, D_MODEL), lambda i, j: (i, 0)),
            pl.BlockSpec((1, D_MODEL), lambda i, j: (0, 0)),
            pl.BlockSpec((None, D_MODEL, tn), lambda i, j: (layer, 0, j)),
        ],
        out_specs=pl.BlockSpec((tm, tn), lambda i, j: (i, j)),
        out_shape=jax.ShapeDtypeStruct((n, cols), F32),
        scratch_shapes=[pltpu.VMEM((tm, D_MODEL), BF16)],
        compiler_params=_params("parallel", "arbitrary"),
        name="in_proj",
    )(x, g, w)


def _projected_columns(x_ref, g_ref, w_ref):
    h = _rms(x_ref[...], g_ref[...]).astype(BF16)
    cache = {}

    def cols(c0, c1):
        g0, g1 = next((a, b) for a, b in zip(PROJ_GROUPS[:-1], PROJ_GROUPS[1:]) if a <= c0 and c1 <= b)
        if g0 not in cache:
            cache[g0] = _dot(h, w_ref[:, g0:g1])
        return cache[g0][:, c0 - g0:c1 - g0]

    return cols


def _mixer_body(*refs, tt, log_gammas, fused, n_prev):
    if fused:
        cols = _projected_columns(*refs[:3])
        refs = refs[3:]
    else:
        proj_ref = refs[0]
        cols = lambda c0, c1: proj_ref[:, c0:c1]
        refs = refs[1:]
    prev_refs, refs = refs[11:11 + 2 * n_prev], refs[:11] + refs[11 + 2 * n_prev:]
    (hist_ref, state_ref, cos_ref, sin_ref, cw_ref, cb_ref, lng_ref, lnb_ref, gng_ref, qg_ref, kg_ref,
     a_ref, bo_ref, qn_ref, kn_ref, vb_ref, ck_ref, cv_ref, nh_ref, nr_ref, aext, zbuf, sret) = refs
    t = pl.program_id(1)

    @pl.when(t == 0)
    def _():
        aext[0:A_HIST, :] = hist_ref[0]
        sret[...] = state_ref[0]

    glu = cols(0, A_WIDTH) * jax.nn.sigmoid(cols(A_WIDTH, A_COLS))
    aext[A_HIST:A_HIST + tt, :] = glu
    first = A_HIST - (A_KW - 1)
    acc = jnp.zeros((tt, A_WIDTH), F32) + cb_ref[...]
    for r in range(8):
        rows = tt if r == 0 else tt + 8
        z = None
        for off in range(r, first + A_KW, 8):
            if off < first:
                continue
            term = aext[off - r:off - r + rows, :] * cw_ref[off - first:off - first + 1, :]
            z = term if z is None else z + term
        if r == 0:
            acc = acc + z
        else:
            zbuf[r - 1] = z
            acc = acc + zbuf[r - 1, r:r + tt, :]
    mu = jnp.mean(acc, axis=-1, keepdims=True)
    xc = acc - mu
    var = jnp.mean(xc * xc, axis=-1, keepdims=True)
    ln = xc * lax.rsqrt(var + EPS) * lng_ref[...] + lnb_ref[...]
    a_ref[...] = (ln * jax.nn.sigmoid(ln)).astype(BF16)

    last_rows = aext[tt:tt + A_HIST, :]
    nh_ref[0] = last_rows
    aext[0:A_HIST, :] = last_rows

    cos = cos_ref[...]
    sin = sin_ref[...]
    ri = lax.broadcasted_iota(jnp.int32, (tt, tt), 0)
    ci = lax.broadcasted_iota(jnp.int32, (tt, tt), 1)
    dij = (ri - ci).astype(F32)
    causal = ri >= ci
    rowf = lax.broadcasted_iota(jnp.int32, (tt, 1), 0).astype(F32)
    for h in range(B_HEADS):
        lg = log_gammas[h]
        q = cols(B_OFF + h * B_DK, B_OFF + (h + 1) * B_DK)
        k = cols(B_OFF + B_HEADS * B_DK + h * B_DK, B_OFF + B_HEADS * B_DK + (h + 1) * B_DK)
        voff = B_OFF + 2 * B_HEADS * B_DK
        v = cols(voff + h * B_DV, voff + (h + 1) * B_DV)
        goff = voff + B_HEADS * B_DV
        g = cols(goff + h * B_DV, goff + (h + 1) * B_DV)
        qr = q * cos + pltpu.roll(q, B_DK // 2, 1) * sin
        kr = (k * cos + pltpu.roll(k, B_DK // 2, 1) * sin) * (B_DK ** -0.5)
        qb = qr.astype(BF16)
        vb = v.astype(BF16)
        decay = jnp.where(causal, jnp.exp(lg * jnp.maximum(dij, 0.0)), 0.0)
        scores = _dot_nt(qb, kr.astype(BF16)) * decay
        inner = _dot(scores.astype(BF16), vb)
        s_old = sret[h]
        cross = _dot(qb, s_old.astype(BF16)) * jnp.exp(lg * (rowf + 1.0))
        o = inner + cross
        kd = kr * jnp.exp(lg * (tt - 1.0 - rowf))
        s_new = s_old * math.exp(lg * tt) + _dot_tn(kd.astype(BF16), vb)
        sret[h] = s_new
        nr_ref[0, h] = s_new
        y = _rms(o, gng_ref[:, h * B_DV:(h + 1) * B_DV])
        bo_ref[:, h * B_DV:(h + 1) * B_DV] = (y * (g * jax.nn.sigmoid(g))).astype(BF16)

    lane = lax.broadcasted_iota(jnp.int32, (tt, 2 * C_DQK), 1)
    lo = lane < C_DQK

    def qk_norm(x, g2):
        sq = x * x
        s_lo = jnp.sum(jnp.where(lo, sq, 0.0), axis=-1, keepdims=True)
        s_hi = jnp.sum(jnp.where(lo, 0.0, sq), axis=-1, keepdims=True)
        ms = jnp.where(lo, s_lo, s_hi) * (1.0 / C_DQK)
        return x * lax.rsqrt(ms + EPS) * g2

    if n_prev:
        for e in range(n_prev):
            ck_ref[e] = prev_refs[2 * e][...]
            cv_ref[e] = prev_refs[2 * e + 1][...]
        ck_own, cv_own = ck_ref.at[n_prev], cv_ref.at[n_prev]
    else:
        ck_own, cv_own = ck_ref, cv_ref
    w = 2 * C_DQK
    for h in range(C_HEADS):
        cq = cols(C_OFF + h * w, C_OFF + (h + 1) * w)
        ck = cols(C_OFF + C_HEADS * w + h * w, C_OFF + C_HEADS * w + (h + 1) * w)
        qn = qk_norm(cq, qg_ref[...])
        kn = qk_norm(ck, kg_ref[...])
        qn_ref[:, h * w:(h + 1) * w] = (qn * (C_DQK ** -0.5)).astype(BF16)
        kn_ref[:, h * w:(h + 1) * w] = kn.astype(BF16)
        ck_own[pl.ds(h, tt, stride=C_HEADS), :] = kn
    cv = cols(C_OFF + 2 * C_HEADS * w, IN_COLS)
    ones = jnp.ones((tt, C_DV), BF16)
    for h in range(C_HEADS):
        cv_own[pl.ds(h, tt, stride=C_HEADS), :] = cv[:, h * C_DV:(h + 1) * C_DV]
        vb_ref[:, 2 * h * C_DV:(2 * h + 1) * C_DV] = cv[:, h * C_DV:(h + 1) * C_DV].astype(BF16)
        vb_ref[:, (2 * h + 1) * C_DV:(2 * h + 2) * C_DV] = ones


def _mixer(src, hist, state, cos2, sin2, cw, cb, lng, lnb, gng, qg2, kg2, *, bsz, t, tt, earlier=()):
    n = bsz * t
    nt = t // tt
    log_gammas = tuple(math.log(1.0 - 2.0 ** (-5.0 - h)) for h in range(B_HEADS))
    tok = lambda cols: pl.BlockSpec((tt, cols), lambda b, i: (b * nt + i, 0))
    const = lambda r, c: pl.BlockSpec((r, c), lambda b, i: (0, 0))
    cw_dim = C_HEADS * 2 * C_DQK
    fused = isinstance(src, tuple)
    if fused:
        x, g1, w_in, layer = src
        src_args = (x, g1, w_in)
        src_specs = [tok(D_MODEL), const(1, D_MODEL),
                     pl.BlockSpec((None, D_MODEL, IN_COLS), lambda b, i: (layer, 0, 0), pipeline_mode=pl.Buffered(1))]
    else:
        src_args = (src,)
        src_specs = [tok(IN_COLS)]
    n_prev = len(earlier)
    leaf = lambda width: pl.BlockSpec((None, tt * C_HEADS, width), lambda b, i: (b, i, 0))
    if n_prev:
        kv_spec = lambda width: pl.BlockSpec((n_prev + 1, None, tt * C_HEADS, width), lambda b, i: (0, b, i, 0))
        kv_shape = lambda width: jax.ShapeDtypeStruct((n_prev + 1, bsz, t * C_HEADS, width), F32)
    else:
        kv_spec = leaf
        kv_shape = lambda width: jax.ShapeDtypeStruct((bsz, t * C_HEADS, width), F32)
    return pl.pallas_call(
        functools.partial(_mixer_body, tt=tt, log_gammas=log_gammas, fused=fused, n_prev=n_prev),
        grid=(bsz, nt),
        in_specs=src_specs + [
            pl.BlockSpec((1, A_HIST, A_WIDTH), lambda b, i: (b, 0, 0)),
            pl.BlockSpec((1, B_HEADS, B_DK, B_DV), lambda b, i: (b, 0, 0, 0)),
            pl.BlockSpec((tt, B_DK), lambda b, i: (i, 0)),
            pl.BlockSpec((tt, B_DK), lambda b, i: (i, 0)),
            const(A_KW, A_WIDTH), const(1, A_WIDTH), const(1, A_WIDTH), const(1, A_WIDTH),
            const(1, B_HEADS * B_DV), const(1, 2 * C_DQK), const(1, 2 * C_DQK),
        ] + [leaf(2 * C_DQK), leaf(C_DV)] * n_prev,
        out_specs=[
            tok(A_WIDTH), tok(B_HEADS * B_DV), tok(cw_dim), tok(cw_dim), tok(2 * C_HEADS * C_DV),
            kv_spec(2 * C_DQK), kv_spec(C_DV),
            pl.BlockSpec((1, A_HIST, A_WIDTH), lambda b, i: (b, 0, 0)),
            pl.BlockSpec((1, B_HEADS, B_DK, B_DV), lambda b, i: (b, 0, 0, 0)),
        ],
        out_shape=[
            jax.ShapeDtypeStruct((n, A_WIDTH), BF16),
            jax.ShapeDtypeStruct((n, B_HEADS * B_DV), BF16),
            jax.ShapeDtypeStruct((n, cw_dim), BF16),
            jax.ShapeDtypeStruct((n, cw_dim), BF16),
            jax.ShapeDtypeStruct((n, 2 * C_HEADS * C_DV), BF16),
            kv_shape(2 * C_DQK), kv_shape(C_DV),
            jax.ShapeDtypeStruct((bsz, A_HIST, A_WIDTH), F32),
            jax.ShapeDtypeStruct((bsz, B_HEADS, B_DK, B_DV), F32),
        ],
        scratch_shapes=[pltpu.VMEM((A_HIST + tt, A_WIDTH), F32), pltpu.VMEM((7, tt + 8, A_WIDTH), F32),
                        pltpu.VMEM((B_HEADS, B_DK, B_DV), F32)],
        compiler_params=_params("parallel", "arbitrary"),
        name="mixer",
    )(*src_args, hist, state, cos2, sin2, cw, cb, lng, lnb, gng, qg2, kg2, *[a for pair in earlier for a in pair])


def _t5_bucket(rel):
    half = REL_BUCKETS // 2
    exact = half // 2
    n = np.abs(rel)
    large = exact + (np.log(np.maximum(n, 1).astype(np.float32) / exact) / math.log(REL_MAX_DIST / exact)
                     * (half - exact)).astype(np.int32)
    large = np.minimum(large, half - 1)
    return (np.where(rel > 0, half, 0) + np.where(n < exact, n, large)).astype(np.int32)


def _bias_body(rb_ref, idx_ref, vis_ref, o_ref, *, row_tile, buckets):
    h = pl.program_id(0)
    far = rb_ref[REL_BUCKETS // 2 - 1, h]
    for k, present in enumerate(buckets):
        rows = slice(k * row_tile, (k + 1) * row_tile)
        idx = idx_ref[rows, :]
        acc = jnp.zeros(idx.shape, F32)
        for b in present:
            acc = jnp.where(idx == b, rb_ref[b, h], acc)
        o_ref[0, rows, :] = jnp.where(vis_ref[rows, :] != 0, acc - far, NEG_INF)


def _bias_tiles(rel_bias, idx, vis, row_tile):
    r, c = idx.shape
    buckets = tuple(tuple(int(b) for b in np.unique(idx[k:k + row_tile])) for k in range(0, r, row_tile))
    idx, vis = jnp.asarray(idx), jnp.asarray(vis)
    return pl.pallas_call(
        functools.partial(_bias_body, row_tile=row_tile, buckets=buckets),
        grid=(C_HEADS,),
        in_specs=[
            pl.BlockSpec(memory_space=pltpu.SMEM),
            pl.BlockSpec((r, c), lambda h: (0, 0)),
            pl.BlockSpec((r, c), lambda h: (0, 0)),
        ],
        out_specs=pl.BlockSpec((1, r, c), lambda h: (h, 0, 0)),
        out_shape=jax.ShapeDtypeStruct((C_HEADS, r, c), F32),
        compiler_params=_params("arbitrary"),
        name="bias_tiles",
    )(rel_bias, idx, vis)


def _lambda(lp_ref, lam_init):
    lp = lp_ref[...]
    e1 = jnp.exp(jnp.sum(lp[0:1] * lp[1:2], axis=-1, keepdims=True))
    e2 = jnp.exp(jnp.sum(lp[2:3] * lp[3:4], axis=-1, keepdims=True))
    return e1 - e2 + lam_init


def _stack_maps(q):
    lane = lax.broadcasted_iota(jnp.int32, q.shape, 1)
    zero = jnp.zeros_like(q)
    return jnp.concatenate([jnp.where(lane < C_DQK, q, zero), jnp.where(lane < C_DQK, zero, q)], axis=0)


def _attn_finish(acc, l, lam, g, lam_init, tq):
    o = acc / l
    o = o[0:tq] - lam * o[tq:2 * tq]
    return _rms(o, g) * (1.0 - lam_init)


def _attn_prompt_body(*refs, tq, lam_init, n_cast):
    _cast_slabs(refs[6:6 + n_cast], refs[7 + n_cast:7 + 2 * n_cast])
    q_ref, k_ref, v_ref, bias_ref, lp_ref, g_ref = refs[:6]
    o_ref = refs[6 + n_cast]
    qq_scr, m_scr, alpha_scr, p_scr, acc_scr = refs[7 + 2 * n_cast:]
    qi = pl.program_id(2)
    qq_scr[...] = _stack_maps(q_ref[...])
    m_scr[...] = jnp.full(m_scr.shape, NEG_INF, F32)
    acc_scr[...] = jnp.zeros(acc_scr.shape, F32)

    def tile(ref, idx):
        start = idx * tq if isinstance(idx, int) else pl.multiple_of(idx * tq, tq)
        return ref[pl.ds(start, tq), :]

    nl = tq // ATT_LANES
    chunks = [slice(c * ATT_ROWS, (c + 1) * ATT_ROWS) for c in range(2 * tq // ATT_ROWS)]

    def accumulate(rows, vb):
        alpha = alpha_scr[rows, :]
        acc_scr[rows, :] = jnp.concatenate([alpha, alpha], axis=-1) * acc_scr[rows, :] + _dot(p_scr[rows, :], vb)

    def step(ki, bias_idx, first=False):
        kb = tile(k_ref, ki)
        if not first:
            v_prev = tile(v_ref, ki - 1)
        for rows in chunks:
            if not first:
                accumulate(rows, v_prev)
            s = _dot_nt(qq_scr[rows, :], kb)
            if bias_idx is not None:
                s = s + bias_ref[0, bias_idx, pl.ds(rows.start % tq, ATT_ROWS), :]
            slabs = [s[:, j * ATT_LANES:(j + 1) * ATT_LANES] for j in range(nl)]
            m_old = m_scr[rows, :]
            m_new = jnp.maximum(m_old, jnp.max(functools.reduce(jnp.maximum, slabs), axis=-1, keepdims=True))
            alpha_scr[rows, :] = jnp.exp(m_old - m_new)
            p_scr[rows, :] = jnp.concatenate([jnp.exp(sl - m_new) for sl in slabs], axis=-1).astype(BF16)
            m_scr[rows, :] = m_new

    n_far = jnp.maximum(qi - 1, 0)
    step(0, jnp.minimum(qi, 2), first=True)

    def far_body(ki, c):
        step(ki, None)
        return c

    lax.fori_loop(1, n_far, far_body, 0)

    def near_body(ki, c):
        step(ki, qi - ki)
        return c

    lax.fori_loop(jnp.maximum(n_far, 1), qi + 1, near_body, 0)

    v_last = tile(v_ref, qi)
    for rows in chunks:
        accumulate(rows, v_last)
    lam = _lambda(lp_ref, lam_init)
    acc = acc_scr[...]
    o_ref[...] = _attn_finish(acc[:, 0:C_DV], acc[:, C_DV:2 * C_DV], lam, g_ref[...], lam_init, tq).astype(BF16)


def _attn_prompt(qn, kn, vb, bias, lp, g, *, bsz, t, lam_init, cast_ride=()):
    tq = ATT_TILE
    nq = t // tq
    n = bsz * t
    w = 2 * C_DQK
    cast_args, cast_in, cast_out, cast_shapes = (), [], [], []
    if cast_ride:
        cast_args, cast_in, cast_out, cast_shapes = _weight_cast_ride(
            cast_ride, bsz * C_HEADS * nq, lambda b, h, i: (b * C_HEADS + h) * nq + i)
    results = pl.pallas_call(
        functools.partial(_attn_prompt_body, tq=tq, lam_init=lam_init, n_cast=len(cast_args)),
        grid=(bsz, C_HEADS, nq),
        in_specs=[
            pl.BlockSpec((tq, w), lambda b, h, i: (b * nq + i, h)),
            pl.BlockSpec((t, w), lambda b, h, i: (b, h)),
            pl.BlockSpec((t, 2 * C_DV), lambda b, h, i: (b, h)),
            pl.BlockSpec((1, 3, tq, tq), lambda b, h, i: (h, 0, 0, 0)),
            pl.BlockSpec((4, C_DQK), lambda b, h, i: (0, 0)),
            pl.BlockSpec((1, C_DV), lambda b, h, i: (0, 0)),
        ] + cast_in,
        out_specs=[pl.BlockSpec((tq, C_DV), lambda b, h, i: (b * nq + i, h))] + cast_out,
        out_shape=[jax.ShapeDtypeStruct((n, C_HEADS * C_DV), BF16)] + cast_shapes,
        scratch_shapes=[pltpu.VMEM((2 * tq, w), BF16), pltpu.VMEM((2 * tq, ATT_LANES), F32),
                        pltpu.VMEM((2 * tq, ATT_LANES), F32), pltpu.VMEM((2 * tq, tq), BF16),
                        pltpu.VMEM((2 * tq, 2 * C_DV), F32)],
        compiler_params=_params("arbitrary", "arbitrary", "arbitrary"),
        name="attn_prompt",
    )(qn, kn, vb, bias, lp, g, *cast_args)
    return results[0], tuple(results[1:])


def _attn_sample_body(q_ref, k_ref, v_ref, pk_ref, pv_ref, bp_ref, bn_ref, lp_ref, g_ref, o_ref, *, tq, lam_init):
    lam = _lambda(lp_ref, lam_init)
    w = 2 * C_DQK
    for h in range(C_HEADS):
        qq = _stack_maps(q_ref[:, h * w:(h + 1) * w])
        bp = bp_ref[h]
        bn = bn_ref[h]
        past = pk_ref.shape[0] // C_HEADS
        head_rows = pl.ds(h, past, stride=C_HEADS)
        s_p = _dot_nt(qq, pk_ref[head_rows, :].astype(BF16)) + jnp.concatenate([bp, bp], axis=0)
        s_n = _dot_nt(qq, k_ref[:, h * w:(h + 1) * w]) + jnp.concatenate([bn, bn], axis=0)
        m = jnp.maximum(jnp.max(s_p, axis=-1, keepdims=True), jnp.max(s_n, axis=-1, keepdims=True))
        p_p = jnp.exp(s_p - m)
        p_n = jnp.exp(s_n - m)
        l = jnp.sum(p_p, axis=-1, keepdims=True) + jnp.sum(p_n, axis=-1, keepdims=True)
        acc = (_dot(p_p.astype(BF16), pv_ref[head_rows, :].astype(BF16))
               + _dot(p_n.astype(BF16), v_ref[:, 2 * h * C_DV:(2 * h + 1) * C_DV]))
        o_ref[:, h * C_DV:(h + 1) * C_DV] = _attn_finish(acc, l, lam, g_ref[...], lam_init, tq).astype(BF16)


def _attn_sample(qn, kn, vb, past_k, past_v, bias_p, bias_n, lp, g, layer, *, bsz, t, lam_init):
    n = bsz * t
    w = 2 * C_DQK
    depth, _, past = past_k.shape[:3]
    past_k = past_k.reshape(depth, bsz, past * C_HEADS, w)
    past_v = past_v.reshape(depth, bsz, past * C_HEADS, C_DV)
    full = lambda a: pl.BlockSpec(a.shape, lambda b: (0,) * a.ndim)
    return pl.pallas_call(
        functools.partial(_attn_sample_body, tq=t, lam_init=lam_init),
        grid=(bsz,),
        in_specs=[
            pl.BlockSpec((t, C_HEADS * w), lambda b: (b, 0)),
            pl.BlockSpec((t, C_HEADS * w), lambda b: (b, 0)),
            pl.BlockSpec((t, 2 * C_HEADS * C_DV), lambda b: (b, 0)),
            pl.BlockSpec((None, None, past * C_HEADS, w), lambda b: (layer, b, 0, 0)),
            pl.BlockSpec((None, None, past * C_HEADS, C_DV), lambda b: (layer, b, 0, 0)),
            full(bias_p), full(bias_n),
            pl.BlockSpec((4, C_DQK), lambda b: (0, 0)),
            pl.BlockSpec((1, C_DV), lambda b: (0, 0)),
        ],
        out_specs=pl.BlockSpec((t, C_HEADS * C_DV), lambda b: (b, 0)),
        out_shape=jax.ShapeDtypeStruct((n, C_HEADS * C_DV), BF16),
        compiler_params=_params("parallel"),
        name="attn_sample",
    )(qn, kn, vb, past_k, past_v, bias_p, bias_n, lp, g)


def _memkv_body(mem_ref, g_ref, wk_ref, wv_ref, kg_ref, k_ref, v_ref, kb_ref, vb_ref):
    hm = _rms(mem_ref[0], g_ref[...]).astype(BF16)
    k = _dot(hm, wk_ref[...])
    v = _dot(hm, wv_ref[...])
    for h in range(M_HEADS):
        kn = _rms(k[:, h * M_DH:(h + 1) * M_DH], kg_ref[...])
        k_ref[0, :, h * M_DH:(h + 1) * M_DH] = kn
        kb_ref[0, :, h * M_DH:(h + 1) * M_DH] = kn.astype(BF16)
    v_ref[0] = v
    vb_ref[0] = v.astype(BF16)


def _memkv(mem, g, wk, wv, kg):
    bsz, m, _ = mem.shape
    w = M_HEADS * M_DH
    blk = pl.BlockSpec((1, m, w), lambda b: (b, 0, 0))
    return pl.pallas_call(
        _memkv_body,
        grid=(bsz,),
        in_specs=[
            pl.BlockSpec((1, m, D_MODEL), lambda b: (b, 0, 0)),
            pl.BlockSpec((1, D_MODEL), lambda b: (0, 0)),
            pl.BlockSpec((D_MODEL, w), lambda b: (0, 0)),
            pl.BlockSpec((D_MODEL, w), lambda b: (0, 0)),
            pl.BlockSpec((1, M_DH), lambda b: (0, 0)),
        ],
        out_specs=[blk, blk, blk, blk],
        out_shape=[jax.ShapeDtypeStruct((bsz, m, w), F32), jax.ShapeDtypeStruct((bsz, m, w), F32),
                   jax.ShapeDtypeStruct((bsz, m, w), BF16), jax.ShapeDtypeStruct((bsz, m, w), BF16)],
        compiler_params=_params("parallel"),
        name="memkv",
    )(mem, g, wk, wv, kg)


def _post_body(x_ref, a_ref, bo_ref, co_ref, wo_ref, g_ref, wq_ref, qg_ref, mk_ref, mv_ref, wxo_ref,
               o_ref, att_scr, *, nseq, rps):
    rc = x_ref.shape[0] // POST_CHAINS
    seg = min(rc, rps)
    for c in range(POST_CHAINS):
        rows = slice(c * rc, (c + 1) * rc)
        y = (_dot(a_ref[rows, :], wo_ref[0:A_WIDTH, :])
             + _dot(bo_ref[rows, :], wo_ref[A_WIDTH:A_WIDTH + B_HEADS * B_DV, :])
             + _dot(co_ref[rows, :], wo_ref[A_WIDTH + B_HEADS * B_DV:, :]))
        x1 = x_ref[rows, :] + y
        q = _dot(_rms(x1, g_ref[...]).astype(BF16), wq_ref[...])
        for h in range(M_HEADS):
            sl = slice(h * M_DH, (h + 1) * M_DH)
            qn = _rms(q[:, sl], qg_ref[...]).astype(BF16)
            for u in range(rc // seg):
                r0 = c * rc + u * seg
                s = r0 // rps
                logits = _dot_nt(qn[u * seg:(u + 1) * seg], mk_ref[s, :, sl]) * (M_DH ** -0.5)
                m = jnp.max(logits, axis=-1, keepdims=True)
                p = jnp.exp(logits - m)
                l = jnp.sum(p, axis=-1, keepdims=True)
                o = _dot(p.astype(BF16), mv_ref[s, :, sl]) / l
                att_scr[r0:r0 + seg, sl] = o.astype(BF16)
        o_ref[rows, :] = x1 + _dot(att_scr[rows, :], wxo_ref[...])


def _post(x, a, bo, co, wo, g, wq, qg, mk, mv, wxo, layer, *, t, tm):
    n = x.shape[0]
    w = M_HEADS * M_DH
    assert n % tm == 0 and (t % tm == 0 or tm % t == 0), (n, t, tm)
    if t >= tm:
        nseq, rps = 1, tm
        per = t // tm
        mem_map = lambda i: (i // per, 0, 0)
    else:
        nseq, rps = tm // t, t
        mem_map = lambda i: (i, 0, 0)
    tok = lambda cols: pl.BlockSpec((tm, cols), lambda i: (i, 0))
    const = lambda r, c: pl.BlockSpec((r, c), lambda i: (0, 0), pipeline_mode=pl.Buffered(1))
    stacked = lambda r, c: pl.BlockSpec((None, r, c), lambda i: (layer, 0, 0), pipeline_mode=pl.Buffered(1))
    return pl.pallas_call(
        functools.partial(_post_body, nseq=nseq, rps=rps),
        grid=(n // tm,),
        in_specs=[
            tok(D_MODEL), tok(A_WIDTH), tok(B_HEADS * B_DV), tok(C_HEADS * C_DV),
            stacked(D_MODEL, D_MODEL), const(1, D_MODEL), stacked(D_MODEL, w), const(1, M_DH),
            pl.BlockSpec((nseq, MEM_LEN, w), mem_map),
            pl.BlockSpec((nseq, MEM_LEN, w), mem_map),
            stacked(w, D_MODEL),
        ],
        out_specs=tok(D_MODEL),
        out_shape=jax.ShapeDtypeStruct((n, D_MODEL), F32),
        scratch_shapes=[pltpu.VMEM((tm, w), BF16)],
        compiler_params=_params("parallel"),
        name="post",
    )(x, a, bo, co, wo, g, wq, qg, mk, mv, wxo)


def _ffn_body(*refs, nseq, rps, per, n_cast):
    _cast_slabs(refs[11:11 + n_cast], refs[14 + n_cast:14 + 2 * n_cast])
    x_ref, g_ref, wv_ref, wg_ref, cwv_ref, cwg_ref, cbv_ref, cbg_ref, hv_ref, hg_ref, wd_ref = refs[:11]
    o_ref, nv_ref, ng_ref = refs[11 + n_cast:14 + n_cast]
    h_scr, ubuf, tail = refs[14 + 2 * n_cast:]
    i = pl.program_id(0)
    j = pl.program_id(1)

    @pl.when(j == 0)
    def _():
        x = x_ref[...]
        h_scr[...] = _rms(x, g_ref[...]).astype(BF16)
        o_ref[...] = x

    if per > 1:
        @pl.when((i == 0) & (j == 0))
        def _():
            tail[...] = jnp.zeros(tail.shape, F32)

    seq_start = (i % per) == 0
    stride = rps + F_HIST
    tm = x_ref.shape[0]
    rc = tm // FFN_CHAINS
    assert rc % rps == 0 or rps % rc == 0

    seg = min(rc, rps)
    chains = [slice(r * rc, (r + 1) * rc) for r in range(FFN_CHAINS)]

    def up(rows):
        return _dot(h_scr[rows, :], wv_ref[...]), _dot(h_scr[rows, :], wg_ref[...])

    def keep(us):
        for rows, pair in zip(chains, us):
            for half, u in enumerate(pair):
                for q in range(rc // seg):
                    s, o = divmod(rows.start + q * seg, rps)
                    dst = s * stride + F_HIST + o
                    ubuf[half, dst:dst + seg, :] = u[q * seg:(q + 1) * seg]

    def conv(half, r0, cw_ref, cb_ref, hist_ref, new_ref):
        outs = []
        for q in range(rc // seg):
            s, o = divmod(r0 + q * seg, rps)
            base = s * stride
            if o == 0:
                if per == 1:
                    prev = hist_ref[s]
                else:
                    prev = jnp.where(seq_start, hist_ref[s], tail[half, j])
                ubuf[half, base:base + F_HIST, :] = prev
            c = cb_ref[...]
            for k in range(F_KW):
                off = base + F_HIST + o - (F_KW - 1) + k
                c = c + ubuf[half, off:off + seg, :] * cw_ref[k:k + 1, :]
            outs.append(c)
            if o + seg == rps:
                last = ubuf[half, base + rps:base + stride, :]
                new_ref[s] = last
                if per > 1:
                    tail[half, j] = last
        return outs[0] if len(outs) == 1 else jnp.concatenate(outs, axis=0)

    def down(rows):
        val = conv(0, rows.start, cwv_ref, cbv_ref, hv_ref, nv_ref)
        gate = conv(1, rows.start, cwg_ref, cbg_ref, hg_ref, ng_ref)
        act = (gate * jax.nn.sigmoid(gate) * val).astype(BF16)
        o_ref[rows, :] += _dot(act, wd_ref[...])

    keep([up(rows) for rows in chains])
    for rows in chains:
        down(rows)


def _slab(total, unit, steps):
    return next(unit * k for k in range(1, total // unit + 1) if total % (unit * k) == 0 and total // (unit * k) <= steps)


def _weight_cast_ride(entries, steps, step_of):
    args, in_specs, out_specs, out_shapes = [], [], [], []
    for w, layer, axis in entries:
        _, k, ncol = w.shape
        size = _slab((k, ncol)[axis], (BF16_SUBLANES, LANES)[axis], steps)
        last = (k, ncol)[axis] // size - 1
        block = (None, size, ncol) if axis == 0 else (None, k, size)

        def index(*g, lead, axis=axis, last=last):
            s = jnp.minimum(step_of(*g), last)
            return (lead, s, 0) if axis == 0 else (lead, 0, s)

        args.append(w)
        in_specs.append(pl.BlockSpec(block, functools.partial(index, lead=layer)))
        out_specs.append(pl.BlockSpec(block, functools.partial(index, lead=0)))
        out_shapes.append(jax.ShapeDtypeStruct((1, k, ncol), BF16))
    return tuple(args), in_specs, out_specs, out_shapes


def _cast_slabs(f32_refs, bf16_refs):
    for src, dst in zip(f32_refs, bf16_refs):
        dst[...] = src[...].astype(BF16)


def _cast_weights(entries, steps=CAST_STEPS):
    args, in_specs, out_specs, out_shapes = _weight_cast_ride(entries, steps, lambda s: s)
    return pl.pallas_call(
        lambda *refs: _cast_slabs(refs[:len(args)], refs[len(args):]),
        grid=(steps,),
        in_specs=in_specs,
        out_specs=out_specs,
        out_shape=out_shapes,
        compiler_params=_params("arbitrary"),
        name="cast_weights",
    )(*args)


def _ffn(x, g, w_up, cw, cb, hist, w_down, layer, *, t, cast_next=None, tm=TOKEN_TILE, tn=COL_TILE):
    n = x.shape[0]
    nj = D_FF // tn
    assert n % tm == 0 and (t % tm == 0 or tm % t == 0), (n, t, tm)
    if t >= tm:
        nseq, rps, per = 1, tm, t // tm
        seq_map = lambda i: i // per
    else:
        nseq, rps, per = tm // t, t, 1
        seq_map = lambda i: i
    bsz = hist.shape[0]
    hist_spec = lambda off: pl.BlockSpec((nseq, F_HIST, tn), lambda i, j: (seq_map(i), 0, j + off))
    new_spec = pl.BlockSpec((nseq, F_HIST, tn), lambda i, j: (i, 0, j))
    cast_args, cast_in, cast_out, cast_shapes = (), [], [], []
    if cast_next:
        cast_args, cast_in, cast_out, cast_shapes = _weight_cast_ride(cast_next, (n // tm) * nj,
                                                                      lambda i, j: i * nj + j)
    results = pl.pallas_call(
        functools.partial(_ffn_body, nseq=nseq, rps=rps, per=per, n_cast=len(cast_args)),
        grid=(n // tm, nj),
        in_specs=[
            pl.BlockSpec((tm, D_MODEL), lambda i, j: (i, 0), pipeline_mode=pl.Buffered(1)),
            pl.BlockSpec((1, D_MODEL), lambda i, j: (0, 0)),
            pl.BlockSpec((None, D_MODEL, tn), lambda i, j: (layer, 0, j)),
            pl.BlockSpec((None, D_MODEL, tn), lambda i, j: (layer, 0, j + nj)),
            pl.BlockSpec((F_KW, tn), lambda i, j: (0, j)),
            pl.BlockSpec((F_KW, tn), lambda i, j: (0, j + nj)),
            pl.BlockSpec((1, tn), lambda i, j: (0, j)),
            pl.BlockSpec((1, tn), lambda i, j: (0, j + nj)),
            hist_spec(0), hist_spec(nj),
            pl.BlockSpec((None, tn, D_MODEL), lambda i, j: (layer, j, 0)),
        ] + cast_in,
        out_specs=[pl.BlockSpec((tm, D_MODEL), lambda i, j: (i, 0)), new_spec, new_spec] + cast_out,
        out_shape=[jax.ShapeDtypeStruct((n, D_MODEL), F32),
                   jax.ShapeDtypeStruct((bsz * per, F_HIST, D_FF), F32),
                   jax.ShapeDtypeStruct((bsz * per, F_HIST, D_FF), F32)] + cast_shapes,
        scratch_shapes=[
            pltpu.VMEM((tm, D_MODEL), BF16),
            pltpu.VMEM((2, nseq * (rps + F_HIST), tn), F32),
            pltpu.VMEM((2, nj, F_HIST, tn), F32),
        ],
        compiler_params=_params("arbitrary", "arbitrary"),
        name="ffn",
    )(x, g, w_up, w_up, cw, cw, cb, cb, hist, hist, w_down, *cast_args)
    x_out, tail_v, tail_g = results[:3]
    last = lambda a: a.reshape(bsz, per, F_HIST, D_FF)[:, per - 1]
    return x_out, last(tail_v), last(tail_g), tuple(results[3:])


def _rope_tables(pos0, t):
    half = B_DK // 2
    inv = 1.0 / (ROPE_BASE ** (jnp.arange(half, dtype=F32) / half))
    ang = (pos0 + jnp.arange(t, dtype=jnp.int32)).astype(F32)[:, None] * inv[None, :]
    cos = jnp.cos(ang)
    sin = jnp.sin(ang)
    return jnp.concatenate([cos, cos], axis=-1), jnp.concatenate([-sin, sin], axis=-1)


def _pad_rows(h, rows):
    return jnp.pad(h, ((0, 0), (rows - h.shape[1], 0), (0, 0)))


def _row(v):
    return v.reshape(1, -1).astype(F32)


def kernel(x_prompt, x_sample, mem_prompt, state_conv_a, state_ret, cache_diff_k, cache_diff_v, cache_mem_k,
           cache_mem_v, state_conv_f, norm1_g, w_in, conv_a_w, conv_a_b, ln_a_g, ln_a_b, ret_gn_g, diff_qn_g,
           diff_kn_g, diff_lq1, diff_lk1, diff_lq2, diff_lk2, diff_subln_g, w_out, rel_bias, norm2_g, mem_norm_g,
           w_xq, w_xk, w_xv, xqn_g, xkn_g, w_xo, norm3_g, w_up, conv_f_w, conv_f_b, w_down):
    bp, tp, _ = x_prompt.shape
    bs, ts, _ = x_sample.shape
    depth = w_in.shape[0]
    past = cache_diff_k.shape[2]
    cw_dim = C_HEADS * 2 * C_DQK

    tq = ATT_TILE
    r = np.arange(tq)
    rel_diag = r[None, :] - r[:, None]
    idx_p = np.stack([_t5_bucket(rel_diag - d * tq) for d in range(3)]).reshape(3 * tq, tq)
    vis_p = np.stack([(r[None, :] // CHUNK) <= (r[:, None] // CHUNK), np.ones((tq, tq), bool), np.ones((tq, tq), bool)])
    vis_p = vis_p.reshape(3 * tq, tq).astype(np.int32)
    bias_p = _bias_tiles(rel_bias, idx_p, vis_p, tq).reshape(C_HEADS, 3, tq, tq)
    rel_s = np.arange(past + ts)[None, :] - (past + np.arange(ts))[:, None]
    bias_s = _bias_tiles(rel_bias, _t5_bucket(rel_s), np.ones(rel_s.shape, np.int32), ts)
    bias_s_past, bias_s_new = bias_s[:, :, :past], bias_s[:, :, past:]

    cos_p, sin_p = _rope_tables(0, tp)
    cos_s, sin_s = _rope_tables(past, ts)

    xp = x_prompt.reshape(bp * tp, D_MODEL)
    xs = x_sample.reshape(bs * ts, D_MODEL)
    zero_a = jnp.zeros((bp, A_HIST, A_WIDTH), F32)
    zero_r = jnp.zeros((bp, B_HEADS, B_DK, B_DV), F32)
    zero_f = jnp.zeros((bp, F_HIST, 2 * D_FF), F32)

    outs = {k: [] for k in ("p_ca", "p_rs", "p_mk", "p_mv", "p_cf", "s_ca", "s_rs", "s_cf")}
    kv_p, kv_s = [], []
    small = {"in": w_in, "out": w_out, "xq": w_xq, "xo": w_xo}
    cur = dict(zip(small, _cast_weights([(w, 0, 1) for w in small.values()])))
    nxt = {}
    ffn_w = [None]
    for l in range(depth):
        lam_init = 0.8 - 0.6 * math.exp(-0.3 * l)
        has_next = l + 1 < depth
        w_in_b, w_out_b, w_xq_b, w_xo_b = cur["in"], cur["out"], cur["xq"], cur["xo"]
        lp = jnp.stack([diff_lq1[l], diff_lk1[l], diff_lq2[l], diff_lk2[l]]).astype(F32)
        qg2 = _row(jnp.concatenate([diff_qn_g[l], diff_qn_g[l]]))
        kg2 = _row(jnp.concatenate([diff_kn_g[l], diff_kn_g[l]]))
        mix_args = (conv_a_w[l], _row(conv_a_b[l]), _row(ln_a_g[l]), _row(ln_a_b[l]), _row(ret_gn_g[l]), qg2, kg2)
        subln = _row(diff_subln_g[l])

        def block(x, bsz, t, tt, hist_a, state_r, cos2, sin2, attend, mk_b, mv_b, hist_f, tm_post, kv_leaves,
                  cast_next=None):
            if tt >= FUSED_PROJ_MIN_ROWS:
                src = (x, _row(norm1_g[l]), w_in_b, 0)
            else:
                src = _in_proj(x, _row(norm1_g[l]), w_in_b, 0)
            a, bo, qn, kn, vb, ck, cv, nh, nr = _mixer(src, hist_a, state_r, cos2, sin2, *mix_args,
                                                        bsz=bsz, t=t, tt=tt, earlier=() if has_next else tuple(kv_leaves))
            kv_leaves.append((ck, cv))
            co = attend(qn, kn, vb)
            x = _post(x, a, bo, co, w_out_b, _row(norm2_g[l]), w_xq_b, _row(xqn_g[l]), mk_b, mv_b, w_xo_b,
                      0, t=t, tm=tm_post)
            w_up_b, w_down_b = ffn_w[0]
            x, nfv, nfg, next_w = _ffn(x, _row(norm3_g[l]), w_up_b, conv_f_w[l], _row(conv_f_b[l]), hist_f, w_down_b,
                                       0, t=t, cast_next=cast_next)
            new_f = jnp.concatenate([nfv[:, F_HIST - (F_KW - 1):], nfg[:, F_HIST - (F_KW - 1):]], axis=-1)
            return x, nh[:, A_HIST - (A_KW - 1):], nr, new_f, next_w

        mk, mv, mk_b, mv_b = _memkv(mem_prompt, _row(mem_norm_g[l]), w_xk[l].astype(BF16), w_xv[l].astype(BF16),
                                    _row(xkn_g[l]))
        def attend_p(qn, kn, vb):
            ride = [(w, l + 1, 1) for w in small.values()] if has_next else []
            if ffn_w[0] is None:
                ride += [(w_up, l, 1), (w_down, l, 0)]
            co, cast = _attn_prompt(qn, kn, vb, bias_p, lp, subln, bsz=bp, t=tp, lam_init=lam_init, cast_ride=ride)
            if has_next:
                nxt.update(zip(small, cast[:len(small)]))
            if ffn_w[0] is None:
                ffn_w[0] = cast[-2:]
            return co

        xp, ca, rs, cf, next_w = block(xp, bp, tp, MIX_TILE, zero_a, zero_r, cos_p, sin_p, attend_p, mk_b,
                                       mv_b, zero_f, POST_TILE, kv_p,
                                       cast_next=[(w_up, l + 1, 1), (w_down, l + 1, 0)] if has_next else None)
        outs["p_ca"].append(ca); outs["p_rs"].append(rs)
        outs["p_mk"].append(mk.reshape(bp, MEM_LEN, M_HEADS, M_DH))
        outs["p_mv"].append(mv.reshape(bp, MEM_LEN, M_HEADS, M_DH))
        outs["p_cf"].append(cf)

        attend_s = lambda qn, kn, vb: _attn_sample(qn, kn, vb, cache_diff_k, cache_diff_v, bias_s_past,
                                                   bias_s_new, lp, subln, l,
                                                   bsz=bs, t=ts, lam_init=lam_init)
        smk = cache_mem_k[l].reshape(bs, MEM_LEN, M_HEADS * M_DH).astype(BF16)
        smv = cache_mem_v[l].reshape(bs, MEM_LEN, M_HEADS * M_DH).astype(BF16)
        xs, sca, srs, scf, _ = block(xs, bs, ts, ts, _pad_rows(state_conv_a[l], A_HIST), state_ret[l],
                                     cos_s, sin_s, attend_s, smk, smv, _pad_rows(state_conv_f[l], F_HIST),
                                     POST_TILE, kv_s)
        outs["s_ca"].append(sca); outs["s_rs"].append(srs)
        outs["s_cf"].append(scf)
        if has_next:
            ffn_w[0] = next_w
            cur = dict(nxt)

    st = lambda k: jnp.stack(outs[k])

    def heads(leaf, bsz, t):
        leaf = leaf if leaf.ndim == 4 else leaf[None]
        return leaf.reshape(depth, bsz, t, C_HEADS, leaf.shape[-1])

    return (xp.reshape(bp, tp, D_MODEL), xs.reshape(bs, ts, D_MODEL),
            st("p_ca"), st("p_rs"), heads(kv_p[-1][0], bp, tp), heads(kv_p[-1][1], bp, tp), st("p_mk"), st("p_mv"),
            st("p_cf"),
            st("s_ca"), st("s_rs"), heads(kv_s[-1][0], bs, ts), heads(kv_s[-1][1], bs, ts), st("s_cf"))
```

```python
import functools
import math

import numpy as np
import jax
import jax.numpy as jnp
from jax import lax
from jax.experimental import pallas as pl
from jax.experimental.pallas import tpu as pltpu

F32 = jnp.float32
BF16 = jnp.bfloat16
EPS = 1e-6
NEG_INF = -1e30

D_MODEL = 2048
CHUNK = 64
A_WIDTH = 512
A_KW = 31
A_HIST = 32
B_HEADS = 4
B_DK = 128
B_DV = 256
ROPE_BASE = 10000.0
C_HEADS = 4
C_DQK = 64
C_DV = 128
REL_BUCKETS = 32
REL_MAX_DIST = 128
M_HEADS = 4
M_DH = 128
MEM_LEN = 256
D_FF = 5632
F_KW = 3
F_HIST = 8

A_COLS = 2 * A_WIDTH
B_COLS = B_HEADS * (2 * B_DK + 2 * B_DV)
C_COLS = C_HEADS * (4 * C_DQK + C_DV)
IN_COLS = A_COLS + B_COLS + C_COLS
B_OFF = A_COLS
C_OFF = A_COLS + B_COLS
PROJ_GROUPS = (0, A_WIDTH, A_COLS, B_OFF + 2 * B_HEADS * B_DK, B_OFF + 2 * B_HEADS * B_DK + B_HEADS * B_DV, C_OFF,
               C_OFF + 2 * C_HEADS * 2 * C_DQK, IN_COLS)

VMEM_LIMIT_BYTES = 60 * 1024 * 1024
LANES = 128
BF16_SUBLANES = 16
TOKEN_TILE = 1024
COL_TILE = 512
MIX_TILE = 256
ATT_TILE = 512
ATT_ROWS = 256
ATT_LANES = 128
FFN_CHAINS = 2
POST_TILE = 512
POST_CHAINS = 2
CAST_STEPS = 11
FUSED_PROJ_MIN_ROWS = 128


def _params(*sem):
    return pltpu.CompilerParams(dimension_semantics=sem, vmem_limit_bytes=VMEM_LIMIT_BYTES)


def _rms(x, g):
    return x * lax.rsqrt(jnp.mean(x * x, axis=-1, keepdims=True) + EPS) * g


def _dot(a, b):
    return jnp.dot(a, b, preferred_element_type=F32)


def _dot_nt(a, b):
    return lax.dot_general(a, b, (((1,), (1,)), ((), ())), preferred_element_type=F32)


def _dot_tn(a, b):
    return lax.dot_general(a, b, (((0,), (0,)), ((), ())), preferred_element_type=F32)


def _in_proj_body(x_ref, g_ref, w_ref, o_ref, h_ref):
    @pl.when(pl.program_id(1) == 0)
    def _():
        h_ref[...] = _rms(x_ref[...], g_ref[...]).astype(BF16)

    o_ref[...] = _dot(h_ref[...], w_ref[...])


def _in_proj(x, g, w, layer, *, tm=TOKEN_TILE, tn=COL_TILE):
    n = x.shape[0]
    cols = w.shape[2]
    return pl.pallas_call(
        _in_proj_body,
        grid=(n // tm, cols // tn),
        in_specs=[
            pl.BlockSpec((tm, D_MODEL), lambda i, j: (i, 0)),
            pl.BlockSpec((1, D_MODEL), lambda i, j: (0, 0)),
            pl.BlockSpec((None, D_MODEL, tn), lambda i, j: (layer, 0, j)),
        ],
        out_specs=pl.BlockSpec((tm, tn), lambda i, j: (i, j)),
        out_shape=jax.ShapeDtypeStruct((n, cols), F32),
        scratch_shapes=[pltpu.VMEM((tm, D_MODEL), BF16)],
        compiler_params=_params("parallel", "arbitrary"),
        name="in_proj",
    )(x, g, w)


def _projected_columns(x_ref, g_ref, w_ref):
    h = _rms(x_ref[...], g_ref[...]).astype(BF16)
    cache = {}

    def cols(c0, c1):
        g0, g1 = next((a, b) for a, b in zip(PROJ_GROUPS[:-1], PROJ_GROUPS[1:]) if a <= c0 and c1 <= b)
        if g0 not in cache:
            cache[g0] = _dot(h, w_ref[:, g0:g1])
        return cache[g0][:, c0 - g0:c1 - g0]

    return cols


def _mixer_body(*refs, tt, log_gammas, fused, n_prev):
    if fused:
        cols = _projected_columns(*refs[:3])
        refs = refs[3:]
    else:
        proj_ref = refs[0]
        cols = lambda c0, c1: proj_ref[:, c0:c1]
        refs = refs[1:]
    prev_refs, refs = refs[11:11 + 2 * n_prev], refs[:11] + refs[11 + 2 * n_prev:]
    (hist_ref, state_ref, cos_ref, sin_ref, cw_ref, cb_ref, lng_ref, lnb_ref, gng_ref, qg_ref, kg_ref,
     a_ref, bo_ref, qn_ref, kn_ref, vb_ref, ck_ref, cv_ref, nh_ref, nr_ref, aext, zbuf, sret) = refs
    t = pl.program_id(1)

    @pl.when(t == 0)
    def _():
        aext[0:A_HIST, :] = hist_ref[0]
        sret[...] = state_ref[0]

    glu = cols(0, A_WIDTH) * jax.nn.sigmoid(cols(A_WIDTH, A_COLS))
    aext[A_HIST:A_HIST + tt, :] = glu
    first = A_HIST - (A_KW - 1)
    acc = jnp.zeros((tt, A_WIDTH), F32) + cb_ref[...]
    for r in range(8):
        rows = tt if r == 0 else tt + 8
        z = None
        for off in range(r, first + A_KW, 8):
            if off < first:
                continue
            term = aext[off - r:off - r + rows, :] * cw_ref[off - first:off - first + 1, :]
            z = term if z is None else z + term
        if r == 0:
            acc = acc + z
        else:
            zbuf[r - 1] = z
            acc = acc + zbuf[r - 1, r:r + tt, :]
    mu = jnp.mean(acc, axis=-1, keepdims=True)
    xc = acc - mu
    var = jnp.mean(xc * xc, axis=-1, keepdims=True)
    ln = xc * lax.rsqrt(var + EPS) * lng_ref[...] + lnb_ref[...]
    a_ref[...] = (ln * jax.nn.sigmoid(ln)).astype(BF16)

    last_rows = aext[tt:tt + A_HIST, :]
    nh_ref[0] = last_rows
    aext[0:A_HIST, :] = last_rows

    cos = cos_ref[...]
    sin = sin_ref[...]
    ri = lax.broadcasted_iota(jnp.int32, (tt, tt), 0)
    ci = lax.broadcasted_iota(jnp.int32, (tt, tt), 1)
    dij = (ri - ci).astype(F32)
    causal = ri >= ci
    rowf = lax.broadcasted_iota(jnp.int32, (tt, 1), 0).astype(F32)
    for h in range(B_HEADS):
        lg = log_gammas[h]
        q = cols(B_OFF + h * B_DK, B_OFF + (h + 1) * B_DK)
        k = cols(B_OFF + B_HEADS * B_DK + h * B_DK, B_OFF + B_HEADS * B_DK + (h + 1) * B_DK)
        voff = B_OFF + 2 * B_HEADS * B_DK
        v = cols(voff + h * B_DV, voff + (h + 1) * B_DV)
        goff = voff + B_HEADS * B_DV
        g = cols(goff + h * B_DV, goff + (h + 1) * B_DV)
        qr = q * cos + pltpu.roll(q, B_DK // 2, 1) * sin
        kr = (k * cos + pltpu.roll(k, B_DK // 2, 1) * sin) * (B_DK ** -0.5)
        qb = qr.astype(BF16)
        vb = v.astype(BF16)
        decay = jnp.where(causal, jnp.exp(lg * jnp.maximum(dij, 0.0)), 0.0)
        scores = _dot_nt(qb, kr.astype(BF16)) * decay
        inner = _dot(scores.astype(BF16), vb)
        s_old = sret[h]
        cross = _dot(qb, s_old.astype(BF16)) * jnp.exp(lg * (rowf + 1.0))
        o = inner + cross
        kd = kr * jnp.exp(lg * (tt - 1.0 - rowf))
        s_new = s_old * math.exp(lg * tt) + _dot_tn(kd.astype(BF16), vb)
        sret[h] = s_new
        nr_ref[0, h] = s_new
        y = _rms(o, gng_ref[:, h * B_DV:(h + 1) * B_DV])
        bo_ref[:, h * B_DV:(h + 1) * B_DV] = (y * (g * jax.nn.sigmoid(g))).astype(BF16)

    lane = lax.broadcasted_iota(jnp.int32, (tt, 2 * C_DQK), 1)
    lo = lane < C_DQK

    def qk_norm(x, g2):
        sq = x * x
        s_lo = jnp.sum(jnp.where(lo, sq, 0.0), axis=-1, keepdims=True)
        s_hi = jnp.sum(jnp.where(lo, 0.0, sq), axis=-1, keepdims=True)
        ms = jnp.where(lo, s_lo, s_hi) * (1.0 / C_DQK)
        return x * lax.rsqrt(ms + EPS) * g2

    if n_prev:
        for e in range(n_prev):
            ck_ref[e] = prev_refs[2 * e][...]
            cv_ref[e] = prev_refs[2 * e + 1][...]
        ck_own, cv_own = ck_ref.at[n_prev], cv_ref.at[n_prev]
    else:
        ck_own, cv_own = ck_ref, cv_ref
    w = 2 * C_DQK
    for h in range(C_HEADS):
        cq = cols(C_OFF + h * w, C_OFF + (h + 1) * w)
        ck = cols(C_OFF + C_HEADS * w + h * w, C_OFF + C_HEADS * w + (h + 1) * w)
        qn = qk_norm(cq, qg_ref[...])
        kn = qk_norm(ck, kg_ref[...])
        qn_ref[:, h * w:(h + 1) * w] = (qn * (C_DQK ** -0.5)).astype(BF16)
        kn_ref[:, h * w:(h + 1) * w] = kn.astype(BF16)
        ck_own[pl.ds(h, tt, stride=C_HEADS), :] = kn
    cv = cols(C_OFF + 2 * C_HEADS * w, IN_COLS)
    ones = jnp.ones((tt, C_DV), BF16)
    for h in range(C_HEADS):
        cv_own[pl.ds(h, tt, stride=C_HEADS), :] = cv[:, h * C_DV:(h + 1) * C_DV]
        vb_ref[:, 2 * h * C_DV:(2 * h + 1) * C_DV] = cv[:, h * C_DV:(h + 1) * C_DV].astype(BF16)
        vb_ref[:, (2 * h + 1) * C_DV:(2 * h + 2) * C_DV] = ones


def _mixer(src, hist, state, cos2, sin2, cw, cb, lng, lnb, gng, qg2, kg2, *, bsz, t, tt, earlier=()):
    n = bsz * t
    nt = t // tt
    log_gammas = tuple(math.log(1.0 - 2.0 ** (-5.0 - h)) for h in range(B_HEADS))
    tok = lambda cols: pl.BlockSpec((tt, cols), lambda b, i: (b * nt + i, 0))
    const = lambda r, c: pl.BlockSpec((r, c), lambda b, i: (0, 0))
    cw_dim = C_HEADS * 2 * C_DQK
    fused = isinstance(src, tuple)
    if fused:
        x, g1, w_in, layer = src
        src_args = (x, g1, w_in)
        src_specs = [tok(D_MODEL), const(1, D_MODEL),
                     pl.BlockSpec((None, D_MODEL, IN_COLS), lambda b, i: (layer, 0, 0), pipeline_mode=pl.Buffered(1))]
    else:
        src_args = (src,)
        src_specs = [tok(IN_COLS)]
    n_prev = len(earlier)
    leaf = lambda width: pl.BlockSpec((None, tt * C_HEADS, width), lambda b, i: (b, i, 0))
    if n_prev:
        kv_spec = lambda width: pl.BlockSpec((n_prev + 1, None, tt * C_HEADS, width), lambda b, i: (0, b, i, 0))
        kv_shape = lambda width: jax.ShapeDtypeStruct((n_prev + 1, bsz, t * C_HEADS, width), F32)
    else:
        kv_spec = leaf
        kv_shape = lambda width: jax.ShapeDtypeStruct((bsz, t * C_HEADS, width), F32)
    return pl.pallas_call(
        functools.partial(_mixer_body, tt=tt, log_gammas=log_gammas, fused=fused, n_prev=n_prev),
        grid=(bsz, nt),
        in_specs=src_specs + [
            pl.BlockSpec((1, A_HIST, A_WIDTH), lambda b, i: (b, 0, 0)),
            pl.BlockSpec((1, B_HEADS, B_DK, B_DV), lambda b, i: (b, 0, 0, 0)),
            pl.BlockSpec((tt, B_DK), lambda b, i: (i, 0)),
            pl.BlockSpec((tt, B_DK), lambda b, i: (i, 0)),
            const(A_KW, A_WIDTH), const(1, A_WIDTH), const(1, A_WIDTH), const(1, A_WIDTH),
            const(1, B_HEADS * B_DV), const(1, 2 * C_DQK), const(1, 2 * C_DQK),
        ] + [leaf(2 * C_DQK), leaf(C_DV)] * n_prev,
        out_specs=[
            tok(A_WIDTH), tok(B_HEADS * B_DV), tok(cw_dim), tok(cw_dim), tok(2 * C_HEADS * C_DV),
            kv_spec(2 * C_DQK), kv_spec(C_DV),
            pl.BlockSpec((1, A_HIST, A_WIDTH), lambda b, i: (b, 0, 0)),
            pl.BlockSpec((1, B_HEADS, B_DK, B_DV), lambda b, i: (b, 0, 0, 0)),
        ],
        out_shape=[
            jax.ShapeDtypeStruct((n, A_WIDTH), BF16),
            jax.ShapeDtypeStruct((n, B_HEADS * B_DV), BF16),
            jax.ShapeDtypeStruct((n, cw_dim), BF16),
            jax.ShapeDtypeStruct((n, cw_dim), BF16),
            jax.ShapeDtypeStruct((n, 2 * C_HEADS * C_DV), BF16),
            kv_shape(2 * C_DQK), kv_shape(C_DV),
            jax.ShapeDtypeStruct((bsz, A_HIST, A_WIDTH), F32),
            jax.ShapeDtypeStruct((bsz, B_HEADS, B_DK, B_DV), F32),
        ],
        scratch_shapes=[pltpu.VMEM((A_HIST + tt, A_WIDTH), F32), pltpu.VMEM((7, tt + 8, A_WIDTH), F32),
                        pltpu.VMEM((B_HEADS, B_DK, B_DV), F32)],
        compiler_params=_params("parallel", "arbitrary"),
        name="mixer",
    )(*src_args, hist, state, cos2, sin2, cw, cb, lng, lnb, gng, qg2, kg2, *[a for pair in earlier for a in pair])


def _t5_bucket(rel):
    half = REL_BUCKETS // 2
    exact = half // 2
    n = np.abs(rel)
    large = exact + (np.log(np.maximum(n, 1).astype(np.float32) / exact) / math.log(REL_MAX_DIST / exact)
                     * (half - exact)).astype(np.int32)
    large = np.minimum(large, half - 1)
    return (np.where(rel > 0, half, 0) + np.where(n < exact, n, large)).astype(np.int32)


def _bias_body(rb_ref, idx_ref, vis_ref, o_ref, *, row_tile, buckets):
    h = pl.program_id(0)
    far = rb_ref[REL_BUCKETS // 2 - 1, h]
    for k, present in enumerate(buckets):
        rows = slice(k * row_tile, (k + 1) * row_tile)
        idx = idx_ref[rows, :]
        acc = jnp.zeros(idx.shape, F32)
        for b in present:
            acc = jnp.where(idx == b, rb_ref[b, h], acc)
        o_ref[0, rows, :] = jnp.where(vis_ref[rows, :] != 0, acc - far, NEG_INF)


def _bias_tiles(rel_bias, idx, vis, row_tile):
    r, c = idx.shape
    buckets = tuple(tuple(int(b) for b in np.unique(idx[k:k + row_tile])) for k in range(0, r, row_tile))
    idx, vis = jnp.asarray(idx), jnp.asarray(vis)
    return pl.pallas_call(
        functools.partial(_bias_body, row_tile=row_tile, buckets=buckets),
        grid=(C_HEADS,),
        in_specs=[
            pl.BlockSpec(memory_space=pltpu.SMEM),
            pl.BlockSpec((r, c), lambda h: (0, 0)),
            pl.BlockSpec((r, c), lambda h: (0, 0)),
        ],
        out_specs=pl.BlockSpec((1, r, c), lambda h: (h, 0, 0)),
        out_shape=jax.ShapeDtypeStruct((C_HEADS, r, c), F32),
        compiler_params=_params("arbitrary"),
        name="bias_tiles",
    )(rel_bias, idx, vis)


def _lambda(lp_ref, lam_init):
    lp = lp_ref[...]
    e1 = jnp.exp(jnp.sum(lp[0:1] * lp[1:2], axis=-1, keepdims=True))
    e2 = jnp.exp(jnp.sum(lp[2:3] * lp[3:4], axis=-1, keepdims=True))
    return e1 - e2 + lam_init


def _stack_maps(q):
    lane = lax.broadcasted_iota(jnp.int32, q.shape, 1)
    zero = jnp.zeros_like(q)
    return jnp.concatenate([jnp.where(lane < C_DQK, q, zero), jnp.where(lane < C_DQK, zero, q)], axis=0)


def _attn_finish(acc, l, lam, g, lam_init, tq):
    o = acc / l
    o = o[0:tq] - lam * o[tq:2 * tq]
    return _rms(o, g) * (1.0 - lam_init)


def _attn_prompt_body(*refs, tq, lam_init, n_cast):
    _cast_slabs(refs[6:6 + n_cast], refs[7 + n_cast:7 + 2 * n_cast])
    q_ref, k_ref, v_ref, bias_ref, lp_ref, g_ref = refs[:6]
    o_ref = refs[6 + n_cast]
    qq_scr, m_scr, alpha_scr, p_scr, acc_scr = refs[7 + 2 * n_cast:]
    qi = pl.program_id(2)
    qq_scr[...] = _stack_maps(q_ref[...])
    m_scr[...] = jnp.full(m_scr.shape, NEG_INF, F32)
    acc_scr[...] = jnp.zeros(acc_scr.shape, F32)

    def tile(ref, idx):
        start = idx * tq if isinstance(idx, int) else pl.multiple_of(idx * tq, tq)
        return ref[pl.ds(start, tq), :]

    nl = tq // ATT_LANES
    chunks = [slice(c * ATT_ROWS, (c + 1) * ATT_ROWS) for c in range(2 * tq // ATT_ROWS)]

    def accumulate(rows, vb):
        alpha = alpha_scr[rows, :]
        acc_scr[rows, :] = jnp.concatenate([alpha, alpha], axis=-1) * acc_scr[rows, :] + _dot(p_scr[rows, :], vb)

    def step(ki, bias_idx, first=False):
        kb = tile(k_ref, ki)
        if not first:
            v_prev = tile(v_ref, ki - 1)
        for rows in chunks:
            if not first:
                accumulate(rows, v_prev)
            s = _dot_nt(qq_scr[rows, :], kb)
            if bias_idx is not None:
                s = s + bias_ref[0, bias_idx, pl.ds(rows.start % tq, ATT_ROWS), :]
            slabs = [s[:, j * ATT_LANES:(j + 1) * ATT_LANES] for j in range(nl)]
            m_old = m_scr[rows, :]
            m_new = jnp.maximum(m_old, jnp.max(functools.reduce(jnp.maximum, slabs), axis=-1, keepdims=True))
            alpha_scr[rows, :] = jnp.exp(m_old - m_new)
            p_scr[rows, :] = jnp.concatenate([jnp.exp(sl - m_new) for sl in slabs], axis=-1).astype(BF16)
            m_scr[rows, :] = m_new

    n_far = jnp.maximum(qi - 1, 0)
    step(0, jnp.minimum(qi, 2), first=True)

    def far_body(ki, c):
        step(ki, None)
        return c

    lax.fori_loop(1, n_far, far_body, 0)

    def near_body(ki, c):
        step(ki, qi - ki)
        return c

    lax.fori_loop(jnp.maximum(n_far, 1), qi + 1, near_body, 0)

    v_last = tile(v_ref, qi)
    for rows in chunks:
        accumulate(rows, v_last)
    lam = _lambda(lp_ref, lam_init)
    acc = acc_scr[...]
    o_ref[...] = _attn_finish(acc[:, 0:C_DV], acc[:, C_DV:2 * C_DV], lam, g_ref[...], lam_init, tq).astype(BF16)


def _attn_prompt(qn, kn, vb, bias, lp, g, *, bsz, t, lam_init, cast_ride=()):
    tq = ATT_TILE
    nq = t // tq
    n = bsz * t
    w = 2 * C_DQK
    cast_args, cast_in, cast_out, cast_shapes = (), [], [], []
    if cast_ride:
        cast_args, cast_in, cast_out, cast_shapes = _weight_cast_ride(
            cast_ride, bsz * C_HEADS * nq, lambda b, h, i: (b * C_HEADS + h) * nq + i)
    results = pl.pallas_call(
        functools.partial(_attn_prompt_body, tq=tq, lam_init=lam_init, n_cast=len(cast_args)),
        grid=(bsz, C_HEADS, nq),
        in_specs=[
            pl.BlockSpec((tq, w), lambda b, h, i: (b * nq + i, h)),
            pl.BlockSpec((t, w), lambda b, h, i: (b, h)),
            pl.BlockSpec((t, 2 * C_DV), lambda b, h, i: (b, h)),
            pl.BlockSpec((1, 3, tq, tq), lambda b, h, i: (h, 0, 0, 0)),
            pl.BlockSpec((4, C_DQK), lambda b, h, i: (0, 0)),
            pl.BlockSpec((1, C_DV), lambda b, h, i: (0, 0)),
        ] + cast_in,
        out_specs=[pl.BlockSpec((tq, C_DV), lambda b, h, i: (b * nq + i, h))] + cast_out,
        out_shape=[jax.ShapeDtypeStruct((n, C_HEADS * C_DV), BF16)] + cast_shapes,
        scratch_shapes=[pltpu.VMEM((2 * tq, w), BF16), pltpu.VMEM((2 * tq, ATT_LANES), F32),
                        pltpu.VMEM((2 * tq, ATT_LANES), F32), pltpu.VMEM((2 * tq, tq), BF16),
                        pltpu.VMEM((2 * tq, 2 * C_DV), F32)],
        compiler_params=_params("arbitrary", "arbitrary", "arbitrary"),
        name="attn_prompt",
    )(qn, kn, vb, bias, lp, g, *cast_args)
    return results[0], tuple(results[1:])


def _attn_sample_body(q_ref, k_ref, v_ref, pk_ref, pv_ref, bp_ref, bn_ref, lp_ref, g_ref, o_ref, *, tq, lam_init):
    lam = _lambda(lp_ref, lam_init)
    w = 2 * C_DQK
    for h in range(C_HEADS):
        qq = _stack_maps(q_ref[:, h * w:(h + 1) * w])
        bp = bp_ref[h]
        bn = bn_ref[h]
        past = pk_ref.shape[0] // C_HEADS
        head_rows = pl.ds(h, past, stride=C_HEADS)
        s_p = _dot_nt(qq, pk_ref[head_rows, :].astype(BF16)) + jnp.concatenate([bp, bp], axis=0)
        s_n = _dot_nt(qq, k_ref[:, h * w:(h + 1) * w]) + jnp.concatenate([bn, bn], axis=0)
        m = jnp.maximum(jnp.max(s_p, axis=-1, keepdims=True), jnp.max(s_n, axis=-1, keepdims=True))
        p_p = jnp.exp(s_p - m)
        p_n = jnp.exp(s_n - m)
        l = jnp.sum(p_p, axis=-1, keepdims=True) + jnp.sum(p_n, axis=-1, keepdims=True)
        acc = (_dot(p_p.astype(BF16), pv_ref[head_rows, :].astype(BF16))
               + _dot(p_n.astype(BF16), v_ref[:, 2 * h * C_DV:(2 * h + 1) * C_DV]))
        o_ref[:, h * C_DV:(h + 1) * C_DV] = _attn_finish(acc, l, lam, g_ref[...], lam_init, tq).astype(BF16)


def _attn_sample(qn, kn, vb, past_k, past_v, bias_p, bias_n, lp, g, layer, *, bsz, t, lam_init):
    n = bsz * t
    w = 2 * C_DQK
    depth, _, past = past_k.shape[:3]
    past_k = past_k.reshape(depth, bsz, past * C_HEADS, w)
    past_v = past_v.reshape(depth, bsz, past * C_HEADS, C_DV)
    full = lambda a: pl.BlockSpec(a.shape, lambda b: (0,) * a.ndim)
    return pl.pallas_call(
        functools.partial(_attn_sample_body, tq=t, lam_init=lam_init),
        grid=(bsz,),
        in_specs=[
            pl.BlockSpec((t, C_HEADS * w), lambda b: (b, 0)),
            pl.BlockSpec((t, C_HEADS * w), lambda b: (b, 0)),
            pl.BlockSpec((t, 2 * C_HEADS * C_DV), lambda b: (b, 0)),
            pl.BlockSpec((None, None, past * C_HEADS, w), lambda b: (layer, b, 0, 0)),
            pl.BlockSpec((None, None, past * C_HEADS, C_DV), lambda b: (layer, b, 0, 0)),
            full(bias_p), full(bias_n),
            pl.BlockSpec((4, C_DQK), lambda b: (0, 0)),
            pl.BlockSpec((1, C_DV), lambda b: (0, 0)),
        ],
        out_specs=pl.BlockSpec((t, C_HEADS * C_DV), lambda b: (b, 0)),
        out_shape=jax.ShapeDtypeStruct((n, C_HEADS * C_DV), BF16),
        compiler_params=_params("parallel"),
        name="attn_sample",
    )(qn, kn, vb, past_k, past_v, bias_p, bias_n, lp, g)


def _memkv_body(mem_ref, g_ref, wk_ref, wv_ref, kg_ref, k_ref, v_ref, kb_ref, vb_ref):
    hm = _rms(mem_ref[0], g_ref[...]).astype(BF16)
    k = _dot(hm, wk_ref[...])
    v = _dot(hm, wv_ref[...])
    for h in range(M_HEADS):
        kn = _rms(k[:, h * M_DH:(h + 1) * M_DH], kg_ref[...])
        k_ref[0, :, h * M_DH:(h + 1) * M_DH] = kn
        kb_ref[0, :, h * M_DH:(h + 1) * M_DH] = kn.astype(BF16)
    v_ref[0] = v
    vb_ref[0] = v.astype(BF16)


def _memkv(mem, g, wk, wv, kg):
    bsz, m, _ = mem.shape
    w = M_HEADS * M_DH
    blk = pl.BlockSpec((1, m, w), lambda b: (b, 0, 0))
    return pl.pallas_call(
        _memkv_body,
        grid=(bsz,),
        in_specs=[
            pl.BlockSpec((1, m, D_MODEL), lambda b: (b, 0, 0)),
            pl.BlockSpec((1, D_MODEL), lambda b: (0, 0)),
            pl.BlockSpec((D_MODEL, w), lambda b: (0, 0)),
            pl.BlockSpec((D_MODEL, w), lambda b: (0, 0)),
            pl.BlockSpec((1, M_DH), lambda b: (0, 0)),
        ],
        out_specs=[blk, blk, blk, blk],
        out_shape=[jax.ShapeDtypeStruct((bsz, m, w), F32), jax.ShapeDtypeStruct((bsz, m, w), F32),
                   jax.ShapeDtypeStruct((bsz, m, w), BF16), jax.ShapeDtypeStruct((bsz, m, w), BF16)],
        compiler_params=_params("parallel"),
        name="memkv",
    )(mem, g, wk, wv, kg)


def _post_body(x_ref, a_ref, bo_ref, co_ref, wo_ref, g_ref, wq_ref, qg_ref, mk_ref, mv_ref, wxo_ref,
               o_ref, att_scr, *, nseq, rps):
    rc = x_ref.shape[0] // POST_CHAINS
    seg = min(rc, rps)
    for c in range(POST_CHAINS):
        rows = slice(c * rc, (c + 1) * rc)
        y = (_dot(a_ref[rows, :], wo_ref[0:A_WIDTH, :])
             + _dot(bo_ref[rows, :], wo_ref[A_WIDTH:A_WIDTH + B_HEADS * B_DV, :])
             + _dot(co_ref[rows, :], wo_ref[A_WIDTH + B_HEADS * B_DV:, :]))
        x1 = x_ref[rows, :] + y
        q = _dot(_rms(x1, g_ref[...]).astype(BF16), wq_ref[...])
        for h in range(M_HEADS):
            sl = slice(h * M_DH, (h + 1) * M_DH)
            qn = _rms(q[:, sl], qg_ref[...]).astype(BF16)
            for u in range(rc // seg):
                r0 = c * rc + u * seg
                s = r0 // rps
                logits = _dot_nt(qn[u * seg:(u + 1) * seg], mk_ref[s, :, sl]) * (M_DH ** -0.5)
                m = jnp.max(logits, axis=-1, keepdims=True)
                p = jnp.exp(logits - m)
                l = jnp.sum(p, axis=-1, keepdims=True)
                o = _dot(p.astype(BF16), mv_ref[s, :, sl]) / l
                att_scr[r0:r0 + seg, sl] = o.astype(BF16)
        o_ref[rows, :] = x1 + _dot(att_scr[rows, :], wxo_ref[...])


def _post(x, a, bo, co, wo, g, wq, qg, mk, mv, wxo, layer, *, t, tm):
    n = x.shape[0]
    w = M_HEADS * M_DH
    assert n % tm == 0 and (t % tm == 0 or tm % t == 0), (n, t, tm)
    if t >= tm:
        nseq, rps = 1, tm
        per = t // tm
        mem_map = lambda i: (i // per, 0, 0)
    else:
        nseq, rps = tm // t, t
        mem_map = lambda i: (i, 0, 0)
    tok = lambda cols: pl.BlockSpec((tm, cols), lambda i: (i, 0))
    const = lambda r, c: pl.BlockSpec((r, c), lambda i: (0, 0), pipeline_mode=pl.Buffered(1))
    stacked = lambda r, c: pl.BlockSpec((None, r, c), lambda i: (layer, 0, 0), pipeline_mode=pl.Buffered(1))
    return pl.pallas_call(
        functools.partial(_post_body, nseq=nseq, rps=rps),
        grid=(n // tm,),
        in_specs=[
            tok(D_MODEL), tok(A_WIDTH), tok(B_HEADS * B_DV), tok(C_HEADS * C_DV),
            stacked(D_MODEL, D_MODEL), const(1, D_MODEL), stacked(D_MODEL, w), const(1, M_DH),
            pl.BlockSpec((nseq, MEM_LEN, w), mem_map),
            pl.BlockSpec((nseq, MEM_LEN, w), mem_map),
            stacked(w, D_MODEL),
        ],
        out_specs=tok(D_MODEL),
        out_shape=jax.ShapeDtypeStruct((n, D_MODEL), F32),
        scratch_shapes=[pltpu.VMEM((tm, w), BF16)],
        compiler_params=_params("parallel"),
        name="post",
    )(x, a, bo, co, wo, g, wq, qg, mk, mv, wxo)


def _ffn_body(*refs, nseq, rps, per, n_cast):
    _cast_slabs(refs[11:11 + n_cast], refs[14 + n_cast:14 + 2 * n_cast])
    x_ref, g_ref, wv_ref, wg_ref, cwv_ref, cwg_ref, cbv_ref, cbg_ref, hv_ref, hg_ref, wd_ref = refs[:11]
    o_ref, nv_ref, ng_ref = refs[11 + n_cast:14 + n_cast]
    h_scr, ubuf, tail = refs[14 + 2 * n_cast:]
    i = pl.program_id(0)
    j = pl.program_id(1)

    @pl.when(j == 0)
    def _():
        x = x_ref[...]
        h_scr[...] = _rms(x, g_ref[...]).astype(BF16)
        o_ref[...] = x

    if per > 1:
        @pl.when((i == 0) & (j == 0))
        def _():
            tail[...] = jnp.zeros(tail.shape, F32)

    seq_start = (i % per) == 0
    stride = rps + F_HIST
    tm = x_ref.shape[0]
    rc = tm // FFN_CHAINS
    assert rc % rps == 0 or rps % rc == 0

    seg = min(rc, rps)
    chains = [slice(r * rc, (r + 1) * rc) for r in range(FFN_CHAINS)]

    def up(rows):
        return _dot(h_scr[rows, :], wv_ref[...]), _dot(h_scr[rows, :], wg_ref[...])

    def keep(us):
        for rows, pair in zip(chains, us):
            for half, u in enumerate(pair):
                for q in range(rc // seg):
                    s, o = divmod(rows.start + q * seg, rps)
                    dst = s * stride + F_HIST + o
                    ubuf[half, dst:dst + seg, :] = u[q * seg:(q + 1) * seg]

    def conv(half, r0, cw_ref, cb_ref, hist_ref, new_ref):
        outs = []
        for q in range(rc // seg):
            s, o = divmod(r0 + q * seg, rps)
            base = s * stride
            if o == 0:
                if per == 1:
                    prev = hist_ref[s]
                else:
                    prev = jnp.where(seq_start, hist_ref[s], tail[half, j])
                ubuf[half, base:base + F_HIST, :] = prev
            c = cb_ref[...]
            for k in range(F_KW):
                off = base + F_HIST + o - (F_KW - 1) + k
                c = c + ubuf[half, off:off + seg, :] * cw_ref[k:k + 1, :]
            outs.append(c)
            if o + seg == rps:
                last = ubuf[half, base + rps:base + stride, :]
                new_ref[s] = last
                if per > 1:
                    tail[half, j] = last
        return outs[0] if len(outs) == 1 else jnp.concatenate(outs, axis=0)

    def down(rows):
        val = conv(0, rows.start, cwv_ref, cbv_ref, hv_ref, nv_ref)
        gate = conv(1, rows.start, cwg_ref, cbg_ref, hg_ref, ng_ref)
        act = (gate * jax.nn.sigmoid(gate) * val).astype(BF16)
        o_ref[rows, :] += _dot(act, wd_ref[...])

    keep([up(rows) for rows in chains])
    for rows in chains:
        down(rows)


def _slab(total, unit, steps):
    return next(unit * k for k in range(1, total // unit + 1) if total % (unit * k) == 0 and total // (unit * k) <= steps)


def _weight_cast_ride(entries, steps, step_of):
    args, in_specs, out_specs, out_shapes = [], [], [], []
    for w, layer, axis in entries:
        _, k, ncol = w.shape
        size = _slab((k, ncol)[axis], (BF16_SUBLANES, LANES)[axis], steps)
        last = (k, ncol)[axis] // size - 1
        block = (None, size, ncol) if axis == 0 else (None, k, size)

        def index(*g, lead, axis=axis, last=last):
            s = jnp.minimum(step_of(*g), last)
            return (lead, s, 0) if axis == 0 else (lead, 0, s)

        args.append(w)
        in_specs.append(pl.BlockSpec(block, functools.partial(index, lead=layer)))
        out_specs.append(pl.BlockSpec(block, functools.partial(index, lead=0)))
        out_shapes.append(jax.ShapeDtypeStruct((1, k, ncol), BF16))
    return tuple(args), in_specs, out_specs, out_shapes


def _cast_slabs(f32_refs, bf16_refs):
    for src, dst in zip(f32_refs, bf16_refs):
        dst[...] = src[...].astype(BF16)


def _cast_weights(entries, steps=CAST_STEPS):
    args, in_specs, out_specs, out_shapes = _weight_cast_ride(entries, steps, lambda s: s)
    return pl.pallas_call(
        lambda *refs: _cast_slabs(refs[:len(args)], refs[len(args):]),
        grid=(steps,),
        in_specs=in_specs,
        out_specs=out_specs,
        out_shape=out_shapes,
        compiler_params=_params("arbitrary"),
        name="cast_weights",
    )(*args)


def _ffn(x, g, w_up, cw, cb, hist, w_down, layer, *, t, cast_next=None, tm=TOKEN_TILE, tn=COL_TILE):
    n = x.shape[0]
    nj = D_FF // tn
    assert n % tm == 0 and (t % tm == 0 or tm % t == 0), (n, t, tm)
    if t >= tm:
        nseq, rps, per = 1, tm, t // tm
        seq_map = lambda i: i // per
    else:
        nseq, rps, per = tm // t, t, 1
        seq_map = lambda i: i
    bsz = hist.shape[0]
    hist_spec = lambda off: pl.BlockSpec((nseq, F_HIST, tn), lambda i, j: (seq_map(i), 0, j + off))
    new_spec = pl.BlockSpec((nseq, F_HIST, tn), lambda i, j: (i, 0, j))
    cast_args, cast_in, cast_out, cast_shapes = (), [], [], []
    if cast_next:
        cast_args, cast_in, cast_out, cast_shapes = _weight_cast_ride(cast_next, (n // tm) * nj,
                                                                      lambda i, j: i * nj + j)
    results = pl.pallas_call(
        functools.partial(_ffn_body, nseq=nseq, rps=rps, per=per, n_cast=len(cast_args)),
        grid=(n // tm, nj),
        in_specs=[
            pl.BlockSpec((tm, D_MODEL), lambda i, j: (i, 0)),
            pl.BlockSpec((1, D_MODEL), lambda i, j: (0, 0)),
            pl.BlockSpec((None, D_MODEL, tn), lambda i, j: (layer, 0, j)),
            pl.BlockSpec((None, D_MODEL, tn), lambda i, j: (layer, 0, j + nj)),
            pl.BlockSpec((F_KW, tn), lambda i, j: (0, j)),
            pl.BlockSpec((F_KW, tn), lambda i, j: (0, j + nj)),
            pl.BlockSpec((1, tn), lambda i, j: (0, j)),
            pl.BlockSpec((1, tn), lambda i, j: (0, j + nj)),
            hist_spec(0), hist_spec(nj),
            pl.BlockSpec((None, tn, D_MODEL), lambda i, j: (layer, j, 0)),
        ] + cast_in,
        out_specs=[pl.BlockSpec((tm, D_MODEL), lambda i, j: (i, 0)), new_spec, new_spec] + cast_out,
        out_shape=[jax.ShapeDtypeStruct((n, D_MODEL), F32),
                   jax.ShapeDtypeStruct((bsz * per, F_HIST, D_FF), F32),
                   jax.ShapeDtypeStruct((bsz * per, F_HIST, D_FF), F32)] + cast_shapes,
        scratch_shapes=[
            pltpu.VMEM((tm, D_MODEL), BF16),
            pltpu.VMEM((2, nseq * (rps + F_HIST), tn), F32),
            pltpu.VMEM((2, nj, F_HIST, tn), F32),
        ],
        compiler_params=_params("arbitrary", "arbitrary"),
        name="ffn",
    )(x, g, w_up, w_up, cw, cw, cb, cb, hist, hist, w_down, *cast_args)
    x_out, tail_v, tail_g = results[:3]
    last = lambda a: a.reshape(bsz, per, F_HIST, D_FF)[:, per - 1]
    return x_out, last(tail_v), last(tail_g), tuple(results[3:])


def _rope_tables(pos0, t):
    half = B_DK // 2
    inv = 1.0 / (ROPE_BASE ** (jnp.arange(half, dtype=F32) / half))
    ang = (pos0 + jnp.arange(t, dtype=jnp.int32)).astype(F32)[:, None] * inv[None, :]
    cos = jnp.cos(ang)
    sin = jnp.sin(ang)
    return jnp.concatenate([cos, cos], axis=-1), jnp.concatenate([-sin, sin], axis=-1)


def _pad_rows(h, rows):
    return jnp.pad(h, ((0, 0), (rows - h.shape[1], 0), (0, 0)))


def _row(v):
    return v.reshape(1, -1).astype(F32)


def kernel(x_prompt, x_sample, mem_prompt, state_conv_a, state_ret, cache_diff_k, cache_diff_v, cache_mem_k,
           cache_mem_v, state_conv_f, norm1_g, w_in, conv_a_w, conv_a_b, ln_a_g, ln_a_b, ret_gn_g, diff_qn_g,
           diff_kn_g, diff_lq1, diff_lk1, diff_lq2, diff_lk2, diff_subln_g, w_out, rel_bias, norm2_g, mem_norm_g,
           w_xq, w_xk, w_xv, xqn_g, xkn_g, w_xo, norm3_g, w_up, conv_f_w, conv_f_b, w_down):
    bp, tp, _ = x_prompt.shape
    bs, ts, _ = x_sample.shape
    depth = w_in.shape[0]
    past = cache_diff_k.shape[2]
    cw_dim = C_HEADS * 2 * C_DQK

    tq = ATT_TILE
    r = np.arange(tq)
    rel_diag = r[None, :] - r[:, None]
    idx_p = np.stack([_t5_bucket(rel_diag - d * tq) for d in range(3)]).reshape(3 * tq, tq)
    vis_p = np.stack([(r[None, :] // CHUNK) <= (r[:, None] // CHUNK), np.ones((tq, tq), bool), np.ones((tq, tq), bool)])
    vis_p = vis_p.reshape(3 * tq, tq).astype(np.int32)
    bias_p = _bias_tiles(rel_bias, idx_p, vis_p, tq).reshape(C_HEADS, 3, tq, tq)
    rel_s = np.arange(past + ts)[None, :] - (past + np.arange(ts))[:, None]
    bias_s = _bias_tiles(rel_bias, _t5_bucket(rel_s), np.ones(rel_s.shape, np.int32), ts)
    bias_s_past, bias_s_new = bias_s[:, :, :past], bias_s[:, :, past:]

    cos_p, sin_p = _rope_tables(0, tp)
    cos_s, sin_s = _rope_tables(past, ts)

    xp = x_prompt.reshape(bp * tp, D_MODEL)
    xs = x_sample.reshape(bs * ts, D_MODEL)
    zero_a = jnp.zeros((bp, A_HIST, A_WIDTH), F32)
    zero_r = jnp.zeros((bp, B_HEADS, B_DK, B_DV), F32)
    zero_f = jnp.zeros((bp, F_HIST, 2 * D_FF), F32)

    outs = {k: [] for k in ("p_ca", "p_rs", "p_mk", "p_mv", "p_cf", "s_ca", "s_rs", "s_cf")}
    kv_p, kv_s = [], []
    small = {"in": w_in, "out": w_out, "xq": w_xq, "xo": w_xo}
    cur = dict(zip(small, _cast_weights([(w, 0, 1) for w in small.values()])))
    nxt = {}
    ffn_w = [None]
    for l in range(depth):
        lam_init = 0.8 - 0.6 * math.exp(-0.3 * l)
        has_next = l + 1 < depth
        w_in_b, w_out_b, w_xq_b, w_xo_b = cur["in"], cur["out"], cur["xq"], cur["xo"]
        lp = jnp.stack([diff_lq1[l], diff_lk1[l], diff_lq2[l], diff_lk2[l]]).astype(F32)
        qg2 = _row(jnp.concatenate([diff_qn_g[l], diff_qn_g[l]]))
        kg2 = _row(jnp.concatenate([diff_kn_g[l], diff_kn_g[l]]))
        mix_args = (conv_a_w[l], _row(conv_a_b[l]), _row(ln_a_g[l]), _row(ln_a_b[l]), _row(ret_gn_g[l]), qg2, kg2)
        subln = _row(diff_subln_g[l])

        def block(x, bsz, t, tt, hist_a, state_r, cos2, sin2, attend, mk_b, mv_b, hist_f, tm_post, kv_leaves,
                  cast_next=None):
            if tt >= FUSED_PROJ_MIN_ROWS:
                src = (x, _row(norm1_g[l]), w_in_b, 0)
            else:
                src = _in_proj(x, _row(norm1_g[l]), w_in_b, 0)
            a, bo, qn, kn, vb, ck, cv, nh, nr = _mixer(src, hist_a, state_r, cos2, sin2, *mix_args,
                                                        bsz=bsz, t=t, tt=tt, earlier=() if has_next else tuple(kv_leaves))
            kv_leaves.append((ck, cv))
            co = attend(qn, kn, vb)
            x = _post(x, a, bo, co, w_out_b, _row(norm2_g[l]), w_xq_b, _row(xqn_g[l]), mk_b, mv_b, w_xo_b,
                      0, t=t, tm=tm_post)
            w_up_b, w_down_b = ffn_w[0]
            x, nfv, nfg, next_w = _ffn(x, _row(norm3_g[l]), w_up_b, conv_f_w[l], _row(conv_f_b[l]), hist_f, w_down_b,
                                       0, t=t, cast_next=cast_next)
            new_f = jnp.concatenate([nfv[:, F_HIST - (F_KW - 1):], nfg[:, F_HIST - (F_KW - 1):]], axis=-1)
            return x, nh[:, A_HIST - (A_KW - 1):], nr, new_f, next_w

        mk, mv, mk_b, mv_b = _memkv(mem_prompt, _row(mem_norm_g[l]), w_xk[l].astype(BF16), w_xv[l].astype(BF16),
                                    _row(xkn_g[l]))
        def attend_p(qn, kn, vb):
            ride = [(w, l + 1, 1) for w in small.values()] if has_next else []
            if ffn_w[0] is None:
                ride += [(w_up, l, 1), (w_down, l, 0)]
            co, cast = _attn_prompt(qn, kn, vb, bias_p, lp, subln, bsz=bp, t=tp, lam_init=lam_init, cast_ride=ride)
            if has_next:
                nxt.update(zip(small, cast[:len(small)]))
            if ffn_w[0] is None:
                ffn_w[0] = cast[-2:]
            return co

        xp, ca, rs, cf, next_w = block(xp, bp, tp, MIX_TILE, zero_a, zero_r, cos_p, sin_p, attend_p, mk_b,
                                       mv_b, zero_f, POST_TILE, kv_p,
                                       cast_next=[(w_up, l + 1, 1), (w_down, l + 1, 0)] if has_next else None)
        outs["p_ca"].append(ca); outs["p_rs"].append(rs)
        outs["p_mk"].append(mk.reshape(bp, MEM_LEN, M_HEADS, M_DH))
        outs["p_mv"].append(mv.reshape(bp, MEM_LEN, M_HEADS, M_DH))
        outs["p_cf"].append(cf)

        attend_s = lambda qn, kn, vb: _attn_sample(qn, kn, vb, cache_diff_k, cache_diff_v, bias_s_past,
                                                   bias_s_new, lp, subln, l,
                                                   bsz=bs, t=ts, lam_init=lam_init)
        smk = cache_mem_k[l].reshape(bs, MEM_LEN, M_HEADS * M_DH).astype(BF16)
        smv = cache_mem_v[l].reshape(bs, MEM_LEN, M_HEADS * M_DH).astype(BF16)
        xs, sca, srs, scf, _ = block(xs, bs, ts, ts, _pad_rows(state_conv_a[l], A_HIST), state_ret[l],
                                     cos_s, sin_s, attend_s, smk, smv, _pad_rows(state_conv_f[l], F_HIST),
                                     POST_TILE, kv_s)
        outs["s_ca"].append(sca); outs["s_rs"].append(srs)
        outs["s_cf"].append(scf)
        if has_next:
            ffn_w[0] = next_w
            cur = dict(nxt)

    st = lambda k: jnp.stack(outs[k])

    def heads(leaf, bsz, t):
        leaf = leaf if leaf.ndim == 4 else leaf[None]
        return leaf.reshape(depth, bsz, t, C_HEADS, leaf.shape[-1])

    return (xp.reshape(bp, tp, D_MODEL), xs.reshape(bs, ts, D_MODEL),
            st("p_ca"), st("p_rs"), heads(kv_p[-1][0], bp, tp), heads(kv_p[-1][1], bp, tp), st("p_mk"), st("p_mv"),
            st("p_cf"),
            st("s_ca"), st("s_rs"), heads(kv_s[-1][0], bs, ts), heads(kv_s[-1][1], bs, ts), st("s_cf"))
```

```python
import functools
import math

import numpy as np
import jax
import jax.numpy as jnp
from jax import lax
from jax.experimental import pallas as pl
from jax.experimental.pallas import tpu as pltpu

F32 = jnp.float32
BF16 = jnp.bfloat16
EPS = 1e-6
NEG_INF = -1e30

D_MODEL = 2048
CHUNK = 64
A_WIDTH = 512
A_KW = 31
A_HIST = 32
B_HEADS = 4
B_DK = 128
B_DV = 256
ROPE_BASE = 10000.0
C_HEADS = 4
C_DQK = 64
C_DV = 128
REL_BUCKETS = 32
REL_MAX_DIST = 128
M_HEADS = 4
M_DH = 128
MEM_LEN = 256
D_FF = 5632
F_KW = 3
F_HIST = 8

A_COLS = 2 * A_WIDTH
B_COLS = B_HEADS * (2 * B_DK + 2 * B_DV)
C_COLS = C_HEADS * (4 * C_DQK + C_DV)
IN_COLS = A_COLS + B_COLS + C_COLS
B_OFF = A_COLS
C_OFF = A_COLS + B_COLS
PROJ_GROUPS = (0, A_WIDTH, A_COLS, B_OFF + 2 * B_HEADS * B_DK, B_OFF + 2 * B_HEADS * B_DK + B_HEADS * B_DV, C_OFF,
               C_OFF + 2 * C_HEADS * 2 * C_DQK, IN_COLS)

VMEM_LIMIT_BYTES = 62 * 1024 * 1024
LANES = 128
BF16_SUBLANES = 16
TOKEN_TILE = 1024
COL_TILE = 512
MIX_TILE = 256
ATT_TILE = 512
ATT_ROWS = 256
ATT_LANES = 128
FFN_CHAINS = 2
POST_TILE = 512
POST_TILE_PROMPT = 1024
POST_CHAINS = 2
CAST_STEPS = 11
FUSED_PROJ_MIN_ROWS = 128


def _params(*sem):
    return pltpu.CompilerParams(dimension_semantics=sem, vmem_limit_bytes=VMEM_LIMIT_BYTES)


def _rms(x, g):
    return x * lax.rsqrt(jnp.mean(x * x, axis=-1, keepdims=True) + EPS) * g


def _dot(a, b):
    return jnp.dot(a, b, preferred_element_type=F32)


def _dot_nt(a, b):
    return lax.dot_general(a, b, (((1,), (1,)), ((), ())), preferred_element_type=F32)


def _dot_tn(a, b):
    return lax.dot_general(a, b, (((0,), (0,)), ((), ())), preferred_element_type=F32)


def _in_proj_body(x_ref, g_ref, w_ref, o_ref, h_ref):
    @pl.when(pl.program_id(1) == 0)
    def _():
        h_ref[...] = _rms(x_ref[...], g_ref[...]).astype(BF16)

    o_ref[...] = _dot(h_ref[...], w_ref[...])


def _in_proj(x, g, w, layer, *, tm=TOKEN_TILE, tn=COL_TILE):
    n = x.shape[0]
    cols = w.shape[2]
    return pl.pallas_call(
        _in_proj_body,
        grid=(n // tm, cols // tn),
        in_specs=[
            pl.BlockSpec((tm, D_MODEL), lambda i, j: (i, 0)),
            pl.BlockSpec((1, D_MODEL), lambda i, j: (0, 0)),
            pl.BlockSpec((None, D_MODEL, tn), lambda i, j: (layer, 0, j)),
        ],
        out_specs=pl.BlockSpec((tm, tn), lambda i, j: (i, j)),
        out_shape=jax.ShapeDtypeStruct((n, cols), F32),
        scratch_shapes=[pltpu.VMEM((tm, D_MODEL), BF16)],
        compiler_params=_params("parallel", "arbitrary"),
        name="in_proj",
    )(x, g, w)


def _projected_columns(x_ref, g_ref, w_ref):
    h = _rms(x_ref[...], g_ref[...]).astype(BF16)
    cache = {}

    def cols(c0, c1):
        g0, g1 = next((a, b) for a, b in zip(PROJ_GROUPS[:-1], PROJ_GROUPS[1:]) if a <= c0 and c1 <= b)
        if g0 not in cache:
            cache[g0] = _dot(h, w_ref[:, g0:g1])
        return cache[g0][:, c0 - g0:c1 - g0]

    return cols


def _mixer_body(*refs, tt, log_gammas, fused, n_prev):
    if fused:
        cols = _projected_columns(*refs[:3])
        refs = refs[3:]
    else:
        proj_ref = refs[0]
        cols = lambda c0, c1: proj_ref[:, c0:c1]
        refs = refs[1:]
    prev_refs, refs = refs[11:11 + 2 * n_prev], refs[:11] + refs[11 + 2 * n_prev:]
    (hist_ref, state_ref, cos_ref, sin_ref, cw_ref, cb_ref, lng_ref, lnb_ref, gng_ref, qg_ref, kg_ref,
     a_ref, bo_ref, qn_ref, kn_ref, vb_ref, ck_ref, cv_ref, nh_ref, nr_ref, aext, zbuf, sret) = refs
    t = pl.program_id(1)

    @pl.when(t == 0)
    def _():
        aext[0:A_HIST, :] = hist_ref[0]
        sret[...] = state_ref[0]

    glu = cols(0, A_WIDTH) * jax.nn.sigmoid(cols(A_WIDTH, A_COLS))
    aext[A_HIST:A_HIST + tt, :] = glu
    first = A_HIST - (A_KW - 1)
    acc = jnp.zeros((tt, A_WIDTH), F32) + cb_ref[...]
    for r in range(8):
        rows = tt if r == 0 else tt + 8
        z = None
        for off in range(r, first + A_KW, 8):
            if off < first:
                continue
            term = aext[off - r:off - r + rows, :] * cw_ref[off - first:off - first + 1, :]
            z = term if z is None else z + term
        if r == 0:
            acc = acc + z
        else:
            zbuf[r - 1] = z
            acc = acc + zbuf[r - 1, r:r + tt, :]
    mu = jnp.mean(acc, axis=-1, keepdims=True)
    xc = acc - mu
    var = jnp.mean(xc * xc, axis=-1, keepdims=True)
    ln = xc * lax.rsqrt(var + EPS) * lng_ref[...] + lnb_ref[...]
    a_ref[...] = (ln * jax.nn.sigmoid(ln)).astype(BF16)

    last_rows = aext[tt:tt + A_HIST, :]
    nh_ref[0] = last_rows
    aext[0:A_HIST, :] = last_rows

    cos = cos_ref[...]
    sin = sin_ref[...]
    ri = lax.broadcasted_iota(jnp.int32, (tt, tt), 0)
    ci = lax.broadcasted_iota(jnp.int32, (tt, tt), 1)
    dij = (ri - ci).astype(F32)
    causal = ri >= ci
    rowf = lax.broadcasted_iota(jnp.int32, (tt, 1), 0).astype(F32)
    for h in range(B_HEADS):
        lg = log_gammas[h]
        q = cols(B_OFF + h * B_DK, B_OFF + (h + 1) * B_DK)
        k = cols(B_OFF + B_HEADS * B_DK + h * B_DK, B_OFF + B_HEADS * B_DK + (h + 1) * B_DK)
        voff = B_OFF + 2 * B_HEADS * B_DK
        v = cols(voff + h * B_DV, voff + (h + 1) * B_DV)
        goff = voff + B_HEADS * B_DV
        g = cols(goff + h * B_DV, goff + (h + 1) * B_DV)
        qr = q * cos + pltpu.roll(q, B_DK // 2, 1) * sin
        kr = (k * cos + pltpu.roll(k, B_DK // 2, 1) * sin) * (B_DK ** -0.5)
        qb = qr.astype(BF16)
        vb = v.astype(BF16)
        decay = jnp.where(causal, jnp.exp(lg * jnp.maximum(dij, 0.0)), 0.0)
        scores = _dot_nt(qb, kr.astype(BF16)) * decay
        inner = _dot(scores.astype(BF16), vb)
        s_old = sret[h]
        cross = _dot(qb, s_old.astype(BF16)) * jnp.exp(lg * (rowf + 1.0))
        o = inner + cross
        kd = kr * jnp.exp(lg * (tt - 1.0 - rowf))
        s_new = s_old * math.exp(lg * tt) + _dot_tn(kd.astype(BF16), vb)
        sret[h] = s_new
        nr_ref[0, h] = s_new
        y = _rms(o, gng_ref[:, h * B_DV:(h + 1) * B_DV])
        bo_ref[:, h * B_DV:(h + 1) * B_DV] = (y * (g * jax.nn.sigmoid(g))).astype(BF16)

    lane = lax.broadcasted_iota(jnp.int32, (tt, 2 * C_DQK), 1)
    lo = lane < C_DQK

    def qk_norm(x, g2):
        sq = x * x
        s_lo = jnp.sum(jnp.where(lo, sq, 0.0), axis=-1, keepdims=True)
        s_hi = jnp.sum(jnp.where(lo, 0.0, sq), axis=-1, keepdims=True)
        ms = jnp.where(lo, s_lo, s_hi) * (1.0 / C_DQK)
        return x * lax.rsqrt(ms + EPS) * g2

    if n_prev:
        for e in range(n_prev):
            ck_ref[e] = prev_refs[2 * e][...]
            cv_ref[e] = prev_refs[2 * e + 1][...]
        ck_own, cv_own = ck_ref.at[n_prev], cv_ref.at[n_prev]
    else:
        ck_own, cv_own = ck_ref, cv_ref
    w = 2 * C_DQK
    for h in range(C_HEADS):
        cq = cols(C_OFF + h * w, C_OFF + (h + 1) * w)
        ck = cols(C_OFF + C_HEADS * w + h * w, C_OFF + C_HEADS * w + (h + 1) * w)
        qn = qk_norm(cq, qg_ref[...])
        kn = qk_norm(ck, kg_ref[...])
        qn_ref[:, h * w:(h + 1) * w] = (qn * (C_DQK ** -0.5)).astype(BF16)
        kn_ref[:, h * w:(h + 1) * w] = kn.astype(BF16)
        ck_own[pl.ds(h, tt, stride=C_HEADS), :] = kn
    cv = cols(C_OFF + 2 * C_HEADS * w, IN_COLS)
    ones = jnp.ones((tt, C_DV), BF16)
    for h in range(C_HEADS):
        cv_own[pl.ds(h, tt, stride=C_HEADS), :] = cv[:, h * C_DV:(h + 1) * C_DV]
        vb_ref[:, 2 * h * C_DV:(2 * h + 1) * C_DV] = cv[:, h * C_DV:(h + 1) * C_DV].astype(BF16)
        vb_ref[:, (2 * h + 1) * C_DV:(2 * h + 2) * C_DV] = ones


def _mixer(src, hist, state, cos2, sin2, cw, cb, lng, lnb, gng, qg2, kg2, *, bsz, t, tt, earlier=()):
    n = bsz * t
    nt = t // tt
    log_gammas = tuple(math.log(1.0 - 2.0 ** (-5.0 - h)) for h in range(B_HEADS))
    tok = lambda cols: pl.BlockSpec((tt, cols), lambda b, i: (b * nt + i, 0))
    const = lambda r, c: pl.BlockSpec((r, c), lambda b, i: (0, 0))
    cw_dim = C_HEADS * 2 * C_DQK
    fused = isinstance(src, tuple)
    if fused:
        x, g1, w_in, layer = src
        src_args = (x, g1, w_in)
        src_specs = [tok(D_MODEL), const(1, D_MODEL),
                     pl.BlockSpec((None, D_MODEL, IN_COLS), lambda b, i: (layer, 0, 0), pipeline_mode=pl.Buffered(1))]
    else:
        src_args = (src,)
        src_specs = [tok(IN_COLS)]
    n_prev = len(earlier)
    leaf = lambda width: pl.BlockSpec((None, tt * C_HEADS, width), lambda b, i: (b, i, 0))
    if n_prev:
        kv_spec = lambda width: pl.BlockSpec((n_prev + 1, None, tt * C_HEADS, width), lambda b, i: (0, b, i, 0))
        kv_shape = lambda width: jax.ShapeDtypeStruct((n_prev + 1, bsz, t * C_HEADS, width), F32)
    else:
        kv_spec = leaf
        kv_shape = lambda width: jax.ShapeDtypeStruct((bsz, t * C_HEADS, width), F32)
    return pl.pallas_call(
        functools.partial(_mixer_body, tt=tt, log_gammas=log_gammas, fused=fused, n_prev=n_prev),
        grid=(bsz, nt),
        in_specs=src_specs + [
            pl.BlockSpec((1, A_HIST, A_WIDTH), lambda b, i: (b, 0, 0)),
            pl.BlockSpec((1, B_HEADS, B_DK, B_DV), lambda b, i: (b, 0, 0, 0)),
            pl.BlockSpec((tt, B_DK), lambda b, i: (i, 0)),
            pl.BlockSpec((tt, B_DK), lambda b, i: (i, 0)),
            const(A_KW, A_WIDTH), const(1, A_WIDTH), const(1, A_WIDTH), const(1, A_WIDTH),
            const(1, B_HEADS * B_DV), const(1, 2 * C_DQK), const(1, 2 * C_DQK),
        ] + [leaf(2 * C_DQK), leaf(C_DV)] * n_prev,
        out_specs=[
            tok(A_WIDTH), tok(B_HEADS * B_DV), tok(cw_dim), tok(cw_dim), tok(2 * C_HEADS * C_DV),
            kv_spec(2 * C_DQK), kv_spec(C_DV),
            pl.BlockSpec((1, A_HIST, A_WIDTH), lambda b, i: (b, 0, 0)),
            pl.BlockSpec((1, B_HEADS, B_DK, B_DV), lambda b, i: (b, 0, 0, 0)),
        ],
        out_shape=[
            jax.ShapeDtypeStruct((n, A_WIDTH), BF16),
            jax.ShapeDtypeStruct((n, B_HEADS * B_DV), BF16),
            jax.ShapeDtypeStruct((n, cw_dim), BF16),
            jax.ShapeDtypeStruct((n, cw_dim), BF16),
            jax.ShapeDtypeStruct((n, 2 * C_HEADS * C_DV), BF16),
            kv_shape(2 * C_DQK), kv_shape(C_DV),
            jax.ShapeDtypeStruct((bsz, A_HIST, A_WIDTH), F32),
            jax.ShapeDtypeStruct((bsz, B_HEADS, B_DK, B_DV), F32),
        ],
        scratch_shapes=[pltpu.VMEM((A_HIST + tt, A_WIDTH), F32), pltpu.VMEM((7, tt + 8, A_WIDTH), F32),
                        pltpu.VMEM((B_HEADS, B_DK, B_DV), F32)],
        compiler_params=_params("parallel", "arbitrary"),
        name="mixer",
    )(*src_args, hist, state, cos2, sin2, cw, cb, lng, lnb, gng, qg2, kg2, *[a for pair in earlier for a in pair])


def _t5_bucket(rel):
    half = REL_BUCKETS // 2
    exact = half // 2
    n = np.abs(rel)
    large = exact + (np.log(np.maximum(n, 1).astype(np.float32) / exact) / math.log(REL_MAX_DIST / exact)
                     * (half - exact)).astype(np.int32)
    large = np.minimum(large, half - 1)
    return (np.where(rel > 0, half, 0) + np.where(n < exact, n, large)).astype(np.int32)


def _bias_body(rb_ref, idx_ref, vis_ref, o_ref, *, row_tile, buckets):
    h = pl.program_id(0)
    far = rb_ref[REL_BUCKETS // 2 - 1, h]
    for k, present in enumerate(buckets):
        rows = slice(k * row_tile, (k + 1) * row_tile)
        idx = idx_ref[rows, :]
        acc = jnp.zeros(idx.shape, F32)
        for b in present:
            acc = jnp.where(idx == b, rb_ref[b, h], acc)
        o_ref[0, rows, :] = jnp.where(vis_ref[rows, :] != 0, acc - far, NEG_INF)


def _bias_tiles(rel_bias, idx, vis, row_tile):
    r, c = idx.shape
    buckets = tuple(tuple(int(b) for b in np.unique(idx[k:k + row_tile])) for k in range(0, r, row_tile))
    idx, vis = jnp.asarray(idx), jnp.asarray(vis)
    return pl.pallas_call(
        functools.partial(_bias_body, row_tile=row_tile, buckets=buckets),
        grid=(C_HEADS,),
        in_specs=[
            pl.BlockSpec(memory_space=pltpu.SMEM),
            pl.BlockSpec((r, c), lambda h: (0, 0)),
            pl.BlockSpec((r, c), lambda h: (0, 0)),
        ],
        out_specs=pl.BlockSpec((1, r, c), lambda h: (h, 0, 0)),
        out_shape=jax.ShapeDtypeStruct((C_HEADS, r, c), F32),
        compiler_params=_params("arbitrary"),
        name="bias_tiles",
    )(rel_bias, idx, vis)


def _lambda(lp_ref, lam_init):
    lp = lp_ref[...]
    e1 = jnp.exp(jnp.sum(lp[0:1] * lp[1:2], axis=-1, keepdims=True))
    e2 = jnp.exp(jnp.sum(lp[2:3] * lp[3:4], axis=-1, keepdims=True))
    return e1 - e2 + lam_init


def _stack_maps(q):
    lane = lax.broadcasted_iota(jnp.int32, q.shape, 1)
    zero = jnp.zeros_like(q)
    return jnp.concatenate([jnp.where(lane < C_DQK, q, zero), jnp.where(lane < C_DQK, zero, q)], axis=0)


def _attn_finish(acc, l, lam, g, lam_init, tq):
    o = acc / l
    o = o[0:tq] - lam * o[tq:2 * tq]
    return _rms(o, g) * (1.0 - lam_init)


def _attn_prompt_body(*refs, tq, lam_init, n_cast):
    _cast_slabs(refs[6:6 + n_cast], refs[7 + n_cast:7 + 2 * n_cast])
    q_ref, k_ref, v_ref, bias_ref, lp_ref, g_ref = refs[:6]
    o_ref = refs[6 + n_cast]
    qq_scr, m_scr, alpha_scr, p_scr, acc_scr = refs[7 + 2 * n_cast:]
    qi = pl.program_id(2)
    qq_scr[...] = _stack_maps(q_ref[...])
    m_scr[...] = jnp.full(m_scr.shape, NEG_INF, F32)
    acc_scr[...] = jnp.zeros(acc_scr.shape, F32)

    def tile(ref, idx):
        start = idx * tq if isinstance(idx, int) else pl.multiple_of(idx * tq, tq)
        return ref[pl.ds(start, tq), :]

    nl = tq // ATT_LANES
    chunks = [slice(c * ATT_ROWS, (c + 1) * ATT_ROWS) for c in range(2 * tq // ATT_ROWS)]

    def accumulate(rows, vb):
        alpha = alpha_scr[rows, :]
        acc_scr[rows, :] = jnp.concatenate([alpha, alpha], axis=-1) * acc_scr[rows, :] + _dot(p_scr[rows, :], vb)

    def step(ki, bias_idx, first=False):
        kb = tile(k_ref, ki)
        if not first:
            v_prev = tile(v_ref, ki - 1)
        for rows in chunks:
            if not first:
                accumulate(rows, v_prev)
            s = _dot_nt(qq_scr[rows, :], kb)
            if bias_idx is not None:
                s = s + bias_ref[0, bias_idx, pl.ds(rows.start % tq, ATT_ROWS), :]
            slabs = [s[:, j * ATT_LANES:(j + 1) * ATT_LANES] for j in range(nl)]
            m_old = m_scr[rows, :]
            m_new = jnp.maximum(m_old, jnp.max(functools.reduce(jnp.maximum, slabs), axis=-1, keepdims=True))
            alpha_scr[rows, :] = jnp.exp(m_old - m_new)
            p_scr[rows, :] = jnp.concatenate([jnp.exp(sl - m_new) for sl in slabs], axis=-1).astype(BF16)
            m_scr[rows, :] = m_new

    n_far = jnp.maximum(qi - 1, 0)
    step(0, jnp.minimum(qi, 2), first=True)

    def far_body(ki, c):
        step(ki, None)
        return c

    lax.fori_loop(1, n_far, far_body, 0)

    def near_body(ki, c):
        step(ki, qi - ki)
        return c

    lax.fori_loop(jnp.maximum(n_far, 1), qi + 1, near_body, 0)

    v_last = tile(v_ref, qi)
    for rows in chunks:
        accumulate(rows, v_last)
    lam = _lambda(lp_ref, lam_init)
    acc = acc_scr[...]
    o_ref[...] = _attn_finish(acc[:, 0:C_DV], acc[:, C_DV:2 * C_DV], lam, g_ref[...], lam_init, tq).astype(BF16)


def _attn_prompt(qn, kn, vb, bias, lp, g, *, bsz, t, lam_init, cast_ride=()):
    tq = ATT_TILE
    nq = t // tq
    n = bsz * t
    w = 2 * C_DQK
    cast_args, cast_in, cast_out, cast_shapes = (), [], [], []
    if cast_ride:
        cast_args, cast_in, cast_out, cast_shapes = _weight_cast_ride(
            cast_ride, bsz * C_HEADS * nq, lambda b, h, i: (b * C_HEADS + h) * nq + i)
    results = pl.pallas_call(
        functools.partial(_attn_prompt_body, tq=tq, lam_init=lam_init, n_cast=len(cast_args)),
        grid=(bsz, C_HEADS, nq),
        in_specs=[
            pl.BlockSpec((tq, w), lambda b, h, i: (b * nq + i, h)),
            pl.BlockSpec((t, w), lambda b, h, i: (b, h)),
            pl.BlockSpec((t, 2 * C_DV), lambda b, h, i: (b, h)),
            pl.BlockSpec((1, 3, tq, tq), lambda b, h, i: (h, 0, 0, 0)),
            pl.BlockSpec((4, C_DQK), lambda b, h, i: (0, 0)),
            pl.BlockSpec((1, C_DV), lambda b, h, i: (0, 0)),
        ] + cast_in,
        out_specs=[pl.BlockSpec((tq, C_DV), lambda b, h, i: (b * nq + i, h))] + cast_out,
        out_shape=[jax.ShapeDtypeStruct((n, C_HEADS * C_DV), BF16)] + cast_shapes,
        scratch_shapes=[pltpu.VMEM((2 * tq, w), BF16), pltpu.VMEM((2 * tq, ATT_LANES), F32),
                        pltpu.VMEM((2 * tq, ATT_LANES), F32), pltpu.VMEM((2 * tq, tq), BF16),
                        pltpu.VMEM((2 * tq, 2 * C_DV), F32)],
        compiler_params=_params("arbitrary", "arbitrary", "arbitrary"),
        name="attn_prompt",
    )(qn, kn, vb, bias, lp, g, *cast_args)
    return results[0], tuple(results[1:])


def _attn_sample_body(q_ref, k_ref, v_ref, pk_ref, pv_ref, bp_ref, bn_ref, lp_ref, g_ref, o_ref, *, tq, lam_init):
    lam = _lambda(lp_ref, lam_init)
    w = 2 * C_DQK
    for h in range(C_HEADS):
        qq = _stack_maps(q_ref[:, h * w:(h + 1) * w])
        bp = bp_ref[h]
        bn = bn_ref[h]
        past = pk_ref.shape[0] // C_HEADS
        head_rows = pl.ds(h, past, stride=C_HEADS)
        s_p = _dot_nt(qq, pk_ref[head_rows, :].astype(BF16)) + jnp.concatenate([bp, bp], axis=0)
        s_n = _dot_nt(qq, k_ref[:, h * w:(h + 1) * w]) + jnp.concatenate([bn, bn], axis=0)
        m = jnp.maximum(jnp.max(s_p, axis=-1, keepdims=True), jnp.max(s_n, axis=-1, keepdims=True))
        p_p = jnp.exp(s_p - m)
        p_n = jnp.exp(s_n - m)
        l = jnp.sum(p_p, axis=-1, keepdims=True) + jnp.sum(p_n, axis=-1, keepdims=True)
        acc = (_dot(p_p.astype(BF16), pv_ref[head_rows, :].astype(BF16))
               + _dot(p_n.astype(BF16), v_ref[:, 2 * h * C_DV:(2 * h + 1) * C_DV]))
        o_ref[:, h * C_DV:(h + 1) * C_DV] = _attn_finish(acc, l, lam, g_ref[...], lam_init, tq).astype(BF16)


def _attn_sample(qn, kn, vb, past_k, past_v, bias_p, bias_n, lp, g, layer, *, bsz, t, lam_init):
    n = bsz * t
    w = 2 * C_DQK
    depth, _, past = past_k.shape[:3]
    past_k = past_k.reshape(depth, bsz, past * C_HEADS, w)
    past_v = past_v.reshape(depth, bsz, past * C_HEADS, C_DV)
    full = lambda a: pl.BlockSpec(a.shape, lambda b: (0,) * a.ndim)
    return pl.pallas_call(
        functools.partial(_attn_sample_body, tq=t, lam_init=lam_init),
        grid=(bsz,),
        in_specs=[
            pl.BlockSpec((t, C_HEADS * w), lambda b: (b, 0)),
            pl.BlockSpec((t, C_HEADS * w), lambda b: (b, 0)),
            pl.BlockSpec((t, 2 * C_HEADS * C_DV), lambda b: (b, 0)),
            pl.BlockSpec((None, None, past * C_HEADS, w), lambda b: (layer, b, 0, 0)),
            pl.BlockSpec((None, None, past * C_HEADS, C_DV), lambda b: (layer, b, 0, 0)),
            full(bias_p), full(bias_n),
            pl.BlockSpec((4, C_DQK), lambda b: (0, 0)),
            pl.BlockSpec((1, C_DV), lambda b: (0, 0)),
        ],
        out_specs=pl.BlockSpec((t, C_HEADS * C_DV), lambda b: (b, 0)),
        out_shape=jax.ShapeDtypeStruct((n, C_HEADS * C_DV), BF16),
        compiler_params=_params("parallel"),
        name="attn_sample",
    )(qn, kn, vb, past_k, past_v, bias_p, bias_n, lp, g)


def _memkv_body(mem_ref, g_ref, wk_ref, wv_ref, kg_ref, k_ref, v_ref, kb_ref, vb_ref):
    hm = _rms(mem_ref[0], g_ref[...]).astype(BF16)
    k = _dot(hm, wk_ref[...])
    v = _dot(hm, wv_ref[...])
    for h in range(M_HEADS):
        kn = _rms(k[:, h * M_DH:(h + 1) * M_DH], kg_ref[...])
        k_ref[0, :, h * M_DH:(h + 1) * M_DH] = kn
        kb_ref[0, :, h * M_DH:(h + 1) * M_DH] = kn.astype(BF16)
    v_ref[0] = v
    vb_ref[0] = v.astype(BF16)


def _memkv(mem, g, wk, wv, kg):
    bsz, m, _ = mem.shape
    w = M_HEADS * M_DH
    blk = pl.BlockSpec((1, m, w), lambda b: (b, 0, 0))
    return pl.pallas_call(
        _memkv_body,
        grid=(bsz,),
        in_specs=[
            pl.BlockSpec((1, m, D_MODEL), lambda b: (b, 0, 0)),
            pl.BlockSpec((1, D_MODEL), lambda b: (0, 0)),
            pl.BlockSpec((D_MODEL, w), lambda b: (0, 0)),
            pl.BlockSpec((D_MODEL, w), lambda b: (0, 0)),
            pl.BlockSpec((1, M_DH), lambda b: (0, 0)),
        ],
        out_specs=[blk, blk, blk, blk],
        out_shape=[jax.ShapeDtypeStruct((bsz, m, w), F32), jax.ShapeDtypeStruct((bsz, m, w), F32),
                   jax.ShapeDtypeStruct((bsz, m, w), BF16), jax.ShapeDtypeStruct((bsz, m, w), BF16)],
        compiler_params=_params("parallel"),
        name="memkv",
    )(mem, g, wk, wv, kg)


def _post_body(x_ref, a_ref, bo_ref, co_ref, wo_ref, g_ref, wq_ref, qg_ref, mk_ref, mv_ref, wxo_ref,
               o_ref, att_scr, *, nseq, rps):
    rc = x_ref.shape[0] // POST_CHAINS
    seg = min(rc, rps)
    for c in range(POST_CHAINS):
        rows = slice(c * rc, (c + 1) * rc)
        y = (_dot(a_ref[rows, :], wo_ref[0:A_WIDTH, :])
             + _dot(bo_ref[rows, :], wo_ref[A_WIDTH:A_WIDTH + B_HEADS * B_DV, :])
             + _dot(co_ref[rows, :], wo_ref[A_WIDTH + B_HEADS * B_DV:, :]))
        x1 = x_ref[rows, :] + y
        q = _dot(_rms(x1, g_ref[...]).astype(BF16), wq_ref[...])
        for h in range(M_HEADS):
            sl = slice(h * M_DH, (h + 1) * M_DH)
            qn = _rms(q[:, sl], qg_ref[...]).astype(BF16)
            for u in range(rc // seg):
                r0 = c * rc + u * seg
                s = r0 // rps
                logits = _dot_nt(qn[u * seg:(u + 1) * seg], mk_ref[s, :, sl]) * (M_DH ** -0.5)
                m = jnp.max(logits, axis=-1, keepdims=True)
                p = jnp.exp(logits - m)
                l = jnp.sum(p, axis=-1, keepdims=True)
                o = _dot(p.astype(BF16), mv_ref[s, :, sl]) / l
                att_scr[r0:r0 + seg, sl] = o.astype(BF16)
        o_ref[rows, :] = x1 + _dot(att_scr[rows, :], wxo_ref[...])


def _post(x, a, bo, co, wo, g, wq, qg, mk, mv, wxo, layer, *, t, tm):
    n = x.shape[0]
    w = M_HEADS * M_DH
    assert n % tm == 0 and (t % tm == 0 or tm % t == 0), (n, t, tm)
    if t >= tm:
        nseq, rps = 1, tm
        per = t // tm
        mem_map = lambda i: (i // per, 0, 0)
    else:
        nseq, rps = tm // t, t
        mem_map = lambda i: (i, 0, 0)
    tok = lambda cols: pl.BlockSpec((tm, cols), lambda i: (i, 0))
    const = lambda r, c: pl.BlockSpec((r, c), lambda i: (0, 0), pipeline_mode=pl.Buffered(1))
    stacked = lambda r, c: pl.BlockSpec((None, r, c), lambda i: (layer, 0, 0), pipeline_mode=pl.Buffered(1))
    return pl.pallas_call(
        functools.partial(_post_body, nseq=nseq, rps=rps),
        grid=(n // tm,),
        in_specs=[
            tok(D_MODEL), tok(A_WIDTH), tok(B_HEADS * B_DV), tok(C_HEADS * C_DV),
            stacked(D_MODEL, D_MODEL), const(1, D_MODEL), stacked(D_MODEL, w), const(1, M_DH),
            pl.BlockSpec((nseq, MEM_LEN, w), mem_map),
            pl.BlockSpec((nseq, MEM_LEN, w), mem_map),
            stacked(w, D_MODEL),
        ],
        out_specs=tok(D_MODEL),
        out_shape=jax.ShapeDtypeStruct((n, D_MODEL), F32),
        scratch_shapes=[pltpu.VMEM((tm, w), BF16)],
        compiler_params=_params("parallel"),
        name="post",
    )(x, a, bo, co, wo, g, wq, qg, mk, mv, wxo)


def _ffn_body(*refs, nseq, rps, per, n_cast):
    _cast_slabs(refs[11:11 + n_cast], refs[14 + n_cast:14 + 2 * n_cast])
    x_ref, g_ref, wv_ref, wg_ref, cwv_ref, cwg_ref, cbv_ref, cbg_ref, hv_ref, hg_ref, wd_ref = refs[:11]
    o_ref, nv_ref, ng_ref = refs[11 + n_cast:14 + n_cast]
    h_scr, ubuf, tail = refs[14 + 2 * n_cast:]
    i = pl.program_id(0)
    j = pl.program_id(1)

    @pl.when(j == 0)
    def _():
        x = x_ref[...]
        h_scr[...] = _rms(x, g_ref[...]).astype(BF16)
        o_ref[...] = x

    if per > 1:
        @pl.when((i == 0) & (j == 0))
        def _():
            tail[...] = jnp.zeros(tail.shape, F32)

    seq_start = (i % per) == 0
    stride = rps + F_HIST
    tm = x_ref.shape[0]
    rc = tm // FFN_CHAINS
    assert rc % rps == 0 or rps % rc == 0

    seg = min(rc, rps)
    chains = [slice(r * rc, (r + 1) * rc) for r in range(FFN_CHAINS)]

    def up(rows):
        return _dot(h_scr[rows, :], wv_ref[...]), _dot(h_scr[rows, :], wg_ref[...])

    def keep(us):
        for rows, pair in zip(chains, us):
            for half, u in enumerate(pair):
                for q in range(rc // seg):
                    s, o = divmod(rows.start + q * seg, rps)
                    dst = s * stride + F_HIST + o
                    ubuf[half, dst:dst + seg, :] = u[q * seg:(q + 1) * seg]

    def conv(half, r0, cw_ref, cb_ref, hist_ref, new_ref):
        outs = []
        for q in range(rc // seg):
            s, o = divmod(r0 + q * seg, rps)
            base = s * stride
            if o == 0:
                if per == 1:
                    prev = hist_ref[s]
                else:
                    prev = jnp.where(seq_start, hist_ref[s], tail[half, j])
                ubuf[half, base:base + F_HIST, :] = prev
            c = cb_ref[...]
            for k in range(F_KW):
                off = base + F_HIST + o - (F_KW - 1) + k
                c = c + ubuf[half, off:off + seg, :] * cw_ref[k:k + 1, :]
            outs.append(c)
            if o + seg == rps:
                last = ubuf[half, base + rps:base + stride, :]
                new_ref[s] = last
                if per > 1:
                    tail[half, j] = last
        return outs[0] if len(outs) == 1 else jnp.concatenate(outs, axis=0)

    def down(rows):
        val = conv(0, rows.start, cwv_ref, cbv_ref, hv_ref, nv_ref)
        gate = conv(1, rows.start, cwg_ref, cbg_ref, hg_ref, ng_ref)
        act = (gate * jax.nn.sigmoid(gate) * val).astype(BF16)
        o_ref[rows, :] += _dot(act, wd_ref[...])

    keep([up(rows) for rows in chains])
    for rows in chains:
        down(rows)


def _slab(total, unit, steps):
    return next(unit * k for k in range(1, total // unit + 1) if total % (unit * k) == 0 and total // (unit * k) <= steps)


def _weight_cast_ride(entries, steps, step_of):
    args, in_specs, out_specs, out_shapes = [], [], [], []
    for w, layer, axis in entries:
        _, k, ncol = w.shape
        size = _slab((k, ncol)[axis], (BF16_SUBLANES, LANES)[axis], steps)
        last = (k, ncol)[axis] // size - 1
        block = (None, size, ncol) if axis == 0 else (None, k, size)

        def index(*g, lead, axis=axis, last=last):
            s = jnp.minimum(step_of(*g), last)
            return (lead, s, 0) if axis == 0 else (lead, 0, s)

        args.append(w)
        in_specs.append(pl.BlockSpec(block, functools.partial(index, lead=layer)))
        out_specs.append(pl.BlockSpec(block, functools.partial(index, lead=0)))
        out_shapes.append(jax.ShapeDtypeStruct((1, k, ncol), BF16))
    return tuple(args), in_specs, out_specs, out_shapes


def _cast_slabs(f32_refs, bf16_refs):
    for src, dst in zip(f32_refs, bf16_refs):
        dst[...] = src[...].astype(BF16)


def _cast_weights(entries, steps=CAST_STEPS):
    args, in_specs, out_specs, out_shapes = _weight_cast_ride(entries, steps, lambda s: s)
    return pl.pallas_call(
        lambda *refs: _cast_slabs(refs[:len(args)], refs[len(args):]),
        grid=(steps,),
        in_specs=in_specs,
        out_specs=out_specs,
        out_shape=out_shapes,
        compiler_params=_params("arbitrary"),
        name="cast_weights",
    )(*args)


def _ffn(x, g, w_up, cw, cb, hist, w_down, layer, *, t, cast_next=None, tm=TOKEN_TILE, tn=COL_TILE):
    n = x.shape[0]
    nj = D_FF // tn
    assert n % tm == 0 and (t % tm == 0 or tm % t == 0), (n, t, tm)
    if t >= tm:
        nseq, rps, per = 1, tm, t // tm
        seq_map = lambda i: i // per
    else:
        nseq, rps, per = tm // t, t, 1
        seq_map = lambda i: i
    bsz = hist.shape[0]
    hist_spec = lambda off: pl.BlockSpec((nseq, F_HIST, tn), lambda i, j: (seq_map(i), 0, j + off))
    new_spec = pl.BlockSpec((nseq, F_HIST, tn), lambda i, j: (i, 0, j))
    cast_args, cast_in, cast_out, cast_shapes = (), [], [], []
    if cast_next:
        cast_args, cast_in, cast_out, cast_shapes = _weight_cast_ride(cast_next, (n // tm) * nj,
                                                                      lambda i, j: i * nj + j)
    results = pl.pallas_call(
        functools.partial(_ffn_body, nseq=nseq, rps=rps, per=per, n_cast=len(cast_args)),
        grid=(n // tm, nj),
        in_specs=[
            pl.BlockSpec((tm, D_MODEL), lambda i, j: (i, 0)),
            pl.BlockSpec((1, D_MODEL), lambda i, j: (0, 0)),
            pl.BlockSpec((None, D_MODEL, tn), lambda i, j: (layer, 0, j)),
            pl.BlockSpec((None, D_MODEL, tn), lambda i, j: (layer, 0, j + nj)),
            pl.BlockSpec((F_KW, tn), lambda i, j: (0, j)),
            pl.BlockSpec((F_KW, tn), lambda i, j: (0, j + nj)),
            pl.BlockSpec((1, tn), lambda i, j: (0, j)),
            pl.BlockSpec((1, tn), lambda i, j: (0, j + nj)),
            hist_spec(0), hist_spec(nj),
            pl.BlockSpec((None, tn, D_MODEL), lambda i, j: (layer, j, 0)),
        ] + cast_in,
        out_specs=[pl.BlockSpec((tm, D_MODEL), lambda i, j: (i, 0)), new_spec, new_spec] + cast_out,
        out_shape=[jax.ShapeDtypeStruct((n, D_MODEL), F32),
                   jax.ShapeDtypeStruct((bsz * per, F_HIST, D_FF), F32),
                   jax.ShapeDtypeStruct((bsz * per, F_HIST, D_FF), F32)] + cast_shapes,
        scratch_shapes=[
            pltpu.VMEM((tm, D_MODEL), BF16),
            pltpu.VMEM((2, nseq * (rps + F_HIST), tn), F32),
            pltpu.VMEM((2, nj, F_HIST, tn), F32),
        ],
        compiler_params=_params("arbitrary", "arbitrary"),
        name="ffn",
    )(x, g, w_up, w_up, cw, cw, cb, cb, hist, hist, w_down, *cast_args)
    x_out, tail_v, tail_g = results[:3]
    last = lambda a: a.reshape(bsz, per, F_HIST, D_FF)[:, per - 1]
    return x_out, last(tail_v), last(tail_g), tuple(results[3:])


def _rope_tables(pos0, t):
    half = B_DK // 2
    inv = 1.0 / (ROPE_BASE ** (jnp.arange(half, dtype=F32) / half))
    ang = (pos0 + jnp.arange(t, dtype=jnp.int32)).astype(F32)[:, None] * inv[None, :]
    cos = jnp.cos(ang)
    sin = jnp.sin(ang)
    return jnp.concatenate([cos, cos], axis=-1), jnp.concatenate([-sin, sin], axis=-1)


def _pad_rows(h, rows):
    return jnp.pad(h, ((0, 0), (rows - h.shape[1], 0), (0, 0)))


def _row(v):
    return v.reshape(1, -1).astype(F32)


def kernel(x_prompt, x_sample, mem_prompt, state_conv_a, state_ret, cache_diff_k, cache_diff_v, cache_mem_k,
           cache_mem_v, state_conv_f, norm1_g, w_in, conv_a_w, conv_a_b, ln_a_g, ln_a_b, ret_gn_g, diff_qn_g,
           diff_kn_g, diff_lq1, diff_lk1, diff_lq2, diff_lk2, diff_subln_g, w_out, rel_bias, norm2_g, mem_norm_g,
           w_xq, w_xk, w_xv, xqn_g, xkn_g, w_xo, norm3_g, w_up, conv_f_w, conv_f_b, w_down):
    bp, tp, _ = x_prompt.shape
    bs, ts, _ = x_sample.shape
    depth = w_in.shape[0]
    past = cache_diff_k.shape[2]
    cw_dim = C_HEADS * 2 * C_DQK

    tq = ATT_TILE
    r = np.arange(tq)
    rel_diag = r[None, :] - r[:, None]
    idx_p = np.stack([_t5_bucket(rel_diag - d * tq) for d in range(3)]).reshape(3 * tq, tq)
    vis_p = np.stack([(r[None, :] // CHUNK) <= (r[:, None] // CHUNK), np.ones((tq, tq), bool), np.ones((tq, tq), bool)])
    vis_p = vis_p.reshape(3 * tq, tq).astype(np.int32)
    bias_p = _bias_tiles(rel_bias, idx_p, vis_p, tq).reshape(C_HEADS, 3, tq, tq)
    rel_s = np.arange(past + ts)[None, :] - (past + np.arange(ts))[:, None]
    bias_s = _bias_tiles(rel_bias, _t5_bucket(rel_s), np.ones(rel_s.shape, np.int32), ts)
    bias_s_past, bias_s_new = bias_s[:, :, :past], bias_s[:, :, past:]

    cos_p, sin_p = _rope_tables(0, tp)
    cos_s, sin_s = _rope_tables(past, ts)

    xp = x_prompt.reshape(bp * tp, D_MODEL)
    xs = x_sample.reshape(bs * ts, D_MODEL)
    zero_a = jnp.zeros((bp, A_HIST, A_WIDTH), F32)
    zero_r = jnp.zeros((bp, B_HEADS, B_DK, B_DV), F32)
    zero_f = jnp.zeros((bp, F_HIST, 2 * D_FF), F32)

    outs = {k: [] for k in ("p_ca", "p_rs", "p_mk", "p_mv", "p_cf", "s_ca", "s_rs", "s_cf")}
    kv_p, kv_s = [], []
    small = {"in": w_in, "out": w_out, "xq": w_xq, "xo": w_xo}
    cur = dict(zip(small, _cast_weights([(w, 0, 1) for w in small.values()])))
    nxt = {}
    ffn_w = [None]
    for l in range(depth):
        lam_init = 0.8 - 0.6 * math.exp(-0.3 * l)
        has_next = l + 1 < depth
        w_in_b, w_out_b, w_xq_b, w_xo_b = cur["in"], cur["out"], cur["xq"], cur["xo"]
        lp = jnp.stack([diff_lq1[l], diff_lk1[l], diff_lq2[l], diff_lk2[l]]).astype(F32)
        qg2 = _row(jnp.concatenate([diff_qn_g[l], diff_qn_g[l]]))
        kg2 = _row(jnp.concatenate([diff_kn_g[l], diff_kn_g[l]]))
        mix_args = (conv_a_w[l], _row(conv_a_b[l]), _row(ln_a_g[l]), _row(ln_a_b[l]), _row(ret_gn_g[l]), qg2, kg2)
        subln = _row(diff_subln_g[l])

        def block(x, bsz, t, tt, hist_a, state_r, cos2, sin2, attend, mk_b, mv_b, hist_f, tm_post, kv_leaves,
                  cast_next=None):
            if tt >= FUSED_PROJ_MIN_ROWS:
                src = (x, _row(norm1_g[l]), w_in_b, 0)
            else:
                src = _in_proj(x, _row(norm1_g[l]), w_in_b, 0)
            a, bo, qn, kn, vb, ck, cv, nh, nr = _mixer(src, hist_a, state_r, cos2, sin2, *mix_args,
                                                        bsz=bsz, t=t, tt=tt, earlier=() if has_next else tuple(kv_leaves))
            kv_leaves.append((ck, cv))
            co = attend(qn, kn, vb)
            x = _post(x, a, bo, co, w_out_b, _row(norm2_g[l]), w_xq_b, _row(xqn_g[l]), mk_b, mv_b, w_xo_b,
                      0, t=t, tm=tm_post)
            w_up_b, w_down_b = ffn_w[0]
            x, nfv, nfg, next_w = _ffn(x, _row(norm3_g[l]), w_up_b, conv_f_w[l], _row(conv_f_b[l]), hist_f, w_down_b,
                                       0, t=t, cast_next=cast_next)
            new_f = jnp.concatenate([nfv[:, F_HIST - (F_KW - 1):], nfg[:, F_HIST - (F_KW - 1):]], axis=-1)
            return x, nh[:, A_HIST - (A_KW - 1):], nr, new_f, next_w

        mk, mv, mk_b, mv_b = _memkv(mem_prompt, _row(mem_norm_g[l]), w_xk[l].astype(BF16), w_xv[l].astype(BF16),
                                    _row(xkn_g[l]))
        def attend_p(qn, kn, vb):
            ride = [(w, l + 1, 1) for w in small.values()] if has_next else []
            if ffn_w[0] is None:
                ride += [(w_up, l, 1), (w_down, l, 0)]
            co, cast = _attn_prompt(qn, kn, vb, bias_p, lp, subln, bsz=bp, t=tp, lam_init=lam_init, cast_ride=ride)
            if has_next:
                nxt.update(zip(small, cast[:len(small)]))
            if ffn_w[0] is None:
                ffn_w[0] = cast[-2:]
            return co

        xp, ca, rs, cf, next_w = block(xp, bp, tp, MIX_TILE, zero_a, zero_r, cos_p, sin_p, attend_p, mk_b,
                                       mv_b, zero_f, POST_TILE_PROMPT, kv_p,
                                       cast_next=[(w_up, l + 1, 1), (w_down, l + 1, 0)] if has_next else None)
        outs["p_ca"].append(ca); outs["p_rs"].append(rs)
        outs["p_mk"].append(mk.reshape(bp, MEM_LEN, M_HEADS, M_DH))
        outs["p_mv"].append(mv.reshape(bp, MEM_LEN, M_HEADS, M_DH))
        outs["p_cf"].append(cf)

        attend_s = lambda qn, kn, vb: _attn_sample(qn, kn, vb, cache_diff_k, cache_diff_v, bias_s_past,
                                                   bias_s_new, lp, subln, l,
                                                   bsz=bs, t=ts, lam_init=lam_init)
        smk = cache_mem_k[l].reshape(bs, MEM_LEN, M_HEADS * M_DH).astype(BF16)
        smv = cache_mem_v[l].reshape(bs, MEM_LEN, M_HEADS * M_DH).astype(BF16)
        xs, sca, srs, scf, _ = block(xs, bs, ts, ts, _pad_rows(state_conv_a[l], A_HIST), state_ret[l],
                                     cos_s, sin_s, attend_s, smk, smv, _pad_rows(state_conv_f[l], F_HIST),
                                     POST_TILE, kv_s)
        outs["s_ca"].append(sca); outs["s_rs"].append(srs)
        outs["s_cf"].append(scf)
        if has_next:
            ffn_w[0] = next_w
            cur = dict(nxt)

    st = lambda k: jnp.stack(outs[k])

    def heads(leaf, bsz, t):
        leaf = leaf if leaf.ndim == 4 else leaf[None]
        return leaf.reshape(depth, bsz, t, C_HEADS, leaf.shape[-1])

    return (xp.reshape(bp, tp, D_MODEL), xs.reshape(bs, ts, D_MODEL),
            st("p_ca"), st("p_rs"), heads(kv_p[-1][0], bp, tp), heads(kv_p[-1][1], bp, tp), st("p_mk"), st("p_mv"),
            st("p_cf"),
            st("s_ca"), st("s_rs"), heads(kv_s[-1][0], bs, ts), heads(kv_s[-1][1], bs, ts), st("s_cf"))
```
